```python
import jax, jax.numpy as jnp
from jax import lax
import numpy as np

D_MODEL = 1024
BATCH = 8
SEQ = 4096
DEPTH = 1
DEC_BATCH = 128
DEC_SEQ = 4
PAST_LEN = 16384
PAGE_SIZE = 128

N_HEADS = 16
HEAD_DIM = D_MODEL // N_HEADS
N_KV_HEADS = 4
GQA_GROUP = N_HEADS // N_KV_HEADS
WINDOW = 128
ATTN_BLOCK = WINDOW
ROPE_THETA = 10000.0
NEG_INF = -1e30
LRU_WIDTH = D_MODEL
LRU_BLOCKS = 16
LRU_BLOCK_W = LRU_WIDTH // LRU_BLOCKS
CONV_WIDTH = 4
LRU_C = 8.0
N_GROUPS = 4
EXPERTS_PER_GROUP = 8
N_EXPERTS = N_GROUPS * EXPERTS_PER_GROUP
TOP_K = 2
D_EXPERT = D_MODEL // 2
MOE_BLOCK = 128
NORM_EPS = 1e-6
Q_DIM = N_HEADS * HEAD_DIM
KV_DIM = N_KV_HEADS * HEAD_DIM
IN_DIM = Q_DIM + 2 * KV_DIM + 2 * LRU_WIDTH + 2 * D_MODEL

PARAM_NAMES = ('w_ada', 'b_ada', 'g_pre_mix', 'g_post_mix', 'g_pre_ffn', 'g_post_ffn', 'w_in', 'sinks',
               'w_conv', 'b_conv', 'w_lru_a', 'b_lru_a', 'w_lru_x', 'b_lru_x', 'lru_lambda',
               'w_attn_branch', 'w_lru_branch', 'w_out', 'w_router_group', 'b_router_group',
               'w_router_expert', 'b_router_expert', 'w_exp_gate', 'w_exp_up', 'w_exp_down')

kernel_name = 'hybrid_swa_rglru_hmoe_decode_step'


def _rms_norm(x, g):
    xf = x.astype(jnp.float32)
    y = xf * lax.rsqrt(jnp.mean(xf * xf, axis=-1, keepdims=True) + NORM_EPS)
    return (y * g.astype(jnp.float32)).astype(x.dtype)


def _rope(x, pos):
    half = HEAD_DIM // 2
    inv = jnp.power(jnp.float32(ROPE_THETA), -jnp.arange(half, dtype=jnp.float32) / half)
    ang = pos.astype(jnp.float32)[:, None] * inv[None, :]
    cos = jnp.cos(ang)[:, None, :]
    sin = jnp.sin(ang)[:, None, :]
    xf = x.astype(jnp.float32)
    x1, x2 = xf[..., :half], xf[..., half:]
    return jnp.concatenate([x1 * cos - x2 * sin, x2 * cos + x1 * sin], axis=-1).astype(x.dtype)


def _window_attention(q, k, v, q_pos, k_pos, sinks):
    s = jnp.einsum('...qhgd,...khd->...hgqk', q, k, preferred_element_type=jnp.float32) * (HEAD_DIM ** -0.5)
    d = q_pos[..., :, None] - k_pos[..., None, :]
    valid = (d >= 0) & (d <= WINDOW) & (k_pos[..., None, :] >= 0)
    s = jnp.where(valid[..., None, None, :, :], s, NEG_INF)
    sink = sinks.astype(jnp.float32).reshape(N_KV_HEADS, GQA_GROUP, 1, 1)
    m = jnp.maximum(jnp.max(s, axis=-1, keepdims=True), sink)
    p = jnp.exp(s - m)
    denom = jnp.sum(p, axis=-1, keepdims=True) + jnp.exp(sink - m)
    o = jnp.einsum('...hgqk,...khd->...qhgd', p / denom, v.astype(jnp.float32))
    return o.astype(q.dtype)


def _prompt_attention(q, k, v, sinks):
    B, S = q.shape[0], q.shape[1]
    nb = S // ATTN_BLOCK
    qb = q.reshape(B, nb, ATTN_BLOCK, N_KV_HEADS, GQA_GROUP, HEAD_DIM)
    kb = k.reshape(B, nb, ATTN_BLOCK, N_KV_HEADS, HEAD_DIM)
    vb = v.reshape(B, nb, ATTN_BLOCK, N_KV_HEADS, HEAD_DIM)

    def with_prev(t):
        prev = jnp.concatenate([jnp.zeros_like(t[:, :1]), t[:, :-1]], axis=1)
        return jnp.concatenate([prev, t], axis=2)

    pos = jnp.arange(S, dtype=jnp.int32).reshape(nb, ATTN_BLOCK)
    k_pos = jnp.concatenate([pos - ATTN_BLOCK, pos], axis=-1)
    o = _window_attention(qb, with_prev(kb), with_prev(vb), pos, k_pos, sinks)
    return o.reshape(B, S, Q_DIM)


def _sample_attention(q, k, v, k_cache, v_cache, sinks, q_pos):
    B, T = q.shape[0], q.shape[1]
    k_all = jnp.concatenate([k_cache.astype(k.dtype), k], axis=1)
    v_all = jnp.concatenate([v_cache.astype(v.dtype), v], axis=1)
    k_pos = (PAST_LEN - k_cache.shape[1]) + jnp.arange(k_all.shape[1], dtype=jnp.int32)
    qg = q.reshape(B, T, N_KV_HEADS, GQA_GROUP, HEAD_DIM)
    o = _window_attention(qg, k_all, v_all, q_pos, k_pos, sinks)
    return o.reshape(B, T, Q_DIM), k_all[:, -WINDOW:], v_all[:, -WINDOW:]


def _causal_conv(x, prev, w, b):
    T = x.shape[1]
    xp = jnp.concatenate([prev.astype(x.dtype), x], axis=1)
    y = sum(xp[:, j:j + T] * w[j] for j in range(CONV_WIDTH)) + b
    return y, xp[:, -(CONV_WIDTH - 1):]


def _rg_lru(x, h0, w_a, b_a, w_x, b_x, lam):
    B, T, _ = x.shape
    xf = x.astype(jnp.float32)
    xb = xf.reshape(B, T, LRU_BLOCKS, LRU_BLOCK_W)
    r = jax.nn.sigmoid(jnp.einsum('btni,nij->btnj', xb, w_a.astype(jnp.float32)).reshape(B, T, LRU_WIDTH)
                       + b_a.astype(jnp.float32))
    i = jax.nn.sigmoid(jnp.einsum('btni,nij->btnj', xb, w_x.astype(jnp.float32)).reshape(B, T, LRU_WIDTH)
                       + b_x.astype(jnp.float32))
    log_a = -LRU_C * r * jax.nn.softplus(-lam.astype(jnp.float32))
    a = jnp.exp(log_a)
    u = jnp.sqrt(-jnp.expm1(2.0 * log_a)) * (i * xf)
    u = u.at[:, 0].add(a[:, 0] * h0.astype(jnp.float32))

    def combine(left, right):
        return (left[0] * right[0], right[0] * left[1] + right[1])

    _, h = lax.associative_scan(combine, (a, u), axis=1)
    return h, h[:, -1]


def _hier_moe(h, w_rg, b_rg, w_re, b_re, w_gate, w_up, w_down):
    B, T, D = h.shape
    N = B * T
    hf = h.reshape(N, D)
    g_logits = (hf @ w_rg).astype(jnp.float32) + b_rg.astype(jnp.float32)
    p_group = jax.nn.softmax(g_logits, axis=-1)
    g_star = jnp.argmax(g_logits, axis=-1).astype(jnp.int32)
    e_logits = ((hf @ w_re).astype(jnp.float32) + b_re.astype(jnp.float32)).reshape(N, N_GROUPS, EXPERTS_PER_GROUP)
    in_group = jnp.take_along_axis(e_logits, g_star[:, None, None], axis=1)[:, 0]
    top_v, top_i = lax.top_k(in_group, TOP_K)
    gate_w = jax.nn.softmax(top_v, axis=-1) * jnp.take_along_axis(p_group, g_star[:, None], axis=1)
    e_idx = g_star[:, None] * EXPERTS_PER_GROUP + top_i.astype(jnp.int32)

    A = N * TOP_K
    flat_e = e_idx.reshape(-1)
    flat_w = gate_w.reshape(-1)
    order = jnp.argsort(flat_e)
    sorted_e = flat_e[order]
    tok = (order // TOP_K).astype(jnp.int32)
    counts = jnp.bincount(flat_e, length=N_EXPERTS).astype(jnp.int32)
    padded = (counts + MOE_BLOCK - 1) // MOE_BLOCK * MOE_BLOCK
    starts = jnp.cumsum(counts) - counts
    pends = jnp.cumsum(padded)
    pstarts = pends - padded
    dest = pstarts[sorted_e] + jnp.arange(A, dtype=jnp.int32) - starts[sorted_e]
    n_blocks = -(-A // MOE_BLOCK) + N_EXPERTS
    P = n_blocks * MOE_BLOCK
    slot_tok = jnp.full((P,), N, jnp.int32).at[dest].set(tok)
    slot_w = jnp.zeros((P,), jnp.float32).at[dest].set(flat_w[order])
    block_e = jnp.minimum(jnp.searchsorted(pends, jnp.arange(n_blocks, dtype=jnp.int32) * MOE_BLOCK, side='right'),
                          N_EXPERTS - 1).astype(jnp.int32)
    h_pad = jnp.concatenate([hf, jnp.zeros((1, D), hf.dtype)], axis=0)
    xb = h_pad[slot_tok].reshape(n_blocks, MOE_BLOCK, D)

    def expert_block(args):
        xblk, e = args
        g = xblk @ w_gate[e]
        u = xblk @ w_up[e]
        return (jax.nn.silu(g) * u) @ w_down[e]

    yb = lax.map(expert_block, (xb, block_e)).reshape(P, D)
    y = jnp.zeros((N + 1, D), jnp.float32).at[slot_tok].add(yb.astype(jnp.float32) * slot_w[:, None])[:N]
    return y.astype(h.dtype).reshape(B, T, D)


def _layer(x, c, k_prev, v_prev, conv_prev, h_prev, p, sample):
    B, T, _ = x.shape
    mod = jax.nn.silu(c) @ p['w_ada'] + p['b_ada']
    sh_a, sc_a, gt_a, sh_f, sc_f, gt_f = [m[:, None, :] for m in jnp.split(mod, 6, axis=-1)]

    h = _rms_norm(x, p['g_pre_mix']) * (1.0 + sc_a) + sh_a
    proj = h @ p['w_in']
    o1 = Q_DIM
    o2 = o1 + KV_DIM
    o3 = o2 + KV_DIM
    o4 = o3 + LRU_WIDTH
    o5 = o4 + LRU_WIDTH
    o6 = o5 + D_MODEL
    q, k, v, xl, yl, ga, gr = jnp.split(proj, [o1, o2, o3, o4, o5, o6], axis=-1)

    start = PAST_LEN if sample else 0
    pos = start + jnp.arange(T, dtype=jnp.int32)
    q = _rope(q.reshape(B, T, N_HEADS, HEAD_DIM), pos)
    k = _rope(k.reshape(B, T, N_KV_HEADS, HEAD_DIM), pos)
    v = v.reshape(B, T, N_KV_HEADS, HEAD_DIM)
    if sample:
        o_attn, k_new, v_new = _sample_attention(q, k, v, k_prev, v_prev, p['sinks'], pos)
    else:
        o_attn = _prompt_attention(q, k, v, p['sinks'])
        k_new, v_new = k[:, -WINDOW:], v[:, -WINDOW:]

    xc, conv_new = _causal_conv(xl, conv_prev, p['w_conv'], p['b_conv'])
    hl, h_new = _rg_lru(xc, h_prev, p['w_lru_a'], p['b_lru_a'], p['w_lru_x'], p['b_lru_x'], p['lru_lambda'])
    o_lru = (hl * jax.nn.gelu(yl.astype(jnp.float32))).astype(x.dtype)

    b_attn = o_attn @ p['w_attn_branch']
    b_lru = o_lru @ p['w_lru_branch']
    merged = jax.nn.sigmoid(ga) * b_attn + jax.nn.sigmoid(gr) * b_lru
    mix = merged @ p['w_out']
    x = x + gt_a * _rms_norm(mix, p['g_post_mix'])

    h2 = _rms_norm(x, p['g_pre_ffn']) * (1.0 + sc_f) + sh_f
    f = _hier_moe(h2, p['w_router_group'], p['b_router_group'], p['w_router_expert'], p['b_router_expert'],
                  p['w_exp_gate'], p['w_exp_up'], p['w_exp_down'])
    x = x + gt_f * _rms_norm(f, p['g_post_ffn'])
    return x, k_new, v_new, conv_new, h_new.astype(x.dtype)


def setup_inputs(seed: int = 0) -> dict:
    key = jax.random.key(seed)
    ks = jax.random.split(key, 40)
    f32 = jnp.float32

    def nrm(k, shape, scale):
        return jax.random.normal(k, shape, f32) * scale

    L = DEPTH
    D = D_MODEL
    a8 = jax.random.uniform(ks[20], (L, LRU_WIDTH), f32, minval=0.9, maxval=0.999)
    a = a8 ** (1.0 / LRU_C)
    lam = jnp.log(a) - jnp.log1p(-a)
    return {
        'x_prompt': nrm(ks[0], (BATCH, SEQ, D), 1.0),
        'x_sample': nrm(ks[1], (DEC_BATCH, DEC_SEQ, D), 1.0),
        'cache_k_win': nrm(ks[2], (L, DEC_BATCH, WINDOW, N_KV_HEADS, HEAD_DIM), 1.0),
        'cache_v_win': nrm(ks[3], (L, DEC_BATCH, WINDOW, N_KV_HEADS, HEAD_DIM), 1.0),
        'state_conv': nrm(ks[4], (L, DEC_BATCH, CONV_WIDTH - 1, LRU_WIDTH), 1.0),
        'state_h': nrm(ks[5], (L, DEC_BATCH, LRU_WIDTH), 0.5),
        'c_prompt': nrm(ks[6], (BATCH, D), 1.0),
        'c_sample': nrm(ks[7], (DEC_BATCH, D), 1.0),
        'w_ada': nrm(ks[8], (L, D, 6 * D), 0.5 * D ** -0.5),
        'b_ada': nrm(ks[9], (L, 6 * D), 0.01),
        'g_pre_mix': 1.0 + nrm(ks[10], (L, D), 0.1),
        'g_post_mix': 1.0 + nrm(ks[11], (L, D), 0.1),
        'g_pre_ffn': 1.0 + nrm(ks[12], (L, D), 0.1),
        'g_post_ffn': 1.0 + nrm(ks[13], (L, D), 0.1),
        'w_in': nrm(ks[14], (L, D, IN_DIM), D ** -0.5),
        'sinks': nrm(ks[15], (L, N_HEADS), 1.0),
        'w_conv': nrm(ks[16], (L, CONV_WIDTH, LRU_WIDTH), CONV_WIDTH ** -0.5),
        'b_conv': nrm(ks[17], (L, LRU_WIDTH), 0.01),
        'w_lru_a': nrm(ks[18], (L, LRU_BLOCKS, LRU_BLOCK_W, LRU_BLOCK_W), LRU_BLOCK_W ** -0.5),
        'b_lru_a': nrm(ks[19], (L, LRU_WIDTH), 0.01),
        'w_lru_x': nrm(ks[21], (L, LRU_BLOCKS, LRU_BLOCK_W, LRU_BLOCK_W), LRU_BLOCK_W ** -0.5),
        'b_lru_x': nrm(ks[22], (L, LRU_WIDTH), 0.01),
        'lru_lambda': lam,
        'w_attn_branch': nrm(ks[23], (L, Q_DIM, D), Q_DIM ** -0.5),
        'w_lru_branch': nrm(ks[24], (L, LRU_WIDTH, D), LRU_WIDTH ** -0.5),
        'w_out': nrm(ks[25], (L, D, D), D ** -0.5),
        'w_router_group': nrm(ks[26], (L, D, N_GROUPS), D ** -0.5),
        'b_router_group': nrm(ks[27], (L, N_GROUPS), 0.01),
        'w_router_expert': nrm(ks[28], (L, D, N_EXPERTS), D ** -0.5),
        'b_router_expert': nrm(ks[29], (L, N_EXPERTS), 0.01),
        'w_exp_gate': nrm(ks[30], (L, N_EXPERTS, D, D_EXPERT), D ** -0.5),
        'w_exp_up': nrm(ks[31], (L, N_EXPERTS, D, D_EXPERT), D ** -0.5),
        'w_exp_down': nrm(ks[32], (L, N_EXPERTS, D_EXPERT, D), D_EXPERT ** -0.5),
    }


def reference(x_prompt, x_sample, cache_k_win, cache_v_win, state_conv, state_h, c_prompt, c_sample,
              w_ada, b_ada, g_pre_mix, g_post_mix, g_pre_ffn, g_post_ffn, w_in, sinks,
              w_conv, b_conv, w_lru_a, b_lru_a, w_lru_x, b_lru_x, lru_lambda,
              w_attn_branch, w_lru_branch, w_out, w_router_group, b_router_group,
              w_router_expert, b_router_expert, w_exp_gate, w_exp_up, w_exp_down):
    weights = (w_ada, b_ada, g_pre_mix, g_post_mix, g_pre_ffn, g_post_ffn, w_in, sinks,
               w_conv, b_conv, w_lru_a, b_lru_a, w_lru_x, b_lru_x, lru_lambda,
               w_attn_branch, w_lru_branch, w_out, w_router_group, b_router_group,
               w_router_expert, b_router_expert, w_exp_gate, w_exp_up, w_exp_down)
    y_p, y_s = x_prompt, x_sample
    kp, vp, cp, hp, ks_, vs_, cs_, hs_ = [], [], [], [], [], [], [], []
    for layer in range(DEPTH):
        p = {name: w[layer] for name, w in zip(PARAM_NAMES, weights)}
        zero_conv = jnp.zeros((x_prompt.shape[0], CONV_WIDTH - 1, LRU_WIDTH), x_prompt.dtype)
        zero_h = jnp.zeros((x_prompt.shape[0], LRU_WIDTH), jnp.float32)
        y_p, k1, v1, c1, h1 = _layer(y_p, c_prompt, None, None, zero_conv, zero_h, p, False)
        y_s, k2, v2, c2, h2 = _layer(y_s, c_sample, cache_k_win[layer], cache_v_win[layer],
                                     state_conv[layer], state_h[layer], p, True)
        kp.append(k1); vp.append(v1); cp.append(c1); hp.append(h1)
        ks_.append(k2); vs_.append(v2); cs_.append(c2); hs_.append(h2)
    return (y_p, y_s, jnp.stack(kp), jnp.stack(vp), jnp.stack(cp), jnp.stack(hp),
            jnp.stack(ks_), jnp.stack(vs_), jnp.stack(cs_), jnp.stack(hs_))
```

```python
import functools

import jax
import jax.numpy as jnp
from jax import lax
from jax.experimental import pallas as pl
from jax.experimental.pallas import tpu as pltpu

F32 = jnp.float32
BF16 = jnp.bfloat16
I32 = jnp.int32

D_MODEL = 1024
N_HEADS = 16
HEAD_DIM = 64
N_KV_HEADS = 4
GQA_GROUP = 4
WINDOW = 128
ROPE_THETA = 10000.0
NEG_INF = -1e30
LRU_WIDTH = 1024
LRU_BLOCKS = 16
LRU_BLOCK_W = 64
CONV_WIDTH = 4
LRU_C = 8.0
N_GROUPS = 4
EXPERTS_PER_GROUP = 8
N_EXPERTS = 32
D_EXPERT = 512
MOE_BLOCK = 128
NORM_EPS = 1e-6
Q_DIM = N_HEADS * HEAD_DIM
KV_DIM = N_KV_HEADS * HEAD_DIM
IN_DIM = Q_DIM + 2 * KV_DIM + 2 * LRU_WIDTH + 2 * D_MODEL

LANES = 128
SUBLANES = 8
MXU_DIM = 256
VMEM_LIMIT = 56 * 1024 * 1024

ROW_TILE = 512
LRU_TILE = 256
GATHER_TILE = 256
DISPATCH_TILE = 1024
ROUTE_LANES = LANES


def _cparams(n_axes, vmem=VMEM_LIMIT):
    return pltpu.CompilerParams(dimension_semantics=("arbitrary",) * n_axes, vmem_limit_bytes=vmem)


def _rms(x, g):
    ms = jnp.mean(x * x, axis=-1, keepdims=True)
    return x * lax.rsqrt(ms + NORM_EPS) * g


def _sigmoid(x):
    return 1.0 / (1.0 + jnp.exp(-x))


def _dot(a, b):
    return jnp.dot(a, b, preferred_element_type=F32)


def _ada_kernel(c_ref, w_ref, b_ref, o_ref):
    c = c_ref[...]
    s = (c * _sigmoid(c)).astype(BF16)
    o_ref[...] = _dot(s, w_ref[...]) + b_ref[...]


def _ada(c_all, w_ada, b_ada):
    r = c_all.shape[0]
    n = w_ada.shape[1]
    return pl.pallas_call(
        _ada_kernel,
        grid=(n // D_MODEL,),
        in_specs=[pl.BlockSpec((r, D_MODEL), lambda j: (0, 0)),
                  pl.BlockSpec((D_MODEL, D_MODEL), lambda j: (0, j)),
                  pl.BlockSpec((1, D_MODEL), lambda j: (0, j))],
        out_specs=pl.BlockSpec((r, D_MODEL), lambda j: (0, j)),
        out_shape=jax.ShapeDtypeStruct((r, n), F32),
        compiler_params=_cparams(1),
        name="ada_mod",
    )(c_all, w_ada, b_ada)


def _inproj_kernel(x_ref, sc_ref, sh_ref, g_ref, cos_ref, sin_ref, w_ref,
                   q_ref, k_ref, v_ref, xl_ref, yl_ref, sga_ref, sgr_ref):
    x = x_ref[...]
    h = _rms(x, g_ref[...]) * (1.0 + sc_ref[...]) + sh_ref[...]
    hb = h.astype(BF16)
    cos = cos_ref[...]
    sin = sin_ref[...]
    lane = lax.broadcasted_iota(I32, cos.shape, 1)
    first_half = (lane % HEAD_DIM) < (HEAD_DIM // 2)

    def rope(t):
        rot = jnp.where(first_half, pltpu.roll(t, LANES - HEAD_DIM // 2, 1), pltpu.roll(t, HEAD_DIM // 2, 1))
        return t * cos + rot * sin

    o1 = Q_DIM
    o2 = o1 + KV_DIM
    o3 = o2 + KV_DIM
    o4 = o3 + LRU_WIDTH
    o5 = o4 + LRU_WIDTH
    o6 = o5 + D_MODEL
    qf = _dot(hb, w_ref[:, 0:o1])
    for c in range(Q_DIM // LANES):
        q_ref[:, c * LANES:(c + 1) * LANES] = rope(qf[:, c * LANES:(c + 1) * LANES]).astype(BF16)
    kf = _dot(hb, w_ref[:, o1:o2])
    for c in range(KV_DIM // LANES):
        k_ref[:, c * LANES:(c + 1) * LANES] = rope(kf[:, c * LANES:(c + 1) * LANES])
    v_ref[...] = _dot(hb, w_ref[:, o2:o3])
    xl_ref[...] = _dot(hb, w_ref[:, o3:o4]).astype(BF16)
    yl_ref[...] = _dot(hb, w_ref[:, o4:o5]).astype(BF16)
    sga_ref[...] = _sigmoid(_dot(hb, w_ref[:, o5:o6])).astype(BF16)
    sgr_ref[...] = _sigmoid(_dot(hb, w_ref[:, o6:IN_DIM])).astype(BF16)


def _inproj(x, sc, sh, g, cos, sin, w_in, rows_per_mod, pos_tiles):
    n = x.shape[0]
    tm = min(ROW_TILE, n)
    if rows_per_mod:
        tiles_per_mod = rows_per_mod // tm
        mod_spec = pl.BlockSpec((None, 1, D_MODEL), lambda i: (i // tiles_per_mod, 0, 0))
    else:
        mod_spec = pl.BlockSpec((tm, D_MODEL), lambda i: (i, 0))
    row = lambda w: pl.BlockSpec((tm, w), lambda i: (i, 0))
    outs = [(Q_DIM, BF16), (KV_DIM, F32), (KV_DIM, F32), (LRU_WIDTH, BF16), (LRU_WIDTH, BF16),
            (D_MODEL, BF16), (D_MODEL, BF16)]
    return pl.pallas_call(
        _inproj_kernel,
        grid=(n // tm,),
        in_specs=[row(D_MODEL), mod_spec, mod_spec,
                  pl.BlockSpec((1, D_MODEL), lambda i: (0, 0)),
                  pl.BlockSpec((tm, LANES), lambda i: (i % pos_tiles, 0)),
                  pl.BlockSpec((tm, LANES), lambda i: (i % pos_tiles, 0)),
                  pl.BlockSpec((D_MODEL, IN_DIM), lambda i: (0, 0))],
        out_specs=[row(w) for w, _ in outs],
        out_shape=[jax.ShapeDtypeStruct((n, w), dt) for w, dt in outs],
        compiler_params=_cparams(1),
        name="in_proj",
    )(x, sc, sh, g, cos, sin, w_in)


def _head_masks(shape):
    lane = lax.broadcasted_iota(I32, shape, 1)
    return [(lane // HEAD_DIM) == h for h in range(N_KV_HEADS)]


def _attention_core(q_perm, kall, vall, valid, sink_of, rows):
    masks_b = _head_masks((rows, KV_DIM))
    zero_b = jnp.zeros((rows, KV_DIM), BF16)
    pieces = []
    for h in range(N_KV_HEADS):
        for g in range(GQA_GROUP):
            pieces.append(jnp.where(masks_b[h], q_perm[g], zero_b))
    q_big = jnp.concatenate(pieces, axis=0)
    s_all = lax.dot_general(q_big, kall, (((1,), (1,)), ((), ())), preferred_element_type=F32)
    s_all = s_all * (HEAD_DIM ** -0.5)
    p_chunks, inv_chunks = [], []
    for h in range(N_KV_HEADS):
        for g in range(GQA_GROUP):
            c = h * GQA_GROUP + g
            s = jnp.where(valid, s_all[c * rows:(c + 1) * rows], NEG_INF)
            sink = sink_of(h, g)
            m = jnp.maximum(jnp.max(s, axis=-1, keepdims=True), sink)
            p = jnp.exp(s - m)
            denom = jnp.sum(p, axis=-1, keepdims=True) + jnp.exp(sink - m)
            p_chunks.append(p.astype(BF16))
            inv_chunks.append(1.0 / denom)
    o_all = _dot(jnp.concatenate(p_chunks, axis=0), vall)
    outs = []
    for g in range(GQA_GROUP):
        acc = jnp.zeros((rows, KV_DIM), F32)
        for h in range(N_KV_HEADS):
            c = h * GQA_GROUP + g
            acc = acc + jnp.where(masks_b[h], o_all[c * rows:(c + 1) * rows] * inv_chunks[c], 0.0)
        outs.append(acc)
    return outs


def _attn_prompt_kernel(sink_ref, q_ref, kc_ref, kp_ref, vc_ref, vp_ref, o_ref):
    j = pl.program_id(1)
    kall = jnp.concatenate([kp_ref[...], kc_ref[...]], axis=0).astype(BF16)
    vall = jnp.concatenate([vp_ref[...], vc_ref[...]], axis=0).astype(BF16)
    qi = lax.broadcasted_iota(I32, (WINDOW, 2 * WINDOW), 0)
    kj = lax.broadcasted_iota(I32, (WINDOW, 2 * WINDOW), 1)
    dist = qi + WINDOW - kj
    valid = (dist >= 0) & (dist <= WINDOW) & ((kj >= WINDOW) | (j > 0))
    q_perm = [q_ref[:, g * KV_DIM:(g + 1) * KV_DIM] for g in range(GQA_GROUP)]
    outs = _attention_core(q_perm, kall, vall, valid, lambda h, g: sink_ref[h * GQA_GROUP + g], WINDOW)
    for g in range(GQA_GROUP):
        o_ref[:, g * KV_DIM:(g + 1) * KV_DIM] = outs[g].astype(BF16)


def _attn_prompt(sinks, q, k, v, batch, seq):
    nb = seq // WINDOW
    cur = lambda w: pl.BlockSpec((WINDOW, w), lambda b, j: (b * nb + j, 0))
    prev = lambda w: pl.BlockSpec((WINDOW, w), lambda b, j: (b * nb + jnp.maximum(j - 1, 0), 0))
    return pl.pallas_call(
        _attn_prompt_kernel,
        grid=(batch, nb),
        in_specs=[pl.BlockSpec(memory_space=pltpu.SMEM),
                  cur(Q_DIM), cur(KV_DIM), prev(KV_DIM), cur(KV_DIM), prev(KV_DIM)],
        out_specs=cur(Q_DIM),
        out_shape=jax.ShapeDtypeStruct((batch * seq, Q_DIM), BF16),
        compiler_params=_cparams(2),
        name="attn_prompt",
    )(sinks, q, k, k, v, v)


SEQ_PER_STEP = 8


def _attn_sample_kernel(sink_ref, q_ref, kn_ref, vn_ref, kc_ref, vc_ref, o_ref, kbuf, vbuf, *, t_new):
    rows = GQA_GROUP * t_new
    kbuf[WINDOW:2 * WINDOW, :] = jnp.zeros((WINDOW, KV_DIM), F32)
    vbuf[WINDOW:2 * WINDOW, :] = jnp.zeros((WINDOW, KV_DIM), F32)
    ri = lax.broadcasted_iota(I32, (rows, 2 * WINDOW), 0)
    kj = lax.broadcasted_iota(I32, (rows, 2 * WINDOW), 1)
    tq = ri // GQA_GROUP
    valid = (kj >= tq) & (kj <= tq + WINDOW) & (kj < WINDOW + t_new)
    for s in range(SEQ_PER_STEP):
        kbuf[0:WINDOW, :] = kc_ref[s]
        vbuf[0:WINDOW, :] = vc_ref[s]
        kbuf[WINDOW:WINDOW + t_new, :] = kn_ref[s]
        vbuf[WINDOW:WINDOW + t_new, :] = vn_ref[s]
        kall = kbuf[...].astype(BF16)
        vall = vbuf[...].astype(BF16)
        qs = q_ref[s]
        masks_b = _head_masks((rows, KV_DIM))
        zero_b = jnp.zeros((rows, KV_DIM), BF16)
        q_big = jnp.concatenate([jnp.where(masks_b[h], qs, zero_b) for h in range(N_KV_HEADS)], axis=0)
        s_all = lax.dot_general(q_big, kall, (((1,), (1,)), ((), ())), preferred_element_type=F32)
        s_all = s_all * (HEAD_DIM ** -0.5)
        acc = jnp.zeros((rows, KV_DIM), F32)
        p_chunks, inv_chunks = [], []
        for h in range(N_KV_HEADS):
            sc = jnp.where(valid, s_all[h * rows:(h + 1) * rows], NEG_INF)
            sink = sink_ref[h]
            m = jnp.maximum(jnp.max(sc, axis=-1, keepdims=True), sink)
            p = jnp.exp(sc - m)
            denom = jnp.sum(p, axis=-1, keepdims=True) + jnp.exp(sink - m)
            p_chunks.append(p.astype(BF16))
            inv_chunks.append(1.0 / denom)
        o_all = _dot(jnp.concatenate(p_chunks, axis=0), vall)
        for h in range(N_KV_HEADS):
            acc = acc + jnp.where(masks_b[h], o_all[h * rows:(h + 1) * rows] * inv_chunks[h], 0.0)
        o_ref[s] = acc.astype(BF16)


def _attn_sample(sink_rows, q, kn, vn, kc, vc):
    nseq, rows, _ = q.shape
    t_new = kn.shape[1]
    sb = SEQ_PER_STEP
    blk = lambda r: pl.BlockSpec((sb, r, KV_DIM), lambda i: (i, 0, 0))
    return pl.pallas_call(
        functools.partial(_attn_sample_kernel, t_new=t_new),
        grid=(nseq // sb,),
        in_specs=[pl.BlockSpec((N_KV_HEADS, rows, 1), lambda i: (0, 0, 0)),
                  blk(rows), blk(t_new), blk(t_new), blk(WINDOW), blk(WINDOW)],
        out_specs=blk(rows),
        out_shape=jax.ShapeDtypeStruct((nseq, rows, KV_DIM), BF16),
        scratch_shapes=[pltpu.VMEM((2 * WINDOW, KV_DIM), F32), pltpu.VMEM((2 * WINDOW, KV_DIM), F32)],
        compiler_params=_cparams(1),
        name="attn_sample",
    )(sink_rows, q, kn, vn, kc, vc)


def _gelu_tanh(x):
    return 0.5 * x * (1.0 + jnp.tanh(0.7978845608028654 * (x + 0.044715 * x * x * x)))


def _lru_gates(xc, wbd_ref, ba, bx, lam):
    xcb = xc.astype(BF16)
    r_parts, i_parts = [], []
    for gidx in range(LRU_WIDTH // MXU_DIM):
        z = _dot(xcb[:, gidx * MXU_DIM:(gidx + 1) * MXU_DIM], wbd_ref[gidx])
        r_parts.append(z[:, :MXU_DIM])
        i_parts.append(z[:, MXU_DIM:])
    r = _sigmoid(jnp.concatenate(r_parts, axis=1) + ba)
    i = _sigmoid(jnp.concatenate(i_parts, axis=1) + bx)
    softplus_neg_lam = jnp.maximum(-lam, 0.0) + jnp.log1p(jnp.exp(-jnp.abs(lam)))
    log_a = -LRU_C * r * softplus_neg_lam
    a = jnp.exp(log_a)
    u = jnp.sqrt(1.0 - jnp.exp(2.0 * log_a)) * (i * xc)
    return a, u


def _lru_prompt_kernel(xl_ref, yl_ref, wc_ref, bc_ref, wbd_ref, ba_ref, bx_ref, lam_ref,
                       o_ref, hlast_ref, xbuf, hcar):
    j = pl.program_id(1)
    t = xl_ref.shape[0]
    w = LRU_WIDTH

    @pl.when(j == 0)
    def _():
        xbuf[0:SUBLANES, :] = jnp.zeros((SUBLANES, w), F32)
        hcar[...] = jnp.zeros((1, w), F32)

    x = xl_ref[...].astype(F32)
    xbuf[SUBLANES:SUBLANES + t, :] = x
    xc = x * wc_ref[CONV_WIDTH - 1:CONV_WIDTH, :] + bc_ref[...]
    for k in range(1, CONV_WIDTH):
        xc = xc + xbuf[SUBLANES - k:SUBLANES - k + t, :] * wc_ref[CONV_WIDTH - 1 - k:CONV_WIDTH - k, :]
    xbuf[0:SUBLANES, :] = x[t - SUBLANES:t, :]

    a, u = _lru_gates(xc, wbd_ref, ba_ref[...], bx_ref[...], lam_ref[...])

    ng = t // SUBLANES
    a3 = a.reshape(ng, SUBLANES, w)
    u3 = u.reshape(ng, SUBLANES, w)
    row = lax.broadcasted_iota(I32, (ng, SUBLANES, w), 1)
    d = 1
    while d < SUBLANES:
        a_s = jnp.where(row >= d, pltpu.roll(a3, d, 1), 1.0)
        u_s = jnp.where(row >= d, pltpu.roll(u3, d, 1), 0.0)
        u3 = a3 * u_s + u3
        a3 = a3 * a_s
        d *= 2
    carry = hcar[...]
    hs = []
    for gi in range(ng):
        hg = a3[gi] * carry + u3[gi]
        hs.append(hg)
        carry = hg[SUBLANES - 1:SUBLANES, :]
    hcar[...] = carry
    h = jnp.concatenate(hs, axis=0)
    o_ref[...] = (h * _gelu_tanh(yl_ref[...].astype(F32))).astype(BF16)
    hlast_ref[...] = carry


def _lru_prompt(xl, yl, wc, bc, wbd, ba, bx, lam, batch, seq):
    t = min(LRU_TILE, seq)
    nt = seq // t
    tile = pl.BlockSpec((t, LRU_WIDTH), lambda b, j: (b * nt + j, 0))
    full = lambda shp: pl.BlockSpec(shp, lambda b, j: (0,) * len(shp))
    return pl.pallas_call(
        _lru_prompt_kernel,
        grid=(batch, nt),
        in_specs=[tile, tile, full((CONV_WIDTH, LRU_WIDTH)), full((1, LRU_WIDTH)),
                  full(wbd.shape), full((1, LRU_WIDTH)), full((1, LRU_WIDTH)), full((1, LRU_WIDTH))],
        out_specs=[tile, pl.BlockSpec((None, 1, LRU_WIDTH), lambda b, j: (b, 0, 0))],
        out_shape=[jax.ShapeDtypeStruct((batch * seq, LRU_WIDTH), BF16),
                   jax.ShapeDtypeStruct((batch, 1, LRU_WIDTH), F32)],
        scratch_shapes=[pltpu.VMEM((SUBLANES + t, LRU_WIDTH), F32), pltpu.VMEM((1, LRU_WIDTH), F32)],
        compiler_params=_cparams(2),
        name="lru_prompt",
    )(xl, yl, wc, bc, wbd, ba, bx, lam)


def _lru_sample_kernel(xl_ref, yl_ref, cs_ref, h0_ref, wc_ref, bc_ref, wbd_ref, ba_ref, bx_ref, lam_ref,
                       o_ref, hlast_ref):
    t_new, nseq, w = xl_ref.shape
    xp = [cs_ref[k] for k in range(CONV_WIDTH - 1)] + [xl_ref[k].astype(F32) for k in range(t_new)]
    xcs = []
    for t in range(t_new):
        acc = bc_ref[...] + xp[t] * wc_ref[0:1, :]
        for k in range(1, CONV_WIDTH):
            acc = acc + xp[t + k] * wc_ref[k:k + 1, :]
        xcs.append(acc)
    xc = jnp.concatenate(xcs, axis=0)
    a, u = _lru_gates(xc, wbd_ref, ba_ref[...], bx_ref[...], lam_ref[...])
    h = h0_ref[...]
    for t in range(t_new):
        h = a[t * nseq:(t + 1) * nseq] * h + u[t * nseq:(t + 1) * nseq]
        o_ref[t] = (h * _gelu_tanh(yl_ref[t].astype(F32))).astype(BF16)
    hlast_ref[...] = h


def _lru_sample(xl, yl, cs, h0, wc, bc, wbd, ba, bx, lam):
    t_new, nseq, w = xl.shape
    full = lambda shp: pl.BlockSpec(shp, lambda i: (0,) * len(shp))
    args = (xl, yl, cs, h0, wc, bc, wbd, ba, bx, lam)
    return pl.pallas_call(
        _lru_sample_kernel,
        grid=(1,),
        in_specs=[full(a.shape) for a in args],
        out_specs=[full((t_new, nseq, w)), full((nseq, w))],
        out_shape=[jax.ShapeDtypeStruct((t_new, nseq, w), BF16), jax.ShapeDtypeStruct((nseq, w), F32)],
        compiler_params=_cparams(1),
        name="lru_sample",
    )(*args)


def _post_kernel(oa_ref, ol_ref, sga_ref, sgr_ref, x_ref, gta_ref, scf_ref, shf_ref,
                 wab_ref, wlb_ref, wout_ref, gpm_ref, gpf_ref, wrh_ref, wrl_ref, br_ref, tri_ref, cin_ref,
                 x1_ref, h2_ref, route_ref, cnt_ref, carry):
    i = pl.program_id(0)

    @pl.when(i == 0)
    def _():
        carry[...] = cin_ref[...]

    b_attn = _dot(oa_ref[...], wab_ref[...])
    b_lru = _dot(ol_ref[...], wlb_ref[...])
    merged = sga_ref[...].astype(F32) * b_attn + sgr_ref[...].astype(F32) * b_lru
    mix = _dot(merged.astype(BF16), wout_ref[...])
    x1 = x_ref[...] + gta_ref[...] * _rms(mix, gpm_ref[...])
    x1_ref[...] = x1
    h2 = _rms(x1, gpf_ref[...]) * (1.0 + scf_ref[...]) + shf_ref[...]
    h2_ref[...] = h2

    h_hi = h2.astype(BF16)
    h_lo = (h2 - h_hi.astype(F32)).astype(BF16)
    logits = _dot(h_hi, wrh_ref[...]) + (_dot(h_lo, wrh_ref[...]) + _dot(h_hi, wrl_ref[...])) + br_ref[...]

    tm = logits.shape[0]
    lane = lax.broadcasted_iota(I32, (tm, ROUTE_LANES), 1)
    big = jnp.int32(ROUTE_LANES)
    is_g = lane < N_GROUPS
    lg = jnp.where(is_g, logits, NEG_INF)
    mg = jnp.max(lg, axis=-1, keepdims=True)
    g_star = jnp.min(jnp.where(lg == mg, lane, big), axis=-1, keepdims=True)
    p_star = 1.0 / jnp.sum(jnp.where(is_g, jnp.exp(lg - mg), 0.0), axis=-1, keepdims=True)
    lo = N_GROUPS + g_star * EXPERTS_PER_GROUP
    in_grp = (lane >= lo) & (lane < lo + EXPERTS_PER_GROUP)
    le = jnp.where(in_grp, logits, NEG_INF)
    m1 = jnp.max(le, axis=-1, keepdims=True)
    i1 = jnp.min(jnp.where(le == m1, lane, big), axis=-1, keepdims=True)
    le2 = jnp.where(lane == i1, NEG_INF, le)
    m2 = jnp.max(le2, axis=-1, keepdims=True)
    i2 = jnp.min(jnp.where(le2 == m2, lane, big), axis=-1, keepdims=True)
    e2x = jnp.exp(m2 - m1)
    wsum = 1.0 + e2x
    w1 = (1.0 / wsum) * p_star
    w2 = (e2x / wsum) * p_star

    oh1 = lane == i1
    oh2 = lane == i2
    cnt = jnp.where(oh1 | oh2, 1.0, 0.0)
    excl = _dot(tri_ref[...], cnt.astype(BF16)) + carry[...]
    r1 = jnp.sum(jnp.where(oh1, excl, 0.0), axis=-1, keepdims=True)
    r2 = jnp.sum(jnp.where(oh2, excl, 0.0), axis=-1, keepdims=True)
    carry[...] = carry[...] + jnp.sum(cnt, axis=0, keepdims=True)
    cnt_ref[...] = carry[...]

    e1f = (i1 - N_GROUPS).astype(F32)
    e2f = (i2 - N_GROUPS).astype(F32)
    rec = jnp.where(lane == 0, e1f, 0.0)
    rec = jnp.where(lane == 1, e2f, rec)
    rec = jnp.where(lane == 2, w1, rec)
    rec = jnp.where(lane == 3, w2, rec)
    rec = jnp.where(lane == 4, r1, rec)
    rec = jnp.where(lane == 5, r2, rec)
    route_ref[...] = rec


def _post(oa, ol, sga, sgr, x, gta, scf, shf, wab, wlb, wout, gpm, gpf, wrh, wrl, br, tri, cin, rows_per_mod):
    n = x.shape[0]
    tm = min(ROW_TILE, n)
    if rows_per_mod:
        tiles_per_mod = rows_per_mod // tm
        mod_spec = pl.BlockSpec((None, 1, D_MODEL), lambda i: (i // tiles_per_mod, 0, 0))
    else:
        mod_spec = pl.BlockSpec((tm, D_MODEL), lambda i: (i, 0))
    row = lambda w: pl.BlockSpec((tm, w), lambda i: (i, 0))
    full = lambda a: pl.BlockSpec(a.shape, lambda i: (0,) * a.ndim)
    return pl.pallas_call(
        _post_kernel,
        grid=(n // tm,),
        in_specs=[row(Q_DIM), row(LRU_WIDTH), row(D_MODEL), row(D_MODEL), row(D_MODEL),
                  mod_spec, mod_spec, mod_spec,
                  full(wab), full(wlb), full(wout), full(gpm), full(gpf), full(wrh), full(wrl), full(br),
                  full(tri), full(cin)],
        out_specs=[row(D_MODEL), row(D_MODEL), row(ROUTE_LANES), pl.BlockSpec((1, ROUTE_LANES), lambda i: (0, 0))],
        out_shape=[jax.ShapeDtypeStruct((n, D_MODEL), F32), jax.ShapeDtypeStruct((n, D_MODEL), F32),
                   jax.ShapeDtypeStruct((n, ROUTE_LANES), F32), jax.ShapeDtypeStruct((1, ROUTE_LANES), F32)],
        scratch_shapes=[pltpu.VMEM((1, ROUTE_LANES), F32)],
        compiler_params=_cparams(1),
        name="post_mix",
    )(oa, ol, sga, sgr, x, gta, scf, shf, wab, wlb, wout, gpm, gpf, wrh, wrl, br, tri, cin)


def _padded(c):
    return ((c + (MOE_BLOCK - 1)) // MOE_BLOCK) * MOE_BLOCK


def _plan_kernel(cnt_ref, be_ref, meta_ref, *, n_blocks):
    def fill(j, _):
        be_ref[j] = N_EXPERTS - 1
        return 0
    lax.fori_loop(0, n_blocks, fill, 0)

    def per_expert(e, nb):
        k = _padded(cnt_ref[e]) // MOE_BLOCK

        def put(b, _):
            be_ref[nb + b] = e
            return 0
        lax.fori_loop(0, k, put, 0)
        return nb + k
    n_active = lax.fori_loop(0, N_EXPERTS, per_expert, 0)
    meta_ref[0] = n_active


def _plan(counts, n_blocks):
    return pl.pallas_call(
        functools.partial(_plan_kernel, n_blocks=n_blocks),
        in_specs=[pl.BlockSpec(memory_space=pltpu.SMEM)],
        out_specs=[pl.BlockSpec(memory_space=pltpu.SMEM), pl.BlockSpec(memory_space=pltpu.SMEM)],
        out_shape=[jax.ShapeDtypeStruct((n_blocks,), I32), jax.ShapeDtypeStruct((1,), I32)],
        name="moe_plan",
    )(counts)


def _expert_starts(cnt_ref, pstart):
    def body(e, acc):
        pstart[e] = acc
        return acc + _padded(cnt_ref[e])
    lax.fori_loop(0, N_EXPERTS, body, 0)


def _row_copy(src, s, dst, d, sem):
    return pltpu.make_async_copy(src.at[pl.ds(s, 1)], dst.at[pl.ds(d, 1)], sem)


def _dispatch_kernel(cnt_ref, e1_ref, e2_ref, r1_ref, r2_ref, h2_hbm, zero_hbm, xs_in, xs_out, pstart, sem,
                     *, tm, fill_padding):
    del xs_in
    i = pl.program_id(0)
    _expert_starts(cnt_ref, pstart)
    base = i * tm

    def issue(r, _):
        _row_copy(h2_hbm, base + r, xs_out, pstart[e1_ref[r]] + r1_ref[r], sem).start()
        _row_copy(h2_hbm, base + r, xs_out, pstart[e2_ref[r]] + r2_ref[r], sem).start()
        return 0
    lax.fori_loop(0, tm, issue, 0)

    def drain(r, _):
        _row_copy(h2_hbm, 0, xs_out, 0, sem).wait()
        return 0
    lax.fori_loop(0, 2 * tm, drain, 0)

    if fill_padding:
        @pl.when(i == 0)
        def _():
            def per_expert(e, tot):
                c = cnt_ref[e]
                p = _padded(c)

                def put(s, _):
                    _row_copy(zero_hbm, 0, xs_out, pstart[e] + s, sem).start()
                    return 0
                lax.fori_loop(c, p, put, 0)
                return tot + (p - c)
            total = lax.fori_loop(0, N_EXPERTS, per_expert, 0)
            lax.fori_loop(0, total, drain, 0)


def _dispatch(counts, e1, e2, r1, r2, h2, zero_row, xs, n_slots, fill_padding):
    n = h2.shape[0]
    tm = min(DISPATCH_TILE, n)
    sm = pl.BlockSpec((tm,), lambda i, c: (i,), memory_space=pltpu.SMEM)
    anyspec = pl.BlockSpec(memory_space=pl.ANY)
    if xs is None:
        xs = jnp.zeros((SUBLANES, D_MODEL), F32)
        aliases = {}
    else:
        aliases = {7: 0}
    return pl.pallas_call(
        functools.partial(_dispatch_kernel, tm=tm, fill_padding=fill_padding),
        grid_spec=pltpu.PrefetchScalarGridSpec(
            num_scalar_prefetch=1,
            grid=(n // tm,),
            in_specs=[sm, sm, sm, sm, anyspec, anyspec, anyspec],
            out_specs=anyspec,
            scratch_shapes=[pltpu.SMEM((N_EXPERTS,), I32), pltpu.SemaphoreType.DMA(())]),
        out_shape=jax.ShapeDtypeStruct((n_slots, D_MODEL), F32),
        input_output_aliases=aliases,
        compiler_params=_cparams(1),
        name="moe_dispatch",
    )(counts, e1, e2, r1, r2, h2, zero_row, xs)


def _expert_kernel(be_ref, meta_ref, xs_ref, wg_ref, wu_ref, wd_ref, ys_ref):
    j = pl.program_id(0)

    @pl.when(j < meta_ref[0])
    def _():
        xb = xs_ref[...].astype(BF16)
        g = _dot(xb, wg_ref[...])
        u = _dot(xb, wu_ref[...])
        hmid = (g * _sigmoid(g) * u).astype(BF16)
        ys_ref[...] = _dot(hmid, wd_ref[...])


def _experts(block_e, meta, xs, wg, wu, wd):
    n_blocks = xs.shape[0] // MOE_BLOCK
    act = lambda j, be, meta: jnp.minimum(j, meta[0] - 1)
    return pl.pallas_call(
        _expert_kernel,
        grid_spec=pltpu.PrefetchScalarGridSpec(
            num_scalar_prefetch=2,
            grid=(n_blocks,),
            in_specs=[pl.BlockSpec((MOE_BLOCK, D_MODEL), lambda j, be, meta: (act(j, be, meta), 0)),
                      pl.BlockSpec((None, D_MODEL, D_EXPERT), lambda j, be, meta: (be[act(j, be, meta)], 0, 0)),
                      pl.BlockSpec((None, D_MODEL, D_EXPERT), lambda j, be, meta: (be[act(j, be, meta)], 0, 0)),
                      pl.BlockSpec((None, D_EXPERT, D_MODEL), lambda j, be, meta: (be[act(j, be, meta)], 0, 0))],
            out_specs=pl.BlockSpec((MOE_BLOCK, D_MODEL), lambda j, be, meta: (act(j, be, meta), 0))),
        out_shape=jax.ShapeDtypeStruct(xs.shape, F32),
        compiler_params=_cparams(1),
        name="moe_experts",
    )(block_e, meta, xs, wg, wu, wd)


def _combine_kernel(cnt_ref, e1_ref, e2_ref, r1_ref, r2_ref, route_ref, x1_ref, gtf_ref, g_ref, ys_hbm,
                    y_ref, buf1, buf2, pstart, sem, *, tm):
    _expert_starts(cnt_ref, pstart)

    def issue(r, _):
        _row_copy(ys_hbm, pstart[e1_ref[r]] + r1_ref[r], buf1, r, sem).start()
        _row_copy(ys_hbm, pstart[e2_ref[r]] + r2_ref[r], buf2, r, sem).start()
        return 0
    lax.fori_loop(0, tm, issue, 0)

    def drain(r, _):
        _row_copy(ys_hbm, 0, buf1, 0, sem).wait()
        return 0
    lax.fori_loop(0, 2 * tm, drain, 0)

    route = route_ref[...]
    f = buf1[...] * route[:, 2:3] + buf2[...] * route[:, 3:4]
    y_ref[...] = x1_ref[...] + gtf_ref[...] * _rms(f, g_ref[...])


def _combine(counts, e1, e2, r1, r2, route, x1, gtf, g, ys, rows_per_mod):
    n = x1.shape[0]
    tm = min(GATHER_TILE, n)
    if rows_per_mod:
        tiles_per_mod = rows_per_mod // tm
        mod_spec = pl.BlockSpec((None, 1, D_MODEL), lambda i, c: (i // tiles_per_mod, 0, 0))
    else:
        mod_spec = pl.BlockSpec((tm, D_MODEL), lambda i, c: (i, 0))
    sm = pl.BlockSpec((tm,), lambda i, c: (i,), memory_space=pltpu.SMEM)
    row = lambda w: pl.BlockSpec((tm, w), lambda i, c: (i, 0))
    return pl.pallas_call(
        functools.partial(_combine_kernel, tm=tm),
        grid_spec=pltpu.PrefetchScalarGridSpec(
            num_scalar_prefetch=1,
            grid=(n // tm,),
            in_specs=[sm, sm, sm, sm, row(ROUTE_LANES), row(D_MODEL), mod_spec,
                      pl.BlockSpec((1, D_MODEL), lambda i, c: (0, 0)),
                      pl.BlockSpec(memory_space=pl.ANY)],
            out_specs=row(D_MODEL),
            scratch_shapes=[pltpu.VMEM((tm, D_MODEL), F32), pltpu.VMEM((tm, D_MODEL), F32),
                            pltpu.SMEM((N_EXPERTS,), I32), pltpu.SemaphoreType.DMA(())]),
        out_shape=jax.ShapeDtypeStruct((n, D_MODEL), F32),
        compiler_params=_cparams(1),
        name="moe_combine",
    )(counts, e1, e2, r1, r2, route, x1, gtf, g, ys)


def _rope_tables(pos):
    half = HEAD_DIM // 2
    inv = jnp.power(jnp.float32(ROPE_THETA), -jnp.arange(half, dtype=F32) / half)
    ang = pos.astype(F32)[:, None] * inv[None, :]
    cos = jnp.cos(ang)
    sin = jnp.sin(ang)
    reps = LANES // HEAD_DIM
    cos_t = jnp.tile(jnp.concatenate([cos, cos], axis=-1), (1, reps))
    sin_t = jnp.tile(jnp.concatenate([-sin, sin], axis=-1), (1, reps))
    return cos_t, sin_t


def _q_perm_index():
    g = jnp.arange(GQA_GROUP)[:, None, None]
    h = jnp.arange(N_KV_HEADS)[None, :, None]
    d = jnp.arange(HEAD_DIM)[None, None, :]
    return ((h * GQA_GROUP + g) * HEAD_DIM + d).reshape(-1)


def _block_diag_gates(w_a, w_x):
    per = MXU_DIM // LRU_BLOCK_W
    groups = LRU_BLOCKS // per

    def bd(w):
        w = w.reshape(groups, per, LRU_BLOCK_W, LRU_BLOCK_W)
        eye = jnp.eye(per, dtype=w.dtype)
        full = jnp.einsum('gpij,pq->gpiqj', w, eye)
        return full.reshape(groups, MXU_DIM, MXU_DIM)
    return jnp.concatenate([bd(w_a), bd(w_x)], axis=-1).astype(BF16)


def _layer_forward(xp, xs_tm, ck, cv, cconv, ch, mod_p, mod_s, p):
    batch, seq, _ = xp.shape
    nseq, _, _, _ = ck.shape
    t_new = xs_tm.shape[0] // nseq
    n_p = batch * seq
    n_s = xs_tm.shape[0]

    perm = _q_perm_index()
    w_in = p['w_in']
    w_in_b = jnp.concatenate([w_in[:, :Q_DIM][:, perm], w_in[:, Q_DIM:]], axis=1).astype(BF16)
    wab = p['w_attn_branch'][perm, :].astype(BF16)
    wlb = p['w_lru_branch'].astype(BF16)
    wout = p['w_out'].astype(BF16)
    wbd = _block_diag_gates(p['w_lru_a'], p['w_lru_x'])
    row = lambda v: v.reshape(1, -1)
    wr = jnp.concatenate([p['w_router_group'], p['w_router_expert'],
                          jnp.zeros((D_MODEL, ROUTE_LANES - N_GROUPS - N_EXPERTS), F32)], axis=1)
    wr_hi = wr.astype(BF16)
    wr_lo = (wr - wr_hi.astype(F32)).astype(BF16)
    br = jnp.concatenate([p['b_router_group'], p['b_router_expert'],
                          jnp.zeros((ROUTE_LANES - N_GROUPS - N_EXPERTS,), F32)]).reshape(1, -1)
    wg = p['w_exp_gate'].astype(BF16)
    wu = p['w_exp_up'].astype(BF16)
    wd = p['w_exp_down'].astype(BF16)

    def mods(mod):
        return [mod[:, k * D_MODEL:(k + 1) * D_MODEL] for k in range(6)]
    sh_a_p, sc_a_p, gt_a_p, sh_f_p, sc_f_p, gt_f_p = [m.reshape(batch, 1, D_MODEL) for m in mods(mod_p)]
    sh_a_s, sc_a_s, gt_a_s, sh_f_s, sc_f_s, gt_f_s = [jnp.tile(m, (t_new, 1)) for m in mods(mod_s)]

    tm_p = min(ROW_TILE, n_p)
    cos_p, sin_p = _rope_tables(jnp.arange(seq, dtype=I32))
    q_p, k_p, v_p, xl_p, yl_p, sga_p, sgr_p = _inproj(
        xp.reshape(n_p, D_MODEL), sc_a_p, sh_a_p, row(p['g_pre_mix']), cos_p, sin_p, w_in_b,
        rows_per_mod=seq, pos_tiles=seq // tm_p)
    pos_s = jnp.repeat(PAST_LEN_ + jnp.arange(t_new, dtype=I32), nseq)
    cos_s, sin_s = _rope_tables(pos_s)
    q_s, k_s, v_s, xl_s, yl_s, sga_s, sgr_s = _inproj(
        xs_tm, sc_a_s, sh_a_s, row(p['g_pre_mix']), cos_s, sin_s, w_in_b, rows_per_mod=0,
        pos_tiles=n_s // min(ROW_TILE, n_s))

    sinks_perm = p['sinks']
    oa_p = _attn_prompt(sinks_perm, q_p, k_p, v_p, batch, seq)
    rows = t_new * GQA_GROUP
    q_s3 = q_s.reshape(t_new, nseq, GQA_GROUP, KV_DIM).transpose(1, 0, 2, 3).reshape(nseq, rows, KV_DIM)
    kn = k_s.reshape(t_new, nseq, KV_DIM).transpose(1, 0, 2)
    vn = v_s.reshape(t_new, nseq, KV_DIM).transpose(1, 0, 2)
    kc = ck.reshape(nseq, WINDOW, KV_DIM)
    vc = cv.reshape(nseq, WINDOW, KV_DIM)
    sink_rows = jnp.tile(p['sinks'].reshape(N_KV_HEADS, 1, GQA_GROUP), (1, t_new, 1)).reshape(N_KV_HEADS, rows, 1)
    oa_s3 = _attn_sample(sink_rows, q_s3, kn, vn, kc, vc)
    oa_s = oa_s3.reshape(nseq, t_new, Q_DIM).transpose(1, 0, 2).reshape(n_s, Q_DIM)

    lru_w = (p['w_conv'], row(p['b_conv']), wbd, row(p['b_lru_a']), row(p['b_lru_x']), row(p['lru_lambda']))
    ol_p, hlast_p = _lru_prompt(xl_p, yl_p, *lru_w, batch, seq)
    ol_s3, hlast_s = _lru_sample(xl_s.reshape(t_new, nseq, LRU_WIDTH), yl_s.reshape(t_new, nseq, LRU_WIDTH),
                                 cconv.transpose(1, 0, 2), ch, *lru_w)
    ol_s = ol_s3.reshape(n_s, LRU_WIDTH)

    tm_post = min(ROW_TILE, n_p)
    tri = jnp.tril(jnp.ones((tm_post, tm_post), F32), -1).astype(BF16)
    post_w = (wab, wlb, wout, row(p['g_post_mix']), row(p['g_pre_ffn']), wr_hi, wr_lo, br)
    zero_cnt = jnp.zeros((1, ROUTE_LANES), F32)
    x1_p, h2_p, route_p, cnt_p = _post(oa_p, ol_p, sga_p, sgr_p, xp.reshape(n_p, D_MODEL),
                                       gt_a_p, sc_f_p, sh_f_p, *post_w, tri, zero_cnt, rows_per_mod=seq)
    tm_s = min(ROW_TILE, n_s)
    tri_s = tri if tm_s == tm_post else jnp.tril(jnp.ones((tm_s, tm_s), F32), -1).astype(BF16)
    x1_s, h2_s, route_s, cnt_all = _post(oa_s, ol_s, sga_s, sgr_s, xs_tm,
                                         gt_a_s, sc_f_s, sh_f_s, *post_w, tri_s, cnt_p, rows_per_mod=0)

    counts = cnt_all[0, N_GROUPS:N_GROUPS + N_EXPERTS].astype(I32)
    n_assign = (n_p + n_s) * 2
    n_blocks = -(-n_assign // MOE_BLOCK) + N_EXPERTS
    n_slots = n_blocks * MOE_BLOCK

    def route_ints(route):
        return [route[:, k].astype(I32) for k in (0, 1, 4, 5)]
    ri_p = route_ints(route_p)
    ri_s = route_ints(route_s)
    zero_row = jnp.zeros((SUBLANES, D_MODEL), F32)
    xs = _dispatch(counts, *ri_p, h2_p, zero_row, None, n_slots, fill_padding=True)
    xs = _dispatch(counts, *ri_s, h2_s, zero_row, xs, n_slots, fill_padding=False)
    block_e, meta = _plan(counts, n_blocks)
    ys = _experts(block_e, meta, xs, wg, wu, wd)
    y_p = _combine(counts, *ri_p, route_p, x1_p, gt_f_p, row(p['g_post_ffn']), ys, rows_per_mod=seq)
    y_s = _combine(counts, *ri_s, route_s, x1_s, gt_f_s, row(p['g_post_ffn']), ys, rows_per_mod=0)

    k_new_p = k_p.reshape(batch, seq, N_KV_HEADS, HEAD_DIM)[:, -WINDOW:]
    v_new_p = v_p.reshape(batch, seq, N_KV_HEADS, HEAD_DIM)[:, -WINDOW:]
    conv_p = xl_p.reshape(batch, seq, LRU_WIDTH)[:, -(CONV_WIDTH - 1):].astype(F32)
    h_p = hlast_p.reshape(batch, LRU_WIDTH)
    k_new_s = jnp.concatenate([ck, kn.reshape(nseq, t_new, N_KV_HEADS, HEAD_DIM)], axis=1)[:, -WINDOW:]
    v_new_s = jnp.concatenate([cv, vn.reshape(nseq, t_new, N_KV_HEADS, HEAD_DIM)], axis=1)[:, -WINDOW:]
    xl_s3 = xl_s.reshape(t_new, nseq, LRU_WIDTH).transpose(1, 0, 2).astype(F32)
    conv_s = jnp.concatenate([cconv, xl_s3], axis=1)[:, -(CONV_WIDTH - 1):]
    return (y_p.reshape(batch, seq, D_MODEL), y_s, k_new_p, v_new_p, conv_p, h_p,
            k_new_s, v_new_s, conv_s, hlast_s)


PAST_LEN_ = 16384

PARAM_NAMES = ('w_ada', 'b_ada', 'g_pre_mix', 'g_post_mix', 'g_pre_ffn', 'g_post_ffn', 'w_in', 'sinks',
               'w_conv', 'b_conv', 'w_lru_a', 'b_lru_a', 'w_lru_x', 'b_lru_x', 'lru_lambda',
               'w_attn_branch', 'w_lru_branch', 'w_out', 'w_router_group', 'b_router_group',
               'w_router_expert', 'b_router_expert', 'w_exp_gate', 'w_exp_up', 'w_exp_down')


def kernel(x_prompt, x_sample, cache_k_win, cache_v_win, state_conv, state_h, c_prompt, c_sample, w_ada, b_ada, g_pre_mix, g_post_mix, g_pre_ffn, g_post_ffn, w_in, sinks, w_conv, b_conv, w_lru_a, b_lru_a, w_lru_x, b_lru_x, lru_lambda, w_attn_branch, w_lru_branch, w_out, w_router_group, b_router_group, w_router_expert, b_router_expert, w_exp_gate, w_exp_up, w_exp_down):
    weights = (w_ada, b_ada, g_pre_mix, g_post_mix, g_pre_ffn, g_post_ffn, w_in, sinks,
               w_conv, b_conv, w_lru_a, b_lru_a, w_lru_x, b_lru_x, lru_lambda,
               w_attn_branch, w_lru_branch, w_out, w_router_group, b_router_group,
               w_router_expert, b_router_expert, w_exp_gate, w_exp_up, w_exp_down)
    depth = w_ada.shape[0]
    batch = x_prompt.shape[0]
    nseq, t_new, _ = x_sample.shape
    y_p = x_prompt
    y_s = x_sample.transpose(1, 0, 2).reshape(t_new * nseq, D_MODEL)
    c_all = jnp.concatenate([c_prompt, c_sample], axis=0)
    outs = [[] for _ in range(8)]
    for layer in range(depth):
        p = {name: w[layer] for name, w in zip(PARAM_NAMES, weights)}
        mod = _ada(c_all, p['w_ada'].astype(BF16), p['b_ada'].reshape(1, -1))
        res = _layer_forward(y_p, y_s, cache_k_win[layer], cache_v_win[layer], state_conv[layer],
                             state_h[layer], mod[:batch], mod[batch:], p)
        y_p, y_s = res[0], res[1]
        for o, r in zip(outs, res[2:]):
            o.append(r)
    y_sample = y_s.reshape(t_new, nseq, D_MODEL).transpose(1, 0, 2)
    return (y_p, y_sample) + tuple(jnp.stack(o) for o in outs)
```

```python
import functools

import jax
import jax.numpy as jnp
from jax import lax
from jax.experimental import pallas as pl
from jax.experimental.pallas import tpu as pltpu

F32 = jnp.float32
BF16 = jnp.bfloat16
I32 = jnp.int32

D_MODEL = 1024
N_HEADS = 16
HEAD_DIM = 64
N_KV_HEADS = 4
GQA_GROUP = 4
WINDOW = 128
ROPE_THETA = 10000.0
NEG_INF = -1e30
LRU_WIDTH = 1024
LRU_BLOCKS = 16
LRU_BLOCK_W = 64
CONV_WIDTH = 4
LRU_C = 8.0
N_GROUPS = 4
EXPERTS_PER_GROUP = 8
N_EXPERTS = 32
D_EXPERT = 512
MOE_BLOCK = 128
NORM_EPS = 1e-6
Q_DIM = N_HEADS * HEAD_DIM
KV_DIM = N_KV_HEADS * HEAD_DIM
IN_DIM = Q_DIM + 2 * KV_DIM + 2 * LRU_WIDTH + 2 * D_MODEL

LANES = 128
SUBLANES = 8
MXU_DIM = 256
VMEM_LIMIT = 56 * 1024 * 1024

ROW_TILE = 512
LRU_TILE = 256
ROUTE_LANES = LANES
GRANULE = SUBLANES
EXPERT_ROWS = 512
HALF_D = D_MODEL // 2
U32 = jnp.uint32


def _cparams(n_axes, vmem=VMEM_LIMIT):
    return pltpu.CompilerParams(dimension_semantics=("arbitrary",) * n_axes, vmem_limit_bytes=vmem)


def _rms(x, g):
    ms = jnp.mean(x * x, axis=-1, keepdims=True)
    return x * lax.rsqrt(ms + NORM_EPS) * g


def _sigmoid(x):
    return 1.0 / (1.0 + jnp.exp(-x))


def _dot(a, b):
    return jnp.dot(a, b, preferred_element_type=F32)


def _ada_kernel(c_ref, w_ref, b_ref, o_ref):
    c = c_ref[...]
    s = (c * _sigmoid(c)).astype(BF16)
    o_ref[...] = _dot(s, w_ref[...]) + b_ref[...]


def _ada(c_all, w_ada, b_ada):
    r = c_all.shape[0]
    n = w_ada.shape[1]
    return pl.pallas_call(
        _ada_kernel,
        grid=(n // D_MODEL,),
        in_specs=[pl.BlockSpec((r, D_MODEL), lambda j: (0, 0)),
                  pl.BlockSpec((D_MODEL, D_MODEL), lambda j: (0, j)),
                  pl.BlockSpec((1, D_MODEL), lambda j: (0, j))],
        out_specs=pl.BlockSpec((r, D_MODEL), lambda j: (0, j)),
        out_shape=jax.ShapeDtypeStruct((r, n), F32),
        compiler_params=_cparams(1),
        name="ada_mod",
    )(c_all, w_ada, b_ada)


def _inproj_kernel(x_ref, sc_ref, sh_ref, g_ref, cos_ref, sin_ref, w_ref,
                   q_ref, k_ref, v_ref, xl_ref, yl_ref, sga_ref, sgr_ref):
    x = x_ref[...]
    h = _rms(x, g_ref[...]) * (1.0 + sc_ref[...]) + sh_ref[...]
    hb = h.astype(BF16)
    cos = cos_ref[...]
    sin = sin_ref[...]
    lane = lax.broadcasted_iota(I32, cos.shape, 1)
    first_half = (lane % HEAD_DIM) < (HEAD_DIM // 2)

    def rope(t):
        rot = jnp.where(first_half, pltpu.roll(t, LANES - HEAD_DIM // 2, 1), pltpu.roll(t, HEAD_DIM // 2, 1))
        return t * cos + rot * sin

    o1 = Q_DIM
    o2 = o1 + KV_DIM
    o3 = o2 + KV_DIM
    o4 = o3 + LRU_WIDTH
    o5 = o4 + LRU_WIDTH
    o6 = o5 + D_MODEL
    qf = _dot(hb, w_ref[:, 0:o1])
    for c in range(Q_DIM // LANES):
        q_ref[:, c * LANES:(c + 1) * LANES] = rope(qf[:, c * LANES:(c + 1) * LANES]).astype(BF16)
    kf = _dot(hb, w_ref[:, o1:o2])
    for c in range(KV_DIM // LANES):
        k_ref[:, c * LANES:(c + 1) * LANES] = rope(kf[:, c * LANES:(c + 1) * LANES])
    v_ref[...] = _dot(hb, w_ref[:, o2:o3])
    xl_ref[...] = _dot(hb, w_ref[:, o3:o4]).astype(BF16)
    yl_ref[...] = _dot(hb, w_ref[:, o4:o5]).astype(BF16)
    sga_ref[...] = _sigmoid(_dot(hb, w_ref[:, o5:o6])).astype(BF16)
    sgr_ref[...] = _sigmoid(_dot(hb, w_ref[:, o6:IN_DIM])).astype(BF16)


def _inproj(x, sc, sh, g, cos, sin, w_in, rows_per_mod, pos_tiles):
    n = x.shape[0]
    tm = min(ROW_TILE, n)
    if rows_per_mod:
        tiles_per_mod = rows_per_mod // tm
        mod_spec = pl.BlockSpec((None, 1, D_MODEL), lambda i: (i // tiles_per_mod, 0, 0))
    else:
        mod_spec = pl.BlockSpec((tm, D_MODEL), lambda i: (i, 0))
    row = lambda w: pl.BlockSpec((tm, w), lambda i: (i, 0))
    outs = [(Q_DIM, BF16), (KV_DIM, F32), (KV_DIM, F32), (LRU_WIDTH, BF16), (LRU_WIDTH, BF16),
            (D_MODEL, BF16), (D_MODEL, BF16)]
    return pl.pallas_call(
        _inproj_kernel,
        grid=(n // tm,),
        in_specs=[row(D_MODEL), mod_spec, mod_spec,
                  pl.BlockSpec((1, D_MODEL), lambda i: (0, 0)),
                  pl.BlockSpec((tm, LANES), lambda i: (i % pos_tiles, 0)),
                  pl.BlockSpec((tm, LANES), lambda i: (i % pos_tiles, 0)),
                  pl.BlockSpec((D_MODEL, IN_DIM), lambda i: (0, 0))],
        out_specs=[row(w) for w, _ in outs],
        out_shape=[jax.ShapeDtypeStruct((n, w), dt) for w, dt in outs],
        compiler_params=_cparams(1),
        name="in_proj",
    )(x, sc, sh, g, cos, sin, w_in)


def _head_masks(shape):
    lane = lax.broadcasted_iota(I32, shape, 1)
    return [(lane // HEAD_DIM) == h for h in range(N_KV_HEADS)]


def _attention_core(q_perm, kall, vall, valid, sink_of, rows):
    masks_b = _head_masks((rows, KV_DIM))
    zero_b = jnp.zeros((rows, KV_DIM), BF16)
    pieces = []
    for h in range(N_KV_HEADS):
        for g in range(GQA_GROUP):
            pieces.append(jnp.where(masks_b[h], q_perm[g], zero_b))
    q_big = jnp.concatenate(pieces, axis=0)
    s_all = lax.dot_general(q_big, kall, (((1,), (1,)), ((), ())), preferred_element_type=F32)
    s_all = s_all * (HEAD_DIM ** -0.5)
    p_chunks, inv_chunks = [], []
    for h in range(N_KV_HEADS):
        for g in range(GQA_GROUP):
            c = h * GQA_GROUP + g
            s = jnp.where(valid, s_all[c * rows:(c + 1) * rows], NEG_INF)
            sink = sink_of(h, g)
            m = jnp.maximum(jnp.max(s, axis=-1, keepdims=True), sink)
            p = jnp.exp(s - m)
            denom = jnp.sum(p, axis=-1, keepdims=True) + jnp.exp(sink - m)
            p_chunks.append(p.astype(BF16))
            inv_chunks.append(1.0 / denom)
    o_all = _dot(jnp.concatenate(p_chunks, axis=0), vall)
    outs = []
    for g in range(GQA_GROUP):
        acc = jnp.zeros((rows, KV_DIM), F32)
        for h in range(N_KV_HEADS):
            c = h * GQA_GROUP + g
            acc = acc + jnp.where(masks_b[h], o_all[c * rows:(c + 1) * rows] * inv_chunks[c], 0.0)
        outs.append(acc)
    return outs


def _attn_prompt_kernel(sink_ref, q_ref, kc_ref, kp_ref, vc_ref, vp_ref, o_ref):
    j = pl.program_id(1)
    kall = jnp.concatenate([kp_ref[...], kc_ref[...]], axis=0).astype(BF16)
    vall = jnp.concatenate([vp_ref[...], vc_ref[...]], axis=0).astype(BF16)
    qi = lax.broadcasted_iota(I32, (WINDOW, 2 * WINDOW), 0)
    kj = lax.broadcasted_iota(I32, (WINDOW, 2 * WINDOW), 1)
    dist = qi + WINDOW - kj
    valid = (dist >= 0) & (dist <= WINDOW) & ((kj >= WINDOW) | (j > 0))
    q_perm = [q_ref[:, g * KV_DIM:(g + 1) * KV_DIM] for g in range(GQA_GROUP)]
    outs = _attention_core(q_perm, kall, vall, valid, lambda h, g: sink_ref[h * GQA_GROUP + g], WINDOW)
    for g in range(GQA_GROUP):
        o_ref[:, g * KV_DIM:(g + 1) * KV_DIM] = outs[g].astype(BF16)


def _attn_prompt(sinks, q, k, v, batch, seq):
    nb = seq // WINDOW
    cur = lambda w: pl.BlockSpec((WINDOW, w), lambda b, j: (b * nb + j, 0))
    prev = lambda w: pl.BlockSpec((WINDOW, w), lambda b, j: (b * nb + jnp.maximum(j - 1, 0), 0))
    return pl.pallas_call(
        _attn_prompt_kernel,
        grid=(batch, nb),
        in_specs=[pl.BlockSpec(memory_space=pltpu.SMEM),
                  cur(Q_DIM), cur(KV_DIM), prev(KV_DIM), cur(KV_DIM), prev(KV_DIM)],
        out_specs=cur(Q_DIM),
        out_shape=jax.ShapeDtypeStruct((batch * seq, Q_DIM), BF16),
        compiler_params=_cparams(2),
        name="attn_prompt",
    )(sinks, q, k, k, v, v)


SEQ_PER_STEP = 8


def _attn_sample_kernel(sink_ref, q_ref, kn_ref, vn_ref, kc_ref, vc_ref, o_ref, kbuf, vbuf, *, t_new):
    rows = GQA_GROUP * t_new
    kbuf[WINDOW:2 * WINDOW, :] = jnp.zeros((WINDOW, KV_DIM), F32)
    vbuf[WINDOW:2 * WINDOW, :] = jnp.zeros((WINDOW, KV_DIM), F32)
    ri = lax.broadcasted_iota(I32, (rows, 2 * WINDOW), 0)
    kj = lax.broadcasted_iota(I32, (rows, 2 * WINDOW), 1)
    tq = ri // GQA_GROUP
    valid = (kj >= tq) & (kj <= tq + WINDOW) & (kj < WINDOW + t_new)
    for s in range(SEQ_PER_STEP):
        kbuf[0:WINDOW, :] = kc_ref[s]
        vbuf[0:WINDOW, :] = vc_ref[s]
        kbuf[WINDOW:WINDOW + t_new, :] = kn_ref[s]
        vbuf[WINDOW:WINDOW + t_new, :] = vn_ref[s]
        kall = kbuf[...].astype(BF16)
        vall = vbuf[...].astype(BF16)
        qs = q_ref[s]
        masks_b = _head_masks((rows, KV_DIM))
        zero_b = jnp.zeros((rows, KV_DIM), BF16)
        q_big = jnp.concatenate([jnp.where(masks_b[h], qs, zero_b) for h in range(N_KV_HEADS)], axis=0)
        s_all = lax.dot_general(q_big, kall, (((1,), (1,)), ((), ())), preferred_element_type=F32)
        s_all = s_all * (HEAD_DIM ** -0.5)
        acc = jnp.zeros((rows, KV_DIM), F32)
        p_chunks, inv_chunks = [], []
        for h in range(N_KV_HEADS):
            sc = jnp.where(valid, s_all[h * rows:(h + 1) * rows], NEG_INF)
            sink = sink_ref[h]
            m = jnp.maximum(jnp.max(sc, axis=-1, keepdims=True), sink)
            p = jnp.exp(sc - m)
            denom = jnp.sum(p, axis=-1, keepdims=True) + jnp.exp(sink - m)
            p_chunks.append(p.astype(BF16))
            inv_chunks.append(1.0 / denom)
        o_all = _dot(jnp.concatenate(p_chunks, axis=0), vall)
        for h in range(N_KV_HEADS):
            acc = acc + jnp.where(masks_b[h], o_all[h * rows:(h + 1) * rows] * inv_chunks[h], 0.0)
        o_ref[s] = acc.astype(BF16)


def _attn_sample(sink_rows, q, kn, vn, kc, vc):
    nseq, rows, _ = q.shape
    t_new = kn.shape[1]
    sb = SEQ_PER_STEP
    blk = lambda r: pl.BlockSpec((sb, r, KV_DIM), lambda i: (i, 0, 0))
    return pl.pallas_call(
        functools.partial(_attn_sample_kernel, t_new=t_new),
        grid=(nseq // sb,),
        in_specs=[pl.BlockSpec((N_KV_HEADS, rows, 1), lambda i: (0, 0, 0)),
                  blk(rows), blk(t_new), blk(t_new), blk(WINDOW), blk(WINDOW)],
        out_specs=blk(rows),
        out_shape=jax.ShapeDtypeStruct((nseq, rows, KV_DIM), BF16),
        scratch_shapes=[pltpu.VMEM((2 * WINDOW, KV_DIM), F32), pltpu.VMEM((2 * WINDOW, KV_DIM), F32)],
        compiler_params=_cparams(1),
        name="attn_sample",
    )(sink_rows, q, kn, vn, kc, vc)


def _gelu_tanh(x):
    return 0.5 * x * (1.0 + jnp.tanh(0.7978845608028654 * (x + 0.044715 * x * x * x)))


def _lru_gates(xc, wbd_ref, ba, bx, lam):
    xcb = xc.astype(BF16)
    r_parts, i_parts = [], []
    for gidx in range(LRU_WIDTH // MXU_DIM):
        z = _dot(xcb[:, gidx * MXU_DIM:(gidx + 1) * MXU_DIM], wbd_ref[gidx])
        r_parts.append(z[:, :MXU_DIM])
        i_parts.append(z[:, MXU_DIM:])
    r = _sigmoid(jnp.concatenate(r_parts, axis=1) + ba)
    i = _sigmoid(jnp.concatenate(i_parts, axis=1) + bx)
    softplus_neg_lam = jnp.maximum(-lam, 0.0) + jnp.log1p(jnp.exp(-jnp.abs(lam)))
    log_a = -LRU_C * r * softplus_neg_lam
    a = jnp.exp(log_a)
    u = jnp.sqrt(1.0 - jnp.exp(2.0 * log_a)) * (i * xc)
    return a, u


def _lru_prompt_kernel(xl_ref, yl_ref, wc_ref, bc_ref, wbd_ref, ba_ref, bx_ref, lam_ref,
                       o_ref, hlast_ref, xbuf, hcar):
    j = pl.program_id(1)
    t = xl_ref.shape[0]
    w = LRU_WIDTH

    @pl.when(j == 0)
    def _():
        xbuf[0:SUBLANES, :] = jnp.zeros((SUBLANES, w), F32)
        hcar[...] = jnp.zeros((1, w), F32)

    x = xl_ref[...].astype(F32)
    xbuf[SUBLANES:SUBLANES + t, :] = x
    xc = x * wc_ref[CONV_WIDTH - 1:CONV_WIDTH, :] + bc_ref[...]
    for k in range(1, CONV_WIDTH):
        xc = xc + xbuf[SUBLANES - k:SUBLANES - k + t, :] * wc_ref[CONV_WIDTH - 1 - k:CONV_WIDTH - k, :]
    xbuf[0:SUBLANES, :] = x[t - SUBLANES:t, :]

    a, u = _lru_gates(xc, wbd_ref, ba_ref[...], bx_ref[...], lam_ref[...])

    ng = t // SUBLANES
    a3 = a.reshape(ng, SUBLANES, w)
    u3 = u.reshape(ng, SUBLANES, w)
    row = lax.broadcasted_iota(I32, (ng, SUBLANES, w), 1)
    d = 1
    while d < SUBLANES:
        a_s = jnp.where(row >= d, pltpu.roll(a3, d, 1), 1.0)
        u_s = jnp.where(row >= d, pltpu.roll(u3, d, 1), 0.0)
        u3 = a3 * u_s + u3
        a3 = a3 * a_s
        d *= 2
    carry = hcar[...]
    hs = []
    for gi in range(ng):
        hg = a3[gi] * carry + u3[gi]
        hs.append(hg)
        carry = hg[SUBLANES - 1:SUBLANES, :]
    hcar[...] = carry
    h = jnp.concatenate(hs, axis=0)
    o_ref[...] = (h * _gelu_tanh(yl_ref[...].astype(F32))).astype(BF16)
    hlast_ref[...] = carry


def _lru_prompt(xl, yl, wc, bc, wbd, ba, bx, lam, batch, seq):
    t = min(LRU_TILE, seq)
    nt = seq // t
    tile = pl.BlockSpec((t, LRU_WIDTH), lambda b, j: (b * nt + j, 0))
    full = lambda shp: pl.BlockSpec(shp, lambda b, j: (0,) * len(shp))
    return pl.pallas_call(
        _lru_prompt_kernel,
        grid=(batch, nt),
        in_specs=[tile, tile, full((CONV_WIDTH, LRU_WIDTH)), full((1, LRU_WIDTH)),
                  full(wbd.shape), full((1, LRU_WIDTH)), full((1, LRU_WIDTH)), full((1, LRU_WIDTH))],
        out_specs=[tile, pl.BlockSpec((None, 1, LRU_WIDTH), lambda b, j: (b, 0, 0))],
        out_shape=[jax.ShapeDtypeStruct((batch * seq, LRU_WIDTH), BF16),
                   jax.ShapeDtypeStruct((batch, 1, LRU_WIDTH), F32)],
        scratch_shapes=[pltpu.VMEM((SUBLANES + t, LRU_WIDTH), F32), pltpu.VMEM((1, LRU_WIDTH), F32)],
        compiler_params=_cparams(2),
        name="lru_prompt",
    )(xl, yl, wc, bc, wbd, ba, bx, lam)


def _lru_sample_kernel(xl_ref, yl_ref, cs_ref, h0_ref, wc_ref, bc_ref, wbd_ref, ba_ref, bx_ref, lam_ref,
                       o_ref, hlast_ref):
    t_new, nseq, w = xl_ref.shape
    xp = [cs_ref[k] for k in range(CONV_WIDTH - 1)] + [xl_ref[k].astype(F32) for k in range(t_new)]
    xcs = []
    for t in range(t_new):
        acc = bc_ref[...] + xp[t] * wc_ref[0:1, :]
        for k in range(1, CONV_WIDTH):
            acc = acc + xp[t + k] * wc_ref[k:k + 1, :]
        xcs.append(acc)
    xc = jnp.concatenate(xcs, axis=0)
    a, u = _lru_gates(xc, wbd_ref, ba_ref[...], bx_ref[...], lam_ref[...])
    h = h0_ref[...]
    for t in range(t_new):
        h = a[t * nseq:(t + 1) * nseq] * h + u[t * nseq:(t + 1) * nseq]
        o_ref[t] = (h * _gelu_tanh(yl_ref[t].astype(F32))).astype(BF16)
    hlast_ref[...] = h


def _lru_sample(xl, yl, cs, h0, wc, bc, wbd, ba, bx, lam):
    t_new, nseq, w = xl.shape
    full = lambda shp: pl.BlockSpec(shp, lambda i: (0,) * len(shp))
    args = (xl, yl, cs, h0, wc, bc, wbd, ba, bx, lam)
    return pl.pallas_call(
        _lru_sample_kernel,
        grid=(1,),
        in_specs=[full(a.shape) for a in args],
        out_specs=[full((t_new, nseq, w)), full((nseq, w))],
        out_shape=[jax.ShapeDtypeStruct((t_new, nseq, w), BF16), jax.ShapeDtypeStruct((nseq, w), F32)],
        compiler_params=_cparams(1),
        name="lru_sample",
    )(*args)


def _post_kernel(oa_ref, ol_ref, sga_ref, sgr_ref, x_ref, gta_ref, scf_ref, shf_ref,
                 wab_ref, wlb_ref, wout_ref, gpm_ref, gpf_ref, wrh_ref, wrl_ref, br_ref, tri_ref, cin_ref,
                 x1_ref, h2_ref, route_ref, stat_ref, cnt_ref, carry):
    i = pl.program_id(0)

    @pl.when(i == 0)
    def _():
        carry[...] = cin_ref[...]

    b_attn = _dot(oa_ref[...], wab_ref[...])
    b_lru = _dot(ol_ref[...], wlb_ref[...])
    merged = sga_ref[...].astype(F32) * b_attn + sgr_ref[...].astype(F32) * b_lru
    mix = _dot(merged.astype(BF16), wout_ref[...])
    x1 = x_ref[...] + gta_ref[...] * _rms(mix, gpm_ref[...])
    x1_ref[...] = x1
    h2 = _rms(x1, gpf_ref[...]) * (1.0 + scf_ref[...]) + shf_ref[...]
    h2_ref[...] = h2.astype(BF16)

    h_hi = h2.astype(BF16)
    h_lo = (h2 - h_hi.astype(F32)).astype(BF16)
    logits = _dot(h_hi, wrh_ref[...]) + (_dot(h_lo, wrh_ref[...]) + _dot(h_hi, wrl_ref[...])) + br_ref[...]

    tm = logits.shape[0]
    lane = lax.broadcasted_iota(I32, (tm, ROUTE_LANES), 1)
    big = jnp.int32(ROUTE_LANES)
    is_g = lane < N_GROUPS
    lg = jnp.where(is_g, logits, NEG_INF)
    mg = jnp.max(lg, axis=-1, keepdims=True)
    g_star = jnp.min(jnp.where(lg == mg, lane, big), axis=-1, keepdims=True)
    p_star = 1.0 / jnp.sum(jnp.where(is_g, jnp.exp(lg - mg), 0.0), axis=-1, keepdims=True)
    lo = N_GROUPS + g_star * EXPERTS_PER_GROUP
    in_grp = (lane >= lo) & (lane < lo + EXPERTS_PER_GROUP)
    le = jnp.where(in_grp, logits, NEG_INF)
    m1 = jnp.max(le, axis=-1, keepdims=True)
    i1 = jnp.min(jnp.where(le == m1, lane, big), axis=-1, keepdims=True)
    le2 = jnp.where(lane == i1, NEG_INF, le)
    m2 = jnp.max(le2, axis=-1, keepdims=True)
    i2 = jnp.min(jnp.where(le2 == m2, lane, big), axis=-1, keepdims=True)
    e2x = jnp.exp(m2 - m1)
    wsum = 1.0 + e2x
    w1 = (1.0 / wsum) * p_star
    w2 = (e2x / wsum) * p_star

    oh1 = lane == i1
    oh2 = lane == i2
    cnt = jnp.where(oh1 | oh2, 1.0, 0.0)
    excl = _dot(tri_ref[...], cnt.astype(BF16))
    per_e = jnp.sum(cnt, axis=0, keepdims=True)
    pad8 = jnp.floor((per_e + (GRANULE - 1.0)) * (1.0 / GRANULE)) * GRANULE
    incl = jnp.broadcast_to(pad8, (SUBLANES, ROUTE_LANES))
    lane8 = lax.broadcasted_iota(I32, (SUBLANES, ROUTE_LANES), 1)
    d = 1
    while d < ROUTE_LANES:
        incl = incl + jnp.where(lane8 >= d, pltpu.roll(incl, d, 1), 0.0)
        d *= 2
    seg_start = incl[0:1, :] - pad8
    pos = excl + seg_start
    s1 = jnp.sum(jnp.where(oh1, pos, 0.0), axis=-1, keepdims=True)
    s2 = jnp.sum(jnp.where(oh2, pos, 0.0), axis=-1, keepdims=True)
    rec = jnp.where(lane == 0, s1, 0.0)
    rec = jnp.where(lane == 1, s2, rec)
    rec = jnp.where(lane == 2, w1, rec)
    rec = jnp.where(lane == 3, w2, rec)
    route_ref[...] = rec
    srow = lax.broadcasted_iota(I32, (SUBLANES, ROUTE_LANES), 0)
    stat_ref[...] = jnp.where(srow == 0, pad8, jnp.where(srow == 1, carry[...], 0.0))
    carry[...] = carry[...] + pad8
    cnt_ref[...] = carry[...]


def _post(oa, ol, sga, sgr, x, gta, scf, shf, wab, wlb, wout, gpm, gpf, wrh, wrl, br, tri, cin, rows_per_mod):
    n = x.shape[0]
    tm = min(ROW_TILE, n)
    if rows_per_mod:
        tiles_per_mod = rows_per_mod // tm
        mod_spec = pl.BlockSpec((None, 1, D_MODEL), lambda i: (i // tiles_per_mod, 0, 0))
    else:
        mod_spec = pl.BlockSpec((tm, D_MODEL), lambda i: (i, 0))
    row = lambda w: pl.BlockSpec((tm, w), lambda i: (i, 0))
    full = lambda a: pl.BlockSpec(a.shape, lambda i: (0,) * a.ndim)
    return pl.pallas_call(
        _post_kernel,
        grid=(n // tm,),
        in_specs=[row(Q_DIM), row(LRU_WIDTH), row(D_MODEL), row(D_MODEL), row(D_MODEL),
                  mod_spec, mod_spec, mod_spec,
                  full(wab), full(wlb), full(wout), full(gpm), full(gpf), full(wrh), full(wrl), full(br),
                  full(tri), full(cin)],
        out_specs=[row(D_MODEL), row(D_MODEL), row(ROUTE_LANES),
                   pl.BlockSpec((SUBLANES, ROUTE_LANES), lambda i: (i, 0)),
                   pl.BlockSpec((1, ROUTE_LANES), lambda i: (0, 0))],
        out_shape=[jax.ShapeDtypeStruct((n, D_MODEL), F32), jax.ShapeDtypeStruct((n, D_MODEL), BF16),
                   jax.ShapeDtypeStruct((n, ROUTE_LANES), F32),
                   jax.ShapeDtypeStruct((n // tm * SUBLANES, ROUTE_LANES), F32),
                   jax.ShapeDtypeStruct((1, ROUTE_LANES), F32)],
        scratch_shapes=[pltpu.VMEM((1, ROUTE_LANES), F32)],
        compiler_params=_cparams(1),
        name="post_mix",
    )(oa, ol, sga, sgr, x, gta, scf, shf, wab, wlb, wout, gpm, gpf, wrh, wrl, br, tri, cin)


def _padded(c):
    return ((c + (EXPERT_ROWS - 1)) // EXPERT_ROWS) * EXPERT_ROWS


def _sorted_rows(tm):
    r = 2 * tm + N_EXPERTS * (GRANULE - 1)
    return -(-r // MXU_DIM) * MXU_DIM


def _plan_kernel(tot_ref, be_ref, meta_ref, *, n_blocks):
    def fill(j, _):
        be_ref[j] = N_EXPERTS - 1
        return 0
    lax.fori_loop(0, n_blocks, fill, 0)

    def per_expert(e, nb):
        k = _padded(tot_ref[e]) // EXPERT_ROWS

        def put(b, _):
            be_ref[nb + b] = e
            return 0
        lax.fori_loop(0, k, put, 0)
        return nb + k
    n_active = lax.fori_loop(0, N_EXPERTS, per_expert, 0)
    meta_ref[0] = n_active


def _plan(totals, n_blocks):
    return pl.pallas_call(
        functools.partial(_plan_kernel, n_blocks=n_blocks),
        in_specs=[pl.BlockSpec(memory_space=pltpu.SMEM)],
        out_specs=[pl.BlockSpec(memory_space=pltpu.SMEM), pl.BlockSpec(memory_space=pltpu.SMEM)],
        out_shape=[jax.ShapeDtypeStruct((n_blocks,), I32), jax.ShapeDtypeStruct((1,), I32)],
        name="moe_plan",
    )(totals)


def _expert_starts(tot_ref, pstart):
    def body(e, acc):
        pstart[e] = acc
        return acc + _padded(tot_ref[e])
    return lax.fori_loop(0, N_EXPERTS, body, 0)


def _rows_copy(src, s, dst, d, n, sem):
    return pltpu.make_async_copy(src.at[pl.ds(pl.multiple_of(s, GRANULE), n)],
                                 dst.at[pl.ds(pl.multiple_of(d, GRANULE), n)], sem)


def _segment_copies(tot_ref, cnt8_ref, goff_ref, pstart, tile, copy_granule):
    def per_expert(e, carry):
        seg, n_gran = carry
        rows = cnt8_ref[tile * N_EXPERTS + e]
        dst = pstart[e] + goff_ref[tile * N_EXPERTS + e]
        k = rows // GRANULE

        def gran(g, _):
            copy_granule(seg + g * GRANULE, dst + g * GRANULE)
            return 0
        lax.fori_loop(0, k, gran, 0)
        return seg + rows, n_gran + k
    _, total = lax.fori_loop(0, N_EXPERTS, per_expert, (0, 0))
    return total


def _pack_halves(lo_f32, hi_f32):
    return (pltpu.bitcast(lo_f32, U32) >> 16) | (pltpu.bitcast(hi_f32, U32) & jnp.uint32(0xFFFF0000))


def _unpack_halves(packed):
    lo = pltpu.bitcast(packed << 16, F32).astype(BF16)
    hi = pltpu.bitcast(packed & jnp.uint32(0xFFFF0000), F32).astype(BF16)
    return lo, hi


def _dispatch_kernel(tot_ref, cnt8_ref, goff_ref, rec_ref, h2_ref, zero_hbm, xs_in, xs_out, sbuf, pstart, sem,
                     *, tile_base, n_slots, fill_padding):
    del xs_in
    i = pl.program_id(0)
    used_rows = _expert_starts(tot_ref, pstart)
    tm = h2_ref.shape[0]
    n_sorted = sbuf.shape[0]

    rec_t = rec_ref[...].T
    s1 = rec_t[0:1, :].astype(I32)
    s2 = rec_t[1:2, :].astype(I32)
    rows = lax.broadcasted_iota(I32, (n_sorted, tm), 0)
    sel = jnp.where((rows == s1) | (rows == s2), 1.0, 0.0).astype(BF16)
    srt = _dot(sel, h2_ref[...])
    sbuf[...] = _pack_halves(srt[:, :HALF_D], srt[:, HALF_D:])

    def wait_granules(n):
        def drain(r, _):
            _rows_copy(sbuf, 0, xs_out, 0, GRANULE, sem).wait()
            return 0
        lax.fori_loop(0, n, drain, 0)

    n_gran = _segment_copies(
        tot_ref, cnt8_ref, goff_ref, pstart, tile_base + i,
        lambda src, dst: _rows_copy(sbuf, src, xs_out, dst, GRANULE, sem).start())
    wait_granules(n_gran)

    if fill_padding:
        @pl.when(i == 0)
        def _():
            def zero_fill(act):
                def per_expert(e, _):
                    total = tot_ref[e]
                    pad = _padded(total) - total
                    at = pstart[e] + total
                    p = EXPERT_ROWS // 2
                    while p >= GRANULE:
                        @pl.when((pad & p) != 0)
                        def _(at=at, p=p):
                            act(_rows_copy(zero_hbm, 0, xs_out, at, p, sem))
                        at = at + (pad & p)
                        p //= 2
                    return 0
                lax.fori_loop(0, N_EXPERTS, per_expert, 0)

                def per_block(b, _):
                    act(_rows_copy(zero_hbm, 0, xs_out, b * EXPERT_ROWS, EXPERT_ROWS, sem))
                    return 0
                lax.fori_loop(used_rows // EXPERT_ROWS, n_slots // EXPERT_ROWS, per_block, 0)
            zero_fill(lambda cp: cp.start())
            zero_fill(lambda cp: cp.wait())


def _dispatch(totals, cnt8, goff, rec, h2, zero_blk, xs, n_slots, tile_base, fill_padding):
    n = h2.shape[0]
    tm = min(ROW_TILE, n)
    n_sorted = _sorted_rows(tm)
    row = lambda w: pl.BlockSpec((tm, w), lambda i, *_: (i, 0))
    anyspec = pl.BlockSpec(memory_space=pl.ANY)
    if xs is None:
        xs = jnp.zeros((GRANULE, HALF_D), U32)
        aliases = {}
    else:
        aliases = {6: 0}
    return pl.pallas_call(
        functools.partial(_dispatch_kernel, tile_base=tile_base, n_slots=n_slots, fill_padding=fill_padding),
        grid_spec=pltpu.PrefetchScalarGridSpec(
            num_scalar_prefetch=3,
            grid=(n // tm,),
            in_specs=[row(ROUTE_LANES), row(D_MODEL), anyspec, anyspec],
            out_specs=anyspec,
            scratch_shapes=[pltpu.VMEM((n_sorted, HALF_D), U32), pltpu.SMEM((N_EXPERTS,), I32),
                            pltpu.SemaphoreType.DMA(())]),
        out_shape=jax.ShapeDtypeStruct((n_slots, HALF_D), U32),
        input_output_aliases=aliases,
        compiler_params=_cparams(1),
        name="moe_dispatch",
    )(totals, cnt8, goff, rec, h2, zero_blk, xs)


def _expert_kernel(be_ref, meta_ref, xs_ref, wg_ref, wu_ref, wd_ref, ys_ref):
    j = pl.program_id(0)

    @pl.when(j < meta_ref[0])
    def _():
        x_lo, x_hi = _unpack_halves(xs_ref[...])
        g = _dot(x_lo, wg_ref[0:HALF_D, :]) + _dot(x_hi, wg_ref[HALF_D:D_MODEL, :])
        u = _dot(x_lo, wu_ref[0:HALF_D, :]) + _dot(x_hi, wu_ref[HALF_D:D_MODEL, :])
        hmid = (g * _sigmoid(g) * u).astype(BF16)
        y = _dot(hmid, wd_ref[...])
        ys_ref[...] = _pack_halves(y[:, :HALF_D].astype(BF16).astype(F32), y[:, HALF_D:].astype(BF16).astype(F32))

    @pl.when(j >= meta_ref[0])
    def _():
        ys_ref[...] = jnp.zeros(ys_ref.shape, U32)


def _experts(block_e, meta, xs, wg, wu, wd):
    n_blocks = xs.shape[0] // EXPERT_ROWS
    act = lambda j, be, meta: jnp.minimum(j, meta[0] - 1)
    return pl.pallas_call(
        _expert_kernel,
        grid_spec=pltpu.PrefetchScalarGridSpec(
            num_scalar_prefetch=2,
            grid=(n_blocks,),
            in_specs=[pl.BlockSpec((EXPERT_ROWS, HALF_D), lambda j, be, meta: (act(j, be, meta), 0)),
                      pl.BlockSpec((None, D_MODEL, D_EXPERT), lambda j, be, meta: (be[act(j, be, meta)], 0, 0)),
                      pl.BlockSpec((None, D_MODEL, D_EXPERT), lambda j, be, meta: (be[act(j, be, meta)], 0, 0)),
                      pl.BlockSpec((None, D_EXPERT, D_MODEL), lambda j, be, meta: (be[act(j, be, meta)], 0, 0))],
            out_specs=pl.BlockSpec((EXPERT_ROWS, HALF_D), lambda j, be, meta: (j, 0))),
        out_shape=jax.ShapeDtypeStruct(xs.shape, U32),
        compiler_params=_cparams(1),
        name="moe_experts",
    )(block_e, meta, xs, wg, wu, wd)


def _combine_kernel(tot_ref, cnt8_ref, goff_ref, rec_ref, x1_ref, gtf_ref, g_ref, ys_hbm,
                    y_ref, cbuf, pstart, sem, *, tile_base):
    i = pl.program_id(0)
    tm = x1_ref.shape[0]
    n_sorted = cbuf.shape[0]

    @pl.when(i == 0)
    def _():
        cbuf[...] = jnp.zeros(cbuf.shape, U32)

    _expert_starts(tot_ref, pstart)
    n_gran = _segment_copies(
        tot_ref, cnt8_ref, goff_ref, pstart, tile_base + i,
        lambda row, slot: _rows_copy(ys_hbm, slot, cbuf, row, GRANULE, sem).start())

    def drain(r, _):
        _rows_copy(ys_hbm, 0, cbuf, 0, GRANULE, sem).wait()
        return 0
    lax.fori_loop(0, n_gran, drain, 0)

    rec = rec_ref[...]
    s1 = rec[:, 0:1].astype(I32)
    s2 = rec[:, 1:2].astype(I32)
    col = lax.broadcasted_iota(I32, (tm, n_sorted), 1)
    wmat = (jnp.where(col == s1, rec[:, 2:3], 0.0) + jnp.where(col == s2, rec[:, 3:4], 0.0)).astype(BF16)
    y_lo, y_hi = _unpack_halves(cbuf[...])
    f = jnp.concatenate([_dot(wmat, y_lo), _dot(wmat, y_hi)], axis=1)
    y_ref[...] = x1_ref[...] + gtf_ref[...] * _rms(f, g_ref[...])


def _combine(totals, cnt8, goff, rec, x1, gtf, g, ys, rows_per_mod, tile_base):
    n = x1.shape[0]
    tm = min(ROW_TILE, n)
    if rows_per_mod:
        tiles_per_mod = rows_per_mod // tm
        mod_spec = pl.BlockSpec((None, 1, D_MODEL), lambda i, *_: (i // tiles_per_mod, 0, 0))
    else:
        mod_spec = pl.BlockSpec((tm, D_MODEL), lambda i, *_: (i, 0))
    row = lambda w: pl.BlockSpec((tm, w), lambda i, *_: (i, 0))
    return pl.pallas_call(
        functools.partial(_combine_kernel, tile_base=tile_base),
        grid_spec=pltpu.PrefetchScalarGridSpec(
            num_scalar_prefetch=3,
            grid=(n // tm,),
            in_specs=[row(ROUTE_LANES), row(D_MODEL), mod_spec,
                      pl.BlockSpec((1, D_MODEL), lambda i, *_: (0, 0)),
                      pl.BlockSpec(memory_space=pl.ANY)],
            out_specs=row(D_MODEL),
            scratch_shapes=[pltpu.VMEM((_sorted_rows(tm), HALF_D), U32),
                            pltpu.SMEM((N_EXPERTS,), I32), pltpu.SemaphoreType.DMA(())]),
        out_shape=jax.ShapeDtypeStruct((n, D_MODEL), F32),
        compiler_params=_cparams(1),
        name="moe_combine",
    )(totals, cnt8, goff, rec, x1, gtf, g, ys)


def _rope_tables(pos):
    half = HEAD_DIM // 2
    inv = jnp.power(jnp.float32(ROPE_THETA), -jnp.arange(half, dtype=F32) / half)
    ang = pos.astype(F32)[:, None] * inv[None, :]
    cos = jnp.cos(ang)
    sin = jnp.sin(ang)
    reps = LANES // HEAD_DIM
    cos_t = jnp.tile(jnp.concatenate([cos, cos], axis=-1), (1, reps))
    sin_t = jnp.tile(jnp.concatenate([-sin, sin], axis=-1), (1, reps))
    return cos_t, sin_t


def _q_perm_index():
    g = jnp.arange(GQA_GROUP)[:, None, None]
    h = jnp.arange(N_KV_HEADS)[None, :, None]
    d = jnp.arange(HEAD_DIM)[None, None, :]
    return ((h * GQA_GROUP + g) * HEAD_DIM + d).reshape(-1)


def _block_diag_gates(w_a, w_x):
    per = MXU_DIM // LRU_BLOCK_W
    groups = LRU_BLOCKS // per

    def bd(w):
        w = w.reshape(groups, per, LRU_BLOCK_W, LRU_BLOCK_W)
        eye = jnp.eye(per, dtype=w.dtype)
        full = jnp.einsum('gpij,pq->gpiqj', w, eye)
        return full.reshape(groups, MXU_DIM, MXU_DIM)
    return jnp.concatenate([bd(w_a), bd(w_x)], axis=-1).astype(BF16)


def _layer_forward(xp, xs_tm, ck, cv, cconv, ch, mod_p, mod_s, p):
    batch, seq, _ = xp.shape
    nseq, _, _, _ = ck.shape
    t_new = xs_tm.shape[0] // nseq
    n_p = batch * seq
    n_s = xs_tm.shape[0]

    perm = _q_perm_index()
    w_in = p['w_in']
    w_in_b = jnp.concatenate([w_in[:, :Q_DIM][:, perm], w_in[:, Q_DIM:]], axis=1).astype(BF16)
    wab = p['w_attn_branch'][perm, :].astype(BF16)
    wlb = p['w_lru_branch'].astype(BF16)
    wout = p['w_out'].astype(BF16)
    wbd = _block_diag_gates(p['w_lru_a'], p['w_lru_x'])
    row = lambda v: v.reshape(1, -1)
    wr = jnp.concatenate([p['w_router_group'], p['w_router_expert'],
                          jnp.zeros((D_MODEL, ROUTE_LANES - N_GROUPS - N_EXPERTS), F32)], axis=1)
    wr_hi = wr.astype(BF16)
    wr_lo = (wr - wr_hi.astype(F32)).astype(BF16)
    br = jnp.concatenate([p['b_router_group'], p['b_router_expert'],
                          jnp.zeros((ROUTE_LANES - N_GROUPS - N_EXPERTS,), F32)]).reshape(1, -1)
    wg = p['w_exp_gate'].astype(BF16)
    wu = p['w_exp_up'].astype(BF16)
    wd = p['w_exp_down'].astype(BF16)

    def mods(mod):
        return [mod[:, k * D_MODEL:(k + 1) * D_MODEL] for k in range(6)]
    sh_a_p, sc_a_p, gt_a_p, sh_f_p, sc_f_p, gt_f_p = [m.reshape(batch, 1, D_MODEL) for m in mods(mod_p)]
    sh_a_s, sc_a_s, gt_a_s, sh_f_s, sc_f_s, gt_f_s = [jnp.tile(m, (t_new, 1)) for m in mods(mod_s)]

    tm_p = min(ROW_TILE, n_p)
    cos_p, sin_p = _rope_tables(jnp.arange(seq, dtype=I32))
    q_p, k_p, v_p, xl_p, yl_p, sga_p, sgr_p = _inproj(
        xp.reshape(n_p, D_MODEL), sc_a_p, sh_a_p, row(p['g_pre_mix']), cos_p, sin_p, w_in_b,
        rows_per_mod=seq, pos_tiles=seq // tm_p)
    pos_s = jnp.repeat(PAST_LEN_ + jnp.arange(t_new, dtype=I32), nseq)
    cos_s, sin_s = _rope_tables(pos_s)
    q_s, k_s, v_s, xl_s, yl_s, sga_s, sgr_s = _inproj(
        xs_tm, sc_a_s, sh_a_s, row(p['g_pre_mix']), cos_s, sin_s, w_in_b, rows_per_mod=0,
        pos_tiles=n_s // min(ROW_TILE, n_s))

    sinks_perm = p['sinks']
    oa_p = _attn_prompt(sinks_perm, q_p, k_p, v_p, batch, seq)
    rows = t_new * GQA_GROUP
    q_s3 = q_s.reshape(t_new, nseq, GQA_GROUP, KV_DIM).transpose(1, 0, 2, 3).reshape(nseq, rows, KV_DIM)
    kn = k_s.reshape(t_new, nseq, KV_DIM).transpose(1, 0, 2)
    vn = v_s.reshape(t_new, nseq, KV_DIM).transpose(1, 0, 2)
    kc = ck.reshape(nseq, WINDOW, KV_DIM)
    vc = cv.reshape(nseq, WINDOW, KV_DIM)
    sink_rows = jnp.tile(p['sinks'].reshape(N_KV_HEADS, 1, GQA_GROUP), (1, t_new, 1)).reshape(N_KV_HEADS, rows, 1)
    oa_s3 = _attn_sample(sink_rows, q_s3, kn, vn, kc, vc)
    oa_s = oa_s3.reshape(nseq, t_new, Q_DIM).transpose(1, 0, 2).reshape(n_s, Q_DIM)

    lru_w = (p['w_conv'], row(p['b_conv']), wbd, row(p['b_lru_a']), row(p['b_lru_x']), row(p['lru_lambda']))
    ol_p, hlast_p = _lru_prompt(xl_p, yl_p, *lru_w, batch, seq)
    ol_s3, hlast_s = _lru_sample(xl_s.reshape(t_new, nseq, LRU_WIDTH), yl_s.reshape(t_new, nseq, LRU_WIDTH),
                                 cconv.transpose(1, 0, 2), ch, *lru_w)
    ol_s = ol_s3.reshape(n_s, LRU_WIDTH)

    tm_post = min(ROW_TILE, n_p)
    tri = jnp.tril(jnp.ones((tm_post, tm_post), F32), -1).astype(BF16)
    post_w = (wab, wlb, wout, row(p['g_post_mix']), row(p['g_pre_ffn']), wr_hi, wr_lo, br)
    zero_cnt = jnp.zeros((1, ROUTE_LANES), F32)
    x1_p, h2_p, route_p, stat_p, cnt_p = _post(oa_p, ol_p, sga_p, sgr_p, xp.reshape(n_p, D_MODEL),
                                               gt_a_p, sc_f_p, sh_f_p, *post_w, tri, zero_cnt, rows_per_mod=seq)
    tm_s = min(ROW_TILE, n_s)
    tri_s = tri if tm_s == tm_post else jnp.tril(jnp.ones((tm_s, tm_s), F32), -1).astype(BF16)
    x1_s, h2_s, route_s, stat_s, cnt_all = _post(oa_s, ol_s, sga_s, sgr_s, xs_tm,
                                                 gt_a_s, sc_f_s, sh_f_s, *post_w, tri_s, cnt_p, rows_per_mod=0)

    e_lanes = slice(N_GROUPS, N_GROUPS + N_EXPERTS)
    totals = cnt_all[0, e_lanes].astype(I32)
    stats = jnp.concatenate([stat_p, stat_s], axis=0).reshape(-1, SUBLANES, ROUTE_LANES)
    cnt8 = stats[:, 0, e_lanes].astype(I32).reshape(-1)
    goff = stats[:, 1, e_lanes].astype(I32).reshape(-1)
    tiles_p = n_p // tm_post
    n_tiles = tiles_p + n_s // tm_s
    max_rows = 2 * (n_p + n_s) + n_tiles * N_EXPERTS * (GRANULE - 1) + N_EXPERTS * (EXPERT_ROWS - GRANULE)
    n_blocks = -(-max_rows // EXPERT_ROWS)
    n_slots = n_blocks * EXPERT_ROWS
    zero_blk = jnp.zeros((EXPERT_ROWS, HALF_D), U32)
    xs = _dispatch(totals, cnt8, goff, route_p, h2_p, zero_blk, None, n_slots, 0, fill_padding=True)
    xs = _dispatch(totals, cnt8, goff, route_s, h2_s, zero_blk, xs, n_slots, tiles_p, fill_padding=False)
    block_e, meta = _plan(totals, n_blocks)
    ys = _experts(block_e, meta, xs, wg, wu, wd)
    y_p = _combine(totals, cnt8, goff, route_p, x1_p, gt_f_p, row(p['g_post_ffn']), ys, seq, 0)
    y_s = _combine(totals, cnt8, goff, route_s, x1_s, gt_f_s, row(p['g_post_ffn']), ys, 0, tiles_p)

    k_new_p = k_p.reshape(batch, seq, N_KV_HEADS, HEAD_DIM)[:, -WINDOW:]
    v_new_p = v_p.reshape(batch, seq, N_KV_HEADS, HEAD_DIM)[:, -WINDOW:]
    conv_p = xl_p.reshape(batch, seq, LRU_WIDTH)[:, -(CONV_WIDTH - 1):].astype(F32)
    h_p = hlast_p.reshape(batch, LRU_WIDTH)
    k_new_s = jnp.concatenate([ck, kn.reshape(nseq, t_new, N_KV_HEADS, HEAD_DIM)], axis=1)[:, -WINDOW:]
    v_new_s = jnp.concatenate([cv, vn.reshape(nseq, t_new, N_KV_HEADS, HEAD_DIM)], axis=1)[:, -WINDOW:]
    xl_s3 = xl_s.reshape(t_new, nseq, LRU_WIDTH).transpose(1, 0, 2).astype(F32)
    conv_s = jnp.concatenate([cconv, xl_s3], axis=1)[:, -(CONV_WIDTH - 1):]
    return (y_p.reshape(batch, seq, D_MODEL), y_s, k_new_p, v_new_p, conv_p, h_p,
            k_new_s, v_new_s, conv_s, hlast_s)


PAST_LEN_ = 16384

PARAM_NAMES = ('w_ada', 'b_ada', 'g_pre_mix', 'g_post_mix', 'g_pre_ffn', 'g_post_ffn', 'w_in', 'sinks',
               'w_conv', 'b_conv', 'w_lru_a', 'b_lru_a', 'w_lru_x', 'b_lru_x', 'lru_lambda',
               'w_attn_branch', 'w_lru_branch', 'w_out', 'w_router_group', 'b_router_group',
               'w_router_expert', 'b_router_expert', 'w_exp_gate', 'w_exp_up', 'w_exp_down')


def kernel(x_prompt, x_sample, cache_k_win, cache_v_win, state_conv, state_h, c_prompt, c_sample, w_ada, b_ada, g_pre_mix, g_post_mix, g_pre_ffn, g_post_ffn, w_in, sinks, w_conv, b_conv, w_lru_a, b_lru_a, w_lru_x, b_lru_x, lru_lambda, w_attn_branch, w_lru_branch, w_out, w_router_group, b_router_group, w_router_expert, b_router_expert, w_exp_gate, w_exp_up, w_exp_down):
    weights = (w_ada, b_ada, g_pre_mix, g_post_mix, g_pre_ffn, g_post_ffn, w_in, sinks,
               w_conv, b_conv, w_lru_a, b_lru_a, w_lru_x, b_lru_x, lru_lambda,
               w_attn_branch, w_lru_branch, w_out, w_router_group, b_router_group,
               w_router_expert, b_router_expert, w_exp_gate, w_exp_up, w_exp_down)
    depth = w_ada.shape[0]
    batch = x_prompt.shape[0]
    nseq, t_new, _ = x_sample.shape
    y_p = x_prompt
    y_s = x_sample.transpose(1, 0, 2).reshape(t_new * nseq, D_MODEL)
    c_all = jnp.concatenate([c_prompt, c_sample], axis=0)
    outs = [[] for _ in range(8)]
    for layer in range(depth):
        p = {name: w[layer] for name, w in zip(PARAM_NAMES, weights)}
        mod = _ada(c_all, p['w_ada'].astype(BF16), p['b_ada'].reshape(1, -1))
        res = _layer_forward(y_p, y_s, cache_k_win[layer], cache_v_win[layer], state_conv[layer],
                             state_h[layer], mod[:batch], mod[batch:], p)
        y_p, y_s = res[0], res[1]
        for o, r in zip(outs, res[2:]):
            o.append(r)
    y_sample = y_s.reshape(t_new, nseq, D_MODEL).transpose(1, 0, 2)
    return (y_p, y_sample) + tuple(jnp.stack(o) for o in outs)
```

```python
import functools

import jax
import jax.numpy as jnp
from jax import lax
from jax.experimental import pallas as pl
from jax.experimental.pallas import tpu as pltpu

F32 = jnp.float32
BF16 = jnp.bfloat16
I32 = jnp.int32

D_MODEL = 1024
N_HEADS = 16
HEAD_DIM = 64
N_KV_HEADS = 4
GQA_GROUP = 4
WINDOW = 128
ROPE_THETA = 10000.0
NEG_INF = -1e30
LRU_WIDTH = 1024
LRU_BLOCKS = 16
LRU_BLOCK_W = 64
CONV_WIDTH = 4
LRU_C = 8.0
N_GROUPS = 4
EXPERTS_PER_GROUP = 8
N_EXPERTS = 32
D_EXPERT = 512
MOE_BLOCK = 128
NORM_EPS = 1e-6
Q_DIM = N_HEADS * HEAD_DIM
KV_DIM = N_KV_HEADS * HEAD_DIM
IN_DIM = Q_DIM + 2 * KV_DIM + 2 * LRU_WIDTH + 2 * D_MODEL

LANES = 128
SUBLANES = 8
MXU_DIM = 256
VMEM_LIMIT = 56 * 1024 * 1024

ROW_TILE = 512
LRU_TILE = 256
ROUTE_LANES = LANES
GRANULE = SUBLANES
EXPERT_ROWS = 512
HALF_D = D_MODEL // 2
U32 = jnp.uint32


def _cparams(n_axes, vmem=VMEM_LIMIT):
    return pltpu.CompilerParams(dimension_semantics=("arbitrary",) * n_axes, vmem_limit_bytes=vmem)


def _rms(x, g):
    ms = jnp.mean(x * x, axis=-1, keepdims=True)
    return x * lax.rsqrt(ms + NORM_EPS) * g


def _sigmoid(x):
    return 1.0 / (1.0 + jnp.exp(-x))


def _dot(a, b):
    return jnp.dot(a, b, preferred_element_type=F32)


def _ada_kernel(c_ref, w_ref, b_ref, o_ref):
    c = c_ref[...]
    s = (c * _sigmoid(c)).astype(BF16)
    o_ref[...] = _dot(s, w_ref[...]) + b_ref[...]


def _ada(c_all, w_ada, b_ada):
    r = c_all.shape[0]
    n = w_ada.shape[1]
    return pl.pallas_call(
        _ada_kernel,
        grid=(n // D_MODEL,),
        in_specs=[pl.BlockSpec((r, D_MODEL), lambda j: (0, 0)),
                  pl.BlockSpec((D_MODEL, D_MODEL), lambda j: (0, j)),
                  pl.BlockSpec((1, D_MODEL), lambda j: (0, j))],
        out_specs=pl.BlockSpec((r, D_MODEL), lambda j: (0, j)),
        out_shape=jax.ShapeDtypeStruct((r, n), F32),
        compiler_params=_cparams(1),
        name="ada_mod",
    )(c_all, w_ada, b_ada)


def _inproj_kernel(x_ref, sc_ref, sh_ref, g_ref, cos_ref, sin_ref, w_ref,
                   q_ref, k_ref, v_ref, xl_ref, yl_ref, sga_ref, sgr_ref):
    x = x_ref[...]
    h = _rms(x, g_ref[...]) * (1.0 + sc_ref[...]) + sh_ref[...]
    hb = h.astype(BF16)
    cos = cos_ref[...]
    sin = sin_ref[...]
    lane = lax.broadcasted_iota(I32, cos.shape, 1)
    first_half = (lane % HEAD_DIM) < (HEAD_DIM // 2)

    def rope(t):
        rot = jnp.where(first_half, pltpu.roll(t, LANES - HEAD_DIM // 2, 1), pltpu.roll(t, HEAD_DIM // 2, 1))
        return t * cos + rot * sin

    o1 = Q_DIM
    o2 = o1 + KV_DIM
    o3 = o2 + KV_DIM
    o4 = o3 + LRU_WIDTH
    o5 = o4 + LRU_WIDTH
    o6 = o5 + D_MODEL
    qf = _dot(hb, w_ref[:, 0:o1])
    for c in range(Q_DIM // LANES):
        q_ref[:, c * LANES:(c + 1) * LANES] = rope(qf[:, c * LANES:(c + 1) * LANES]).astype(BF16)
    kf = _dot(hb, w_ref[:, o1:o2])
    for c in range(KV_DIM // LANES):
        k_ref[:, c * LANES:(c + 1) * LANES] = rope(kf[:, c * LANES:(c + 1) * LANES])
    v_ref[...] = _dot(hb, w_ref[:, o2:o3])
    xl_ref[...] = _dot(hb, w_ref[:, o3:o4]).astype(BF16)
    yl_ref[...] = _dot(hb, w_ref[:, o4:o5]).astype(BF16)
    sga_ref[...] = _sigmoid(_dot(hb, w_ref[:, o5:o6])).astype(BF16)
    sgr_ref[...] = _sigmoid(_dot(hb, w_ref[:, o6:IN_DIM])).astype(BF16)


def _inproj(x, sc, sh, g, cos, sin, w_in, rows_per_mod, pos_tiles):
    n = x.shape[0]
    tm = min(ROW_TILE, n)
    if rows_per_mod:
        tiles_per_mod = rows_per_mod // tm
        mod_spec = pl.BlockSpec((None, 1, D_MODEL), lambda i: (i // tiles_per_mod, 0, 0))
    else:
        mod_spec = pl.BlockSpec((tm, D_MODEL), lambda i: (i, 0))
    row = lambda w: pl.BlockSpec((tm, w), lambda i: (i, 0))
    outs = [(Q_DIM, BF16), (KV_DIM, F32), (KV_DIM, F32), (LRU_WIDTH, BF16), (LRU_WIDTH, BF16),
            (D_MODEL, BF16), (D_MODEL, BF16)]
    return pl.pallas_call(
        _inproj_kernel,
        grid=(n // tm,),
        in_specs=[row(D_MODEL), mod_spec, mod_spec,
                  pl.BlockSpec((1, D_MODEL), lambda i: (0, 0)),
                  pl.BlockSpec((tm, LANES), lambda i: (i % pos_tiles, 0)),
                  pl.BlockSpec((tm, LANES), lambda i: (i % pos_tiles, 0)),
                  pl.BlockSpec((D_MODEL, IN_DIM), lambda i: (0, 0))],
        out_specs=[row(w) for w, _ in outs],
        out_shape=[jax.ShapeDtypeStruct((n, w), dt) for w, dt in outs],
        compiler_params=_cparams(1),
        name="in_proj",
    )(x, sc, sh, g, cos, sin, w_in)


def _head_masks(shape):
    lane = lax.broadcasted_iota(I32, shape, 1)
    return [(lane // HEAD_DIM) == h for h in range(N_KV_HEADS)]


def _attention_core(q_perm, kall, vall, valid, sink_of, rows):
    masks_b = _head_masks((rows, KV_DIM))
    zero_b = jnp.zeros((rows, KV_DIM), BF16)
    pieces = []
    for h in range(N_KV_HEADS):
        for g in range(GQA_GROUP):
            pieces.append(jnp.where(masks_b[h], q_perm[g], zero_b))
    q_big = jnp.concatenate(pieces, axis=0)
    s_all = lax.dot_general(q_big, kall, (((1,), (1,)), ((), ())), preferred_element_type=F32)
    s_all = s_all * (HEAD_DIM ** -0.5)
    p_chunks, inv_chunks = [], []
    for h in range(N_KV_HEADS):
        for g in range(GQA_GROUP):
            c = h * GQA_GROUP + g
            s = jnp.where(valid, s_all[c * rows:(c + 1) * rows], NEG_INF)
            sink = sink_of(h, g)
            m = jnp.maximum(jnp.max(s, axis=-1, keepdims=True), sink)
            p = jnp.exp(s - m)
            denom = jnp.sum(p, axis=-1, keepdims=True) + jnp.exp(sink - m)
            p_chunks.append(p.astype(BF16))
            inv_chunks.append(1.0 / denom)
    o_all = _dot(jnp.concatenate(p_chunks, axis=0), vall)
    outs = []
    for g in range(GQA_GROUP):
        acc = jnp.zeros((rows, KV_DIM), F32)
        for h in range(N_KV_HEADS):
            c = h * GQA_GROUP + g
            acc = acc + jnp.where(masks_b[h], o_all[c * rows:(c + 1) * rows] * inv_chunks[c], 0.0)
        outs.append(acc)
    return outs


def _attn_prompt_kernel(sink_ref, q_ref, kc_ref, kp_ref, vc_ref, vp_ref, o_ref):
    j = pl.program_id(1)
    kall = jnp.concatenate([kp_ref[...], kc_ref[...]], axis=0).astype(BF16)
    vall = jnp.concatenate([vp_ref[...], vc_ref[...]], axis=0).astype(BF16)
    qi = lax.broadcasted_iota(I32, (WINDOW, 2 * WINDOW), 0)
    kj = lax.broadcasted_iota(I32, (WINDOW, 2 * WINDOW), 1)
    dist = qi + WINDOW - kj
    valid = (dist >= 0) & (dist <= WINDOW) & ((kj >= WINDOW) | (j > 0))
    q_perm = [q_ref[:, g * KV_DIM:(g + 1) * KV_DIM] for g in range(GQA_GROUP)]
    outs = _attention_core(q_perm, kall, vall, valid, lambda h, g: sink_ref[h * GQA_GROUP + g], WINDOW)
    for g in range(GQA_GROUP):
        o_ref[:, g * KV_DIM:(g + 1) * KV_DIM] = outs[g].astype(BF16)


def _attn_prompt(sinks, q, k, v, batch, seq):
    nb = seq // WINDOW
    cur = lambda w: pl.BlockSpec((WINDOW, w), lambda b, j: (b * nb + j, 0))
    prev = lambda w: pl.BlockSpec((WINDOW, w), lambda b, j: (b * nb + jnp.maximum(j - 1, 0), 0))
    return pl.pallas_call(
        _attn_prompt_kernel,
        grid=(batch, nb),
        in_specs=[pl.BlockSpec(memory_space=pltpu.SMEM),
                  cur(Q_DIM), cur(KV_DIM), prev(KV_DIM), cur(KV_DIM), prev(KV_DIM)],
        out_specs=cur(Q_DIM),
        out_shape=jax.ShapeDtypeStruct((batch * seq, Q_DIM), BF16),
        compiler_params=_cparams(2),
        name="attn_prompt",
    )(sinks, q, k, k, v, v)


SEQ_PER_STEP = 8


def _attn_sample_kernel(sink_ref, q_ref, kn_ref, vn_ref, kc_ref, vc_ref, o_ref, kbuf, vbuf, *, t_new):
    rows = GQA_GROUP * t_new
    kbuf[WINDOW:2 * WINDOW, :] = jnp.zeros((WINDOW, KV_DIM), F32)
    vbuf[WINDOW:2 * WINDOW, :] = jnp.zeros((WINDOW, KV_DIM), F32)
    ri = lax.broadcasted_iota(I32, (rows, 2 * WINDOW), 0)
    kj = lax.broadcasted_iota(I32, (rows, 2 * WINDOW), 1)
    tq = ri // GQA_GROUP
    valid = (kj >= tq) & (kj <= tq + WINDOW) & (kj < WINDOW + t_new)
    for s in range(SEQ_PER_STEP):
        kbuf[0:WINDOW, :] = kc_ref[s]
        vbuf[0:WINDOW, :] = vc_ref[s]
        kbuf[WINDOW:WINDOW + t_new, :] = kn_ref[s]
        vbuf[WINDOW:WINDOW + t_new, :] = vn_ref[s]
        kall = kbuf[...].astype(BF16)
        vall = vbuf[...].astype(BF16)
        qs = q_ref[s]
        masks_b = _head_masks((rows, KV_DIM))
        zero_b = jnp.zeros((rows, KV_DIM), BF16)
        q_big = jnp.concatenate([jnp.where(masks_b[h], qs, zero_b) for h in range(N_KV_HEADS)], axis=0)
        s_all = lax.dot_general(q_big, kall, (((1,), (1,)), ((), ())), preferred_element_type=F32)
        s_all = s_all * (HEAD_DIM ** -0.5)
        acc = jnp.zeros((rows, KV_DIM), F32)
        p_chunks, inv_chunks = [], []
        for h in range(N_KV_HEADS):
            sc = jnp.where(valid, s_all[h * rows:(h + 1) * rows], NEG_INF)
            sink = sink_ref[h]
            m = jnp.maximum(jnp.max(sc, axis=-1, keepdims=True), sink)
            p = jnp.exp(sc - m)
            denom = jnp.sum(p, axis=-1, keepdims=True) + jnp.exp(sink - m)
            p_chunks.append(p.astype(BF16))
            inv_chunks.append(1.0 / denom)
        o_all = _dot(jnp.concatenate(p_chunks, axis=0), vall)
        for h in range(N_KV_HEADS):
            acc = acc + jnp.where(masks_b[h], o_all[h * rows:(h + 1) * rows] * inv_chunks[h], 0.0)
        o_ref[s] = acc.astype(BF16)


def _attn_sample(sink_rows, q, kn, vn, kc, vc):
    nseq, rows, _ = q.shape
    t_new = kn.shape[1]
    sb = SEQ_PER_STEP
    blk = lambda r: pl.BlockSpec((sb, r, KV_DIM), lambda i: (i, 0, 0))
    return pl.pallas_call(
        functools.partial(_attn_sample_kernel, t_new=t_new),
        grid=(nseq // sb,),
        in_specs=[pl.BlockSpec((N_KV_HEADS, rows, 1), lambda i: (0, 0, 0)),
                  blk(rows), blk(t_new), blk(t_new), blk(WINDOW), blk(WINDOW)],
        out_specs=blk(rows),
        out_shape=jax.ShapeDtypeStruct((nseq, rows, KV_DIM), BF16),
        scratch_shapes=[pltpu.VMEM((2 * WINDOW, KV_DIM), F32), pltpu.VMEM((2 * WINDOW, KV_DIM), F32)],
        compiler_params=_cparams(1),
        name="attn_sample",
    )(sink_rows, q, kn, vn, kc, vc)


def _gelu_tanh(x):
    return 0.5 * x * (1.0 + jnp.tanh(0.7978845608028654 * (x + 0.044715 * x * x * x)))


def _lru_gates(xc, wbd_ref, ba, bx, lam):
    xcb = xc.astype(BF16)
    r_parts, i_parts = [], []
    for gidx in range(LRU_WIDTH // MXU_DIM):
        z = _dot(xcb[:, gidx * MXU_DIM:(gidx + 1) * MXU_DIM], wbd_ref[gidx])
        r_parts.append(z[:, :MXU_DIM])
        i_parts.append(z[:, MXU_DIM:])
    r = _sigmoid(jnp.concatenate(r_parts, axis=1) + ba)
    i = _sigmoid(jnp.concatenate(i_parts, axis=1) + bx)
    softplus_neg_lam = jnp.maximum(-lam, 0.0) + jnp.log1p(jnp.exp(-jnp.abs(lam)))
    log_a = -LRU_C * r * softplus_neg_lam
    a = jnp.exp(log_a)
    u = jnp.sqrt(1.0 - jnp.exp(2.0 * log_a)) * (i * xc)
    return a, u


def _lru_prompt_kernel(xl_ref, yl_ref, wc_ref, bc_ref, wbd_ref, ba_ref, bx_ref, lam_ref,
                       o_ref, hlast_ref, xbuf, hcar):
    j = pl.program_id(1)
    t = xl_ref.shape[0]
    w = LRU_WIDTH

    @pl.when(j == 0)
    def _():
        xbuf[0:SUBLANES, :] = jnp.zeros((SUBLANES, w), F32)
        hcar[...] = jnp.zeros((1, w), F32)

    x = xl_ref[...].astype(F32)
    xbuf[SUBLANES:SUBLANES + t, :] = x
    xc = x * wc_ref[CONV_WIDTH - 1:CONV_WIDTH, :] + bc_ref[...]
    for k in range(1, CONV_WIDTH):
        xc = xc + xbuf[SUBLANES - k:SUBLANES - k + t, :] * wc_ref[CONV_WIDTH - 1 - k:CONV_WIDTH - k, :]
    xbuf[0:SUBLANES, :] = x[t - SUBLANES:t, :]

    a, u = _lru_gates(xc, wbd_ref, ba_ref[...], bx_ref[...], lam_ref[...])

    ng = t // SUBLANES
    a3 = a.reshape(ng, SUBLANES, w)
    u3 = u.reshape(ng, SUBLANES, w)
    row = lax.broadcasted_iota(I32, (ng, SUBLANES, w), 1)
    d = 1
    while d < SUBLANES:
        a_s = jnp.where(row >= d, pltpu.roll(a3, d, 1), 1.0)
        u_s = jnp.where(row >= d, pltpu.roll(u3, d, 1), 0.0)
        u3 = a3 * u_s + u3
        a3 = a3 * a_s
        d *= 2
    carry = hcar[...]
    hs = []
    for gi in range(ng):
        hg = a3[gi] * carry + u3[gi]
        hs.append(hg)
        carry = hg[SUBLANES - 1:SUBLANES, :]
    hcar[...] = carry
    h = jnp.concatenate(hs, axis=0)
    o_ref[...] = (h * _gelu_tanh(yl_ref[...].astype(F32))).astype(BF16)
    hlast_ref[...] = carry


def _lru_prompt(xl, yl, wc, bc, wbd, ba, bx, lam, batch, seq):
    t = min(LRU_TILE, seq)
    nt = seq // t
    tile = pl.BlockSpec((t, LRU_WIDTH), lambda b, j: (b * nt + j, 0))
    full = lambda shp: pl.BlockSpec(shp, lambda b, j: (0,) * len(shp))
    return pl.pallas_call(
        _lru_prompt_kernel,
        grid=(batch, nt),
        in_specs=[tile, tile, full((CONV_WIDTH, LRU_WIDTH)), full((1, LRU_WIDTH)),
                  full(wbd.shape), full((1, LRU_WIDTH)), full((1, LRU_WIDTH)), full((1, LRU_WIDTH))],
        out_specs=[tile, pl.BlockSpec((None, 1, LRU_WIDTH), lambda b, j: (b, 0, 0))],
        out_shape=[jax.ShapeDtypeStruct((batch * seq, LRU_WIDTH), BF16),
                   jax.ShapeDtypeStruct((batch, 1, LRU_WIDTH), F32)],
        scratch_shapes=[pltpu.VMEM((SUBLANES + t, LRU_WIDTH), F32), pltpu.VMEM((1, LRU_WIDTH), F32)],
        compiler_params=_cparams(2),
        name="lru_prompt",
    )(xl, yl, wc, bc, wbd, ba, bx, lam)


def _lru_sample_kernel(xl_ref, yl_ref, cs_ref, h0_ref, wc_ref, bc_ref, wbd_ref, ba_ref, bx_ref, lam_ref,
                       o_ref, hlast_ref):
    t_new, nseq, w = xl_ref.shape
    xp = [cs_ref[k] for k in range(CONV_WIDTH - 1)] + [xl_ref[k].astype(F32) for k in range(t_new)]
    xcs = []
    for t in range(t_new):
        acc = bc_ref[...] + xp[t] * wc_ref[0:1, :]
        for k in range(1, CONV_WIDTH):
            acc = acc + xp[t + k] * wc_ref[k:k + 1, :]
        xcs.append(acc)
    xc = jnp.concatenate(xcs, axis=0)
    a, u = _lru_gates(xc, wbd_ref, ba_ref[...], bx_ref[...], lam_ref[...])
    h = h0_ref[...]
    for t in range(t_new):
        h = a[t * nseq:(t + 1) * nseq] * h + u[t * nseq:(t + 1) * nseq]
        o_ref[t] = (h * _gelu_tanh(yl_ref[t].astype(F32))).astype(BF16)
    hlast_ref[...] = h


def _lru_sample(xl, yl, cs, h0, wc, bc, wbd, ba, bx, lam):
    t_new, nseq, w = xl.shape
    full = lambda shp: pl.BlockSpec(shp, lambda i: (0,) * len(shp))
    args = (xl, yl, cs, h0, wc, bc, wbd, ba, bx, lam)
    return pl.pallas_call(
        _lru_sample_kernel,
        grid=(1,),
        in_specs=[full(a.shape) for a in args],
        out_specs=[full((t_new, nseq, w)), full((nseq, w))],
        out_shape=[jax.ShapeDtypeStruct((t_new, nseq, w), BF16), jax.ShapeDtypeStruct((nseq, w), F32)],
        compiler_params=_cparams(1),
        name="lru_sample",
    )(*args)


def _post_kernel(oa_ref, ol_ref, sga_ref, sgr_ref, x_ref, gta_ref, scf_ref, shf_ref,
                 wab_ref, wlb_ref, wout_ref, gpm_ref, gpf_ref, wrh_ref, wrl_ref, br_ref, tri_ref, cin_ref,
                 x1_ref, h2_ref, route_ref, stat_ref, cnt_ref, carry):
    i = pl.program_id(0)

    @pl.when(i == 0)
    def _():
        carry[...] = cin_ref[...]

    b_attn = _dot(oa_ref[...], wab_ref[...])
    b_lru = _dot(ol_ref[...], wlb_ref[...])
    merged = sga_ref[...].astype(F32) * b_attn + sgr_ref[...].astype(F32) * b_lru
    mix = _dot(merged.astype(BF16), wout_ref[...])
    x1 = x_ref[...] + gta_ref[...] * _rms(mix, gpm_ref[...])
    x1_ref[...] = x1
    h2 = _rms(x1, gpf_ref[...]) * (1.0 + scf_ref[...]) + shf_ref[...]
    h2_ref[...] = h2.astype(BF16)

    h_hi = h2.astype(BF16)
    h_lo = (h2 - h_hi.astype(F32)).astype(BF16)
    logits = _dot(h_hi, wrh_ref[...]) + (_dot(h_lo, wrh_ref[...]) + _dot(h_hi, wrl_ref[...])) + br_ref[...]

    tm = logits.shape[0]
    lane = lax.broadcasted_iota(I32, (tm, ROUTE_LANES), 1)
    big = jnp.int32(ROUTE_LANES)
    is_g = lane < N_GROUPS
    lg = jnp.where(is_g, logits, NEG_INF)
    mg = jnp.max(lg, axis=-1, keepdims=True)
    g_star = jnp.min(jnp.where(lg == mg, lane, big), axis=-1, keepdims=True)
    p_star = 1.0 / jnp.sum(jnp.where(is_g, jnp.exp(lg - mg), 0.0), axis=-1, keepdims=True)
    lo = N_GROUPS + g_star * EXPERTS_PER_GROUP
    in_grp = (lane >= lo) & (lane < lo + EXPERTS_PER_GROUP)
    le = jnp.where(in_grp, logits, NEG_INF)
    m1 = jnp.max(le, axis=-1, keepdims=True)
    i1 = jnp.min(jnp.where(le == m1, lane, big), axis=-1, keepdims=True)
    le2 = jnp.where(lane == i1, NEG_INF, le)
    m2 = jnp.max(le2, axis=-1, keepdims=True)
    i2 = jnp.min(jnp.where(le2 == m2, lane, big), axis=-1, keepdims=True)
    e2x = jnp.exp(m2 - m1)
    wsum = 1.0 + e2x
    w1 = (1.0 / wsum) * p_star
    w2 = (e2x / wsum) * p_star

    oh1 = lane == i1
    oh2 = lane == i2
    cnt = jnp.where(oh1 | oh2, 1.0, 0.0)
    excl = _dot(tri_ref[...], cnt.astype(BF16))
    per_e = jnp.sum(cnt, axis=0, keepdims=True)
    pad8 = jnp.floor((per_e + (GRANULE - 1.0)) * (1.0 / GRANULE)) * GRANULE
    incl = jnp.broadcast_to(pad8, (SUBLANES, ROUTE_LANES))
    lane8 = lax.broadcasted_iota(I32, (SUBLANES, ROUTE_LANES), 1)
    d = 1
    while d < ROUTE_LANES:
        incl = incl + jnp.where(lane8 >= d, pltpu.roll(incl, d, 1), 0.0)
        d *= 2
    seg_start = incl[0:1, :] - pad8
    pos = excl + seg_start
    s1 = jnp.sum(jnp.where(oh1, pos, 0.0), axis=-1, keepdims=True)
    s2 = jnp.sum(jnp.where(oh2, pos, 0.0), axis=-1, keepdims=True)
    rec = jnp.where(lane == 0, s1, 0.0)
    rec = jnp.where(lane == 1, s2, rec)
    rec = jnp.where(lane == 2, w1, rec)
    rec = jnp.where(lane == 3, w2, rec)
    route_ref[...] = rec
    srow = lax.broadcasted_iota(I32, (SUBLANES, ROUTE_LANES), 0)
    stat_ref[...] = jnp.where(srow == 0, pad8, jnp.where(srow == 1, carry[...], 0.0))
    carry[...] = carry[...] + pad8
    cnt_ref[...] = carry[...]


def _post(oa, ol, sga, sgr, x, gta, scf, shf, wab, wlb, wout, gpm, gpf, wrh, wrl, br, tri, cin, rows_per_mod):
    n = x.shape[0]
    tm = min(ROW_TILE, n)
    if rows_per_mod:
        tiles_per_mod = rows_per_mod // tm
        mod_spec = pl.BlockSpec((None, 1, D_MODEL), lambda i: (i // tiles_per_mod, 0, 0))
    else:
        mod_spec = pl.BlockSpec((tm, D_MODEL), lambda i: (i, 0))
    row = lambda w: pl.BlockSpec((tm, w), lambda i: (i, 0))
    full = lambda a: pl.BlockSpec(a.shape, lambda i: (0,) * a.ndim)
    return pl.pallas_call(
        _post_kernel,
        grid=(n // tm,),
        in_specs=[row(Q_DIM), row(LRU_WIDTH), row(D_MODEL), row(D_MODEL), row(D_MODEL),
                  mod_spec, mod_spec, mod_spec,
                  full(wab), full(wlb), full(wout), full(gpm), full(gpf), full(wrh), full(wrl), full(br),
                  full(tri), full(cin)],
        out_specs=[row(D_MODEL), row(D_MODEL), row(ROUTE_LANES),
                   pl.BlockSpec((SUBLANES, ROUTE_LANES), lambda i: (i, 0)),
                   pl.BlockSpec((1, ROUTE_LANES), lambda i: (0, 0))],
        out_shape=[jax.ShapeDtypeStruct((n, D_MODEL), F32), jax.ShapeDtypeStruct((n, D_MODEL), BF16),
                   jax.ShapeDtypeStruct((n, ROUTE_LANES), F32),
                   jax.ShapeDtypeStruct((n // tm * SUBLANES, ROUTE_LANES), F32),
                   jax.ShapeDtypeStruct((1, ROUTE_LANES), F32)],
        scratch_shapes=[pltpu.VMEM((1, ROUTE_LANES), F32)],
        compiler_params=_cparams(1),
        name="post_mix",
    )(oa, ol, sga, sgr, x, gta, scf, shf, wab, wlb, wout, gpm, gpf, wrh, wrl, br, tri, cin)


def _padded(c):
    return ((c + (EXPERT_ROWS - 1)) // EXPERT_ROWS) * EXPERT_ROWS


def _sorted_rows(tm):
    r = 2 * tm + N_EXPERTS * (GRANULE - 1)
    return -(-r // MXU_DIM) * MXU_DIM


def _plan_kernel(tot_ref, be_ref, meta_ref, *, n_blocks):
    def fill(j, _):
        be_ref[j] = N_EXPERTS - 1
        return 0
    lax.fori_loop(0, n_blocks, fill, 0)

    def per_expert(e, nb):
        k = _padded(tot_ref[e]) // EXPERT_ROWS

        def put(b, _):
            be_ref[nb + b] = e
            return 0
        lax.fori_loop(0, k, put, 0)
        return nb + k
    n_active = lax.fori_loop(0, N_EXPERTS, per_expert, 0)
    meta_ref[0] = n_active


def _plan(totals, n_blocks):
    return pl.pallas_call(
        functools.partial(_plan_kernel, n_blocks=n_blocks),
        in_specs=[pl.BlockSpec(memory_space=pltpu.SMEM)],
        out_specs=[pl.BlockSpec(memory_space=pltpu.SMEM), pl.BlockSpec(memory_space=pltpu.SMEM)],
        out_shape=[jax.ShapeDtypeStruct((n_blocks,), I32), jax.ShapeDtypeStruct((1,), I32)],
        name="moe_plan",
    )(totals)


def _expert_starts(tot_ref, pstart):
    def body(e, acc):
        pstart[e] = acc
        return acc + _padded(tot_ref[e])
    return lax.fori_loop(0, N_EXPERTS, body, 0)


def _rows_copy(src, s, dst, d, n, sem):
    return pltpu.make_async_copy(src.at[pl.ds(pl.multiple_of(s, GRANULE), n)],
                                 dst.at[pl.ds(pl.multiple_of(d, GRANULE), n)], sem)


def _segment_copies(tot_ref, cnt8_ref, goff_ref, pstart, tile, copy_granule):
    def per_expert(e, carry):
        seg, n_gran = carry
        rows = cnt8_ref[tile * N_EXPERTS + e]
        dst = pstart[e] + goff_ref[tile * N_EXPERTS + e]
        k = rows // GRANULE

        def gran(g, _):
            copy_granule(seg + g * GRANULE, dst + g * GRANULE)
            return 0
        lax.fori_loop(0, k, gran, 0)
        return seg + rows, n_gran + k
    _, total = lax.fori_loop(0, N_EXPERTS, per_expert, (0, 0))
    return total


def _pack_halves(lo_f32, hi_f32):
    return (pltpu.bitcast(lo_f32, U32) >> 16) | (pltpu.bitcast(hi_f32, U32) & jnp.uint32(0xFFFF0000))


def _unpack_halves(packed):
    lo = pltpu.bitcast(packed << 16, F32).astype(BF16)
    hi = pltpu.bitcast(packed & jnp.uint32(0xFFFF0000), F32).astype(BF16)
    return lo, hi


def _sort_kernel(tot_ref, cnt8_ref, goff_ref, rec_p_ref, h2_p_ref, rec_s_ref, h2_s_ref,
                 srt_ref, gsrc_ref, pstart, *, tiles_p, n_sorted, n_slots):
    i = pl.program_id(0)
    from_sample = i >= tiles_p
    rec = jnp.where(from_sample, rec_s_ref[...], rec_p_ref[...])
    h2 = jnp.where(from_sample, h2_s_ref[...], h2_p_ref[...])
    tm = h2.shape[0]

    rec_t = rec.T
    s1 = rec_t[0:1, :].astype(I32)
    s2 = rec_t[1:2, :].astype(I32)
    rows = lax.broadcasted_iota(I32, (n_sorted, tm), 0)
    sel = jnp.where((rows == s1) | (rows == s2), 1.0, 0.0).astype(BF16)
    srt = _dot(sel, h2)
    srt_ref[...] = _pack_halves(srt[:, :HALF_D], srt[:, HALF_D:])

    used_rows = _expert_starts(tot_ref, pstart)
    zero_granule = (n_sorted - GRANULE) // GRANULE

    @pl.when(i == 0)
    def _():
        def per_expert(e, _):
            total = tot_ref[e]

            def put(g, _):
                gsrc_ref[g] = zero_granule
                return 0
            lax.fori_loop((pstart[e] + total) // GRANULE, (pstart[e] + _padded(total)) // GRANULE, put, 0)
            return 0
        lax.fori_loop(0, N_EXPERTS, per_expert, 0)

        def put_tail(g, _):
            gsrc_ref[g] = zero_granule
            return 0
        lax.fori_loop(used_rows // GRANULE, n_slots // GRANULE, put_tail, 0)

    def per_expert(e, seg):
        rows_e = cnt8_ref[i * N_EXPERTS + e]
        dst = (pstart[e] + goff_ref[i * N_EXPERTS + e]) // GRANULE
        src = (i * n_sorted + seg) // GRANULE

        def put(g, _):
            gsrc_ref[dst + g] = src + g
            return 0
        lax.fori_loop(0, rows_e // GRANULE, put, 0)
        return seg + rows_e
    lax.fori_loop(0, N_EXPERTS, per_expert, 0)


def _sort(totals, cnt8, goff, rec_p, h2_p, rec_s, h2_s, n_slots):
    tm = min(ROW_TILE, h2_p.shape[0])
    assert h2_s.shape[0] % tm == 0
    tiles_p = h2_p.shape[0] // tm
    tiles_s = h2_s.shape[0] // tm
    n_sorted = _sorted_rows(tm)
    row_p = lambda w: pl.BlockSpec((tm, w), lambda i, *_: (jnp.minimum(i, tiles_p - 1), 0))
    row_s = lambda w: pl.BlockSpec((tm, w), lambda i, *_: (jnp.maximum(i - tiles_p, 0), 0))
    return pl.pallas_call(
        functools.partial(_sort_kernel, tiles_p=tiles_p, n_sorted=n_sorted, n_slots=n_slots),
        grid_spec=pltpu.PrefetchScalarGridSpec(
            num_scalar_prefetch=3,
            grid=(tiles_p + tiles_s,),
            in_specs=[row_p(ROUTE_LANES), row_p(D_MODEL), row_s(ROUTE_LANES), row_s(D_MODEL)],
            out_specs=[pl.BlockSpec((n_sorted, HALF_D), lambda i, *_: (i, 0)),
                       pl.BlockSpec(memory_space=pltpu.SMEM)],
            scratch_shapes=[pltpu.SMEM((N_EXPERTS,), I32)]),
        out_shape=[jax.ShapeDtypeStruct(((tiles_p + tiles_s) * n_sorted, HALF_D), U32),
                   jax.ShapeDtypeStruct((n_slots // GRANULE,), I32)],
        compiler_params=_cparams(1),
        name="moe_sort",
    )(totals, cnt8, goff, rec_p, h2_p, rec_s, h2_s)


def _expert_kernel(be_ref, meta_ref, gsrc_ref, srt_hbm, wg_ref, wu_ref, wd_ref, ys_ref,
                   xbuf, wgb, wub, wdb, sems):
    j = pl.program_id(0)
    n_active = meta_ref[0]
    gran_per_block = EXPERT_ROWS // GRANULE

    def granule_copy(blk, g, slot):
        src = gsrc_ref[blk * gran_per_block + g] * GRANULE
        return pltpu.make_async_copy(srt_hbm.at[pl.ds(pl.multiple_of(src, GRANULE), GRANULE)],
                                     xbuf.at[slot, pl.ds(g * GRANULE, GRANULE)], sems.at[slot])

    def gather(blk, slot):
        for g in range(gran_per_block):
            granule_copy(blk, g, slot).start()

    @pl.when(j == 0)
    def _():
        gather(0, 0)

    @pl.when(j < n_active)
    def _():
        slot = j % 2

        @pl.when(j + 1 < n_active)
        def _():
            gather(j + 1, 1 - slot)

        @pl.when((j == 0) | (be_ref[j] != be_ref[jnp.maximum(j - 1, 0)]))
        def _():
            wgb[...] = wg_ref[...].astype(BF16)
            wub[...] = wu_ref[...].astype(BF16)
            wdb[...] = wd_ref[...].astype(BF16)

        for g in range(gran_per_block):
            granule_copy(j, g, slot).wait()
        x_lo, x_hi = _unpack_halves(xbuf[slot])
        g = _dot(x_lo, wgb[0:HALF_D, :]) + _dot(x_hi, wgb[HALF_D:D_MODEL, :])
        u = _dot(x_lo, wub[0:HALF_D, :]) + _dot(x_hi, wub[HALF_D:D_MODEL, :])
        hmid = (g * _sigmoid(g) * u).astype(BF16)
        y = _dot(hmid, wdb[...])
        ys_ref[...] = _pack_halves(y[:, :HALF_D].astype(BF16).astype(F32), y[:, HALF_D:].astype(BF16).astype(F32))

    @pl.when(j >= meta_ref[0])
    def _():
        ys_ref[...] = jnp.zeros(ys_ref.shape, U32)


def _experts(block_e, meta, gsrc, srt, wg, wu, wd, n_slots):
    n_blocks = n_slots // EXPERT_ROWS
    wspec = lambda shp: pl.BlockSpec(
        (None,) + shp, lambda j, be, meta, gs: (be[jnp.minimum(j, meta[0] - 1)], 0, 0))
    return pl.pallas_call(
        _expert_kernel,
        grid_spec=pltpu.PrefetchScalarGridSpec(
            num_scalar_prefetch=3,
            grid=(n_blocks,),
            in_specs=[pl.BlockSpec(memory_space=pl.ANY),
                      wspec((D_MODEL, D_EXPERT)), wspec((D_MODEL, D_EXPERT)), wspec((D_EXPERT, D_MODEL))],
            out_specs=pl.BlockSpec((EXPERT_ROWS, HALF_D), lambda j, be, meta, gs: (j, 0)),
            scratch_shapes=[pltpu.VMEM((2, EXPERT_ROWS, HALF_D), U32),
                            pltpu.VMEM((D_MODEL, D_EXPERT), BF16), pltpu.VMEM((D_MODEL, D_EXPERT), BF16),
                            pltpu.VMEM((D_EXPERT, D_MODEL), BF16), pltpu.SemaphoreType.DMA((2,))]),
        out_shape=jax.ShapeDtypeStruct((n_slots, HALF_D), U32),
        compiler_params=_cparams(1),
        name="moe_experts",
    )(block_e, meta, gsrc, srt, wg, wu, wd)


def _combine_kernel(tot_ref, cnt8_ref, goff_ref, rec_ref, x1_ref, gtf_ref, g_ref, ys_hbm,
                    y_ref, cbuf, pstart, sem, *, tile_base):
    i = pl.program_id(0)
    tm = x1_ref.shape[0]
    n_sorted = cbuf.shape[0]

    @pl.when(i == 0)
    def _():
        cbuf[...] = jnp.zeros(cbuf.shape, U32)

    _expert_starts(tot_ref, pstart)
    n_gran = _segment_copies(
        tot_ref, cnt8_ref, goff_ref, pstart, tile_base + i,
        lambda row, slot: _rows_copy(ys_hbm, slot, cbuf, row, GRANULE, sem).start())

    def drain(r, _):
        _rows_copy(ys_hbm, 0, cbuf, 0, GRANULE, sem).wait()
        return 0
    lax.fori_loop(0, n_gran, drain, 0)

    rec = rec_ref[...]
    s1 = rec[:, 0:1].astype(I32)
    s2 = rec[:, 1:2].astype(I32)
    col = lax.broadcasted_iota(I32, (tm, n_sorted), 1)
    wmat = (jnp.where(col == s1, rec[:, 2:3], 0.0) + jnp.where(col == s2, rec[:, 3:4], 0.0)).astype(BF16)
    y_lo, y_hi = _unpack_halves(cbuf[...])
    f = jnp.concatenate([_dot(wmat, y_lo), _dot(wmat, y_hi)], axis=1)
    y_ref[...] = x1_ref[...] + gtf_ref[...] * _rms(f, g_ref[...])


def _combine(totals, cnt8, goff, rec, x1, gtf, g, ys, rows_per_mod, tile_base):
    n = x1.shape[0]
    tm = min(ROW_TILE, n)
    if rows_per_mod:
        tiles_per_mod = rows_per_mod // tm
        mod_spec = pl.BlockSpec((None, 1, D_MODEL), lambda i, *_: (i // tiles_per_mod, 0, 0))
    else:
        mod_spec = pl.BlockSpec((tm, D_MODEL), lambda i, *_: (i, 0))
    row = lambda w: pl.BlockSpec((tm, w), lambda i, *_: (i, 0))
    return pl.pallas_call(
        functools.partial(_combine_kernel, tile_base=tile_base),
        grid_spec=pltpu.PrefetchScalarGridSpec(
            num_scalar_prefetch=3,
            grid=(n // tm,),
            in_specs=[row(ROUTE_LANES), row(D_MODEL), mod_spec,
                      pl.BlockSpec((1, D_MODEL), lambda i, *_: (0, 0)),
                      pl.BlockSpec(memory_space=pl.ANY)],
            out_specs=row(D_MODEL),
            scratch_shapes=[pltpu.VMEM((_sorted_rows(tm), HALF_D), U32),
                            pltpu.SMEM((N_EXPERTS,), I32), pltpu.SemaphoreType.DMA(())]),
        out_shape=jax.ShapeDtypeStruct((n, D_MODEL), F32),
        compiler_params=_cparams(1),
        name="moe_combine",
    )(totals, cnt8, goff, rec, x1, gtf, g, ys)


def _rope_tables(pos):
    half = HEAD_DIM // 2
    inv = jnp.power(jnp.float32(ROPE_THETA), -jnp.arange(half, dtype=F32) / half)
    ang = pos.astype(F32)[:, None] * inv[None, :]
    cos = jnp.cos(ang)
    sin = jnp.sin(ang)
    reps = LANES // HEAD_DIM
    cos_t = jnp.tile(jnp.concatenate([cos, cos], axis=-1), (1, reps))
    sin_t = jnp.tile(jnp.concatenate([-sin, sin], axis=-1), (1, reps))
    return cos_t, sin_t


def _q_perm_index():
    g = jnp.arange(GQA_GROUP)[:, None, None]
    h = jnp.arange(N_KV_HEADS)[None, :, None]
    d = jnp.arange(HEAD_DIM)[None, None, :]
    return ((h * GQA_GROUP + g) * HEAD_DIM + d).reshape(-1)


def _block_diag_gates(w_a, w_x):
    per = MXU_DIM // LRU_BLOCK_W
    groups = LRU_BLOCKS // per

    def bd(w):
        w = w.reshape(groups, per, LRU_BLOCK_W, LRU_BLOCK_W)
        eye = jnp.eye(per, dtype=w.dtype)
        full = jnp.einsum('gpij,pq->gpiqj', w, eye)
        return full.reshape(groups, MXU_DIM, MXU_DIM)
    return jnp.concatenate([bd(w_a), bd(w_x)], axis=-1).astype(BF16)


def _layer_forward(xp, xs_tm, ck, cv, cconv, ch, mod_p, mod_s, p):
    batch, seq, _ = xp.shape
    nseq, _, _, _ = ck.shape
    t_new = xs_tm.shape[0] // nseq
    n_p = batch * seq
    n_s = xs_tm.shape[0]

    perm = _q_perm_index()
    w_in = p['w_in']
    w_in_b = jnp.concatenate([w_in[:, :Q_DIM][:, perm], w_in[:, Q_DIM:]], axis=1).astype(BF16)
    wab = p['w_attn_branch'][perm, :].astype(BF16)
    wlb = p['w_lru_branch'].astype(BF16)
    wout = p['w_out'].astype(BF16)
    wbd = _block_diag_gates(p['w_lru_a'], p['w_lru_x'])
    row = lambda v: v.reshape(1, -1)
    wr = jnp.concatenate([p['w_router_group'], p['w_router_expert'],
                          jnp.zeros((D_MODEL, ROUTE_LANES - N_GROUPS - N_EXPERTS), F32)], axis=1)
    wr_hi = wr.astype(BF16)
    wr_lo = (wr - wr_hi.astype(F32)).astype(BF16)
    br = jnp.concatenate([p['b_router_group'], p['b_router_expert'],
                          jnp.zeros((ROUTE_LANES - N_GROUPS - N_EXPERTS,), F32)]).reshape(1, -1)
    wg = p['w_exp_gate']
    wu = p['w_exp_up']
    wd = p['w_exp_down']

    def mods(mod):
        return [mod[:, k * D_MODEL:(k + 1) * D_MODEL] for k in range(6)]
    sh_a_p, sc_a_p, gt_a_p, sh_f_p, sc_f_p, gt_f_p = [m.reshape(batch, 1, D_MODEL) for m in mods(mod_p)]
    sh_a_s, sc_a_s, gt_a_s, sh_f_s, sc_f_s, gt_f_s = [jnp.tile(m, (t_new, 1)) for m in mods(mod_s)]

    tm_p = min(ROW_TILE, n_p)
    cos_p, sin_p = _rope_tables(jnp.arange(seq, dtype=I32))
    q_p, k_p, v_p, xl_p, yl_p, sga_p, sgr_p = _inproj(
        xp.reshape(n_p, D_MODEL), sc_a_p, sh_a_p, row(p['g_pre_mix']), cos_p, sin_p, w_in_b,
        rows_per_mod=seq, pos_tiles=seq // tm_p)
    pos_s = jnp.repeat(PAST_LEN_ + jnp.arange(t_new, dtype=I32), nseq)
    cos_s, sin_s = _rope_tables(pos_s)
    q_s, k_s, v_s, xl_s, yl_s, sga_s, sgr_s = _inproj(
        xs_tm, sc_a_s, sh_a_s, row(p['g_pre_mix']), cos_s, sin_s, w_in_b, rows_per_mod=0,
        pos_tiles=n_s // min(ROW_TILE, n_s))

    sinks_perm = p['sinks']
    oa_p = _attn_prompt(sinks_perm, q_p, k_p, v_p, batch, seq)
    rows = t_new * GQA_GROUP
    q_s3 = q_s.reshape(t_new, nseq, GQA_GROUP, KV_DIM).transpose(1, 0, 2, 3).reshape(nseq, rows, KV_DIM)
    kn = k_s.reshape(t_new, nseq, KV_DIM).transpose(1, 0, 2)
    vn = v_s.reshape(t_new, nseq, KV_DIM).transpose(1, 0, 2)
    kc = ck.reshape(nseq, WINDOW, KV_DIM)
    vc = cv.reshape(nseq, WINDOW, KV_DIM)
    sink_rows = jnp.tile(p['sinks'].reshape(N_KV_HEADS, 1, GQA_GROUP), (1, t_new, 1)).reshape(N_KV_HEADS, rows, 1)
    oa_s3 = _attn_sample(sink_rows, q_s3, kn, vn, kc, vc)
    oa_s = oa_s3.reshape(nseq, t_new, Q_DIM).transpose(1, 0, 2).reshape(n_s, Q_DIM)

    lru_w = (p['w_conv'], row(p['b_conv']), wbd, row(p['b_lru_a']), row(p['b_lru_x']), row(p['lru_lambda']))
    ol_p, hlast_p = _lru_prompt(xl_p, yl_p, *lru_w, batch, seq)
    ol_s3, hlast_s = _lru_sample(xl_s.reshape(t_new, nseq, LRU_WIDTH), yl_s.reshape(t_new, nseq, LRU_WIDTH),
                                 cconv.transpose(1, 0, 2), ch, *lru_w)
    ol_s = ol_s3.reshape(n_s, LRU_WIDTH)

    tm_post = min(ROW_TILE, n_p)
    tri = jnp.tril(jnp.ones((tm_post, tm_post), F32), -1).astype(BF16)
    post_w = (wab, wlb, wout, row(p['g_post_mix']), row(p['g_pre_ffn']), wr_hi, wr_lo, br)
    zero_cnt = jnp.zeros((1, ROUTE_LANES), F32)
    x1_p, h2_p, route_p, stat_p, cnt_p = _post(oa_p, ol_p, sga_p, sgr_p, xp.reshape(n_p, D_MODEL),
                                               gt_a_p, sc_f_p, sh_f_p, *post_w, tri, zero_cnt, rows_per_mod=seq)
    tm_s = min(ROW_TILE, n_s)
    tri_s = tri if tm_s == tm_post else jnp.tril(jnp.ones((tm_s, tm_s), F32), -1).astype(BF16)
    x1_s, h2_s, route_s, stat_s, cnt_all = _post(oa_s, ol_s, sga_s, sgr_s, xs_tm,
                                                 gt_a_s, sc_f_s, sh_f_s, *post_w, tri_s, cnt_p, rows_per_mod=0)

    e_lanes = slice(N_GROUPS, N_GROUPS + N_EXPERTS)
    totals = cnt_all[0, e_lanes].astype(I32)
    stats = jnp.concatenate([stat_p, stat_s], axis=0).reshape(-1, SUBLANES, ROUTE_LANES)
    cnt8 = stats[:, 0, e_lanes].astype(I32).reshape(-1)
    goff = stats[:, 1, e_lanes].astype(I32).reshape(-1)
    tiles_p = n_p // tm_post
    n_tiles = tiles_p + n_s // tm_s
    max_rows = 2 * (n_p + n_s) + n_tiles * N_EXPERTS * (GRANULE - 1) + N_EXPERTS * (EXPERT_ROWS - GRANULE)
    n_blocks = -(-max_rows // EXPERT_ROWS)
    n_slots = n_blocks * EXPERT_ROWS
    srt, gsrc = _sort(totals, cnt8, goff, route_p, h2_p, route_s, h2_s, n_slots)
    block_e, meta = _plan(totals, n_blocks)
    ys = _experts(block_e, meta, gsrc, srt, wg, wu, wd, n_slots)
    y_p = _combine(totals, cnt8, goff, route_p, x1_p, gt_f_p, row(p['g_post_ffn']), ys, seq, 0)
    y_s = _combine(totals, cnt8, goff, route_s, x1_s, gt_f_s, row(p['g_post_ffn']), ys, 0, tiles_p)

    k_new_p = k_p.reshape(batch, seq, N_KV_HEADS, HEAD_DIM)[:, -WINDOW:]
    v_new_p = v_p.reshape(batch, seq, N_KV_HEADS, HEAD_DIM)[:, -WINDOW:]
    conv_p = xl_p.reshape(batch, seq, LRU_WIDTH)[:, -(CONV_WIDTH - 1):].astype(F32)
    h_p = hlast_p.reshape(batch, LRU_WIDTH)
    k_new_s = jnp.concatenate([ck, kn.reshape(nseq, t_new, N_KV_HEADS, HEAD_DIM)], axis=1)[:, -WINDOW:]
    v_new_s = jnp.concatenate([cv, vn.reshape(nseq, t_new, N_KV_HEADS, HEAD_DIM)], axis=1)[:, -WINDOW:]
    xl_s3 = xl_s.reshape(t_new, nseq, LRU_WIDTH).transpose(1, 0, 2).astype(F32)
    conv_s = jnp.concatenate([cconv, xl_s3], axis=1)[:, -(CONV_WIDTH - 1):]
    return (y_p.reshape(batch, seq, D_MODEL), y_s, k_new_p, v_new_p, conv_p, h_p,
            k_new_s, v_new_s, conv_s, hlast_s)


PAST_LEN_ = 16384

PARAM_NAMES = ('w_ada', 'b_ada', 'g_pre_mix', 'g_post_mix', 'g_pre_ffn', 'g_post_ffn', 'w_in', 'sinks',
               'w_conv', 'b_conv', 'w_lru_a', 'b_lru_a', 'w_lru_x', 'b_lru_x', 'lru_lambda',
               'w_attn_branch', 'w_lru_branch', 'w_out', 'w_router_group', 'b_router_group',
               'w_router_expert', 'b_router_expert', 'w_exp_gate', 'w_exp_up', 'w_exp_down')


def kernel(x_prompt, x_sample, cache_k_win, cache_v_win, state_conv, state_h, c_prompt, c_sample, w_ada, b_ada, g_pre_mix, g_post_mix, g_pre_ffn, g_post_ffn, w_in, sinks, w_conv, b_conv, w_lru_a, b_lru_a, w_lru_x, b_lru_x, lru_lambda, w_attn_branch, w_lru_branch, w_out, w_router_group, b_router_group, w_router_expert, b_router_expert, w_exp_gate, w_exp_up, w_exp_down):
    weights = (w_ada, b_ada, g_pre_mix, g_post_mix, g_pre_ffn, g_post_ffn, w_in, sinks,
               w_conv, b_conv, w_lru_a, b_lru_a, w_lru_x, b_lru_x, lru_lambda,
               w_attn_branch, w_lru_branch, w_out, w_router_group, b_router_group,
               w_router_expert, b_router_expert, w_exp_gate, w_exp_up, w_exp_down)
    depth = w_ada.shape[0]
    batch = x_prompt.shape[0]
    nseq, t_new, _ = x_sample.shape
    y_p = x_prompt
    y_s = x_sample.transpose(1, 0, 2).reshape(t_new * nseq, D_MODEL)
    c_all = jnp.concatenate([c_prompt, c_sample], axis=0)
    outs = [[] for _ in range(8)]
    for layer in range(depth):
        p = {name: w[layer] for name, w in zip(PARAM_NAMES, weights)}
        mod = _ada(c_all, p['w_ada'].astype(BF16), p['b_ada'].reshape(1, -1))
        res = _layer_forward(y_p, y_s, cache_k_win[layer], cache_v_win[layer], state_conv[layer],
                             state_h[layer], mod[:batch], mod[batch:], p)
        y_p, y_s = res[0], res[1]
        for o, r in zip(outs, res[2:]):
            o.append(r)
    y_sample = y_s.reshape(t_new, nseq, D_MODEL).transpose(1, 0, 2)
    return (y_p, y_sample) + tuple(jnp.stack(o) for o in outs)
```

```python
import functools

import jax
import jax.numpy as jnp
from jax import lax
from jax.experimental import pallas as pl
from jax.experimental.pallas import tpu as pltpu

F32 = jnp.float32
BF16 = jnp.bfloat16
I32 = jnp.int32

D_MODEL = 1024
N_HEADS = 16
HEAD_DIM = 64
N_KV_HEADS = 4
GQA_GROUP = 4
WINDOW = 128
ROPE_THETA = 10000.0
NEG_INF = -1e30
LRU_WIDTH = 1024
LRU_BLOCKS = 16
LRU_BLOCK_W = 64
CONV_WIDTH = 4
LRU_C = 8.0
N_GROUPS = 4
EXPERTS_PER_GROUP = 8
N_EXPERTS = 32
D_EXPERT = 512
MOE_BLOCK = 128
NORM_EPS = 1e-6
Q_DIM = N_HEADS * HEAD_DIM
KV_DIM = N_KV_HEADS * HEAD_DIM
IN_DIM = Q_DIM + 2 * KV_DIM + 2 * LRU_WIDTH + 2 * D_MODEL

LANES = 128
SUBLANES = 8
MXU_DIM = 256
VMEM_LIMIT = 56 * 1024 * 1024
VMEM_LIMIT_BIG = 60 * 1024 * 1024

ROW_TILE = 512
ROUTE_LANES = LANES
GRANULE = SUBLANES
EXPERT_ROWS = 512
HALF_D = D_MODEL // 2
U32 = jnp.uint32


def _cparams(n_axes, vmem=VMEM_LIMIT, flags=None):
    return pltpu.CompilerParams(dimension_semantics=("arbitrary",) * n_axes, vmem_limit_bytes=vmem, flags=flags)


def _rms(x, g):
    ms = jnp.mean(x * x, axis=-1, keepdims=True)
    return x * lax.rsqrt(ms + NORM_EPS) * g


def _sigmoid(x):
    return 1.0 / (1.0 + jnp.exp(-x))


def _dot(a, b):
    return jnp.dot(a, b, preferred_element_type=F32)


def _ada_kernel(c_ref, w_ref, b_ref, o_ref):
    c = c_ref[...]
    s = (c * _sigmoid(c)).astype(BF16)
    o_ref[...] = _dot(s, w_ref[...]) + b_ref[...]


def _ada(c_all, w_ada, b_ada):
    r = c_all.shape[0]
    n = w_ada.shape[1]
    return pl.pallas_call(
        _ada_kernel,
        grid=(n // D_MODEL,),
        in_specs=[pl.BlockSpec((r, D_MODEL), lambda j: (0, 0)),
                  pl.BlockSpec((D_MODEL, D_MODEL), lambda j: (0, j)),
                  pl.BlockSpec((1, D_MODEL), lambda j: (0, j))],
        out_specs=pl.BlockSpec((r, D_MODEL), lambda j: (0, j)),
        out_shape=jax.ShapeDtypeStruct((r, n), F32),
        compiler_params=_cparams(1),
        name="ada_mod",
    )(c_all, w_ada, b_ada)


_O1 = Q_DIM
_O2 = _O1 + KV_DIM
_O3 = _O2 + KV_DIM
_O4 = _O3 + LRU_WIDTH
_O5 = _O4 + LRU_WIDTH
_O6 = _O5 + D_MODEL


def _prenorm(x_ref, g_ref, sc_ref, sh_ref):
    h = _rms(x_ref[...], g_ref[...]) * (1.0 + sc_ref[...]) + sh_ref[...]
    return h.astype(BF16)


PIECE_COLS = 256


def _qkv_gate_pieces(h_ref, w_ref, cos_ref, sin_ref, q_ref, k_ref, v_ref, sga_ref, sgr_ref):
    def rope(t):
        cos = cos_ref[...]
        sin = sin_ref[...]
        lane = lax.broadcasted_iota(I32, cos.shape, 1)
        first_half = (lane % HEAD_DIM) < (HEAD_DIM // 2)
        rot = jnp.where(first_half, pltpu.roll(t, LANES - HEAD_DIM // 2, 1), pltpu.roll(t, HEAD_DIM // 2, 1))
        return t * cos + rot * sin

    def q_piece(c0):
        def run():
            qf = _dot(h_ref[...], w_ref[:, c0:c0 + PIECE_COLS])
            for c in range(PIECE_COLS // LANES):
                q_ref[:, c0 + c * LANES:c0 + (c + 1) * LANES] = rope(qf[:, c * LANES:(c + 1) * LANES]).astype(BF16)
        return run

    def kv_piece():
        kv = _dot(h_ref[...], w_ref[:, _O1:_O3])
        for c in range(KV_DIM // LANES):
            k_ref[:, c * LANES:(c + 1) * LANES] = rope(kv[:, c * LANES:(c + 1) * LANES])
        v_ref[...] = kv[:, KV_DIM:]

    def gate_piece(o_ref, base, c0):
        def run():
            z = _dot(h_ref[...], w_ref[:, base + c0:base + c0 + PIECE_COLS])
            o_ref[:, c0:c0 + PIECE_COLS] = _sigmoid(z).astype(BF16)
        return run

    pieces = [q_piece(c0) for c0 in range(0, Q_DIM, PIECE_COLS)] + [kv_piece]
    pieces += [gate_piece(sga_ref, _O5, c0) for c0 in range(0, D_MODEL, PIECE_COLS)]
    pieces += [gate_piece(sgr_ref, _O6, c0) for c0 in range(0, D_MODEL, PIECE_COLS)]
    return pieces


def _inproj_kernel(x_ref, sc_ref, sh_ref, g_ref, cos_ref, sin_ref, w_ref,
                   q_ref, k_ref, v_ref, xl_ref, yl_ref, sga_ref, sgr_ref, hbuf):
    hbuf[...] = _prenorm(x_ref, g_ref, sc_ref, sh_ref)
    xl_ref[...] = _dot(hbuf[...], w_ref[:, _O3:_O4]).astype(BF16)
    yl_ref[...] = _dot(hbuf[...], w_ref[:, _O4:_O5]).astype(BF16)
    for piece in _qkv_gate_pieces(hbuf, w_ref, cos_ref, sin_ref, q_ref, k_ref, v_ref, sga_ref, sgr_ref):
        piece()


def _inproj(x, sc, sh, g, cos, sin, w_in, rows_per_mod, pos_tiles):
    n = x.shape[0]
    tm = min(ROW_TILE, n)
    if rows_per_mod:
        tiles_per_mod = rows_per_mod // tm
        mod_spec = pl.BlockSpec((None, 1, D_MODEL), lambda i: (i // tiles_per_mod, 0, 0))
    else:
        mod_spec = pl.BlockSpec((tm, D_MODEL), lambda i: (i, 0))
    row = lambda w: pl.BlockSpec((tm, w), lambda i: (i, 0))
    outs = [(Q_DIM, BF16), (KV_DIM, F32), (KV_DIM, F32), (LRU_WIDTH, BF16), (LRU_WIDTH, BF16),
            (D_MODEL, BF16), (D_MODEL, BF16)]
    return pl.pallas_call(
        _inproj_kernel,
        grid=(n // tm,),
        in_specs=[row(D_MODEL), mod_spec, mod_spec,
                  pl.BlockSpec((1, D_MODEL), lambda i: (0, 0)),
                  pl.BlockSpec((tm, LANES), lambda i: (i % pos_tiles, 0)),
                  pl.BlockSpec((tm, LANES), lambda i: (i % pos_tiles, 0)),
                  pl.BlockSpec((D_MODEL, IN_DIM), lambda i: (0, 0))],
        out_specs=[row(w) for w, _ in outs],
        out_shape=[jax.ShapeDtypeStruct((n, w), dt) for w, dt in outs],
        scratch_shapes=[pltpu.VMEM((tm, D_MODEL), BF16)],
        compiler_params=_cparams(1),
        name="in_proj",
    )(x, sc, sh, g, cos, sin, w_in)


def _head_masks(shape):
    lane = lax.broadcasted_iota(I32, shape, 1)
    return [(lane // HEAD_DIM) == h for h in range(N_KV_HEADS)]


def _attention_core(q_perm, kall, vall, valid, sink_of, rows):
    masks_b = _head_masks((rows, KV_DIM))
    zero_b = jnp.zeros((rows, KV_DIM), BF16)
    pieces = []
    for h in range(N_KV_HEADS):
        for g in range(GQA_GROUP):
            pieces.append(jnp.where(masks_b[h], q_perm[g], zero_b))
    q_big = jnp.concatenate(pieces, axis=0)
    s_all = lax.dot_general(q_big, kall, (((1,), (1,)), ((), ())), preferred_element_type=F32)
    s_all = s_all * (HEAD_DIM ** -0.5)
    p_chunks, inv_chunks = [], []
    for h in range(N_KV_HEADS):
        for g in range(GQA_GROUP):
            c = h * GQA_GROUP + g
            s = jnp.where(valid, s_all[c * rows:(c + 1) * rows], NEG_INF)
            sink = sink_of(h, g)
            m = jnp.maximum(jnp.max(s, axis=-1, keepdims=True), sink)
            p = jnp.exp(s - m)
            denom = jnp.sum(p, axis=-1, keepdims=True) + jnp.exp(sink - m)
            p_chunks.append(p.astype(BF16))
            inv_chunks.append(1.0 / denom)
    o_all = _dot(jnp.concatenate(p_chunks, axis=0), vall)
    outs = []
    for g in range(GQA_GROUP):
        acc = jnp.zeros((rows, KV_DIM), F32)
        for h in range(N_KV_HEADS):
            c = h * GQA_GROUP + g
            acc = acc + jnp.where(masks_b[h], o_all[c * rows:(c + 1) * rows] * inv_chunks[c], 0.0)
        outs.append(acc)
    return outs


def _attn_prompt_kernel(sink_ref, q_ref, kc_ref, kp_ref, vc_ref, vp_ref, o_ref):
    j = pl.program_id(1)
    kall = jnp.concatenate([kp_ref[...], kc_ref[...]], axis=0).astype(BF16)
    vall = jnp.concatenate([vp_ref[...], vc_ref[...]], axis=0).astype(BF16)
    qi = lax.broadcasted_iota(I32, (WINDOW, 2 * WINDOW), 0)
    kj = lax.broadcasted_iota(I32, (WINDOW, 2 * WINDOW), 1)
    dist = qi + WINDOW - kj
    valid = (dist >= 0) & (dist <= WINDOW) & ((kj >= WINDOW) | (j > 0))
    q_perm = [q_ref[:, g * KV_DIM:(g + 1) * KV_DIM] for g in range(GQA_GROUP)]
    outs = _attention_core(q_perm, kall, vall, valid, lambda h, g: sink_ref[h * GQA_GROUP + g], WINDOW)
    for g in range(GQA_GROUP):
        o_ref[:, g * KV_DIM:(g + 1) * KV_DIM] = outs[g].astype(BF16)


def _attn_prompt(sinks, q, k, v, batch, seq):
    nb = seq // WINDOW
    cur = lambda w: pl.BlockSpec((WINDOW, w), lambda b, j: (b * nb + j, 0))
    prev = lambda w: pl.BlockSpec((WINDOW, w), lambda b, j: (b * nb + jnp.maximum(j - 1, 0), 0))
    return pl.pallas_call(
        _attn_prompt_kernel,
        grid=(batch, nb),
        in_specs=[pl.BlockSpec(memory_space=pltpu.SMEM),
                  cur(Q_DIM), cur(KV_DIM), prev(KV_DIM), cur(KV_DIM), prev(KV_DIM)],
        out_specs=cur(Q_DIM),
        out_shape=jax.ShapeDtypeStruct((batch * seq, Q_DIM), BF16),
        compiler_params=_cparams(2),
        name="attn_prompt",
    )(sinks, q, k, k, v, v)


SEQ_PER_STEP = 8


def _attn_sample_kernel(sink_ref, q_ref, kn_ref, vn_ref, kc_ref, vc_ref, o_ref, kbuf, vbuf, *, t_new):
    rows = GQA_GROUP * t_new
    kbuf[WINDOW:2 * WINDOW, :] = jnp.zeros((WINDOW, KV_DIM), F32)
    vbuf[WINDOW:2 * WINDOW, :] = jnp.zeros((WINDOW, KV_DIM), F32)
    ri = lax.broadcasted_iota(I32, (rows, 2 * WINDOW), 0)
    kj = lax.broadcasted_iota(I32, (rows, 2 * WINDOW), 1)
    tq = ri // GQA_GROUP
    valid = (kj >= tq) & (kj <= tq + WINDOW) & (kj < WINDOW + t_new)
    for s in range(SEQ_PER_STEP):
        kbuf[0:WINDOW, :] = kc_ref[s]
        vbuf[0:WINDOW, :] = vc_ref[s]
        kbuf[WINDOW:WINDOW + t_new, :] = kn_ref[s]
        vbuf[WINDOW:WINDOW + t_new, :] = vn_ref[s]
        kall = kbuf[...].astype(BF16)
        vall = vbuf[...].astype(BF16)
        qs = q_ref[s]
        masks_b = _head_masks((rows, KV_DIM))
        zero_b = jnp.zeros((rows, KV_DIM), BF16)
        q_big = jnp.concatenate([jnp.where(masks_b[h], qs, zero_b) for h in range(N_KV_HEADS)], axis=0)
        s_all = lax.dot_general(q_big, kall, (((1,), (1,)), ((), ())), preferred_element_type=F32)
        s_all = s_all * (HEAD_DIM ** -0.5)
        acc = jnp.zeros((rows, KV_DIM), F32)
        p_chunks, inv_chunks = [], []
        for h in range(N_KV_HEADS):
            sc = jnp.where(valid, s_all[h * rows:(h + 1) * rows], NEG_INF)
            sink = sink_ref[h]
            m = jnp.maximum(jnp.max(sc, axis=-1, keepdims=True), sink)
            p = jnp.exp(sc - m)
            denom = jnp.sum(p, axis=-1, keepdims=True) + jnp.exp(sink - m)
            p_chunks.append(p.astype(BF16))
            inv_chunks.append(1.0 / denom)
        o_all = _dot(jnp.concatenate(p_chunks, axis=0), vall)
        for h in range(N_KV_HEADS):
            acc = acc + jnp.where(masks_b[h], o_all[h * rows:(h + 1) * rows] * inv_chunks[h], 0.0)
        o_ref[s] = acc.astype(BF16)


def _attn_sample(sink_rows, q, kn, vn, kc, vc):
    nseq, rows, _ = q.shape
    t_new = kn.shape[1]
    sb = SEQ_PER_STEP
    blk = lambda r: pl.BlockSpec((sb, r, KV_DIM), lambda i: (i, 0, 0))
    return pl.pallas_call(
        functools.partial(_attn_sample_kernel, t_new=t_new),
        grid=(nseq // sb,),
        in_specs=[pl.BlockSpec((N_KV_HEADS, rows, 1), lambda i: (0, 0, 0)),
                  blk(rows), blk(t_new), blk(t_new), blk(WINDOW), blk(WINDOW)],
        out_specs=blk(rows),
        out_shape=jax.ShapeDtypeStruct((nseq, rows, KV_DIM), BF16),
        scratch_shapes=[pltpu.VMEM((2 * WINDOW, KV_DIM), F32), pltpu.VMEM((2 * WINDOW, KV_DIM), F32)],
        compiler_params=_cparams(1),
        name="attn_sample",
    )(sink_rows, q, kn, vn, kc, vc)


def _gelu_tanh(x):
    return 0.5 * x * (1.0 + jnp.tanh(0.7978845608028654 * (x + 0.044715 * x * x * x)))


def _lru_gates(xc, wbd_ref, ba, bx, lam):
    xcb = xc.astype(BF16)
    r_parts, i_parts = [], []
    for gidx in range(LRU_WIDTH // MXU_DIM):
        z = _dot(xcb[:, gidx * MXU_DIM:(gidx + 1) * MXU_DIM], wbd_ref[gidx])
        r_parts.append(z[:, :MXU_DIM])
        i_parts.append(z[:, MXU_DIM:])
    r = _sigmoid(jnp.concatenate(r_parts, axis=1) + ba)
    i = _sigmoid(jnp.concatenate(i_parts, axis=1) + bx)
    softplus_neg_lam = jnp.maximum(-lam, 0.0) + jnp.log1p(jnp.exp(-jnp.abs(lam)))
    log_a = -LRU_C * r * softplus_neg_lam
    a = jnp.exp(log_a)
    u = jnp.sqrt(1.0 - jnp.exp(2.0 * log_a)) * (i * xc)
    return a, u


LRU_CHUNK = 64


def _lru_chunk(c, xbuf, ybuf, hcar, ol_ref, wc_ref, bc_ref, wbd_ref, ba_ref, bx_ref, lam_ref):
    w = LRU_WIDTH
    n = LRU_CHUNK
    r0 = SUBLANES + c * n
    xc = xbuf[r0:r0 + n, :] * wc_ref[CONV_WIDTH - 1:CONV_WIDTH, :] + bc_ref[...]
    for k in range(1, CONV_WIDTH):
        xc = xc + xbuf[r0 - k:r0 - k + n, :] * wc_ref[CONV_WIDTH - 1 - k:CONV_WIDTH - k, :]
    a, u = _lru_gates(xc, wbd_ref, ba_ref[...], bx_ref[...], lam_ref[...])

    ng = n // SUBLANES
    a3 = a.reshape(ng, SUBLANES, w)
    u3 = u.reshape(ng, SUBLANES, w)
    row = lax.broadcasted_iota(I32, (ng, SUBLANES, w), 1)
    d = 1
    while d < SUBLANES:
        a_s = jnp.where(row >= d, pltpu.roll(a3, d, 1), 1.0)
        u_s = jnp.where(row >= d, pltpu.roll(u3, d, 1), 0.0)
        u3 = a3 * u_s + u3
        a3 = a3 * a_s
        d *= 2
    carry = hcar[...]
    hs = []
    for gi in range(ng):
        hg = a3[gi] * carry + u3[gi]
        hs.append(hg)
        carry = hg[SUBLANES - 1:SUBLANES, :]
    hcar[...] = carry
    h = jnp.concatenate(hs, axis=0)
    ol_ref[c * n:(c + 1) * n, :] = (h * _gelu_tanh(ybuf[c * n:(c + 1) * n, :])).astype(BF16)


def _inproj_lru_kernel(x_ref, sc_ref, sh_ref, g_ref, cos_ref, sin_ref, w_ref,
                       wc_ref, bc_ref, wbd_ref, ba_ref, bx_ref, lam_ref,
                       q_ref, k_ref, v_ref, sga_ref, sgr_ref, ol_ref, xtail_ref, hlast_ref,
                       hbuf, xbuf, ybuf, hcar, *, tiles_per_seq):
    t = x_ref.shape[0]

    @pl.when(pl.program_id(0) % tiles_per_seq == 0)
    def _():
        xbuf[0:SUBLANES, :] = jnp.zeros((SUBLANES, LRU_WIDTH), F32)
        hcar[...] = jnp.zeros((1, LRU_WIDTH), F32)

    hbuf[...] = _prenorm(x_ref, g_ref, sc_ref, sh_ref)
    xbuf[SUBLANES:SUBLANES + t, :] = _dot(hbuf[...], w_ref[:, _O3:_O4])
    ybuf[...] = _dot(hbuf[...], w_ref[:, _O4:_O5])

    pieces = _qkv_gate_pieces(hbuf, w_ref, cos_ref, sin_ref, q_ref, k_ref, v_ref, sga_ref, sgr_ref)
    n_chunks = t // LRU_CHUNK
    per_chunk = -(-len(pieces) // n_chunks)
    for c in range(n_chunks):
        _lru_chunk(c, xbuf, ybuf, hcar, ol_ref, wc_ref, bc_ref, wbd_ref, ba_ref, bx_ref, lam_ref)
        for piece in pieces[c * per_chunk:(c + 1) * per_chunk]:
            piece()

    tail = xbuf[t:t + SUBLANES, :]
    xtail_ref[...] = tail
    xbuf[0:SUBLANES, :] = tail
    hlast_ref[...] = hcar[...]


def _inproj_lru(x, sc, sh, g, cos, sin, w_in, wc, bc, wbd, ba, bx, lam, batch, seq):
    n = x.shape[0]
    tm = min(ROW_TILE, seq)
    tps = seq // tm
    mod_spec = pl.BlockSpec((None, 1, D_MODEL), lambda i: (i // tps, 0, 0))
    row = lambda w: pl.BlockSpec((tm, w), lambda i: (i, 0))
    full = lambda a: pl.BlockSpec(a.shape, lambda i: (0,) * a.ndim)
    per_seq = lambda r: pl.BlockSpec((None, r, LRU_WIDTH), lambda i: (i // tps, 0, 0))
    outs = [(Q_DIM, BF16), (KV_DIM, F32), (KV_DIM, F32), (D_MODEL, BF16), (D_MODEL, BF16), (LRU_WIDTH, BF16)]
    return pl.pallas_call(
        functools.partial(_inproj_lru_kernel, tiles_per_seq=tps),
        grid=(n // tm,),
        in_specs=[row(D_MODEL), mod_spec, mod_spec, full(g),
                  pl.BlockSpec((tm, LANES), lambda i: (i % tps, 0)),
                  pl.BlockSpec((tm, LANES), lambda i: (i % tps, 0)),
                  full(w_in), full(wc), full(bc), full(wbd), full(ba), full(bx), full(lam)],
        out_specs=[row(w) for w, _ in outs] + [per_seq(SUBLANES), per_seq(1)],
        out_shape=[jax.ShapeDtypeStruct((n, w), dt) for w, dt in outs]
        + [jax.ShapeDtypeStruct((batch, SUBLANES, LRU_WIDTH), F32),
           jax.ShapeDtypeStruct((batch, 1, LRU_WIDTH), F32)],
        scratch_shapes=[pltpu.VMEM((tm, D_MODEL), BF16), pltpu.VMEM((2 * SUBLANES + tm, LRU_WIDTH), F32),
                        pltpu.VMEM((tm, LRU_WIDTH), F32), pltpu.VMEM((1, LRU_WIDTH), F32)],
        compiler_params=_cparams(1, vmem=VMEM_LIMIT_BIG),
        name="in_proj_lru",
    )(x, sc, sh, g, cos, sin, w_in, wc, bc, wbd, ba, bx, lam)


def _lru_sample_kernel(xl_ref, yl_ref, cs_ref, h0_ref, wc_ref, bc_ref, wbd_ref, ba_ref, bx_ref, lam_ref,
                       o_ref, hlast_ref):
    t_new, nseq, w = xl_ref.shape
    xp = [cs_ref[k] for k in range(CONV_WIDTH - 1)] + [xl_ref[k].astype(F32) for k in range(t_new)]
    xcs = []
    for t in range(t_new):
        acc = bc_ref[...] + xp[t] * wc_ref[0:1, :]
        for k in range(1, CONV_WIDTH):
            acc = acc + xp[t + k] * wc_ref[k:k + 1, :]
        xcs.append(acc)
    xc = jnp.concatenate(xcs, axis=0)
    a, u = _lru_gates(xc, wbd_ref, ba_ref[...], bx_ref[...], lam_ref[...])
    h = h0_ref[...]
    for t in range(t_new):
        h = a[t * nseq:(t + 1) * nseq] * h + u[t * nseq:(t + 1) * nseq]
        o_ref[t] = (h * _gelu_tanh(yl_ref[t].astype(F32))).astype(BF16)
    hlast_ref[...] = h


def _lru_sample(xl, yl, cs, h0, wc, bc, wbd, ba, bx, lam):
    t_new, nseq, w = xl.shape
    full = lambda shp: pl.BlockSpec(shp, lambda i: (0,) * len(shp))
    args = (xl, yl, cs, h0, wc, bc, wbd, ba, bx, lam)
    return pl.pallas_call(
        _lru_sample_kernel,
        grid=(1,),
        in_specs=[full(a.shape) for a in args],
        out_specs=[full((t_new, nseq, w)), full((nseq, w))],
        out_shape=[jax.ShapeDtypeStruct((t_new, nseq, w), BF16), jax.ShapeDtypeStruct((nseq, w), F32)],
        compiler_params=_cparams(1),
        name="lru_sample",
    )(*args)


def _post_kernel(oa_ref, ol_ref, sga_ref, sgr_ref, x_ref, gta_ref, scf_ref, shf_ref,
                 wab_ref, wlb_ref, wout_ref, gpm_ref, gpf_ref, wrh_ref, wrl_ref, br_ref, tri_ref, cin_ref,
                 x1_ref, h2_ref, route_ref, stat_ref, cnt_ref, carry):
    i = pl.program_id(0)

    @pl.when(i == 0)
    def _():
        carry[...] = cin_ref[...]

    b_attn = _dot(oa_ref[...], wab_ref[...])
    b_lru = _dot(ol_ref[...], wlb_ref[...])
    merged = sga_ref[...].astype(F32) * b_attn + sgr_ref[...].astype(F32) * b_lru
    mix = _dot(merged.astype(BF16), wout_ref[...])
    x1 = x_ref[...] + gta_ref[...] * _rms(mix, gpm_ref[...])
    x1_ref[...] = x1
    h2 = _rms(x1, gpf_ref[...]) * (1.0 + scf_ref[...]) + shf_ref[...]
    h2_ref[...] = h2.astype(BF16)

    h_hi = h2.astype(BF16)
    h_lo = (h2 - h_hi.astype(F32)).astype(BF16)
    logits = _dot(h_hi, wrh_ref[...]) + (_dot(h_lo, wrh_ref[...]) + _dot(h_hi, wrl_ref[...])) + br_ref[...]

    tm = logits.shape[0]
    lane = lax.broadcasted_iota(I32, (tm, ROUTE_LANES), 1)
    big = jnp.int32(ROUTE_LANES)
    is_g = lane < N_GROUPS
    lg = jnp.where(is_g, logits, NEG_INF)
    mg = jnp.max(lg, axis=-1, keepdims=True)
    g_star = jnp.min(jnp.where(lg == mg, lane, big), axis=-1, keepdims=True)
    p_star = 1.0 / jnp.sum(jnp.where(is_g, jnp.exp(lg - mg), 0.0), axis=-1, keepdims=True)
    lo = N_GROUPS + g_star * EXPERTS_PER_GROUP
    in_grp = (lane >= lo) & (lane < lo + EXPERTS_PER_GROUP)
    le = jnp.where(in_grp, logits, NEG_INF)
    m1 = jnp.max(le, axis=-1, keepdims=True)
    i1 = jnp.min(jnp.where(le == m1, lane, big), axis=-1, keepdims=True)
    le2 = jnp.where(lane == i1, NEG_INF, le)
    m2 = jnp.max(le2, axis=-1, keepdims=True)
    i2 = jnp.min(jnp.where(le2 == m2, lane, big), axis=-1, keepdims=True)
    e2x = jnp.exp(m2 - m1)
    wsum = 1.0 + e2x
    w1 = (1.0 / wsum) * p_star
    w2 = (e2x / wsum) * p_star

    oh1 = lane == i1
    oh2 = lane == i2
    cnt = jnp.where(oh1 | oh2, 1.0, 0.0)
    excl = _dot(tri_ref[...], cnt.astype(BF16))
    per_e = jnp.sum(cnt, axis=0, keepdims=True)
    pad8 = jnp.floor((per_e + (GRANULE - 1.0)) * (1.0 / GRANULE)) * GRANULE
    incl = jnp.broadcast_to(pad8, (SUBLANES, ROUTE_LANES))
    lane8 = lax.broadcasted_iota(I32, (SUBLANES, ROUTE_LANES), 1)
    d = 1
    while d < ROUTE_LANES:
        incl = incl + jnp.where(lane8 >= d, pltpu.roll(incl, d, 1), 0.0)
        d *= 2
    seg_start = incl[0:1, :] - pad8
    pos = excl + seg_start
    s1 = jnp.sum(jnp.where(oh1, pos, 0.0), axis=-1, keepdims=True)
    s2 = jnp.sum(jnp.where(oh2, pos, 0.0), axis=-1, keepdims=True)
    rec = jnp.where(lane == 0, s1, 0.0)
    rec = jnp.where(lane == 1, s2, rec)
    rec = jnp.where(lane == 2, w1, rec)
    rec = jnp.where(lane == 3, w2, rec)
    route_ref[...] = rec
    srow = lax.broadcasted_iota(I32, (SUBLANES, ROUTE_LANES), 0)
    stat_ref[...] = jnp.where(srow == 0, pad8, jnp.where(srow == 1, carry[...], 0.0))
    carry[...] = carry[...] + pad8
    cnt_ref[...] = carry[...]


def _post(oa, ol, sga, sgr, x, gta, scf, shf, wab, wlb, wout, gpm, gpf, wrh, wrl, br, tri, cin, rows_per_mod):
    n = x.shape[0]
    tm = min(ROW_TILE, n)
    if rows_per_mod:
        tiles_per_mod = rows_per_mod // tm
        mod_spec = pl.BlockSpec((None, 1, D_MODEL), lambda i: (i // tiles_per_mod, 0, 0))
    else:
        mod_spec = pl.BlockSpec((tm, D_MODEL), lambda i: (i, 0))
    row = lambda w: pl.BlockSpec((tm, w), lambda i: (i, 0))
    full = lambda a: pl.BlockSpec(a.shape, lambda i: (0,) * a.ndim)
    return pl.pallas_call(
        _post_kernel,
        grid=(n // tm,),
        in_specs=[row(Q_DIM), row(LRU_WIDTH), row(D_MODEL), row(D_MODEL), row(D_MODEL),
                  mod_spec, mod_spec, mod_spec,
                  full(wab), full(wlb), full(wout), full(gpm), full(gpf), full(wrh), full(wrl), full(br),
                  full(tri), full(cin)],
        out_specs=[row(D_MODEL), row(D_MODEL), row(ROUTE_LANES),
                   pl.BlockSpec((SUBLANES, ROUTE_LANES), lambda i: (i, 0)),
                   pl.BlockSpec((1, ROUTE_LANES), lambda i: (0, 0))],
        out_shape=[jax.ShapeDtypeStruct((n, D_MODEL), F32), jax.ShapeDtypeStruct((n, D_MODEL), BF16),
                   jax.ShapeDtypeStruct((n, ROUTE_LANES), F32),
                   jax.ShapeDtypeStruct((n // tm * SUBLANES, ROUTE_LANES), F32),
                   jax.ShapeDtypeStruct((1, ROUTE_LANES), F32)],
        scratch_shapes=[pltpu.VMEM((1, ROUTE_LANES), F32)],
        compiler_params=_cparams(1),
        name="post_mix",
    )(oa, ol, sga, sgr, x, gta, scf, shf, wab, wlb, wout, gpm, gpf, wrh, wrl, br, tri, cin)


BLOCK_GRANULES = EXPERT_ROWS // GRANULE
BLOCK_SHIFT = BLOCK_GRANULES.bit_length() - 1
assert BLOCK_GRANULES == 1 << BLOCK_SHIFT


def _padded(c):
    return ((c + (BLOCK_GRANULES - 1)) >> BLOCK_SHIFT) << BLOCK_SHIFT


def _sorted_rows(tm):
    r = 2 * tm + N_EXPERTS * (GRANULE - 1)
    return -(-r // MXU_DIM) * MXU_DIM


def _plan_kernel(tot_ref, be_ref, meta_ref, *, n_blocks):
    def fill(j, _):
        be_ref[j] = N_EXPERTS - 1
        return 0
    lax.fori_loop(0, n_blocks, fill, 0)

    def per_expert(e, nb):
        k = _padded(tot_ref[e]) >> BLOCK_SHIFT

        def put(b, _):
            be_ref[nb + b] = e
            return 0
        lax.fori_loop(0, k, put, 0)
        return nb + k
    n_active = lax.fori_loop(0, N_EXPERTS, per_expert, 0)
    meta_ref[0] = n_active


def _plan(totals, n_blocks):
    return pl.pallas_call(
        functools.partial(_plan_kernel, n_blocks=n_blocks),
        in_specs=[pl.BlockSpec(memory_space=pltpu.SMEM)],
        out_specs=[pl.BlockSpec(memory_space=pltpu.SMEM), pl.BlockSpec(memory_space=pltpu.SMEM)],
        out_shape=[jax.ShapeDtypeStruct((n_blocks,), I32), jax.ShapeDtypeStruct((1,), I32)],
        name="moe_plan",
    )(totals)


def _expert_starts(tot_ref, pstart):
    def body(e, acc):
        pstart[e] = acc
        return acc + _padded(tot_ref[e])
    return lax.fori_loop(0, N_EXPERTS, body, 0)


def _granule(ref, g):
    return ref.at[pl.ds(pl.multiple_of(g * GRANULE, GRANULE), GRANULE)]


def _pack_halves(lo_f32, hi_f32):
    return (pltpu.bitcast(lo_f32, U32) >> 16) | (pltpu.bitcast(hi_f32, U32) & jnp.uint32(0xFFFF0000))


def _unpack_halves(packed):
    lo = pltpu.bitcast(packed << 16, F32).astype(BF16)
    hi = pltpu.bitcast(packed & jnp.uint32(0xFFFF0000), F32).astype(BF16)
    return lo, hi


def _sort_kernel(tot_ref, cnt_ref, goff_ref, rec_p_ref, h2_p_ref, rec_s_ref, h2_s_ref,
                 srt_ref, gsrc_ref, pstart, *, tiles_p, n_sorted, n_slots):
    i = pl.program_id(0)
    from_sample = i >= tiles_p
    rec = jnp.where(from_sample, rec_s_ref[...], rec_p_ref[...])
    h2 = jnp.where(from_sample, h2_s_ref[...], h2_p_ref[...])
    tm = h2.shape[0]

    rec_t = rec.T
    s1 = rec_t[0:1, :].astype(I32)
    s2 = rec_t[1:2, :].astype(I32)
    rows = lax.broadcasted_iota(I32, (n_sorted, tm), 0)
    sel = jnp.where((rows == s1) | (rows == s2), 1.0, 0.0).astype(BF16)
    srt = _dot(sel, h2)
    srt_ref[...] = _pack_halves(srt[:, :HALF_D], srt[:, HALF_D:])

    zero_granule = n_sorted // GRANULE - 1

    @pl.when(i == 0)
    def _():
        used = _expert_starts(tot_ref, pstart)

        def put_zero(g, _):
            gsrc_ref[g] = zero_granule
            return 0

        def per_expert(e, _):
            total = tot_ref[e]
            lax.fori_loop(pstart[e] + total, pstart[e] + _padded(total), put_zero, 0)
            return 0
        lax.fori_loop(0, N_EXPERTS, per_expert, 0)
        lax.fori_loop(used, n_slots // GRANULE, put_zero, 0)

    def per_expert(e, seg):
        k = cnt_ref[i * N_EXPERTS + e]
        dst = pstart[e] + goff_ref[i * N_EXPERTS + e]
        src = i * (n_sorted // GRANULE) + seg

        def put(g, _):
            gsrc_ref[dst + g] = src + g
            return 0
        lax.fori_loop(0, k, put, 0)
        return seg + k
    lax.fori_loop(0, N_EXPERTS, per_expert, 0, unroll=4)


def _sort(totals, cnt8, goff, rec_p, h2_p, rec_s, h2_s, n_slots):
    tm = min(ROW_TILE, h2_p.shape[0])
    assert h2_s.shape[0] % tm == 0
    tiles_p = h2_p.shape[0] // tm
    tiles_s = h2_s.shape[0] // tm
    n_sorted = _sorted_rows(tm)
    row_p = lambda w: pl.BlockSpec((tm, w), lambda i, *_: (jnp.minimum(i, tiles_p - 1), 0))
    row_s = lambda w: pl.BlockSpec((tm, w), lambda i, *_: (jnp.maximum(i - tiles_p, 0), 0))
    return pl.pallas_call(
        functools.partial(_sort_kernel, tiles_p=tiles_p, n_sorted=n_sorted, n_slots=n_slots),
        grid_spec=pltpu.PrefetchScalarGridSpec(
            num_scalar_prefetch=3,
            grid=(tiles_p + tiles_s,),
            in_specs=[row_p(ROUTE_LANES), row_p(D_MODEL), row_s(ROUTE_LANES), row_s(D_MODEL)],
            out_specs=[pl.BlockSpec((n_sorted, HALF_D), lambda i, *_: (i, 0)),
                       pl.BlockSpec(memory_space=pltpu.SMEM)],
            scratch_shapes=[pltpu.SMEM((N_EXPERTS,), I32)]),
        out_shape=[jax.ShapeDtypeStruct(((tiles_p + tiles_s) * n_sorted, HALF_D), U32),
                   jax.ShapeDtypeStruct((n_slots // GRANULE,), I32)],
        compiler_params=_cparams(1),
        name="moe_sort",
    )(totals, cnt8, goff, rec_p, h2_p, rec_s, h2_s)


def _expert_kernel(be_ref, meta_ref, gsrc_ref, srt_hbm, wg_ref, wu_ref, wd_ref, ys_ref,
                   xbuf, wgb, wub, wdb, sems):
    j = pl.program_id(0)
    n_active = meta_ref[0]
    gran_per_block = BLOCK_GRANULES

    def granule_copy(blk, g, slot):
        return pltpu.make_async_copy(_granule(srt_hbm, gsrc_ref[blk * gran_per_block + g]),
                                     xbuf.at[slot, pl.ds(g * GRANULE, GRANULE)], sems.at[slot])

    def gather(blk, slot):
        for g in range(gran_per_block):
            granule_copy(blk, g, slot).start()

    @pl.when(j == 0)
    def _():
        gather(0, 0)

    @pl.when(j < n_active)
    def _():
        slot = j % 2

        @pl.when(j + 1 < n_active)
        def _():
            gather(j + 1, 1 - slot)

        @pl.when((j == 0) | (be_ref[j] != be_ref[jnp.maximum(j - 1, 0)]))
        def _():
            wgb[...] = wg_ref[...].astype(BF16)
            wub[...] = wu_ref[...].astype(BF16)
            wdb[...] = wd_ref[...].astype(BF16)

        for g in range(gran_per_block):
            granule_copy(j, g, slot).wait()
        x_lo, x_hi = _unpack_halves(xbuf[slot])
        g = _dot(x_lo, wgb[0:HALF_D, :]) + _dot(x_hi, wgb[HALF_D:D_MODEL, :])
        u = _dot(x_lo, wub[0:HALF_D, :]) + _dot(x_hi, wub[HALF_D:D_MODEL, :])
        hmid = (g * _sigmoid(g) * u).astype(BF16)
        y = _dot(hmid, wdb[...])
        ys_ref[...] = _pack_halves(y[:, :HALF_D].astype(BF16).astype(F32), y[:, HALF_D:].astype(BF16).astype(F32))

    @pl.when(j >= meta_ref[0])
    def _():
        ys_ref[...] = jnp.zeros(ys_ref.shape, U32)


def _experts(block_e, meta, gsrc, srt, wg, wu, wd, n_slots):
    n_blocks = n_slots // EXPERT_ROWS
    wspec = lambda shp: pl.BlockSpec(
        (None,) + shp, lambda j, be, meta, gs: (be[jnp.minimum(j, meta[0] - 1)], 0, 0))
    return pl.pallas_call(
        _expert_kernel,
        grid_spec=pltpu.PrefetchScalarGridSpec(
            num_scalar_prefetch=3,
            grid=(n_blocks,),
            in_specs=[pl.BlockSpec(memory_space=pl.ANY),
                      wspec((D_MODEL, D_EXPERT)), wspec((D_MODEL, D_EXPERT)), wspec((D_EXPERT, D_MODEL))],
            out_specs=pl.BlockSpec((EXPERT_ROWS, HALF_D), lambda j, be, meta, gs: (j, 0)),
            scratch_shapes=[pltpu.VMEM((2, EXPERT_ROWS, HALF_D), U32),
                            pltpu.VMEM((D_MODEL, D_EXPERT), BF16), pltpu.VMEM((D_MODEL, D_EXPERT), BF16),
                            pltpu.VMEM((D_EXPERT, D_MODEL), BF16), pltpu.SemaphoreType.DMA((2,))]),
        out_shape=jax.ShapeDtypeStruct((n_slots, HALF_D), U32),
        compiler_params=_cparams(1),
        name="moe_experts",
    )(block_e, meta, gsrc, srt, wg, wu, wd)


def _combine_kernel(tot_ref, cnt_ref, goff_ref, rec_ref, x1_ref, gtf_ref, g_ref, ys_hbm,
                    y_ref, cbuf, pstart, n_issued, sems, *, tile_base, n_tiles):
    i = pl.program_id(0)
    slot = i % 2
    tm = x1_ref.shape[0]
    n_sorted = cbuf.shape[1]

    def granule_copy(slot_granule, tile_granule, slot_):
        return pltpu.make_async_copy(_granule(ys_hbm, slot_granule), _granule(cbuf.at[slot_], tile_granule),
                                     sems.at[slot_])

    def gather(tile, slot_):
        def per_expert(e, carry):
            seg, n = carry
            k = cnt_ref[tile * N_EXPERTS + e]
            src = pstart[e] + goff_ref[tile * N_EXPERTS + e]

            def gran(g, _):
                granule_copy(src + g, seg + g, slot_).start()
                return 0
            lax.fori_loop(0, k, gran, 0)
            return seg + k, n + k
        _, n = lax.fori_loop(0, N_EXPERTS, per_expert, (0, 0), unroll=4)
        n_issued[slot_] = n

    @pl.when(i == 0)
    def _():
        cbuf[...] = jnp.zeros(cbuf.shape, U32)
        _expert_starts(tot_ref, pstart)
        gather(tile_base, 0)

    @pl.when(i + 1 < n_tiles)
    def _():
        gather(tile_base + i + 1, 1 - slot)

    def drain(r, _):
        granule_copy(0, 0, slot).wait()
        return 0
    lax.fori_loop(0, n_issued[slot], drain, 0)

    rec = rec_ref[...]
    s1 = rec[:, 0:1].astype(I32)
    s2 = rec[:, 1:2].astype(I32)
    col = lax.broadcasted_iota(I32, (tm, n_sorted), 1)
    wmat = (jnp.where(col == s1, rec[:, 2:3], 0.0) + jnp.where(col == s2, rec[:, 3:4], 0.0)).astype(BF16)
    y_lo, y_hi = _unpack_halves(cbuf[slot])
    f = jnp.concatenate([_dot(wmat, y_lo), _dot(wmat, y_hi)], axis=1)
    y_ref[...] = x1_ref[...] + gtf_ref[...] * _rms(f, g_ref[...])


def _combine(totals, cnt8, goff, rec, x1, gtf, g, ys, rows_per_mod, tile_base):
    n = x1.shape[0]
    tm = min(ROW_TILE, n)
    if rows_per_mod:
        tiles_per_mod = rows_per_mod // tm
        mod_spec = pl.BlockSpec((None, 1, D_MODEL), lambda i, *_: (i // tiles_per_mod, 0, 0))
    else:
        mod_spec = pl.BlockSpec((tm, D_MODEL), lambda i, *_: (i, 0))
    row = lambda w: pl.BlockSpec((tm, w), lambda i, *_: (i, 0))
    return pl.pallas_call(
        functools.partial(_combine_kernel, tile_base=tile_base, n_tiles=n // tm),
        grid_spec=pltpu.PrefetchScalarGridSpec(
            num_scalar_prefetch=3,
            grid=(n // tm,),
            in_specs=[row(ROUTE_LANES), row(D_MODEL), mod_spec,
                      pl.BlockSpec((1, D_MODEL), lambda i, *_: (0, 0)),
                      pl.BlockSpec(memory_space=pl.ANY)],
            out_specs=row(D_MODEL),
            scratch_shapes=[pltpu.VMEM((2, _sorted_rows(tm), HALF_D), U32),
                            pltpu.SMEM((N_EXPERTS,), I32), pltpu.SMEM((2,), I32),
                            pltpu.SemaphoreType.DMA((2,))]),
        out_shape=jax.ShapeDtypeStruct((n, D_MODEL), F32),
        compiler_params=_cparams(1),
        name="moe_combine",
    )(totals, cnt8, goff, rec, x1, gtf, g, ys)


def _rope_tables(pos):
    half = HEAD_DIM // 2
    inv = jnp.power(jnp.float32(ROPE_THETA), -jnp.arange(half, dtype=F32) / half)
    ang = pos.astype(F32)[:, None] * inv[None, :]
    cos = jnp.cos(ang)
    sin = jnp.sin(ang)
    reps = LANES // HEAD_DIM
    cos_t = jnp.tile(jnp.concatenate([cos, cos], axis=-1), (1, reps))
    sin_t = jnp.tile(jnp.concatenate([-sin, sin], axis=-1), (1, reps))
    return cos_t, sin_t


def _q_perm_index():
    g = jnp.arange(GQA_GROUP)[:, None, None]
    h = jnp.arange(N_KV_HEADS)[None, :, None]
    d = jnp.arange(HEAD_DIM)[None, None, :]
    return ((h * GQA_GROUP + g) * HEAD_DIM + d).reshape(-1)


def _block_diag_gates(w_a, w_x):
    per = MXU_DIM // LRU_BLOCK_W
    groups = LRU_BLOCKS // per

    def bd(w):
        w = w.reshape(groups, per, LRU_BLOCK_W, LRU_BLOCK_W)
        eye = jnp.eye(per, dtype=w.dtype)
        full = jnp.einsum('gpij,pq->gpiqj', w, eye)
        return full.reshape(groups, MXU_DIM, MXU_DIM)
    return jnp.concatenate([bd(w_a), bd(w_x)], axis=-1).astype(BF16)


def _layer_forward(xp, xs_tm, ck, cv, cconv, ch, mod_p, mod_s, p):
    batch, seq, _ = xp.shape
    nseq, _, _, _ = ck.shape
    t_new = xs_tm.shape[0] // nseq
    n_p = batch * seq
    n_s = xs_tm.shape[0]

    perm = _q_perm_index()
    w_in = p['w_in']
    w_in_b = jnp.concatenate([w_in[:, :Q_DIM][:, perm], w_in[:, Q_DIM:]], axis=1).astype(BF16)
    wab = p['w_attn_branch'][perm, :].astype(BF16)
    wlb = p['w_lru_branch'].astype(BF16)
    wout = p['w_out'].astype(BF16)
    wbd = _block_diag_gates(p['w_lru_a'], p['w_lru_x'])
    row = lambda v: v.reshape(1, -1)
    wr = jnp.concatenate([p['w_router_group'], p['w_router_expert'],
                          jnp.zeros((D_MODEL, ROUTE_LANES - N_GROUPS - N_EXPERTS), F32)], axis=1)
    wr_hi = wr.astype(BF16)
    wr_lo = (wr - wr_hi.astype(F32)).astype(BF16)
    br = jnp.concatenate([p['b_router_group'], p['b_router_expert'],
                          jnp.zeros((ROUTE_LANES - N_GROUPS - N_EXPERTS,), F32)]).reshape(1, -1)
    wg = p['w_exp_gate']
    wu = p['w_exp_up']
    wd = p['w_exp_down']

    def mods(mod):
        return [mod[:, k * D_MODEL:(k + 1) * D_MODEL] for k in range(6)]
    sh_a_p, sc_a_p, gt_a_p, sh_f_p, sc_f_p, gt_f_p = [m.reshape(batch, 1, D_MODEL) for m in mods(mod_p)]
    sh_a_s, sc_a_s, gt_a_s, sh_f_s, sc_f_s, gt_f_s = [jnp.tile(m, (t_new, 1)) for m in mods(mod_s)]

    lru_w = (p['w_conv'], row(p['b_conv']), wbd, row(p['b_lru_a']), row(p['b_lru_x']), row(p['lru_lambda']))
    cos_p, sin_p = _rope_tables(jnp.arange(seq, dtype=I32))
    q_p, k_p, v_p, sga_p, sgr_p, ol_p, xtail_p, hlast_p = _inproj_lru(
        xp.reshape(n_p, D_MODEL), sc_a_p, sh_a_p, row(p['g_pre_mix']), cos_p, sin_p, w_in_b, *lru_w, batch, seq)
    pos_s = jnp.repeat(PAST_LEN_ + jnp.arange(t_new, dtype=I32), nseq)
    cos_s, sin_s = _rope_tables(pos_s)
    q_s, k_s, v_s, xl_s, yl_s, sga_s, sgr_s = _inproj(
        xs_tm, sc_a_s, sh_a_s, row(p['g_pre_mix']), cos_s, sin_s, w_in_b, rows_per_mod=0,
        pos_tiles=n_s // min(ROW_TILE, n_s))

    sinks_perm = p['sinks']
    oa_p = _attn_prompt(sinks_perm, q_p, k_p, v_p, batch, seq)
    rows = t_new * GQA_GROUP
    q_s3 = q_s.reshape(t_new, nseq, GQA_GROUP, KV_DIM).transpose(1, 0, 2, 3).reshape(nseq, rows, KV_DIM)
    kn = k_s.reshape(t_new, nseq, KV_DIM).transpose(1, 0, 2)
    vn = v_s.reshape(t_new, nseq, KV_DIM).transpose(1, 0, 2)
    kc = ck.reshape(nseq, WINDOW, KV_DIM)
    vc = cv.reshape(nseq, WINDOW, KV_DIM)
    sink_rows = jnp.tile(p['sinks'].reshape(N_KV_HEADS, 1, GQA_GROUP), (1, t_new, 1)).reshape(N_KV_HEADS, rows, 1)
    oa_s3 = _attn_sample(sink_rows, q_s3, kn, vn, kc, vc)
    oa_s = oa_s3.reshape(nseq, t_new, Q_DIM).transpose(1, 0, 2).reshape(n_s, Q_DIM)

    ol_s3, hlast_s = _lru_sample(xl_s.reshape(t_new, nseq, LRU_WIDTH), yl_s.reshape(t_new, nseq, LRU_WIDTH),
                                 cconv.transpose(1, 0, 2), ch, *lru_w)
    ol_s = ol_s3.reshape(n_s, LRU_WIDTH)

    tm_post = min(ROW_TILE, n_p)
    tri = jnp.tril(jnp.ones((tm_post, tm_post), F32), -1).astype(BF16)
    post_w = (wab, wlb, wout, row(p['g_post_mix']), row(p['g_pre_ffn']), wr_hi, wr_lo, br)
    zero_cnt = jnp.zeros((1, ROUTE_LANES), F32)
    x1_p, h2_p, route_p, stat_p, cnt_p = _post(oa_p, ol_p, sga_p, sgr_p, xp.reshape(n_p, D_MODEL),
                                               gt_a_p, sc_f_p, sh_f_p, *post_w, tri, zero_cnt, rows_per_mod=seq)
    tm_s = min(ROW_TILE, n_s)
    tri_s = tri if tm_s == tm_post else jnp.tril(jnp.ones((tm_s, tm_s), F32), -1).astype(BF16)
    x1_s, h2_s, route_s, stat_s, cnt_all = _post(oa_s, ol_s, sga_s, sgr_s, xs_tm,
                                                 gt_a_s, sc_f_s, sh_f_s, *post_w, tri_s, cnt_p, rows_per_mod=0)

    e_lanes = slice(N_GROUPS, N_GROUPS + N_EXPERTS)
    to_granules = lambda v: (v.astype(I32) // GRANULE).reshape(-1)
    totals = to_granules(cnt_all[0, e_lanes])
    stats = jnp.concatenate([stat_p, stat_s], axis=0).reshape(-1, SUBLANES, ROUTE_LANES)
    cnt8 = to_granules(stats[:, 0, e_lanes])
    goff = to_granules(stats[:, 1, e_lanes])
    tiles_p = n_p // tm_post
    n_tiles = tiles_p + n_s // tm_s
    max_rows = 2 * (n_p + n_s) + n_tiles * N_EXPERTS * (GRANULE - 1) + N_EXPERTS * (EXPERT_ROWS - GRANULE)
    n_blocks = -(-max_rows // EXPERT_ROWS)
    n_slots = n_blocks * EXPERT_ROWS
    srt, gsrc = _sort(totals, cnt8, goff, route_p, h2_p, route_s, h2_s, n_slots)
    block_e, meta = _plan(totals, n_blocks)
    ys = _experts(block_e, meta, gsrc, srt, wg, wu, wd, n_slots)
    y_p = _combine(totals, cnt8, goff, route_p, x1_p, gt_f_p, row(p['g_post_ffn']), ys, seq, 0)
    y_s = _combine(totals, cnt8, goff, route_s, x1_s, gt_f_s, row(p['g_post_ffn']), ys, 0, tiles_p)

    k_new_p = k_p.reshape(batch, seq, N_KV_HEADS, HEAD_DIM)[:, -WINDOW:]
    v_new_p = v_p.reshape(batch, seq, N_KV_HEADS, HEAD_DIM)[:, -WINDOW:]
    conv_p = xtail_p[:, -(CONV_WIDTH - 1):]
    h_p = hlast_p.reshape(batch, LRU_WIDTH)
    k_new_s = jnp.concatenate([ck, kn.reshape(nseq, t_new, N_KV_HEADS, HEAD_DIM)], axis=1)[:, -WINDOW:]
    v_new_s = jnp.concatenate([cv, vn.reshape(nseq, t_new, N_KV_HEADS, HEAD_DIM)], axis=1)[:, -WINDOW:]
    xl_s3 = xl_s.reshape(t_new, nseq, LRU_WIDTH).transpose(1, 0, 2).astype(F32)
    conv_s = jnp.concatenate([cconv, xl_s3], axis=1)[:, -(CONV_WIDTH - 1):]
    return (y_p.reshape(batch, seq, D_MODEL), y_s, k_new_p, v_new_p, conv_p, h_p,
            k_new_s, v_new_s, conv_s, hlast_s)


PAST_LEN_ = 16384

PARAM_NAMES = ('w_ada', 'b_ada', 'g_pre_mix', 'g_post_mix', 'g_pre_ffn', 'g_post_ffn', 'w_in', 'sinks',
               'w_conv', 'b_conv', 'w_lru_a', 'b_lru_a', 'w_lru_x', 'b_lru_x', 'lru_lambda',
               'w_attn_branch', 'w_lru_branch', 'w_out', 'w_router_group', 'b_router_group',
               'w_router_expert', 'b_router_expert', 'w_exp_gate', 'w_exp_up', 'w_exp_down')


def kernel(x_prompt, x_sample, cache_k_win, cache_v_win, state_conv, state_h, c_prompt, c_sample, w_ada, b_ada, g_pre_mix, g_post_mix, g_pre_ffn, g_post_ffn, w_in, sinks, w_conv, b_conv, w_lru_a, b_lru_a, w_lru_x, b_lru_x, lru_lambda, w_attn_branch, w_lru_branch, w_out, w_router_group, b_router_group, w_router_expert, b_router_expert, w_exp_gate, w_exp_up, w_exp_down):
    weights = (w_ada, b_ada, g_pre_mix, g_post_mix, g_pre_ffn, g_post_ffn, w_in, sinks,
               w_conv, b_conv, w_lru_a, b_lru_a, w_lru_x, b_lru_x, lru_lambda,
               w_attn_branch, w_lru_branch, w_out, w_router_group, b_router_group,
               w_router_expert, b_router_expert, w_exp_gate, w_exp_up, w_exp_down)
    depth = w_ada.shape[0]
    batch = x_prompt.shape[0]
    nseq, t_new, _ = x_sample.shape
    y_p = x_prompt
    y_s = x_sample.transpose(1, 0, 2).reshape(t_new * nseq, D_MODEL)
    c_all = jnp.concatenate([c_prompt, c_sample], axis=0)
    outs = [[] for _ in range(8)]
    for layer in range(depth):
        p = {name: w[layer] for name, w in zip(PARAM_NAMES, weights)}
        mod = _ada(c_all, p['w_ada'].astype(BF16), p['b_ada'].reshape(1, -1))
        res = _layer_forward(y_p, y_s, cache_k_win[layer], cache_v_win[layer], state_conv[layer],
                             state_h[layer], mod[:batch], mod[batch:], p)
        y_p, y_s = res[0], res[1]
        for o, r in zip(outs, res[2:]):
            o.append(r)
    y_sample = y_s.reshape(t_new, nseq, D_MODEL).transpose(1, 0, 2)
    return (y_p, y_sample) + tuple(jnp.stack(o) for o in outs)
```

```python
import functools

import jax
import jax.numpy as jnp
from jax import lax
from jax.experimental import pallas as pl
from jax.experimental.pallas import tpu as pltpu

F32 = jnp.float32
BF16 = jnp.bfloat16
I32 = jnp.int32

D_MODEL = 1024
N_HEADS = 16
HEAD_DIM = 64
N_KV_HEADS = 4
GQA_GROUP = 4
WINDOW = 128
ROPE_THETA = 10000.0
NEG_INF = -1e30
LRU_WIDTH = 1024
LRU_BLOCKS = 16
LRU_BLOCK_W = 64
CONV_WIDTH = 4
LRU_C = 8.0
N_GROUPS = 4
EXPERTS_PER_GROUP = 8
N_EXPERTS = 32
D_EXPERT = 512
MOE_BLOCK = 128
NORM_EPS = 1e-6
Q_DIM = N_HEADS * HEAD_DIM
KV_DIM = N_KV_HEADS * HEAD_DIM
IN_DIM = Q_DIM + 2 * KV_DIM + 2 * LRU_WIDTH + 2 * D_MODEL

LANES = 128
SUBLANES = 8
MXU_DIM = 256
VMEM_LIMIT = 56 * 1024 * 1024
VMEM_LIMIT_BIG = 60 * 1024 * 1024

ROW_TILE = 512
ROUTE_LANES = LANES
GRANULE = SUBLANES
EXPERT_ROWS = 256
HALF_D = D_MODEL // 2
U32 = jnp.uint32


def _cparams(n_axes, vmem=VMEM_LIMIT, flags=None):
    return pltpu.CompilerParams(dimension_semantics=("arbitrary",) * n_axes, vmem_limit_bytes=vmem, flags=flags)


def _rms(x, g):
    ms = jnp.mean(x * x, axis=-1, keepdims=True)
    return x * lax.rsqrt(ms + NORM_EPS) * g


def _sigmoid(x):
    return 1.0 / (1.0 + jnp.exp(-x))


def _dot(a, b):
    return jnp.dot(a, b, preferred_element_type=F32)


def _ada_kernel(c_ref, w_ref, b_ref, o_ref):
    c = c_ref[...]
    s = (c * _sigmoid(c)).astype(BF16)
    o_ref[...] = _dot(s, w_ref[...].astype(BF16)) + b_ref[...]


def _ada(c_all, w_ada, b_ada):
    r = c_all.shape[0]
    n = w_ada.shape[1]
    return pl.pallas_call(
        _ada_kernel,
        grid=(n // D_MODEL,),
        in_specs=[pl.BlockSpec((r, D_MODEL), lambda j: (0, 0)),
                  pl.BlockSpec((D_MODEL, D_MODEL), lambda j: (0, j)),
                  pl.BlockSpec((1, D_MODEL), lambda j: (0, j))],
        out_specs=pl.BlockSpec((r, D_MODEL), lambda j: (0, j)),
        out_shape=jax.ShapeDtypeStruct((r, n), F32),
        compiler_params=_cparams(1),
        name="ada_mod",
    )(c_all, w_ada, b_ada)


_O1 = Q_DIM
_O2 = _O1 + KV_DIM
_O3 = _O2 + KV_DIM
_O4 = _O3 + LRU_WIDTH
_O5 = _O4 + LRU_WIDTH
_O6 = _O5 + D_MODEL


def _prenorm(x_ref, g_ref, sc_ref, sh_ref):
    h = _rms(x_ref[...], g_ref[...]) * (1.0 + sc_ref[...]) + sh_ref[...]
    return h.astype(BF16)


PIECE_COLS = 256


def _plain_pieces(h_ref, w_ref, base, width, store):
    def piece(c0):
        def run():
            store(c0, _dot(h_ref[...], w_ref[:, base + c0:base + c0 + PIECE_COLS]))
        return run
    return [piece(c0) for c0 in range(0, width, PIECE_COLS)]


def _qkv_gate_pieces(h_ref, w_ref, cos_ref, sin_ref, q_ref, k_ref, v_ref, sga_ref, sgr_ref, last_refs=None):
    def rope(t):
        cos = cos_ref[...]
        sin = sin_ref[...]
        lane = lax.broadcasted_iota(I32, cos.shape, 1)
        first_half = (lane % HEAD_DIM) < (HEAD_DIM // 2)
        rot = jnp.where(first_half, pltpu.roll(t, LANES - HEAD_DIM // 2, 1), pltpu.roll(t, HEAD_DIM // 2, 1))
        return t * cos + rot * sin

    def q_piece(c0):
        def run():
            qf = _dot(h_ref[...], w_ref[:, c0:c0 + PIECE_COLS])
            for c in range(PIECE_COLS // LANES):
                q_ref[:, c0 + c * LANES:c0 + (c + 1) * LANES] = rope(qf[:, c * LANES:(c + 1) * LANES]).astype(BF16)
        return run

    def kv_piece():
        kv = _dot(h_ref[...], w_ref[:, _O1:_O3])
        for c in range(KV_DIM // LANES):
            k_ref[:, c * LANES:(c + 1) * LANES] = rope(kv[:, c * LANES:(c + 1) * LANES])
        v_ref[...] = kv[:, KV_DIM:]
        if last_refs is not None:
            t = k_ref.shape[0]
            last_refs[0][...] = k_ref[t - WINDOW:t, :]
            last_refs[1][...] = v_ref[t - WINDOW:t, :]

    def gate_piece(o_ref, base, c0):
        def run():
            z = _dot(h_ref[...], w_ref[:, base + c0:base + c0 + PIECE_COLS])
            o_ref[:, c0:c0 + PIECE_COLS] = _sigmoid(z).astype(BF16)
        return run

    pieces = [q_piece(c0) for c0 in range(0, Q_DIM, PIECE_COLS)] + [kv_piece]
    pieces += [gate_piece(sga_ref, _O5, c0) for c0 in range(0, D_MODEL, PIECE_COLS)]
    pieces += [gate_piece(sgr_ref, _O6, c0) for c0 in range(0, D_MODEL, PIECE_COLS)]
    return pieces


def _inproj_kernel(x_ref, sc_ref, sh_ref, g_ref, cos_ref, sin_ref, w_ref,
                   q_ref, k_ref, v_ref, xl_ref, yl_ref, sga_ref, sgr_ref, hbuf):
    hbuf[...] = _prenorm(x_ref, g_ref, sc_ref, sh_ref)
    xl_ref[...] = _dot(hbuf[...], w_ref[:, _O3:_O4]).astype(BF16)
    yl_ref[...] = _dot(hbuf[...], w_ref[:, _O4:_O5]).astype(BF16)
    for piece in _qkv_gate_pieces(hbuf, w_ref, cos_ref, sin_ref, q_ref, k_ref, v_ref, sga_ref, sgr_ref):
        piece()


def _inproj(x, sc, sh, g, cos, sin, w_in, rows_per_mod, pos_tiles):
    n = x.shape[0]
    tm = min(ROW_TILE, n)
    if rows_per_mod:
        tiles_per_mod = rows_per_mod // tm
        mod_spec = pl.BlockSpec((None, 1, D_MODEL), lambda i: (i // tiles_per_mod, 0, 0))
    else:
        mod_spec = pl.BlockSpec((tm, D_MODEL), lambda i: (i, 0))
    row = lambda w: pl.BlockSpec((tm, w), lambda i: (i, 0))
    outs = [(Q_DIM, BF16), (KV_DIM, F32), (KV_DIM, F32), (LRU_WIDTH, BF16), (LRU_WIDTH, BF16),
            (D_MODEL, BF16), (D_MODEL, BF16)]
    return pl.pallas_call(
        _inproj_kernel,
        grid=(n // tm,),
        in_specs=[row(D_MODEL), mod_spec, mod_spec,
                  pl.BlockSpec((1, D_MODEL), lambda i: (0, 0)),
                  pl.BlockSpec((tm, LANES), lambda i: (i % pos_tiles, 0)),
                  pl.BlockSpec((tm, LANES), lambda i: (i % pos_tiles, 0)),
                  pl.BlockSpec((D_MODEL, IN_DIM), lambda i: (0, 0))],
        out_specs=[row(w) for w, _ in outs],
        out_shape=[jax.ShapeDtypeStruct((n, w), dt) for w, dt in outs],
        scratch_shapes=[pltpu.VMEM((tm, D_MODEL), BF16)],
        compiler_params=_cparams(1),
        name="in_proj",
    )(x, sc, sh, g, cos, sin, w_in)


def _head_masks(shape):
    lane = lax.broadcasted_iota(I32, shape, 1)
    return [(lane // HEAD_DIM) == h for h in range(N_KV_HEADS)]


def _attention_core(q_perm, kall, vall, valid, sink_of, rows):
    masks_b = _head_masks((rows, KV_DIM))
    zero_b = jnp.zeros((rows, KV_DIM), BF16)
    pieces = []
    for h in range(N_KV_HEADS):
        for g in range(GQA_GROUP):
            pieces.append(jnp.where(masks_b[h], q_perm[g], zero_b))
    q_big = jnp.concatenate(pieces, axis=0)
    s_all = lax.dot_general(q_big, kall, (((1,), (1,)), ((), ())), preferred_element_type=F32)
    s_all = s_all * (HEAD_DIM ** -0.5)
    p_chunks, inv_chunks = [], []
    for h in range(N_KV_HEADS):
        for g in range(GQA_GROUP):
            c = h * GQA_GROUP + g
            s = jnp.where(valid, s_all[c * rows:(c + 1) * rows], NEG_INF)
            sink = sink_of(h, g)
            m = jnp.maximum(jnp.max(s, axis=-1, keepdims=True), sink)
            p = jnp.exp(s - m)
            denom = jnp.sum(p, axis=-1, keepdims=True) + jnp.exp(sink - m)
            p_chunks.append(p.astype(BF16))
            inv_chunks.append(1.0 / denom)
    o_all = _dot(jnp.concatenate(p_chunks, axis=0), vall)
    outs = []
    for g in range(GQA_GROUP):
        acc = jnp.zeros((rows, KV_DIM), F32)
        for h in range(N_KV_HEADS):
            c = h * GQA_GROUP + g
            acc = acc + jnp.where(masks_b[h], o_all[c * rows:(c + 1) * rows] * inv_chunks[c], 0.0)
        outs.append(acc)
    return outs


def _attn_prompt_kernel(sink_ref, q_ref, kc_ref, kp_ref, vc_ref, vp_ref, o_ref):
    j = pl.program_id(1)
    kall = jnp.concatenate([kp_ref[...], kc_ref[...]], axis=0).astype(BF16)
    vall = jnp.concatenate([vp_ref[...], vc_ref[...]], axis=0).astype(BF16)
    qi = lax.broadcasted_iota(I32, (WINDOW, 2 * WINDOW), 0)
    kj = lax.broadcasted_iota(I32, (WINDOW, 2 * WINDOW), 1)
    dist = qi + WINDOW - kj
    valid = (dist >= 0) & (dist <= WINDOW) & ((kj >= WINDOW) | (j > 0))
    q_perm = [q_ref[:, g * KV_DIM:(g + 1) * KV_DIM] for g in range(GQA_GROUP)]
    outs = _attention_core(q_perm, kall, vall, valid, lambda h, g: sink_ref[h * GQA_GROUP + g], WINDOW)
    for g in range(GQA_GROUP):
        o_ref[:, g * KV_DIM:(g + 1) * KV_DIM] = outs[g].astype(BF16)


def _attn_prompt(sinks, q, k, v, batch, seq):
    nb = seq // WINDOW
    cur = lambda w: pl.BlockSpec((WINDOW, w), lambda b, j: (b * nb + j, 0))
    prev = lambda w: pl.BlockSpec((WINDOW, w), lambda b, j: (b * nb + jnp.maximum(j - 1, 0), 0))
    return pl.pallas_call(
        _attn_prompt_kernel,
        grid=(batch, nb),
        in_specs=[pl.BlockSpec(memory_space=pltpu.SMEM),
                  cur(Q_DIM), cur(KV_DIM), prev(KV_DIM), cur(KV_DIM), prev(KV_DIM)],
        out_specs=cur(Q_DIM),
        out_shape=jax.ShapeDtypeStruct((batch * seq, Q_DIM), BF16),
        compiler_params=_cparams(2),
        name="attn_prompt",
    )(sinks, q, k, k, v, v)


SEQ_PER_STEP = 8


def _attn_sample_kernel(sink_ref, q_ref, kn_ref, vn_ref, kc_ref, vc_ref, o_ref, kbuf, vbuf, *, t_new):
    rows = GQA_GROUP * t_new
    kbuf[WINDOW:2 * WINDOW, :] = jnp.zeros((WINDOW, KV_DIM), F32)
    vbuf[WINDOW:2 * WINDOW, :] = jnp.zeros((WINDOW, KV_DIM), F32)
    ri = lax.broadcasted_iota(I32, (rows, 2 * WINDOW), 0)
    kj = lax.broadcasted_iota(I32, (rows, 2 * WINDOW), 1)
    tq = ri // GQA_GROUP
    valid = (kj >= tq) & (kj <= tq + WINDOW) & (kj < WINDOW + t_new)
    for s in range(SEQ_PER_STEP):
        kbuf[0:WINDOW, :] = kc_ref[s]
        vbuf[0:WINDOW, :] = vc_ref[s]
        kbuf[WINDOW:WINDOW + t_new, :] = kn_ref[s]
        vbuf[WINDOW:WINDOW + t_new, :] = vn_ref[s]
        kall = kbuf[...].astype(BF16)
        vall = vbuf[...].astype(BF16)
        qs = q_ref[s]
        masks_b = _head_masks((rows, KV_DIM))
        zero_b = jnp.zeros((rows, KV_DIM), BF16)
        q_big = jnp.concatenate([jnp.where(masks_b[h], qs, zero_b) for h in range(N_KV_HEADS)], axis=0)
        s_all = lax.dot_general(q_big, kall, (((1,), (1,)), ((), ())), preferred_element_type=F32)
        s_all = s_all * (HEAD_DIM ** -0.5)
        acc = jnp.zeros((rows, KV_DIM), F32)
        p_chunks, inv_chunks = [], []
        for h in range(N_KV_HEADS):
            sc = jnp.where(valid, s_all[h * rows:(h + 1) * rows], NEG_INF)
            sink = sink_ref[h]
            m = jnp.maximum(jnp.max(sc, axis=-1, keepdims=True), sink)
            p = jnp.exp(sc - m)
            denom = jnp.sum(p, axis=-1, keepdims=True) + jnp.exp(sink - m)
            p_chunks.append(p.astype(BF16))
            inv_chunks.append(1.0 / denom)
        o_all = _dot(jnp.concatenate(p_chunks, axis=0), vall)
        for h in range(N_KV_HEADS):
            acc = acc + jnp.where(masks_b[h], o_all[h * rows:(h + 1) * rows] * inv_chunks[h], 0.0)
        o_ref[s] = acc.astype(BF16)


def _attn_sample(sink_rows, q, kn, vn, kc, vc):
    nseq, rows, _ = q.shape
    t_new = kn.shape[1]
    sb = SEQ_PER_STEP
    blk = lambda r: pl.BlockSpec((sb, r, KV_DIM), lambda i: (i, 0, 0))
    return pl.pallas_call(
        functools.partial(_attn_sample_kernel, t_new=t_new),
        grid=(nseq // sb,),
        in_specs=[pl.BlockSpec((N_KV_HEADS, rows, 1), lambda i: (0, 0, 0)),
                  blk(rows), blk(t_new), blk(t_new), blk(WINDOW), blk(WINDOW)],
        out_specs=blk(rows),
        out_shape=jax.ShapeDtypeStruct((nseq, rows, KV_DIM), BF16),
        scratch_shapes=[pltpu.VMEM((2 * WINDOW, KV_DIM), F32), pltpu.VMEM((2 * WINDOW, KV_DIM), F32)],
        compiler_params=_cparams(1),
        name="attn_sample",
    )(sink_rows, q, kn, vn, kc, vc)


def _gelu_tanh(x):
    return 0.5 * x * (1.0 + jnp.tanh(0.7978845608028654 * (x + 0.044715 * x * x * x)))


def _lru_gates(xc, wbd_ref, ba, bx, lam):
    xcb = xc.astype(BF16)
    r_parts, i_parts = [], []
    for gidx in range(LRU_WIDTH // MXU_DIM):
        z = _dot(xcb[:, gidx * MXU_DIM:(gidx + 1) * MXU_DIM], wbd_ref[gidx])
        r_parts.append(z[:, :MXU_DIM])
        i_parts.append(z[:, MXU_DIM:])
    return _gate_math(jnp.concatenate(r_parts, axis=1), jnp.concatenate(i_parts, axis=1), xc, ba, bx, lam)


def _gate_math(zr, zi, xc, ba, bx, lam):
    r = _sigmoid(zr + ba)
    i = _sigmoid(zi + bx)
    softplus_neg_lam = jnp.maximum(-lam, 0.0) + jnp.log1p(jnp.exp(-jnp.abs(lam)))
    log_a = -LRU_C * r * softplus_neg_lam
    a = jnp.exp(log_a)
    u = jnp.sqrt(1.0 - jnp.exp(2.0 * log_a)) * (i * xc)
    return a, u


LRU_CHUNK = 64


def _lru_chunk(c, xbuf, ybuf, hcar, ol_ref, wc_ref, bc_ref, wbd_ref, ba_ref, bx_ref, lam_ref):
    w = LRU_WIDTH
    n = LRU_CHUNK
    rows = slice(c * n, (c + 1) * n)
    r0 = SUBLANES + c * n
    xc = xbuf[r0:r0 + n, :] * wc_ref[CONV_WIDTH - 1:CONV_WIDTH, :] + bc_ref[...]
    for k in range(1, CONV_WIDTH):
        xc = xc + xbuf[r0 - k:r0 - k + n, :] * wc_ref[CONV_WIDTH - 1 - k:CONV_WIDTH - k, :]
    a, u = _lru_gates(xc, wbd_ref, ba_ref[...], bx_ref[...], lam_ref[...])

    ng = n // SUBLANES
    a3 = a.reshape(ng, SUBLANES, w)
    u3 = u.reshape(ng, SUBLANES, w)
    row = lax.broadcasted_iota(I32, (ng, SUBLANES, w), 1)
    d = 1
    while d < SUBLANES:
        a_s = jnp.where(row >= d, pltpu.roll(a3, d, 1), 1.0)
        u_s = jnp.where(row >= d, pltpu.roll(u3, d, 1), 0.0)
        u3 = a3 * u_s + u3
        a3 = a3 * a_s
        d *= 2
    carry = hcar[...]
    hs = []
    for gi in range(ng):
        hg = a3[gi] * carry + u3[gi]
        hs.append(hg)
        carry = hg[SUBLANES - 1:SUBLANES, :]
    hcar[...] = carry
    h = jnp.concatenate(hs, axis=0)
    ol_ref[rows, :] = (h * _gelu_tanh(ybuf[rows, :])).astype(BF16)


def _inproj_lru_kernel(x_ref, sc_ref, sh_ref, g_ref, cos_ref, sin_ref, w_ref,
                       wc_ref, bc_ref, wbd_ref, ba_ref, bx_ref, lam_ref,
                       q_ref, k_ref, v_ref, sga_ref, sgr_ref, klast_ref, vlast_ref, ol_ref, xtail_ref, hlast_ref,
                       hbuf, xbuf, ybuf, hcar, *, tiles_per_seq):
    t = x_ref.shape[0]

    @pl.when(pl.program_id(0) % tiles_per_seq == 0)
    def _():
        xbuf[0:SUBLANES, :] = jnp.zeros((SUBLANES, LRU_WIDTH), F32)
        hcar[...] = jnp.zeros((1, LRU_WIDTH), F32)

    hbuf[...] = _prenorm(x_ref, g_ref, sc_ref, sh_ref)
    xbuf[SUBLANES:SUBLANES + t, :] = _dot(hbuf[...], w_ref[:, _O3:_O4])
    ybuf[...] = _dot(hbuf[...], w_ref[:, _O4:_O5])

    pieces = _qkv_gate_pieces(hbuf, w_ref, cos_ref, sin_ref, q_ref, k_ref, v_ref, sga_ref, sgr_ref,
                              last_refs=(klast_ref, vlast_ref))
    n_chunks = t // LRU_CHUNK
    per_chunk = -(-len(pieces) // n_chunks)
    for c in range(n_chunks):
        _lru_chunk(c, xbuf, ybuf, hcar, ol_ref, wc_ref, bc_ref, wbd_ref, ba_ref, bx_ref, lam_ref)
        for piece in pieces[c * per_chunk:(c + 1) * per_chunk]:
            piece()

    tail = xbuf[t:t + SUBLANES, :]
    xtail_ref[...] = tail
    xbuf[0:SUBLANES, :] = tail
    hlast_ref[...] = hcar[...]


def _inproj_lru(x, sc, sh, g, cos, sin, w_in, wc, bc, wbd, ba, bx, lam, batch, seq):
    n = x.shape[0]
    tm = min(ROW_TILE, seq)
    tps = seq // tm
    mod_spec = pl.BlockSpec((None, 1, D_MODEL), lambda i: (i // tps, 0, 0))
    row = lambda w: pl.BlockSpec((tm, w), lambda i: (i, 0))
    full = lambda a: pl.BlockSpec(a.shape, lambda i: (0,) * a.ndim)
    per_seq = lambda r, w: pl.BlockSpec((None, r, w), lambda i: (i // tps, 0, 0))
    outs = [(Q_DIM, BF16), (KV_DIM, F32), (KV_DIM, F32), (D_MODEL, BF16), (D_MODEL, BF16)]
    return pl.pallas_call(
        functools.partial(_inproj_lru_kernel, tiles_per_seq=tps),
        grid=(n // tm,),
        in_specs=[row(D_MODEL), mod_spec, mod_spec, full(g),
                  pl.BlockSpec((tm, LANES), lambda i: (i % tps, 0)),
                  pl.BlockSpec((tm, LANES), lambda i: (i % tps, 0)),
                  full(w_in), full(wc), full(bc), full(wbd), full(ba), full(bx), full(lam)],
        out_specs=[row(w) for w, _ in outs] + [per_seq(WINDOW, KV_DIM), per_seq(WINDOW, KV_DIM)]
        + [row(LRU_WIDTH), per_seq(SUBLANES, LRU_WIDTH), per_seq(1, LRU_WIDTH)],
        out_shape=[jax.ShapeDtypeStruct((n, w), dt) for w, dt in outs]
        + [jax.ShapeDtypeStruct((batch, WINDOW, KV_DIM), F32)] * 2
        + [jax.ShapeDtypeStruct((n, LRU_WIDTH), BF16),
           jax.ShapeDtypeStruct((batch, SUBLANES, LRU_WIDTH), F32),
           jax.ShapeDtypeStruct((batch, 1, LRU_WIDTH), F32)],
        scratch_shapes=[pltpu.VMEM((tm, D_MODEL), BF16),
                        pltpu.VMEM((2 * SUBLANES + tm, LRU_WIDTH), F32), pltpu.VMEM((tm, LRU_WIDTH), F32),
                        pltpu.VMEM((1, LRU_WIDTH), F32)],
        compiler_params=_cparams(1, vmem=VMEM_LIMIT_BIG),
        name="in_proj_lru",
    )(x, sc, sh, g, cos, sin, w_in, wc, bc, wbd, ba, bx, lam)


def _lru_sample_kernel(xl_ref, yl_ref, cs_ref, h0_ref, wc_ref, bc_ref, wbd_ref, ba_ref, bx_ref, lam_ref,
                       o_ref, hlast_ref):
    t_new, nseq, w = xl_ref.shape
    xp = [cs_ref[k] for k in range(CONV_WIDTH - 1)] + [xl_ref[k].astype(F32) for k in range(t_new)]
    xcs = []
    for t in range(t_new):
        acc = bc_ref[...] + xp[t] * wc_ref[0:1, :]
        for k in range(1, CONV_WIDTH):
            acc = acc + xp[t + k] * wc_ref[k:k + 1, :]
        xcs.append(acc)
    xc = jnp.concatenate(xcs, axis=0)
    a, u = _lru_gates(xc, wbd_ref, ba_ref[...], bx_ref[...], lam_ref[...])
    h = h0_ref[...]
    for t in range(t_new):
        h = a[t * nseq:(t + 1) * nseq] * h + u[t * nseq:(t + 1) * nseq]
        o_ref[t] = (h * _gelu_tanh(yl_ref[t].astype(F32))).astype(BF16)
    hlast_ref[...] = h


def _lru_sample(xl, yl, cs, h0, wc, bc, wbd, ba, bx, lam):
    t_new, nseq, w = xl.shape
    full = lambda shp: pl.BlockSpec(shp, lambda i: (0,) * len(shp))
    args = (xl, yl, cs, h0, wc, bc, wbd, ba, bx, lam)
    return pl.pallas_call(
        _lru_sample_kernel,
        grid=(1,),
        in_specs=[full(a.shape) for a in args],
        out_specs=[full((t_new, nseq, w)), full((nseq, w))],
        out_shape=[jax.ShapeDtypeStruct((t_new, nseq, w), BF16), jax.ShapeDtypeStruct((nseq, w), F32)],
        compiler_params=_cparams(1),
        name="lru_sample",
    )(*args)


def _post_kernel(oa_ref, ol_ref, sga_ref, sgr_ref, x_ref, gta_ref, scf_ref, shf_ref,
                 wab_ref, wlb_ref, wout_ref, gpm_ref, gpf_ref, wrh_ref, wrl_ref, br_ref, tri_ref, cin_ref,
                 x1_ref, h2_ref, route_ref, stat_ref, cnt_ref, carry):
    i = pl.program_id(0)

    @pl.when(i == 0)
    def _():
        carry[...] = cin_ref[...]

    b_attn = _dot(oa_ref[...], wab_ref[...])
    b_lru = _dot(ol_ref[...], wlb_ref[...])
    merged = sga_ref[...].astype(F32) * b_attn + sgr_ref[...].astype(F32) * b_lru
    mix = _dot(merged.astype(BF16), wout_ref[...])
    x1 = x_ref[...] + gta_ref[...] * _rms(mix, gpm_ref[...])
    x1_ref[...] = x1
    h2 = _rms(x1, gpf_ref[...]) * (1.0 + scf_ref[...]) + shf_ref[...]
    h2_ref[...] = h2.astype(BF16)

    h_hi = h2.astype(BF16)
    h_lo = (h2 - h_hi.astype(F32)).astype(BF16)
    logits = _dot(h_hi, wrh_ref[...]) + (_dot(h_lo, wrh_ref[...]) + _dot(h_hi, wrl_ref[...])) + br_ref[...]

    tm = logits.shape[0]
    lane = lax.broadcasted_iota(I32, (tm, ROUTE_LANES), 1)
    big = jnp.int32(ROUTE_LANES)
    is_g = lane < N_GROUPS
    lg = jnp.where(is_g, logits, NEG_INF)
    mg = jnp.max(lg, axis=-1, keepdims=True)
    g_star = jnp.min(jnp.where(lg == mg, lane, big), axis=-1, keepdims=True)
    p_star = 1.0 / jnp.sum(jnp.where(is_g, jnp.exp(lg - mg), 0.0), axis=-1, keepdims=True)
    lo = N_GROUPS + g_star * EXPERTS_PER_GROUP
    in_grp = (lane >= lo) & (lane < lo + EXPERTS_PER_GROUP)
    le = jnp.where(in_grp, logits, NEG_INF)
    m1 = jnp.max(le, axis=-1, keepdims=True)
    i1 = jnp.min(jnp.where(le == m1, lane, big), axis=-1, keepdims=True)
    le2 = jnp.where(lane == i1, NEG_INF, le)
    m2 = jnp.max(le2, axis=-1, keepdims=True)
    i2 = jnp.min(jnp.where(le2 == m2, lane, big), axis=-1, keepdims=True)
    e2x = jnp.exp(m2 - m1)
    wsum = 1.0 + e2x
    w1 = (1.0 / wsum) * p_star
    w2 = (e2x / wsum) * p_star

    oh1 = lane == i1
    oh2 = lane == i2
    cnt = jnp.where(oh1 | oh2, 1.0, 0.0)
    excl = _dot(tri_ref[...], cnt.astype(BF16))
    per_e = jnp.sum(cnt, axis=0, keepdims=True)
    pad8 = jnp.floor((per_e + (GRANULE - 1.0)) * (1.0 / GRANULE)) * GRANULE
    incl = jnp.broadcast_to(pad8, (SUBLANES, ROUTE_LANES))
    lane8 = lax.broadcasted_iota(I32, (SUBLANES, ROUTE_LANES), 1)
    d = 1
    while d < ROUTE_LANES:
        incl = incl + jnp.where(lane8 >= d, pltpu.roll(incl, d, 1), 0.0)
        d *= 2
    seg_start = incl[0:1, :] - pad8
    pos = excl + seg_start
    s1 = jnp.sum(jnp.where(oh1, pos, 0.0), axis=-1, keepdims=True)
    s2 = jnp.sum(jnp.where(oh2, pos, 0.0), axis=-1, keepdims=True)
    rec = jnp.where(lane == 0, s1, 0.0)
    rec = jnp.where(lane == 1, s2, rec)
    rec = jnp.where(lane == 2, w1, rec)
    rec = jnp.where(lane == 3, w2, rec)
    route_ref[...] = rec
    srow = lax.broadcasted_iota(I32, (SUBLANES, ROUTE_LANES), 0)
    stat_ref[...] = jnp.where(srow == 0, pad8, jnp.where(srow == 1, carry[...], 0.0))
    carry[...] = carry[...] + pad8
    cnt_ref[...] = carry[...]


def _post(oa, ol, sga, sgr, x, gta, scf, shf, wab, wlb, wout, gpm, gpf, wrh, wrl, br, tri, cin, rows_per_mod):
    n = x.shape[0]
    tm = min(ROW_TILE, n)
    if rows_per_mod:
        tiles_per_mod = rows_per_mod // tm
        mod_spec = pl.BlockSpec((None, 1, D_MODEL), lambda i: (i // tiles_per_mod, 0, 0))
    else:
        mod_spec = pl.BlockSpec((tm, D_MODEL), lambda i: (i, 0))
    row = lambda w: pl.BlockSpec((tm, w), lambda i: (i, 0))
    full = lambda a: pl.BlockSpec(a.shape, lambda i: (0,) * a.ndim)
    return pl.pallas_call(
        _post_kernel,
        grid=(n // tm,),
        in_specs=[row(Q_DIM), row(LRU_WIDTH), row(D_MODEL), row(D_MODEL), row(D_MODEL),
                  mod_spec, mod_spec, mod_spec,
                  full(wab), full(wlb), full(wout), full(gpm), full(gpf), full(wrh), full(wrl), full(br),
                  full(tri), full(cin)],
        out_specs=[row(D_MODEL), row(D_MODEL), row(ROUTE_LANES),
                   pl.BlockSpec((SUBLANES, ROUTE_LANES), lambda i: (i, 0)),
                   pl.BlockSpec((1, ROUTE_LANES), lambda i: (0, 0))],
        out_shape=[jax.ShapeDtypeStruct((n, D_MODEL), F32), jax.ShapeDtypeStruct((n, D_MODEL), BF16),
                   jax.ShapeDtypeStruct((n, ROUTE_LANES), F32),
                   jax.ShapeDtypeStruct((n // tm * SUBLANES, ROUTE_LANES), F32),
                   jax.ShapeDtypeStruct((1, ROUTE_LANES), F32)],
        scratch_shapes=[pltpu.VMEM((1, ROUTE_LANES), F32)],
        compiler_params=_cparams(1),
        name="post_mix",
    )(oa, ol, sga, sgr, x, gta, scf, shf, wab, wlb, wout, gpm, gpf, wrh, wrl, br, tri, cin)


BLOCK_GRANULES = EXPERT_ROWS // GRANULE
BLOCK_SHIFT = BLOCK_GRANULES.bit_length() - 1
assert BLOCK_GRANULES == 1 << BLOCK_SHIFT


def _padded(c):
    return ((c + (BLOCK_GRANULES - 1)) >> BLOCK_SHIFT) << BLOCK_SHIFT


def _sorted_rows(tm):
    r = 2 * tm + N_EXPERTS * (GRANULE - 1)
    return -(-r // MXU_DIM) * MXU_DIM


def _plan_kernel(tot_ref, be_ref, meta_ref, *, n_blocks):
    def fill(j, _):
        be_ref[j] = N_EXPERTS - 1
        return 0
    lax.fori_loop(0, n_blocks, fill, 0)

    def per_expert(e, nb):
        k = _padded(tot_ref[e]) >> BLOCK_SHIFT

        def put(b, _):
            be_ref[nb + b] = e
            return 0
        lax.fori_loop(0, k, put, 0)
        return nb + k
    n_active = lax.fori_loop(0, N_EXPERTS, per_expert, 0)
    meta_ref[0] = n_active


def _plan(totals, n_blocks):
    return pl.pallas_call(
        functools.partial(_plan_kernel, n_blocks=n_blocks),
        in_specs=[pl.BlockSpec(memory_space=pltpu.SMEM)],
        out_specs=[pl.BlockSpec(memory_space=pltpu.SMEM), pl.BlockSpec(memory_space=pltpu.SMEM)],
        out_shape=[jax.ShapeDtypeStruct((n_blocks,), I32), jax.ShapeDtypeStruct((1,), I32)],
        name="moe_plan",
    )(totals)


def _expert_starts(tot_ref, pstart):
    def body(e, acc):
        pstart[e] = acc
        return acc + _padded(tot_ref[e])
    return lax.fori_loop(0, N_EXPERTS, body, 0)


def _granule(ref, g):
    return ref.at[pl.ds(pl.multiple_of(g * GRANULE, GRANULE), GRANULE)]


def _pack_halves(lo_f32, hi_f32):
    return (pltpu.bitcast(lo_f32, U32) >> 16) | (pltpu.bitcast(hi_f32, U32) & jnp.uint32(0xFFFF0000))


def _unpack_halves(packed):
    lo = pltpu.bitcast(packed << 16, F32).astype(BF16)
    hi = pltpu.bitcast(packed & jnp.uint32(0xFFFF0000), F32).astype(BF16)
    return lo, hi


def _sort_kernel(tot_ref, cnt_ref, goff_ref, rec_p_ref, h2_p_ref, rec_s_ref, h2_s_ref,
                 srt_ref, gsrc_ref, pstart, *, tiles_p, n_sorted, n_slots):
    i = pl.program_id(0)
    from_sample = i >= tiles_p
    rec = jnp.where(from_sample, rec_s_ref[...], rec_p_ref[...])
    h2 = jnp.where(from_sample, h2_s_ref[...], h2_p_ref[...])
    tm = h2.shape[0]

    rec_t = rec.T
    s1 = rec_t[0:1, :].astype(I32)
    s2 = rec_t[1:2, :].astype(I32)
    rows = lax.broadcasted_iota(I32, (n_sorted, tm), 0)
    sel = jnp.where((rows == s1) | (rows == s2), 1.0, 0.0).astype(BF16)
    srt = _dot(sel, h2)
    srt_ref[...] = _pack_halves(srt[:, :HALF_D], srt[:, HALF_D:])

    zero_granule = n_sorted // GRANULE - 1

    @pl.when(i == 0)
    def _():
        used = _expert_starts(tot_ref, pstart)

        def put_zero(g, _):
            gsrc_ref[g] = zero_granule
            return 0

        def per_expert(e, _):
            total = tot_ref[e]
            lax.fori_loop(pstart[e] + total, pstart[e] + _padded(total), put_zero, 0)
            return 0
        lax.fori_loop(0, N_EXPERTS, per_expert, 0)
        lax.fori_loop(used, n_slots // GRANULE, put_zero, 0)

    def per_expert(e, seg):
        k = cnt_ref[i * N_EXPERTS + e]
        dst = pstart[e] + goff_ref[i * N_EXPERTS + e]
        src = i * (n_sorted // GRANULE) + seg

        def put(g, _):
            gsrc_ref[dst + g] = src + g
            return 0
        lax.fori_loop(0, k, put, 0)
        return seg + k
    lax.fori_loop(0, N_EXPERTS, per_expert, 0, unroll=4)


def _sort(totals, cnt8, goff, rec_p, h2_p, rec_s, h2_s, n_slots):
    tm = min(ROW_TILE, h2_p.shape[0])
    assert h2_s.shape[0] % tm == 0
    tiles_p = h2_p.shape[0] // tm
    tiles_s = h2_s.shape[0] // tm
    n_sorted = _sorted_rows(tm)
    row_p = lambda w: pl.BlockSpec((tm, w), lambda i, *_: (jnp.minimum(i, tiles_p - 1), 0))
    row_s = lambda w: pl.BlockSpec((tm, w), lambda i, *_: (jnp.maximum(i - tiles_p, 0), 0))
    return pl.pallas_call(
        functools.partial(_sort_kernel, tiles_p=tiles_p, n_sorted=n_sorted, n_slots=n_slots),
        grid_spec=pltpu.PrefetchScalarGridSpec(
            num_scalar_prefetch=3,
            grid=(tiles_p + tiles_s,),
            in_specs=[row_p(ROUTE_LANES), row_p(D_MODEL), row_s(ROUTE_LANES), row_s(D_MODEL)],
            out_specs=[pl.BlockSpec((n_sorted, HALF_D), lambda i, *_: (i, 0)),
                       pl.BlockSpec(memory_space=pltpu.SMEM)],
            scratch_shapes=[pltpu.SMEM((N_EXPERTS,), I32)]),
        out_shape=[jax.ShapeDtypeStruct(((tiles_p + tiles_s) * n_sorted, HALF_D), U32),
                   jax.ShapeDtypeStruct((n_slots // GRANULE,), I32)],
        compiler_params=_cparams(1),
        name="moe_sort",
    )(totals, cnt8, goff, rec_p, h2_p, rec_s, h2_s)


def _expert_kernel(be_ref, meta_ref, gsrc_ref, srt_hbm, wg_ref, wu_ref, wd_ref, ys_ref,
                   xbuf, wgb, wub, wdb, sems):
    j = pl.program_id(0)
    n_active = meta_ref[0]
    gran_per_block = BLOCK_GRANULES

    def granule_copy(blk, g, slot):
        return pltpu.make_async_copy(_granule(srt_hbm, gsrc_ref[blk * gran_per_block + g]),
                                     xbuf.at[slot, pl.ds(g * GRANULE, GRANULE)], sems.at[slot])

    def gather(blk, slot):
        for g in range(gran_per_block):
            granule_copy(blk, g, slot).start()

    @pl.when(j == 0)
    def _():
        gather(0, 0)

    @pl.when(j < n_active)
    def _():
        slot = j % 2

        @pl.when(j + 1 < n_active)
        def _():
            gather(j + 1, 1 - slot)

        @pl.when((j == 0) | (be_ref[j] != be_ref[jnp.maximum(j - 1, 0)]))
        def _():
            wgb[...] = wg_ref[...].astype(BF16)
            wub[...] = wu_ref[...].astype(BF16)
            wdb[...] = wd_ref[...].astype(BF16)

        for g in range(gran_per_block):
            granule_copy(j, g, slot).wait()
        x_lo, x_hi = _unpack_halves(xbuf[slot])
        g = _dot(x_lo, wgb[0:HALF_D, :]) + _dot(x_hi, wgb[HALF_D:D_MODEL, :])
        u = _dot(x_lo, wub[0:HALF_D, :]) + _dot(x_hi, wub[HALF_D:D_MODEL, :])
        hmid = (g * _sigmoid(g) * u).astype(BF16)
        y = _dot(hmid, wdb[...])
        ys_ref[...] = _pack_halves(y[:, :HALF_D].astype(BF16).astype(F32), y[:, HALF_D:].astype(BF16).astype(F32))

    @pl.when(j >= meta_ref[0])
    def _():
        ys_ref[...] = jnp.zeros(ys_ref.shape, U32)


def _experts(block_e, meta, gsrc, srt, wg, wu, wd, n_slots):
    n_blocks = n_slots // EXPERT_ROWS
    wspec = lambda shp: pl.BlockSpec(
        (None,) + shp, lambda j, be, meta, gs: (be[jnp.minimum(j, meta[0] - 1)], 0, 0))
    return pl.pallas_call(
        _expert_kernel,
        grid_spec=pltpu.PrefetchScalarGridSpec(
            num_scalar_prefetch=3,
            grid=(n_blocks,),
            in_specs=[pl.BlockSpec(memory_space=pl.ANY),
                      wspec((D_MODEL, D_EXPERT)), wspec((D_MODEL, D_EXPERT)), wspec((D_EXPERT, D_MODEL))],
            out_specs=pl.BlockSpec((EXPERT_ROWS, HALF_D), lambda j, be, meta, gs: (j, 0)),
            scratch_shapes=[pltpu.VMEM((2, EXPERT_ROWS, HALF_D), U32),
                            pltpu.VMEM((D_MODEL, D_EXPERT), BF16), pltpu.VMEM((D_MODEL, D_EXPERT), BF16),
                            pltpu.VMEM((D_EXPERT, D_MODEL), BF16), pltpu.SemaphoreType.DMA((2,))]),
        out_shape=jax.ShapeDtypeStruct((n_slots, HALF_D), U32),
        compiler_params=_cparams(1),
        name="moe_experts",
    )(block_e, meta, gsrc, srt, wg, wu, wd)


def _combine_kernel(tot_ref, cnt_ref, goff_ref, rec_ref, x1_ref, gtf_ref, g_ref, ys_hbm,
                    y_ref, cbuf, pstart, n_issued, sems, *, tile_base, n_tiles):
    i = pl.program_id(0)
    slot = i % 2
    tm = x1_ref.shape[0]
    n_sorted = cbuf.shape[1]

    def granule_copy(slot_granule, tile_granule, slot_):
        return pltpu.make_async_copy(_granule(ys_hbm, slot_granule), _granule(cbuf.at[slot_], tile_granule),
                                     sems.at[slot_])

    def gather(tile, slot_):
        def per_expert(e, carry):
            seg, n = carry
            k = cnt_ref[tile * N_EXPERTS + e]
            src = pstart[e] + goff_ref[tile * N_EXPERTS + e]

            def gran(g, _):
                granule_copy(src + g, seg + g, slot_).start()
                return 0
            lax.fori_loop(0, k, gran, 0)
            return seg + k, n + k
        _, n = lax.fori_loop(0, N_EXPERTS, per_expert, (0, 0), unroll=4)
        n_issued[slot_] = n

    @pl.when(i == 0)
    def _():
        cbuf[...] = jnp.zeros(cbuf.shape, U32)
        _expert_starts(tot_ref, pstart)
        gather(tile_base, 0)

    @pl.when(i + 1 < n_tiles)
    def _():
        gather(tile_base + i + 1, 1 - slot)

    def drain(r, _):
        granule_copy(0, 0, slot).wait()
        return 0
    lax.fori_loop(0, n_issued[slot], drain, 0)

    rec = rec_ref[...]
    s1 = rec[:, 0:1].astype(I32)
    s2 = rec[:, 1:2].astype(I32)
    col = lax.broadcasted_iota(I32, (tm, n_sorted), 1)
    wmat = (jnp.where(col == s1, rec[:, 2:3], 0.0) + jnp.where(col == s2, rec[:, 3:4], 0.0)).astype(BF16)
    y_lo, y_hi = _unpack_halves(cbuf[slot])
    f = jnp.concatenate([_dot(wmat, y_lo), _dot(wmat, y_hi)], axis=1)
    y_ref[...] = x1_ref[...] + gtf_ref[...] * _rms(f, g_ref[...])


def _combine(totals, cnt8, goff, rec, x1, gtf, g, ys, rows_per_mod, tile_base):
    n = x1.shape[0]
    tm = min(ROW_TILE, n)
    if rows_per_mod:
        tiles_per_mod = rows_per_mod // tm
        mod_spec = pl.BlockSpec((None, 1, D_MODEL), lambda i, *_: (i // tiles_per_mod, 0, 0))
    else:
        mod_spec = pl.BlockSpec((tm, D_MODEL), lambda i, *_: (i, 0))
    row = lambda w: pl.BlockSpec((tm, w), lambda i, *_: (i, 0))
    return pl.pallas_call(
        functools.partial(_combine_kernel, tile_base=tile_base, n_tiles=n // tm),
        grid_spec=pltpu.PrefetchScalarGridSpec(
            num_scalar_prefetch=3,
            grid=(n // tm,),
            in_specs=[row(ROUTE_LANES), row(D_MODEL), mod_spec,
                      pl.BlockSpec((1, D_MODEL), lambda i, *_: (0, 0)),
                      pl.BlockSpec(memory_space=pl.ANY)],
            out_specs=row(D_MODEL),
            scratch_shapes=[pltpu.VMEM((2, _sorted_rows(tm), HALF_D), U32),
                            pltpu.SMEM((N_EXPERTS,), I32), pltpu.SMEM((2,), I32),
                            pltpu.SemaphoreType.DMA((2,))]),
        out_shape=jax.ShapeDtypeStruct((n, D_MODEL), F32),
        compiler_params=_cparams(1),
        name="moe_combine",
    )(totals, cnt8, goff, rec, x1, gtf, g, ys)


def _rope_tables(pos):
    half = HEAD_DIM // 2
    inv = jnp.power(jnp.float32(ROPE_THETA), -jnp.arange(half, dtype=F32) / half)
    ang = pos.astype(F32)[:, None] * inv[None, :]
    cos = jnp.cos(ang)
    sin = jnp.sin(ang)
    reps = LANES // HEAD_DIM
    cos_t = jnp.tile(jnp.concatenate([cos, cos], axis=-1), (1, reps))
    sin_t = jnp.tile(jnp.concatenate([-sin, sin], axis=-1), (1, reps))
    return cos_t, sin_t


def _q_perm_index():
    g = jnp.arange(GQA_GROUP)[:, None, None]
    h = jnp.arange(N_KV_HEADS)[None, :, None]
    d = jnp.arange(HEAD_DIM)[None, None, :]
    return ((h * GQA_GROUP + g) * HEAD_DIM + d).reshape(-1)


def _block_diag_gates(w_a, w_x):
    per = MXU_DIM // LRU_BLOCK_W
    groups = LRU_BLOCKS // per

    def bd(w):
        w = w.reshape(groups, per, LRU_BLOCK_W, LRU_BLOCK_W)
        eye = jnp.eye(per, dtype=w.dtype)
        full = jnp.einsum('gpij,pq->gpiqj', w, eye)
        return full.reshape(groups, MXU_DIM, MXU_DIM)
    return jnp.concatenate([bd(w_a), bd(w_x)], axis=-1).astype(BF16)


def _layer_forward(xp, xs_tm, ck, cv, cconv, ch, mod_p, mod_s, p):
    batch, seq, _ = xp.shape
    nseq, _, _, _ = ck.shape
    t_new = xs_tm.shape[0] // nseq
    n_p = batch * seq
    n_s = xs_tm.shape[0]

    perm = _q_perm_index()
    w_in = p['w_in']
    w_in_b = jnp.concatenate([w_in[:, :Q_DIM][:, perm], w_in[:, Q_DIM:]], axis=1).astype(BF16)
    wab = p['w_attn_branch'][perm, :].astype(BF16)
    wlb = p['w_lru_branch'].astype(BF16)
    wout = p['w_out'].astype(BF16)
    wbd = _block_diag_gates(p['w_lru_a'], p['w_lru_x'])
    row = lambda v: v.reshape(1, -1)
    wr = jnp.concatenate([p['w_router_group'], p['w_router_expert'],
                          jnp.zeros((D_MODEL, ROUTE_LANES - N_GROUPS - N_EXPERTS), F32)], axis=1)
    wr_hi = wr.astype(BF16)
    wr_lo = (wr - wr_hi.astype(F32)).astype(BF16)
    br = jnp.concatenate([p['b_router_group'], p['b_router_expert'],
                          jnp.zeros((ROUTE_LANES - N_GROUPS - N_EXPERTS,), F32)]).reshape(1, -1)
    wg = p['w_exp_gate']
    wu = p['w_exp_up']
    wd = p['w_exp_down']

    def mods(mod):
        return [mod[:, k * D_MODEL:(k + 1) * D_MODEL] for k in range(6)]
    sh_a_p, sc_a_p, gt_a_p, sh_f_p, sc_f_p, gt_f_p = [m.reshape(batch, 1, D_MODEL) for m in mods(mod_p)]
    sh_a_s, sc_a_s, gt_a_s, sh_f_s, sc_f_s, gt_f_s = [jnp.tile(m, (t_new, 1)) for m in mods(mod_s)]

    lru_w = (p['w_conv'], row(p['b_conv']), wbd, row(p['b_lru_a']), row(p['b_lru_x']), row(p['lru_lambda']))
    cos_p, sin_p = _rope_tables(jnp.arange(seq, dtype=I32))
    q_p, k_p, v_p, sga_p, sgr_p, klast_p, vlast_p, ol_p, xtail_p, hlast_p = _inproj_lru(
        xp.reshape(n_p, D_MODEL), sc_a_p, sh_a_p, row(p['g_pre_mix']), cos_p, sin_p, w_in_b, *lru_w, batch, seq)
    pos_s = jnp.repeat(PAST_LEN_ + jnp.arange(t_new, dtype=I32), nseq)
    cos_s, sin_s = _rope_tables(pos_s)
    q_s, k_s, v_s, xl_s, yl_s, sga_s, sgr_s = _inproj(
        xs_tm, sc_a_s, sh_a_s, row(p['g_pre_mix']), cos_s, sin_s, w_in_b, rows_per_mod=0,
        pos_tiles=n_s // min(ROW_TILE, n_s))

    sinks_perm = p['sinks']
    oa_p = _attn_prompt(sinks_perm, q_p, k_p, v_p, batch, seq)
    rows = t_new * GQA_GROUP
    q_s3 = q_s.reshape(t_new, nseq, GQA_GROUP, KV_DIM).transpose(1, 0, 2, 3).reshape(nseq, rows, KV_DIM)
    kn = k_s.reshape(t_new, nseq, KV_DIM).transpose(1, 0, 2)
    vn = v_s.reshape(t_new, nseq, KV_DIM).transpose(1, 0, 2)
    kc = ck.reshape(nseq, WINDOW, KV_DIM)
    vc = cv.reshape(nseq, WINDOW, KV_DIM)
    sink_rows = jnp.tile(p['sinks'].reshape(N_KV_HEADS, 1, GQA_GROUP), (1, t_new, 1)).reshape(N_KV_HEADS, rows, 1)
    oa_s3 = _attn_sample(sink_rows, q_s3, kn, vn, kc, vc)
    oa_s = oa_s3.reshape(nseq, t_new, Q_DIM).transpose(1, 0, 2).reshape(n_s, Q_DIM)

    ol_s3, hlast_s = _lru_sample(xl_s.reshape(t_new, nseq, LRU_WIDTH), yl_s.reshape(t_new, nseq, LRU_WIDTH),
                                 cconv.transpose(1, 0, 2), ch, *lru_w)
    ol_s = ol_s3.reshape(n_s, LRU_WIDTH)

    tm_post = min(ROW_TILE, n_p)
    tri = jnp.tril(jnp.ones((tm_post, tm_post), F32), -1).astype(BF16)
    post_w = (wab, wlb, wout, row(p['g_post_mix']), row(p['g_pre_ffn']), wr_hi, wr_lo, br)
    zero_cnt = jnp.zeros((1, ROUTE_LANES), F32)
    x1_p, h2_p, route_p, stat_p, cnt_p = _post(oa_p, ol_p, sga_p, sgr_p, xp.reshape(n_p, D_MODEL),
                                               gt_a_p, sc_f_p, sh_f_p, *post_w, tri, zero_cnt, rows_per_mod=seq)
    tm_s = min(ROW_TILE, n_s)
    tri_s = tri if tm_s == tm_post else jnp.tril(jnp.ones((tm_s, tm_s), F32), -1).astype(BF16)
    x1_s, h2_s, route_s, stat_s, cnt_all = _post(oa_s, ol_s, sga_s, sgr_s, xs_tm,
                                                 gt_a_s, sc_f_s, sh_f_s, *post_w, tri_s, cnt_p, rows_per_mod=0)

    e_lanes = slice(N_GROUPS, N_GROUPS + N_EXPERTS)
    to_granules = lambda v: (v.astype(I32) // GRANULE).reshape(-1)
    totals = to_granules(cnt_all[0, e_lanes])
    stats = jnp.concatenate([stat_p, stat_s], axis=0).reshape(-1, SUBLANES, ROUTE_LANES)
    cnt8 = to_granules(stats[:, 0, e_lanes])
    goff = to_granules(stats[:, 1, e_lanes])
    tiles_p = n_p // tm_post
    n_tiles = tiles_p + n_s // tm_s
    max_rows = 2 * (n_p + n_s) + n_tiles * N_EXPERTS * (GRANULE - 1) + N_EXPERTS * (EXPERT_ROWS - GRANULE)
    n_blocks = -(-max_rows // EXPERT_ROWS)
    n_slots = n_blocks * EXPERT_ROWS
    srt, gsrc = _sort(totals, cnt8, goff, route_p, h2_p, route_s, h2_s, n_slots)
    block_e, meta = _plan(totals, n_blocks)
    ys = _experts(block_e, meta, gsrc, srt, wg, wu, wd, n_slots)
    y_p = _combine(totals, cnt8, goff, route_p, x1_p, gt_f_p, row(p['g_post_ffn']), ys, seq, 0)
    y_s = _combine(totals, cnt8, goff, route_s, x1_s, gt_f_s, row(p['g_post_ffn']), ys, 0, tiles_p)

    k_new_p = klast_p.reshape(batch, WINDOW, N_KV_HEADS, HEAD_DIM)
    v_new_p = vlast_p.reshape(batch, WINDOW, N_KV_HEADS, HEAD_DIM)
    conv_p = xtail_p[:, -(CONV_WIDTH - 1):]
    h_p = hlast_p.reshape(batch, LRU_WIDTH)
    k_new_s = jnp.concatenate([ck, kn.reshape(nseq, t_new, N_KV_HEADS, HEAD_DIM)], axis=1)[:, -WINDOW:]
    v_new_s = jnp.concatenate([cv, vn.reshape(nseq, t_new, N_KV_HEADS, HEAD_DIM)], axis=1)[:, -WINDOW:]
    xl_s3 = xl_s.reshape(t_new, nseq, LRU_WIDTH).transpose(1, 0, 2).astype(F32)
    conv_s = jnp.concatenate([cconv, xl_s3], axis=1)[:, -(CONV_WIDTH - 1):]
    return (y_p.reshape(batch, seq, D_MODEL), y_s, k_new_p, v_new_p, conv_p, h_p,
            k_new_s, v_new_s, conv_s, hlast_s)


PAST_LEN_ = 16384

PARAM_NAMES = ('w_ada', 'b_ada', 'g_pre_mix', 'g_post_mix', 'g_pre_ffn', 'g_post_ffn', 'w_in', 'sinks',
               'w_conv', 'b_conv', 'w_lru_a', 'b_lru_a', 'w_lru_x', 'b_lru_x', 'lru_lambda',
               'w_attn_branch', 'w_lru_branch', 'w_out', 'w_router_group', 'b_router_group',
               'w_router_expert', 'b_router_expert', 'w_exp_gate', 'w_exp_up', 'w_exp_down')


def kernel(x_prompt, x_sample, cache_k_win, cache_v_win, state_conv, state_h, c_prompt, c_sample, w_ada, b_ada, g_pre_mix, g_post_mix, g_pre_ffn, g_post_ffn, w_in, sinks, w_conv, b_conv, w_lru_a, b_lru_a, w_lru_x, b_lru_x, lru_lambda, w_attn_branch, w_lru_branch, w_out, w_router_group, b_router_group, w_router_expert, b_router_expert, w_exp_gate, w_exp_up, w_exp_down):
    weights = (w_ada, b_ada, g_pre_mix, g_post_mix, g_pre_ffn, g_post_ffn, w_in, sinks,
               w_conv, b_conv, w_lru_a, b_lru_a, w_lru_x, b_lru_x, lru_lambda,
               w_attn_branch, w_lru_branch, w_out, w_router_group, b_router_group,
               w_router_expert, b_router_expert, w_exp_gate, w_exp_up, w_exp_down)
    depth = w_ada.shape[0]
    batch = x_prompt.shape[0]
    nseq, t_new, _ = x_sample.shape
    y_p = x_prompt
    y_s = x_sample.transpose(1, 0, 2).reshape(t_new * nseq, D_MODEL)
    c_all = jnp.concatenate([c_prompt, c_sample], axis=0)
    outs = [[] for _ in range(8)]
    for layer in range(depth):
        p = {name: w[layer] for name, w in zip(PARAM_NAMES, weights)}
        mod = _ada(c_all, p['w_ada'], p['b_ada'].reshape(1, -1))
        res = _layer_forward(y_p, y_s, cache_k_win[layer], cache_v_win[layer], state_conv[layer],
                             state_h[layer], mod[:batch], mod[batch:], p)
        y_p, y_s = res[0], res[1]
        for o, r in zip(outs, res[2:]):
            o.append(r)
    y_sample = y_s.reshape(t_new, nseq, D_MODEL).transpose(1, 0, 2)
    return (y_p, y_sample) + tuple(jnp.stack(o) for o in outs)
```

```python
import functools

import jax
import jax.numpy as jnp
from jax import lax
from jax.experimental import pallas as pl
from jax.experimental.pallas import tpu as pltpu

F32 = jnp.float32
BF16 = jnp.bfloat16
I32 = jnp.int32

D_MODEL = 1024
N_HEADS = 16
HEAD_DIM = 64
N_KV_HEADS = 4
GQA_GROUP = 4
WINDOW = 128
ROPE_THETA = 10000.0
NEG_INF = -1e30
LRU_WIDTH = 1024
LRU_BLOCKS = 16
LRU_BLOCK_W = 64
CONV_WIDTH = 4
LRU_C = 8.0
N_GROUPS = 4
EXPERTS_PER_GROUP = 8
N_EXPERTS = 32
D_EXPERT = 512
MOE_BLOCK = 128
NORM_EPS = 1e-6
Q_DIM = N_HEADS * HEAD_DIM
KV_DIM = N_KV_HEADS * HEAD_DIM
IN_DIM = Q_DIM + 2 * KV_DIM + 2 * LRU_WIDTH + 2 * D_MODEL

LANES = 128
SUBLANES = 8
MXU_DIM = 256
VMEM_LIMIT = 56 * 1024 * 1024
VMEM_LIMIT_BIG = 60 * 1024 * 1024

ROW_TILE = 512
ROUTE_LANES = LANES
GRANULE = SUBLANES
EXPERT_ROWS = 512
HALF_D = D_MODEL // 2
U32 = jnp.uint32


def _cparams(n_axes, vmem=VMEM_LIMIT, flags=None):
    return pltpu.CompilerParams(dimension_semantics=("arbitrary",) * n_axes, vmem_limit_bytes=vmem, flags=flags)


def _rms(x, g):
    ms = jnp.mean(x * x, axis=-1, keepdims=True)
    return x * lax.rsqrt(ms + NORM_EPS) * g


def _sigmoid(x):
    return 1.0 / (1.0 + jnp.exp(-x))


def _dot(a, b):
    return jnp.dot(a, b, preferred_element_type=F32)


def _ada_kernel(c_ref, w_ref, b_ref, o_ref):
    c = c_ref[...]
    s = (c * _sigmoid(c)).astype(BF16)
    o_ref[...] = _dot(s, w_ref[...].astype(BF16)) + b_ref[...]


def _ada(c_all, w_ada, b_ada):
    r = c_all.shape[0]
    n = w_ada.shape[1]
    return pl.pallas_call(
        _ada_kernel,
        grid=(n // D_MODEL,),
        in_specs=[pl.BlockSpec((r, D_MODEL), lambda j: (0, 0)),
                  pl.BlockSpec((D_MODEL, D_MODEL), lambda j: (0, j)),
                  pl.BlockSpec((1, D_MODEL), lambda j: (0, j))],
        out_specs=pl.BlockSpec((r, D_MODEL), lambda j: (0, j)),
        out_shape=jax.ShapeDtypeStruct((r, n), F32),
        compiler_params=_cparams(1),
        name="ada_mod",
    )(c_all, w_ada, b_ada)


_O1 = Q_DIM
_O2 = _O1 + KV_DIM
_O3 = _O2 + KV_DIM
_O4 = _O3 + LRU_WIDTH
_O5 = _O4 + LRU_WIDTH
_O6 = _O5 + D_MODEL


def _prenorm(x_ref, g_ref, sc_ref, sh_ref):
    h = _rms(x_ref[...], g_ref[...]) * (1.0 + sc_ref[...]) + sh_ref[...]
    return h.astype(BF16)


PIECE_COLS = 256


def _plain_pieces(h_ref, w_ref, base, width, store):
    def piece(c0):
        def run():
            store(c0, _dot(h_ref[...], w_ref[:, base + c0:base + c0 + PIECE_COLS]))
        return run
    return [piece(c0) for c0 in range(0, width, PIECE_COLS)]


def _qkv_gate_pieces(h_ref, w_ref, cos_ref, sin_ref, q_ref, k_ref, v_ref, sga_ref, sgr_ref, last_refs=None):
    def rope(t):
        cos = cos_ref[...]
        sin = sin_ref[...]
        lane = lax.broadcasted_iota(I32, cos.shape, 1)
        first_half = (lane % HEAD_DIM) < (HEAD_DIM // 2)
        rot = jnp.where(first_half, pltpu.roll(t, LANES - HEAD_DIM // 2, 1), pltpu.roll(t, HEAD_DIM // 2, 1))
        return t * cos + rot * sin

    def q_piece(c0):
        def run():
            qf = _dot(h_ref[...], w_ref[:, c0:c0 + PIECE_COLS])
            for c in range(PIECE_COLS // LANES):
                q_ref[:, c0 + c * LANES:c0 + (c + 1) * LANES] = rope(qf[:, c * LANES:(c + 1) * LANES]).astype(BF16)
        return run

    def kv_piece():
        kv = _dot(h_ref[...], w_ref[:, _O1:_O3])
        for c in range(KV_DIM // LANES):
            k_ref[:, c * LANES:(c + 1) * LANES] = rope(kv[:, c * LANES:(c + 1) * LANES])
        v_ref[...] = kv[:, KV_DIM:]
        if last_refs is not None:
            t = k_ref.shape[0]
            last_refs[0][...] = k_ref[t - WINDOW:t, :]
            last_refs[1][...] = v_ref[t - WINDOW:t, :]

    def gate_piece(o_ref, base, c0):
        def run():
            z = _dot(h_ref[...], w_ref[:, base + c0:base + c0 + PIECE_COLS])
            o_ref[:, c0:c0 + PIECE_COLS] = _sigmoid(z).astype(BF16)
        return run

    pieces = [q_piece(c0) for c0 in range(0, Q_DIM, PIECE_COLS)] + [kv_piece]
    pieces += [gate_piece(sga_ref, _O5, c0) for c0 in range(0, D_MODEL, PIECE_COLS)]
    pieces += [gate_piece(sgr_ref, _O6, c0) for c0 in range(0, D_MODEL, PIECE_COLS)]
    return pieces


def _inproj_kernel(x_ref, sc_ref, sh_ref, g_ref, cos_ref, sin_ref, w_ref,
                   q_ref, k_ref, v_ref, xl_ref, yl_ref, sga_ref, sgr_ref, hbuf):
    hbuf[...] = _prenorm(x_ref, g_ref, sc_ref, sh_ref)
    xl_ref[...] = _dot(hbuf[...], w_ref[:, _O3:_O4]).astype(BF16)
    yl_ref[...] = _dot(hbuf[...], w_ref[:, _O4:_O5]).astype(BF16)
    for piece in _qkv_gate_pieces(hbuf, w_ref, cos_ref, sin_ref, q_ref, k_ref, v_ref, sga_ref, sgr_ref):
        piece()


def _inproj(x, sc, sh, g, cos, sin, w_in, rows_per_mod, pos_tiles):
    n = x.shape[0]
    tm = min(ROW_TILE, n)
    if rows_per_mod:
        tiles_per_mod = rows_per_mod // tm
        mod_spec = pl.BlockSpec((None, 1, D_MODEL), lambda i: (i // tiles_per_mod, 0, 0))
    else:
        mod_spec = pl.BlockSpec((tm, D_MODEL), lambda i: (i, 0))
    row = lambda w: pl.BlockSpec((tm, w), lambda i: (i, 0))
    outs = [(Q_DIM, BF16), (KV_DIM, F32), (KV_DIM, F32), (LRU_WIDTH, BF16), (LRU_WIDTH, BF16),
            (D_MODEL, BF16), (D_MODEL, BF16)]
    return pl.pallas_call(
        _inproj_kernel,
        grid=(n // tm,),
        in_specs=[row(D_MODEL), mod_spec, mod_spec,
                  pl.BlockSpec((1, D_MODEL), lambda i: (0, 0)),
                  pl.BlockSpec((tm, LANES), lambda i: (i % pos_tiles, 0)),
                  pl.BlockSpec((tm, LANES), lambda i: (i % pos_tiles, 0)),
                  pl.BlockSpec((D_MODEL, IN_DIM), lambda i: (0, 0))],
        out_specs=[row(w) for w, _ in outs],
        out_shape=[jax.ShapeDtypeStruct((n, w), dt) for w, dt in outs],
        scratch_shapes=[pltpu.VMEM((tm, D_MODEL), BF16)],
        compiler_params=_cparams(1),
        name="in_proj",
    )(x, sc, sh, g, cos, sin, w_in)


def _head_masks(shape):
    lane = lax.broadcasted_iota(I32, shape, 1)
    return [(lane // HEAD_DIM) == h for h in range(N_KV_HEADS)]


def _attention_core(q_perm, kall, vall, valid, sink_of, rows):
    masks_b = _head_masks((rows, KV_DIM))
    zero_b = jnp.zeros((rows, KV_DIM), BF16)
    pieces = []
    for h in range(N_KV_HEADS):
        for g in range(GQA_GROUP):
            pieces.append(jnp.where(masks_b[h], q_perm[g], zero_b))
    q_big = jnp.concatenate(pieces, axis=0)
    s_all = lax.dot_general(q_big, kall, (((1,), (1,)), ((), ())), preferred_element_type=F32)
    s_all = s_all * (HEAD_DIM ** -0.5)
    p_chunks, inv_chunks = [], []
    for h in range(N_KV_HEADS):
        for g in range(GQA_GROUP):
            c = h * GQA_GROUP + g
            s = jnp.where(valid, s_all[c * rows:(c + 1) * rows], NEG_INF)
            sink = sink_of(h, g)
            m = jnp.maximum(jnp.max(s, axis=-1, keepdims=True), sink)
            p = jnp.exp(s - m)
            denom = jnp.sum(p, axis=-1, keepdims=True) + jnp.exp(sink - m)
            p_chunks.append(p.astype(BF16))
            inv_chunks.append(1.0 / denom)
    o_all = _dot(jnp.concatenate(p_chunks, axis=0), vall)
    outs = []
    for g in range(GQA_GROUP):
        acc = jnp.zeros((rows, KV_DIM), F32)
        for h in range(N_KV_HEADS):
            c = h * GQA_GROUP + g
            acc = acc + jnp.where(masks_b[h], o_all[c * rows:(c + 1) * rows] * inv_chunks[c], 0.0)
        outs.append(acc)
    return outs


ATTN_Q_BLOCKS = 4


def _attn_prompt_kernel(sink_ref, q_ref, kc_ref, kp_ref, vc_ref, vp_ref, o_ref):
    j = pl.program_id(1)
    kall = jnp.concatenate([kp_ref[...], kc_ref[...]], axis=0).astype(BF16)
    vall = jnp.concatenate([vp_ref[...], vc_ref[...]], axis=0).astype(BF16)
    qi = lax.broadcasted_iota(I32, (WINDOW, 2 * WINDOW), 0)
    kj = lax.broadcasted_iota(I32, (WINDOW, 2 * WINDOW), 1)
    dist = qi + WINDOW - kj
    in_window = (dist >= 0) & (dist <= WINDOW)
    for c in range(ATTN_Q_BLOCKS):
        rows = slice(c * WINDOW, (c + 1) * WINDOW)
        keys = slice(c * WINDOW, (c + 2) * WINDOW)
        valid = in_window & ((kj >= WINDOW) | (j > 0)) if c == 0 else in_window
        q_perm = [q_ref[rows, g * KV_DIM:(g + 1) * KV_DIM] for g in range(GQA_GROUP)]
        outs = _attention_core(q_perm, kall[keys], vall[keys], valid,
                               lambda h, g: sink_ref[h * GQA_GROUP + g], WINDOW)
        for g in range(GQA_GROUP):
            o_ref[rows, g * KV_DIM:(g + 1) * KV_DIM] = outs[g].astype(BF16)


def _attn_prompt(sinks, q, k, v, batch, seq):
    step_rows = ATTN_Q_BLOCKS * WINDOW
    nb = seq // step_rows
    cur = lambda w: pl.BlockSpec((step_rows, w), lambda b, j: (b * nb + j, 0))
    prev = lambda w: pl.BlockSpec(
        (WINDOW, w), lambda b, j: (jnp.maximum((b * nb + j) * ATTN_Q_BLOCKS - 1, 0), 0))
    return pl.pallas_call(
        _attn_prompt_kernel,
        grid=(batch, nb),
        in_specs=[pl.BlockSpec(memory_space=pltpu.SMEM),
                  cur(Q_DIM), cur(KV_DIM), prev(KV_DIM), cur(KV_DIM), prev(KV_DIM)],
        out_specs=cur(Q_DIM),
        out_shape=jax.ShapeDtypeStruct((batch * seq, Q_DIM), BF16),
        compiler_params=_cparams(2),
        name="attn_prompt",
    )(sinks, q, k, k, v, v)


SEQ_PER_STEP = 8


def _attn_sample_kernel(sink_ref, q_ref, kn_ref, vn_ref, kc_ref, vc_ref, o_ref, kbuf, vbuf, *, t_new):
    rows = GQA_GROUP * t_new
    kbuf[WINDOW:2 * WINDOW, :] = jnp.zeros((WINDOW, KV_DIM), F32)
    vbuf[WINDOW:2 * WINDOW, :] = jnp.zeros((WINDOW, KV_DIM), F32)
    ri = lax.broadcasted_iota(I32, (rows, 2 * WINDOW), 0)
    kj = lax.broadcasted_iota(I32, (rows, 2 * WINDOW), 1)
    tq = ri // GQA_GROUP
    valid = (kj >= tq) & (kj <= tq + WINDOW) & (kj < WINDOW + t_new)
    for s in range(SEQ_PER_STEP):
        kbuf[0:WINDOW, :] = kc_ref[s]
        vbuf[0:WINDOW, :] = vc_ref[s]
        kbuf[WINDOW:WINDOW + t_new, :] = kn_ref[s]
        vbuf[WINDOW:WINDOW + t_new, :] = vn_ref[s]
        kall = kbuf[...].astype(BF16)
        vall = vbuf[...].astype(BF16)
        qs = q_ref[s]
        masks_b = _head_masks((rows, KV_DIM))
        zero_b = jnp.zeros((rows, KV_DIM), BF16)
        q_big = jnp.concatenate([jnp.where(masks_b[h], qs, zero_b) for h in range(N_KV_HEADS)], axis=0)
        s_all = lax.dot_general(q_big, kall, (((1,), (1,)), ((), ())), preferred_element_type=F32)
        s_all = s_all * (HEAD_DIM ** -0.5)
        acc = jnp.zeros((rows, KV_DIM), F32)
        p_chunks, inv_chunks = [], []
        for h in range(N_KV_HEADS):
            sc = jnp.where(valid, s_all[h * rows:(h + 1) * rows], NEG_INF)
            sink = sink_ref[h]
            m = jnp.maximum(jnp.max(sc, axis=-1, keepdims=True), sink)
            p = jnp.exp(sc - m)
            denom = jnp.sum(p, axis=-1, keepdims=True) + jnp.exp(sink - m)
            p_chunks.append(p.astype(BF16))
            inv_chunks.append(1.0 / denom)
        o_all = _dot(jnp.concatenate(p_chunks, axis=0), vall)
        for h in range(N_KV_HEADS):
            acc = acc + jnp.where(masks_b[h], o_all[h * rows:(h + 1) * rows] * inv_chunks[h], 0.0)
        o_ref[s] = acc.astype(BF16)


def _attn_sample(sink_rows, q, kn, vn, kc, vc):
    nseq, rows, _ = q.shape
    t_new = kn.shape[1]
    sb = SEQ_PER_STEP
    blk = lambda r: pl.BlockSpec((sb, r, KV_DIM), lambda i: (i, 0, 0))
    return pl.pallas_call(
        functools.partial(_attn_sample_kernel, t_new=t_new),
        grid=(nseq // sb,),
        in_specs=[pl.BlockSpec((N_KV_HEADS, rows, 1), lambda i: (0, 0, 0)),
                  blk(rows), blk(t_new), blk(t_new), blk(WINDOW), blk(WINDOW)],
        out_specs=blk(rows),
        out_shape=jax.ShapeDtypeStruct((nseq, rows, KV_DIM), BF16),
        scratch_shapes=[pltpu.VMEM((2 * WINDOW, KV_DIM), F32), pltpu.VMEM((2 * WINDOW, KV_DIM), F32)],
        compiler_params=_cparams(1),
        name="attn_sample",
    )(sink_rows, q, kn, vn, kc, vc)


def _gelu_tanh(x):
    return 0.5 * x * (1.0 + jnp.tanh(0.7978845608028654 * (x + 0.044715 * x * x * x)))


def _lru_gates(xc, wbd_ref, ba, bx, lam):
    xcb = xc.astype(BF16)
    r_parts, i_parts = [], []
    for gidx in range(LRU_WIDTH // MXU_DIM):
        z = _dot(xcb[:, gidx * MXU_DIM:(gidx + 1) * MXU_DIM], wbd_ref[gidx])
        r_parts.append(z[:, :MXU_DIM])
        i_parts.append(z[:, MXU_DIM:])
    return _gate_math(jnp.concatenate(r_parts, axis=1), jnp.concatenate(i_parts, axis=1), xc, ba, bx, lam)


def _gate_math(zr, zi, xc, ba, bx, lam):
    r = _sigmoid(zr + ba)
    i = _sigmoid(zi + bx)
    softplus_neg_lam = jnp.maximum(-lam, 0.0) + jnp.log1p(jnp.exp(-jnp.abs(lam)))
    log_a = -LRU_C * r * softplus_neg_lam
    a = jnp.exp(log_a)
    y = 1.0 - a * a
    u = jnp.where(y > 0.0, y * lax.rsqrt(y), 0.0) * (i * xc)
    return a, u


LRU_CHUNK = 64


def _lru_chunk(c, xbuf, ybuf, hcar, ol_ref, wc_ref, bc_ref, wbd_ref, ba_ref, bx_ref, lam_ref):
    w = LRU_WIDTH
    n = LRU_CHUNK
    rows = slice(c * n, (c + 1) * n)
    r0 = SUBLANES + c * n
    xc = xbuf[r0:r0 + n, :] * wc_ref[CONV_WIDTH - 1:CONV_WIDTH, :] + bc_ref[...]
    for k in range(1, CONV_WIDTH):
        xc = xc + xbuf[r0 - k:r0 - k + n, :] * wc_ref[CONV_WIDTH - 1 - k:CONV_WIDTH - k, :]
    a, u = _lru_gates(xc, wbd_ref, ba_ref[...], bx_ref[...], lam_ref[...])

    ng = n // SUBLANES
    a3 = a.reshape(ng, SUBLANES, w)
    u3 = u.reshape(ng, SUBLANES, w)
    row = lax.broadcasted_iota(I32, (ng, SUBLANES, w), 1)
    d = 1
    while d < SUBLANES:
        a_s = jnp.where(row >= d, pltpu.roll(a3, d, 1), 1.0)
        u_s = jnp.where(row >= d, pltpu.roll(u3, d, 1), 0.0)
        u3 = a3 * u_s + u3
        a3 = a3 * a_s
        d *= 2
    carry = hcar[...]
    hs = []
    for gi in range(ng):
        hg = a3[gi] * carry + u3[gi]
        hs.append(hg)
        carry = hg[SUBLANES - 1:SUBLANES, :]
    hcar[...] = carry
    h = jnp.concatenate(hs, axis=0)
    ol_ref[rows, :] = (h * _gelu_tanh(ybuf[rows, :])).astype(BF16)


def _inproj_lru_kernel(x_ref, sc_ref, sh_ref, g_ref, cos_ref, sin_ref, w_ref,
                       wc_ref, bc_ref, wbd_ref, ba_ref, bx_ref, lam_ref,
                       q_ref, k_ref, v_ref, sga_ref, sgr_ref, klast_ref, vlast_ref, ol_ref, xtail_ref, hlast_ref,
                       hbuf, xbuf, ybuf, hcar, *, tiles_per_seq):
    t = x_ref.shape[0]

    @pl.when(pl.program_id(0) % tiles_per_seq == 0)
    def _():
        xbuf[0:SUBLANES, :] = jnp.zeros((SUBLANES, LRU_WIDTH), F32)
        hcar[...] = jnp.zeros((1, LRU_WIDTH), F32)

    hbuf[...] = _prenorm(x_ref, g_ref, sc_ref, sh_ref)
    xbuf[SUBLANES:SUBLANES + t, :] = _dot(hbuf[...], w_ref[:, _O3:_O4])
    ybuf[...] = _dot(hbuf[...], w_ref[:, _O4:_O5])

    pieces = _qkv_gate_pieces(hbuf, w_ref, cos_ref, sin_ref, q_ref, k_ref, v_ref, sga_ref, sgr_ref,
                              last_refs=(klast_ref, vlast_ref))
    n_chunks = t // LRU_CHUNK
    per_chunk = -(-len(pieces) // n_chunks)
    for c in range(n_chunks):
        _lru_chunk(c, xbuf, ybuf, hcar, ol_ref, wc_ref, bc_ref, wbd_ref, ba_ref, bx_ref, lam_ref)
        for piece in pieces[c * per_chunk:(c + 1) * per_chunk]:
            piece()

    tail = xbuf[t:t + SUBLANES, :]
    xtail_ref[...] = tail
    xbuf[0:SUBLANES, :] = tail
    hlast_ref[...] = hcar[...]


def _inproj_lru(x, sc, sh, g, cos, sin, w_in, wc, bc, wbd, ba, bx, lam, batch, seq):
    n = x.shape[0]
    tm = min(ROW_TILE, seq)
    tps = seq // tm
    mod_spec = pl.BlockSpec((None, 1, D_MODEL), lambda i: (i // tps, 0, 0))
    row = lambda w: pl.BlockSpec((tm, w), lambda i: (i, 0))
    full = lambda a: pl.BlockSpec(a.shape, lambda i: (0,) * a.ndim)
    per_seq = lambda r, w: pl.BlockSpec((None, r, w), lambda i: (i // tps, 0, 0))
    outs = [(Q_DIM, BF16), (KV_DIM, F32), (KV_DIM, F32), (D_MODEL, BF16), (D_MODEL, BF16)]
    return pl.pallas_call(
        functools.partial(_inproj_lru_kernel, tiles_per_seq=tps),
        grid=(n // tm,),
        in_specs=[row(D_MODEL), mod_spec, mod_spec, full(g),
                  pl.BlockSpec((tm, LANES), lambda i: (i % tps, 0)),
                  pl.BlockSpec((tm, LANES), lambda i: (i % tps, 0)),
                  full(w_in), full(wc), full(bc), full(wbd), full(ba), full(bx), full(lam)],
        out_specs=[row(w) for w, _ in outs] + [per_seq(WINDOW, KV_DIM), per_seq(WINDOW, KV_DIM)]
        + [row(LRU_WIDTH), per_seq(SUBLANES, LRU_WIDTH), per_seq(1, LRU_WIDTH)],
        out_shape=[jax.ShapeDtypeStruct((n, w), dt) for w, dt in outs]
        + [jax.ShapeDtypeStruct((batch, WINDOW, KV_DIM), F32)] * 2
        + [jax.ShapeDtypeStruct((n, LRU_WIDTH), BF16),
           jax.ShapeDtypeStruct((batch, SUBLANES, LRU_WIDTH), F32),
           jax.ShapeDtypeStruct((batch, 1, LRU_WIDTH), F32)],
        scratch_shapes=[pltpu.VMEM((tm, D_MODEL), BF16),
                        pltpu.VMEM((2 * SUBLANES + tm, LRU_WIDTH), F32), pltpu.VMEM((tm, LRU_WIDTH), F32),
                        pltpu.VMEM((1, LRU_WIDTH), F32)],
        compiler_params=_cparams(1, vmem=VMEM_LIMIT_BIG),
        name="in_proj_lru",
    )(x, sc, sh, g, cos, sin, w_in, wc, bc, wbd, ba, bx, lam)


def _lru_sample_kernel(xl_ref, yl_ref, cs_ref, h0_ref, wc_ref, bc_ref, wbd_ref, ba_ref, bx_ref, lam_ref,
                       o_ref, hlast_ref):
    t_new, nseq, w = xl_ref.shape
    xp = [cs_ref[k] for k in range(CONV_WIDTH - 1)] + [xl_ref[k].astype(F32) for k in range(t_new)]
    xcs = []
    for t in range(t_new):
        acc = bc_ref[...] + xp[t] * wc_ref[0:1, :]
        for k in range(1, CONV_WIDTH):
            acc = acc + xp[t + k] * wc_ref[k:k + 1, :]
        xcs.append(acc)
    xc = jnp.concatenate(xcs, axis=0)
    a, u = _lru_gates(xc, wbd_ref, ba_ref[...], bx_ref[...], lam_ref[...])
    h = h0_ref[...]
    for t in range(t_new):
        h = a[t * nseq:(t + 1) * nseq] * h + u[t * nseq:(t + 1) * nseq]
        o_ref[t] = (h * _gelu_tanh(yl_ref[t].astype(F32))).astype(BF16)
    hlast_ref[...] = h


def _lru_sample(xl, yl, cs, h0, wc, bc, wbd, ba, bx, lam):
    t_new, nseq, w = xl.shape
    full = lambda shp: pl.BlockSpec(shp, lambda i: (0,) * len(shp))
    args = (xl, yl, cs, h0, wc, bc, wbd, ba, bx, lam)
    return pl.pallas_call(
        _lru_sample_kernel,
        grid=(1,),
        in_specs=[full(a.shape) for a in args],
        out_specs=[full((t_new, nseq, w)), full((nseq, w))],
        out_shape=[jax.ShapeDtypeStruct((t_new, nseq, w), BF16), jax.ShapeDtypeStruct((nseq, w), F32)],
        compiler_params=_cparams(1),
        name="lru_sample",
    )(*args)


def _post_kernel(oa_ref, ol_ref, sga_ref, sgr_ref, x_ref, gta_ref, scf_ref, shf_ref,
                 wab_ref, wlb_ref, wout_ref, gpm_ref, gpf_ref, wrp_ref, wrt_ref, br_ref, tri_ref, cin_ref,
                 x1_ref, h2_ref, route_ref, stat_ref, cnt_ref, carry):
    i = pl.program_id(0)

    @pl.when(i == 0)
    def _():
        carry[...] = cin_ref[...]

    b_attn = _dot(oa_ref[...], wab_ref[...])
    b_lru = _dot(ol_ref[...], wlb_ref[...])
    merged = sga_ref[...].astype(F32) * b_attn + sgr_ref[...].astype(F32) * b_lru
    mix = _dot(merged.astype(BF16), wout_ref[...])
    x1 = x_ref[...] + gta_ref[...] * _rms(mix, gpm_ref[...])
    x1_ref[...] = x1
    h2 = _rms(x1, gpf_ref[...]) * (1.0 + scf_ref[...]) + shf_ref[...]
    h2_ref[...] = h2.astype(BF16)

    h_hi = h2.astype(BF16)
    h_lo = (h2 - h_hi.astype(F32)).astype(BF16)
    hi_terms = _dot(h_hi, wrp_ref[...])
    logits = (hi_terms[:, :ROUTE_LANES] + (hi_terms[:, ROUTE_LANES:] + _dot(h_lo, wrt_ref[...]))) + br_ref[...]

    tm = logits.shape[0]
    lane = lax.broadcasted_iota(I32, (tm, ROUTE_LANES), 1)
    big = jnp.int32(ROUTE_LANES)
    is_g = lane < N_GROUPS
    lg = jnp.where(is_g, logits, NEG_INF)
    mg = jnp.max(lg, axis=-1, keepdims=True)
    g_star = jnp.min(jnp.where(lg == mg, lane, big), axis=-1, keepdims=True)
    p_star = 1.0 / jnp.sum(jnp.where(is_g, jnp.exp(lg - mg), 0.0), axis=-1, keepdims=True)
    lo = N_GROUPS + g_star * EXPERTS_PER_GROUP
    in_grp = (lane >= lo) & (lane < lo + EXPERTS_PER_GROUP)
    le = jnp.where(in_grp, logits, NEG_INF)
    m1 = jnp.max(le, axis=-1, keepdims=True)
    i1 = jnp.min(jnp.where(le == m1, lane, big), axis=-1, keepdims=True)
    le2 = jnp.where(lane == i1, NEG_INF, le)
    m2 = jnp.max(le2, axis=-1, keepdims=True)
    i2 = jnp.min(jnp.where(le2 == m2, lane, big), axis=-1, keepdims=True)
    e2x = jnp.exp(m2 - m1)
    wsum = 1.0 + e2x
    w1 = (1.0 / wsum) * p_star
    w2 = (e2x / wsum) * p_star

    oh1 = lane == i1
    oh2 = lane == i2
    cnt = jnp.where(oh1 | oh2, 1.0, 0.0)
    excl = _dot(tri_ref[...], cnt.astype(BF16))
    per_e = jnp.sum(cnt, axis=0, keepdims=True)
    pad8 = jnp.floor((per_e + (GRANULE - 1.0)) * (1.0 / GRANULE)) * GRANULE
    incl = jnp.broadcast_to(pad8, (SUBLANES, ROUTE_LANES))
    lane8 = lax.broadcasted_iota(I32, (SUBLANES, ROUTE_LANES), 1)
    d = 1
    while d < ROUTE_LANES:
        incl = incl + jnp.where(lane8 >= d, pltpu.roll(incl, d, 1), 0.0)
        d *= 2
    seg_start = incl[0:1, :] - pad8
    pos = excl + seg_start
    s1 = jnp.sum(jnp.where(oh1, pos, 0.0), axis=-1, keepdims=True)
    s2 = jnp.sum(jnp.where(oh2, pos, 0.0), axis=-1, keepdims=True)
    rec = jnp.where(lane == 0, s1, 0.0)
    rec = jnp.where(lane == 1, s2, rec)
    rec = jnp.where(lane == 2, w1, rec)
    rec = jnp.where(lane == 3, w2, rec)
    route_ref[...] = rec
    srow = lax.broadcasted_iota(I32, (SUBLANES, ROUTE_LANES), 0)
    stat_ref[...] = jnp.where(srow == 0, pad8, jnp.where(srow == 1, carry[...], 0.0))
    carry[...] = carry[...] + pad8
    cnt_ref[...] = carry[...]


def _post(oa, ol, sga, sgr, x, gta, scf, shf, wab, wlb, wout, gpm, gpf, wrh, wrl, br, tri, cin, rows_per_mod):
    n = x.shape[0]
    tm = min(ROW_TILE, n)
    if rows_per_mod:
        tiles_per_mod = rows_per_mod // tm
        mod_spec = pl.BlockSpec((None, 1, D_MODEL), lambda i: (i // tiles_per_mod, 0, 0))
    else:
        mod_spec = pl.BlockSpec((tm, D_MODEL), lambda i: (i, 0))
    row = lambda w: pl.BlockSpec((tm, w), lambda i: (i, 0))
    full = lambda a: pl.BlockSpec(a.shape, lambda i: (0,) * a.ndim)
    return pl.pallas_call(
        _post_kernel,
        grid=(n // tm,),
        in_specs=[row(Q_DIM), row(LRU_WIDTH), row(D_MODEL), row(D_MODEL), row(D_MODEL),
                  mod_spec, mod_spec, mod_spec,
                  full(wab), full(wlb), full(wout), full(gpm), full(gpf), full(wrh), full(wrl), full(br),
                  full(tri), full(cin)],
        out_specs=[row(D_MODEL), row(D_MODEL), row(ROUTE_LANES),
                   pl.BlockSpec((SUBLANES, ROUTE_LANES), lambda i: (i, 0)),
                   pl.BlockSpec((1, ROUTE_LANES), lambda i: (0, 0))],
        out_shape=[jax.ShapeDtypeStruct((n, D_MODEL), F32), jax.ShapeDtypeStruct((n, D_MODEL), BF16),
                   jax.ShapeDtypeStruct((n, ROUTE_LANES), F32),
                   jax.ShapeDtypeStruct((n // tm * SUBLANES, ROUTE_LANES), F32),
                   jax.ShapeDtypeStruct((1, ROUTE_LANES), F32)],
        scratch_shapes=[pltpu.VMEM((1, ROUTE_LANES), F32)],
        compiler_params=_cparams(1),
        name="post_mix",
    )(oa, ol, sga, sgr, x, gta, scf, shf, wab, wlb, wout, gpm, gpf, wrh, wrl, br, tri, cin)


BLOCK_GRANULES = EXPERT_ROWS // GRANULE
BLOCK_SHIFT = BLOCK_GRANULES.bit_length() - 1
assert BLOCK_GRANULES == 1 << BLOCK_SHIFT


def _padded(c):
    return ((c + (BLOCK_GRANULES - 1)) >> BLOCK_SHIFT) << BLOCK_SHIFT


def _sorted_rows(tm):
    r = 2 * tm + N_EXPERTS * (GRANULE - 1)
    return -(-r // MXU_DIM) * MXU_DIM


def _plan_kernel(tot_ref, be_ref, meta_ref, *, n_blocks):
    def fill(j, _):
        be_ref[j] = N_EXPERTS - 1
        return 0
    lax.fori_loop(0, n_blocks, fill, 0)

    def per_expert(e, nb):
        k = _padded(tot_ref[e]) >> BLOCK_SHIFT

        def put(b, _):
            be_ref[nb + b] = e
            return 0
        lax.fori_loop(0, k, put, 0)
        return nb + k
    n_active = lax.fori_loop(0, N_EXPERTS, per_expert, 0)
    meta_ref[0] = n_active


def _plan(totals, n_blocks):
    return pl.pallas_call(
        functools.partial(_plan_kernel, n_blocks=n_blocks),
        in_specs=[pl.BlockSpec(memory_space=pltpu.SMEM)],
        out_specs=[pl.BlockSpec(memory_space=pltpu.SMEM), pl.BlockSpec(memory_space=pltpu.SMEM)],
        out_shape=[jax.ShapeDtypeStruct((n_blocks,), I32), jax.ShapeDtypeStruct((1,), I32)],
        name="moe_plan",
    )(totals)


def _expert_starts(tot_ref, pstart):
    def body(e, acc):
        pstart[e] = acc
        return acc + _padded(tot_ref[e])
    return lax.fori_loop(0, N_EXPERTS, body, 0)


def _granule(ref, g):
    return ref.at[pl.ds(pl.multiple_of(g * GRANULE, GRANULE), GRANULE)]


def _pack_halves(lo_f32, hi_f32):
    return (pltpu.bitcast(lo_f32, U32) >> 16) | (pltpu.bitcast(hi_f32, U32) & jnp.uint32(0xFFFF0000))


def _unpack_halves(packed):
    lo = pltpu.bitcast(packed << 16, F32).astype(BF16)
    hi = pltpu.bitcast(packed & jnp.uint32(0xFFFF0000), F32).astype(BF16)
    return lo, hi


def _sort_kernel(tot_ref, cnt_ref, goff_ref, rec_p_ref, h2_p_ref, rec_s_ref, h2_s_ref,
                 srt_ref, gsrc_ref, pstart, *, tiles_p, n_sorted, n_slots):
    i = pl.program_id(0)
    from_sample = i >= tiles_p
    rec = jnp.where(from_sample, rec_s_ref[...], rec_p_ref[...])
    h2 = jnp.where(from_sample, h2_s_ref[...], h2_p_ref[...])
    tm = h2.shape[0]

    rec_t = rec.T
    s1 = rec_t[0:1, :].astype(I32)
    s2 = rec_t[1:2, :].astype(I32)
    rows = lax.broadcasted_iota(I32, (n_sorted, tm), 0)
    sel = jnp.where((rows == s1) | (rows == s2), 1.0, 0.0).astype(BF16)
    srt = _dot(sel, h2)
    srt_ref[...] = _pack_halves(srt[:, :HALF_D], srt[:, HALF_D:])

    zero_granule = n_sorted // GRANULE - 1

    @pl.when(i == 0)
    def _():
        used = _expert_starts(tot_ref, pstart)

        def put_zero(g, _):
            gsrc_ref[g] = zero_granule
            return 0

        def per_expert(e, _):
            total = tot_ref[e]
            lax.fori_loop(pstart[e] + total, pstart[e] + _padded(total), put_zero, 0)
            return 0
        lax.fori_loop(0, N_EXPERTS, per_expert, 0)
        lax.fori_loop(used, n_slots // GRANULE, put_zero, 0)

    def per_expert(e, seg):
        k = cnt_ref[i * N_EXPERTS + e]
        dst = pstart[e] + goff_ref[i * N_EXPERTS + e]
        src = i * (n_sorted // GRANULE) + seg

        def put(g, _):
            gsrc_ref[dst + g] = src + g
            return 0
        lax.fori_loop(0, k, put, 0)
        return seg + k
    lax.fori_loop(0, N_EXPERTS, per_expert, 0, unroll=4)


def _sort(totals, cnt8, goff, rec_p, h2_p, rec_s, h2_s, n_slots):
    tm = min(ROW_TILE, h2_p.shape[0])
    assert h2_s.shape[0] % tm == 0
    tiles_p = h2_p.shape[0] // tm
    tiles_s = h2_s.shape[0] // tm
    n_sorted = _sorted_rows(tm)
    row_p = lambda w: pl.BlockSpec((tm, w), lambda i, *_: (jnp.minimum(i, tiles_p - 1), 0))
    row_s = lambda w: pl.BlockSpec((tm, w), lambda i, *_: (jnp.maximum(i - tiles_p, 0), 0))
    return pl.pallas_call(
        functools.partial(_sort_kernel, tiles_p=tiles_p, n_sorted=n_sorted, n_slots=n_slots),
        grid_spec=pltpu.PrefetchScalarGridSpec(
            num_scalar_prefetch=3,
            grid=(tiles_p + tiles_s,),
            in_specs=[row_p(ROUTE_LANES), row_p(D_MODEL), row_s(ROUTE_LANES), row_s(D_MODEL)],
            out_specs=[pl.BlockSpec((n_sorted, HALF_D), lambda i, *_: (i, 0)),
                       pl.BlockSpec(memory_space=pltpu.SMEM)],
            scratch_shapes=[pltpu.SMEM((N_EXPERTS,), I32)]),
        out_shape=[jax.ShapeDtypeStruct(((tiles_p + tiles_s) * n_sorted, HALF_D), U32),
                   jax.ShapeDtypeStruct((n_slots // GRANULE,), I32)],
        compiler_params=_cparams(1),
        name="moe_sort",
    )(totals, cnt8, goff, rec_p, h2_p, rec_s, h2_s)


def _expert_kernel(be_ref, meta_ref, gsrc_ref, srt_hbm, wg_ref, wu_ref, wd_ref, ys_ref,
                   xbuf, wgb, wub, wdb, sems):
    j = pl.program_id(0)
    n_active = meta_ref[0]
    gran_per_block = BLOCK_GRANULES

    def granule_copy(blk, g, slot):
        return pltpu.make_async_copy(_granule(srt_hbm, gsrc_ref[blk * gran_per_block + g]),
                                     xbuf.at[slot, pl.ds(g * GRANULE, GRANULE)], sems.at[slot])

    def gather(blk, slot):
        for g in range(gran_per_block):
            granule_copy(blk, g, slot).start()

    @pl.when(j == 0)
    def _():
        gather(0, 0)

    @pl.when(j < n_active)
    def _():
        slot = j % 2

        @pl.when(j + 1 < n_active)
        def _():
            gather(j + 1, 1 - slot)

        @pl.when((j == 0) | (be_ref[j] != be_ref[jnp.maximum(j - 1, 0)]))
        def _():
            wgb[...] = wg_ref[...].astype(BF16)
            wub[...] = wu_ref[...].astype(BF16)
            wdb[...] = wd_ref[...].astype(BF16)

        for g in range(gran_per_block):
            granule_copy(j, g, slot).wait()
        x_lo, x_hi = _unpack_halves(xbuf[slot])
        g = _dot(x_lo, wgb[0:HALF_D, :]) + _dot(x_hi, wgb[HALF_D:D_MODEL, :])
        u = _dot(x_lo, wub[0:HALF_D, :]) + _dot(x_hi, wub[HALF_D:D_MODEL, :])
        hmid = (g * _sigmoid(g) * u).astype(BF16)
        y = _dot(hmid, wdb[...])
        ys_ref[...] = _pack_halves(y[:, :HALF_D].astype(BF16).astype(F32), y[:, HALF_D:].astype(BF16).astype(F32))

    @pl.when(j >= meta_ref[0])
    def _():
        ys_ref[...] = jnp.zeros(ys_ref.shape, U32)


def _experts(block_e, meta, gsrc, srt, wg, wu, wd, n_slots):
    n_blocks = n_slots // EXPERT_ROWS
    wspec = lambda shp: pl.BlockSpec(
        (None,) + shp, lambda j, be, meta, gs: (be[jnp.minimum(j, meta[0] - 1)], 0, 0))
    return pl.pallas_call(
        _expert_kernel,
        grid_spec=pltpu.PrefetchScalarGridSpec(
            num_scalar_prefetch=3,
            grid=(n_blocks,),
            in_specs=[pl.BlockSpec(memory_space=pl.ANY),
                      wspec((D_MODEL, D_EXPERT)), wspec((D_MODEL, D_EXPERT)), wspec((D_EXPERT, D_MODEL))],
            out_specs=pl.BlockSpec((EXPERT_ROWS, HALF_D), lambda j, be, meta, gs: (j, 0)),
            scratch_shapes=[pltpu.VMEM((2, EXPERT_ROWS, HALF_D), U32),
                            pltpu.VMEM((D_MODEL, D_EXPERT), BF16), pltpu.VMEM((D_MODEL, D_EXPERT), BF16),
                            pltpu.VMEM((D_EXPERT, D_MODEL), BF16), pltpu.SemaphoreType.DMA((2,))]),
        out_shape=jax.ShapeDtypeStruct((n_slots, HALF_D), U32),
        compiler_params=_cparams(1),
        name="moe_experts",
    )(block_e, meta, gsrc, srt, wg, wu, wd)


def _combine_kernel(tot_ref, cnt_ref, goff_ref, rec_ref, x1_ref, gtf_ref, g_ref, ys_hbm,
                    y_ref, cbuf, pstart, n_issued, sems, *, tile_base, n_tiles):
    i = pl.program_id(0)
    slot = i % 2
    tm = x1_ref.shape[0]
    n_sorted = cbuf.shape[1]

    def granule_copy(slot_granule, tile_granule, slot_):
        return pltpu.make_async_copy(_granule(ys_hbm, slot_granule), _granule(cbuf.at[slot_], tile_granule),
                                     sems.at[slot_])

    def gather(tile, slot_):
        def per_expert(e, carry):
            seg, n = carry
            k = cnt_ref[tile * N_EXPERTS + e]
            src = pstart[e] + goff_ref[tile * N_EXPERTS + e]

            def gran(g, _):
                granule_copy(src + g, seg + g, slot_).start()
                return 0
            lax.fori_loop(0, k, gran, 0)
            return seg + k, n + k
        _, n = lax.fori_loop(0, N_EXPERTS, per_expert, (0, 0), unroll=4)
        n_issued[slot_] = n

    @pl.when(i == 0)
    def _():
        cbuf[...] = jnp.zeros(cbuf.shape, U32)
        _expert_starts(tot_ref, pstart)
        gather(tile_base, 0)

    @pl.when(i + 1 < n_tiles)
    def _():
        gather(tile_base + i + 1, 1 - slot)

    def drain(r, _):
        granule_copy(0, 0, slot).wait()
        return 0
    lax.fori_loop(0, n_issued[slot], drain, 0)

    rec = rec_ref[...]
    s1 = rec[:, 0:1].astype(I32)
    s2 = rec[:, 1:2].astype(I32)
    col = lax.broadcasted_iota(I32, (tm, n_sorted), 1)
    wmat = (jnp.where(col == s1, rec[:, 2:3], 0.0) + jnp.where(col == s2, rec[:, 3:4], 0.0)).astype(BF16)
    y_lo, y_hi = _unpack_halves(cbuf[slot])
    f = jnp.concatenate([_dot(wmat, y_lo), _dot(wmat, y_hi)], axis=1)
    y_ref[...] = x1_ref[...] + gtf_ref[...] * _rms(f, g_ref[...])


def _combine(totals, cnt8, goff, rec, x1, gtf, g, ys, rows_per_mod, tile_base):
    n = x1.shape[0]
    tm = min(ROW_TILE, n)
    if rows_per_mod:
        tiles_per_mod = rows_per_mod // tm
        mod_spec = pl.BlockSpec((None, 1, D_MODEL), lambda i, *_: (i // tiles_per_mod, 0, 0))
    else:
        mod_spec = pl.BlockSpec((tm, D_MODEL), lambda i, *_: (i, 0))
    row = lambda w: pl.BlockSpec((tm, w), lambda i, *_: (i, 0))
    return pl.pallas_call(
        functools.partial(_combine_kernel, tile_base=tile_base, n_tiles=n // tm),
        grid_spec=pltpu.PrefetchScalarGridSpec(
            num_scalar_prefetch=3,
            grid=(n // tm,),
            in_specs=[row(ROUTE_LANES), row(D_MODEL), mod_spec,
                      pl.BlockSpec((1, D_MODEL), lambda i, *_: (0, 0)),
                      pl.BlockSpec(memory_space=pl.ANY)],
            out_specs=row(D_MODEL),
            scratch_shapes=[pltpu.VMEM((2, _sorted_rows(tm), HALF_D), U32),
                            pltpu.SMEM((N_EXPERTS,), I32), pltpu.SMEM((2,), I32),
                            pltpu.SemaphoreType.DMA((2,))]),
        out_shape=jax.ShapeDtypeStruct((n, D_MODEL), F32),
        compiler_params=_cparams(1),
        name="moe_combine",
    )(totals, cnt8, goff, rec, x1, gtf, g, ys)


def _rope_tables(pos):
    half = HEAD_DIM // 2
    inv = jnp.power(jnp.float32(ROPE_THETA), -jnp.arange(half, dtype=F32) / half)
    ang = pos.astype(F32)[:, None] * inv[None, :]
    cos = jnp.cos(ang)
    sin = jnp.sin(ang)
    reps = LANES // HEAD_DIM
    cos_t = jnp.tile(jnp.concatenate([cos, cos], axis=-1), (1, reps))
    sin_t = jnp.tile(jnp.concatenate([-sin, sin], axis=-1), (1, reps))
    return cos_t, sin_t


def _q_perm_index():
    g = jnp.arange(GQA_GROUP)[:, None, None]
    h = jnp.arange(N_KV_HEADS)[None, :, None]
    d = jnp.arange(HEAD_DIM)[None, None, :]
    return ((h * GQA_GROUP + g) * HEAD_DIM + d).reshape(-1)


def _block_diag_gates(w_a, w_x):
    per = MXU_DIM // LRU_BLOCK_W
    groups = LRU_BLOCKS // per

    def bd(w):
        w = w.reshape(groups, per, LRU_BLOCK_W, LRU_BLOCK_W)
        eye = jnp.eye(per, dtype=w.dtype)
        full = jnp.einsum('gpij,pq->gpiqj', w, eye)
        return full.reshape(groups, MXU_DIM, MXU_DIM)
    return jnp.concatenate([bd(w_a), bd(w_x)], axis=-1).astype(BF16)


def _layer_forward(xp, xs_tm, ck, cv, cconv, ch, mod_p, mod_s, p):
    batch, seq, _ = xp.shape
    nseq, _, _, _ = ck.shape
    t_new = xs_tm.shape[0] // nseq
    n_p = batch * seq
    n_s = xs_tm.shape[0]

    perm = _q_perm_index()
    w_in = p['w_in']
    w_in_b = jnp.concatenate([w_in[:, :Q_DIM][:, perm], w_in[:, Q_DIM:]], axis=1).astype(BF16)
    wab = p['w_attn_branch'][perm, :].astype(BF16)
    wlb = p['w_lru_branch'].astype(BF16)
    wout = p['w_out'].astype(BF16)
    wbd = _block_diag_gates(p['w_lru_a'], p['w_lru_x'])
    row = lambda v: v.reshape(1, -1)
    wr = jnp.concatenate([p['w_router_group'], p['w_router_expert'],
                          jnp.zeros((D_MODEL, ROUTE_LANES - N_GROUPS - N_EXPERTS), F32)], axis=1)
    wr_top = wr.astype(BF16)
    wr_pair = jnp.concatenate([wr_top, (wr - wr_top.astype(F32)).astype(BF16)], axis=1)
    br = jnp.concatenate([p['b_router_group'], p['b_router_expert'],
                          jnp.zeros((ROUTE_LANES - N_GROUPS - N_EXPERTS,), F32)]).reshape(1, -1)
    wg = p['w_exp_gate']
    wu = p['w_exp_up']
    wd = p['w_exp_down']

    def mods(mod):
        return [mod[:, k * D_MODEL:(k + 1) * D_MODEL] for k in range(6)]
    sh_a_p, sc_a_p, gt_a_p, sh_f_p, sc_f_p, gt_f_p = [m.reshape(batch, 1, D_MODEL) for m in mods(mod_p)]
    sh_a_s, sc_a_s, gt_a_s, sh_f_s, sc_f_s, gt_f_s = [jnp.tile(m, (t_new, 1)) for m in mods(mod_s)]

    lru_w = (p['w_conv'], row(p['b_conv']), wbd, row(p['b_lru_a']), row(p['b_lru_x']), row(p['lru_lambda']))
    cos_p, sin_p = _rope_tables(jnp.arange(seq, dtype=I32))
    q_p, k_p, v_p, sga_p, sgr_p, klast_p, vlast_p, ol_p, xtail_p, hlast_p = _inproj_lru(
        xp.reshape(n_p, D_MODEL), sc_a_p, sh_a_p, row(p['g_pre_mix']), cos_p, sin_p, w_in_b, *lru_w, batch, seq)
    pos_s = jnp.repeat(PAST_LEN_ + jnp.arange(t_new, dtype=I32), nseq)
    cos_s, sin_s = _rope_tables(pos_s)
    q_s, k_s, v_s, xl_s, yl_s, sga_s, sgr_s = _inproj(
        xs_tm, sc_a_s, sh_a_s, row(p['g_pre_mix']), cos_s, sin_s, w_in_b, rows_per_mod=0,
        pos_tiles=n_s // min(ROW_TILE, n_s))

    sinks_perm = p['sinks']
    oa_p = _attn_prompt(sinks_perm, q_p, k_p, v_p, batch, seq)
    rows = t_new * GQA_GROUP
    q_s3 = q_s.reshape(t_new, nseq, GQA_GROUP, KV_DIM).transpose(1, 0, 2, 3).reshape(nseq, rows, KV_DIM)
    kn = k_s.reshape(t_new, nseq, KV_DIM).transpose(1, 0, 2)
    vn = v_s.reshape(t_new, nseq, KV_DIM).transpose(1, 0, 2)
    kc = ck.reshape(nseq, WINDOW, KV_DIM)
    vc = cv.reshape(nseq, WINDOW, KV_DIM)
    sink_rows = jnp.tile(p['sinks'].reshape(N_KV_HEADS, 1, GQA_GROUP), (1, t_new, 1)).reshape(N_KV_HEADS, rows, 1)
    oa_s3 = _attn_sample(sink_rows, q_s3, kn, vn, kc, vc)
    oa_s = oa_s3.reshape(nseq, t_new, Q_DIM).transpose(1, 0, 2).reshape(n_s, Q_DIM)

    ol_s3, hlast_s = _lru_sample(xl_s.reshape(t_new, nseq, LRU_WIDTH), yl_s.reshape(t_new, nseq, LRU_WIDTH),
                                 cconv.transpose(1, 0, 2), ch, *lru_w)
    ol_s = ol_s3.reshape(n_s, LRU_WIDTH)

    tm_post = min(ROW_TILE, n_p)
    tri = jnp.tril(jnp.ones((tm_post, tm_post), F32), -1).astype(BF16)
    post_w = (wab, wlb, wout, row(p['g_post_mix']), row(p['g_pre_ffn']), wr_pair, wr_top, br)
    zero_cnt = jnp.zeros((1, ROUTE_LANES), F32)
    x1_p, h2_p, route_p, stat_p, cnt_p = _post(oa_p, ol_p, sga_p, sgr_p, xp.reshape(n_p, D_MODEL),
                                               gt_a_p, sc_f_p, sh_f_p, *post_w, tri, zero_cnt, rows_per_mod=seq)
    tm_s = min(ROW_TILE, n_s)
    tri_s = tri if tm_s == tm_post else jnp.tril(jnp.ones((tm_s, tm_s), F32), -1).astype(BF16)
    x1_s, h2_s, route_s, stat_s, cnt_all = _post(oa_s, ol_s, sga_s, sgr_s, xs_tm,
                                                 gt_a_s, sc_f_s, sh_f_s, *post_w, tri_s, cnt_p, rows_per_mod=0)

    e_lanes = slice(N_GROUPS, N_GROUPS + N_EXPERTS)
    to_granules = lambda v: (v.astype(I32) // GRANULE).reshape(-1)
    totals = to_granules(cnt_all[0, e_lanes])
    stats = jnp.concatenate([stat_p, stat_s], axis=0).reshape(-1, SUBLANES, ROUTE_LANES)
    cnt8 = to_granules(stats[:, 0, e_lanes])
    goff = to_granules(stats[:, 1, e_lanes])
    tiles_p = n_p // tm_post
    n_tiles = tiles_p + n_s // tm_s
    max_rows = 2 * (n_p + n_s) + n_tiles * N_EXPERTS * (GRANULE - 1) + N_EXPERTS * (EXPERT_ROWS - GRANULE)
    n_blocks = -(-max_rows // EXPERT_ROWS)
    n_slots = n_blocks * EXPERT_ROWS
    srt, gsrc = _sort(totals, cnt8, goff, route_p, h2_p, route_s, h2_s, n_slots)
    block_e, meta = _plan(totals, n_blocks)
    ys = _experts(block_e, meta, gsrc, srt, wg, wu, wd, n_slots)
    y_p = _combine(totals, cnt8, goff, route_p, x1_p, gt_f_p, row(p['g_post_ffn']), ys, seq, 0)
    y_s = _combine(totals, cnt8, goff, route_s, x1_s, gt_f_s, row(p['g_post_ffn']), ys, 0, tiles_p)

    k_new_p = klast_p.reshape(batch, WINDOW, N_KV_HEADS, HEAD_DIM)
    v_new_p = vlast_p.reshape(batch, WINDOW, N_KV_HEADS, HEAD_DIM)
    conv_p = xtail_p[:, -(CONV_WIDTH - 1):]
    h_p = hlast_p.reshape(batch, LRU_WIDTH)
    k_new_s = jnp.concatenate([ck, kn.reshape(nseq, t_new, N_KV_HEADS, HEAD_DIM)], axis=1)[:, -WINDOW:]
    v_new_s = jnp.concatenate([cv, vn.reshape(nseq, t_new, N_KV_HEADS, HEAD_DIM)], axis=1)[:, -WINDOW:]
    xl_s3 = xl_s.reshape(t_new, nseq, LRU_WIDTH).transpose(1, 0, 2).astype(F32)
    conv_s = jnp.concatenate([cconv, xl_s3], axis=1)[:, -(CONV_WIDTH - 1):]
    return (y_p.reshape(batch, seq, D_MODEL), y_s, k_new_p, v_new_p, conv_p, h_p,
            k_new_s, v_new_s, conv_s, hlast_s)


PAST_LEN_ = 16384

PARAM_NAMES = ('w_ada', 'b_ada', 'g_pre_mix', 'g_post_mix', 'g_pre_ffn', 'g_post_ffn', 'w_in', 'sinks',
               'w_conv', 'b_conv', 'w_lru_a', 'b_lru_a', 'w_lru_x', 'b_lru_x', 'lru_lambda',
               'w_attn_branch', 'w_lru_branch', 'w_out', 'w_router_group', 'b_router_group',
               'w_router_expert', 'b_router_expert', 'w_exp_gate', 'w_exp_up', 'w_exp_down')


def kernel(x_prompt, x_sample, cache_k_win, cache_v_win, state_conv, state_h, c_prompt, c_sample, w_ada, b_ada, g_pre_mix, g_post_mix, g_pre_ffn, g_post_ffn, w_in, sinks, w_conv, b_conv, w_lru_a, b_lru_a, w_lru_x, b_lru_x, lru_lambda, w_attn_branch, w_lru_branch, w_out, w_router_group, b_router_group, w_router_expert, b_router_expert, w_exp_gate, w_exp_up, w_exp_down):
    weights = (w_ada, b_ada, g_pre_mix, g_post_mix, g_pre_ffn, g_post_ffn, w_in, sinks,
               w_conv, b_conv, w_lru_a, b_lru_a, w_lru_x, b_lru_x, lru_lambda,
               w_attn_branch, w_lru_branch, w_out, w_router_group, b_router_group,
               w_router_expert, b_router_expert, w_exp_gate, w_exp_up, w_exp_down)
    depth = w_ada.shape[0]
    batch = x_prompt.shape[0]
    nseq, t_new, _ = x_sample.shape
    y_p = x_prompt
    y_s = x_sample.transpose(1, 0, 2).reshape(t_new * nseq, D_MODEL)
    c_all = jnp.concatenate([c_prompt, c_sample], axis=0)
    outs = [[] for _ in range(8)]
    for layer in range(depth):
        p = {name: w[layer] for name, w in zip(PARAM_NAMES, weights)}
        mod = _ada(c_all, p['w_ada'], p['b_ada'].reshape(1, -1))
        res = _layer_forward(y_p, y_s, cache_k_win[layer], cache_v_win[layer], state_conv[layer],
                             state_h[layer], mod[:batch], mod[batch:], p)
        y_p, y_s = res[0], res[1]
        for o, r in zip(outs, res[2:]):
            o.append(r)
    y_sample = y_s.reshape(t_new, nseq, D_MODEL).transpose(1, 0, 2)
    return (y_p, y_sample) + tuple(jnp.stack(o) for o in outs)
```

```python
import functools

import jax
import jax.numpy as jnp
from jax import lax
from jax.experimental import pallas as pl
from jax.experimental.pallas import tpu as pltpu

F32 = jnp.float32
BF16 = jnp.bfloat16
I32 = jnp.int32

D_MODEL = 1024
N_HEADS = 16
HEAD_DIM = 64
N_KV_HEADS = 4
GQA_GROUP = 4
WINDOW = 128
ROPE_THETA = 10000.0
NEG_INF = -1e30
LRU_WIDTH = 1024
LRU_BLOCKS = 16
LRU_BLOCK_W = 64
CONV_WIDTH = 4
LRU_C = 8.0
N_GROUPS = 4
EXPERTS_PER_GROUP = 8
N_EXPERTS = 32
D_EXPERT = 512
MOE_BLOCK = 128
NORM_EPS = 1e-6
Q_DIM = N_HEADS * HEAD_DIM
KV_DIM = N_KV_HEADS * HEAD_DIM
IN_DIM = Q_DIM + 2 * KV_DIM + 2 * LRU_WIDTH + 2 * D_MODEL

LANES = 128
SUBLANES = 8
MXU_DIM = 256
VMEM_LIMIT = 56 * 1024 * 1024
VMEM_LIMIT_BIG = 60 * 1024 * 1024

ROW_TILE = 512
ROUTE_LANES = LANES
GRANULE = SUBLANES
EXPERT_ROWS = 512
HALF_D = D_MODEL // 2
U32 = jnp.uint32


def _cparams(n_axes, vmem=VMEM_LIMIT, flags=None):
    return pltpu.CompilerParams(dimension_semantics=("arbitrary",) * n_axes, vmem_limit_bytes=vmem, flags=flags)


def _rms(x, g):
    ms = jnp.mean(x * x, axis=-1, keepdims=True)
    return x * lax.rsqrt(ms + NORM_EPS) * g


def _sigmoid(x):
    return 1.0 / (1.0 + jnp.exp(-x))


def _dot(a, b):
    return jnp.dot(a, b, preferred_element_type=F32)


def _ada_kernel(c_ref, w_ref, b_ref, o_ref):
    c = c_ref[...]
    s = (c * _sigmoid(c)).astype(BF16)
    o_ref[...] = _dot(s, w_ref[...].astype(BF16)) + b_ref[...]


def _ada(c_all, w_ada, b_ada):
    r = c_all.shape[0]
    n = w_ada.shape[1]
    return pl.pallas_call(
        _ada_kernel,
        grid=(n // D_MODEL,),
        in_specs=[pl.BlockSpec((r, D_MODEL), lambda j: (0, 0)),
                  pl.BlockSpec((D_MODEL, D_MODEL), lambda j: (0, j)),
                  pl.BlockSpec((1, D_MODEL), lambda j: (0, j))],
        out_specs=pl.BlockSpec((r, D_MODEL), lambda j: (0, j)),
        out_shape=jax.ShapeDtypeStruct((r, n), F32),
        compiler_params=_cparams(1),
        name="ada_mod",
    )(c_all, w_ada, b_ada)


_O1 = Q_DIM
_O2 = _O1 + KV_DIM
_O3 = _O2 + KV_DIM
_O4 = _O3 + LRU_WIDTH
_O5 = _O4 + LRU_WIDTH
_O6 = _O5 + D_MODEL


def _prenorm(x_ref, g_ref, sc_ref, sh_ref):
    h = _rms(x_ref[...], g_ref[...]) * (1.0 + sc_ref[...]) + sh_ref[...]
    return h.astype(BF16)


PIECE_COLS = 256


def _plain_pieces(h_ref, w_ref, base, width, store):
    def piece(c0):
        def run():
            store(c0, _dot(h_ref[...], w_ref[:, base + c0:base + c0 + PIECE_COLS]))
        return run
    return [piece(c0) for c0 in range(0, width, PIECE_COLS)]


def _qkv_gate_pieces(h_ref, w_ref, cos_ref, sin_ref, q_ref, k_ref, v_ref, sga_ref, sgr_ref, last_refs=None):
    def rope(t):
        cos = cos_ref[...]
        sin = sin_ref[...]
        lane = lax.broadcasted_iota(I32, cos.shape, 1)
        first_half = (lane % HEAD_DIM) < (HEAD_DIM // 2)
        rot = jnp.where(first_half, pltpu.roll(t, LANES - HEAD_DIM // 2, 1), pltpu.roll(t, HEAD_DIM // 2, 1))
        return t * cos + rot * sin

    def q_piece(c0):
        def run():
            qf = _dot(h_ref[...], w_ref[:, c0:c0 + PIECE_COLS])
            for c in range(PIECE_COLS // LANES):
                q_ref[:, c0 + c * LANES:c0 + (c + 1) * LANES] = rope(qf[:, c * LANES:(c + 1) * LANES]).astype(BF16)
        return run

    def kv_piece():
        kv = _dot(h_ref[...], w_ref[:, _O1:_O3])
        for c in range(KV_DIM // LANES):
            k_ref[:, c * LANES:(c + 1) * LANES] = rope(kv[:, c * LANES:(c + 1) * LANES])
        v_ref[...] = kv[:, KV_DIM:]
        if last_refs is not None:
            t = k_ref.shape[0]
            last_refs[0][...] = k_ref[t - WINDOW:t, :]
            last_refs[1][...] = v_ref[t - WINDOW:t, :]

    def gate_piece(o_ref, base, c0):
        def run():
            z = _dot(h_ref[...], w_ref[:, base + c0:base + c0 + PIECE_COLS])
            o_ref[:, c0:c0 + PIECE_COLS] = _sigmoid(z).astype(BF16)
        return run

    pieces = [q_piece(c0) for c0 in range(0, Q_DIM, PIECE_COLS)] + [kv_piece]
    pieces += [gate_piece(sga_ref, _O5, c0) for c0 in range(0, D_MODEL, PIECE_COLS)]
    pieces += [gate_piece(sgr_ref, _O6, c0) for c0 in range(0, D_MODEL, PIECE_COLS)]
    return pieces


def _inproj_kernel(x_ref, sc_ref, sh_ref, g_ref, cos_ref, sin_ref, w_ref,
                   q_ref, k_ref, v_ref, xl_ref, yl_ref, sga_ref, sgr_ref, hbuf):
    hbuf[...] = _prenorm(x_ref, g_ref, sc_ref, sh_ref)
    xl_ref[...] = _dot(hbuf[...], w_ref[:, _O3:_O4]).astype(BF16)
    yl_ref[...] = _dot(hbuf[...], w_ref[:, _O4:_O5]).astype(BF16)
    for piece in _qkv_gate_pieces(hbuf, w_ref, cos_ref, sin_ref, q_ref, k_ref, v_ref, sga_ref, sgr_ref):
        piece()


def _inproj(x, sc, sh, g, cos, sin, w_in, rows_per_mod, pos_tiles):
    n = x.shape[0]
    tm = min(ROW_TILE, n)
    if rows_per_mod:
        tiles_per_mod = rows_per_mod // tm
        mod_spec = pl.BlockSpec((None, 1, D_MODEL), lambda i: (i // tiles_per_mod, 0, 0))
    else:
        mod_spec = pl.BlockSpec((tm, D_MODEL), lambda i: (i, 0))
    row = lambda w: pl.BlockSpec((tm, w), lambda i: (i, 0))
    outs = [(Q_DIM, BF16), (KV_DIM, F32), (KV_DIM, F32), (LRU_WIDTH, BF16), (LRU_WIDTH, BF16),
            (D_MODEL, BF16), (D_MODEL, BF16)]
    return pl.pallas_call(
        _inproj_kernel,
        grid=(n // tm,),
        in_specs=[row(D_MODEL), mod_spec, mod_spec,
                  pl.BlockSpec((1, D_MODEL), lambda i: (0, 0)),
                  pl.BlockSpec((tm, LANES), lambda i: (i % pos_tiles, 0)),
                  pl.BlockSpec((tm, LANES), lambda i: (i % pos_tiles, 0)),
                  pl.BlockSpec((D_MODEL, IN_DIM), lambda i: (0, 0))],
        out_specs=[row(w) for w, _ in outs],
        out_shape=[jax.ShapeDtypeStruct((n, w), dt) for w, dt in outs],
        scratch_shapes=[pltpu.VMEM((tm, D_MODEL), BF16)],
        compiler_params=_cparams(1),
        name="in_proj",
    )(x, sc, sh, g, cos, sin, w_in)


def _head_masks(shape):
    lane = lax.broadcasted_iota(I32, shape, 1)
    return [(lane // HEAD_DIM) == h for h in range(N_KV_HEADS)]


def _attention_core(q_perm, kall, vall, valid, sink_of, rows):
    masks_b = _head_masks((rows, KV_DIM))
    zero_b = jnp.zeros((rows, KV_DIM), BF16)
    pieces = []
    for h in range(N_KV_HEADS):
        for g in range(GQA_GROUP):
            pieces.append(jnp.where(masks_b[h], q_perm[g], zero_b))
    q_big = jnp.concatenate(pieces, axis=0)
    s_all = lax.dot_general(q_big, kall, (((1,), (1,)), ((), ())), preferred_element_type=F32)
    s_all = s_all * (HEAD_DIM ** -0.5)
    p_chunks, inv_chunks = [], []
    for h in range(N_KV_HEADS):
        for g in range(GQA_GROUP):
            c = h * GQA_GROUP + g
            s = jnp.where(valid, s_all[c * rows:(c + 1) * rows], NEG_INF)
            sink = sink_of(h, g)
            m = jnp.maximum(jnp.max(s, axis=-1, keepdims=True), sink)
            p = jnp.exp(s - m)
            denom = jnp.sum(p, axis=-1, keepdims=True) + jnp.exp(sink - m)
            p_chunks.append(p.astype(BF16))
            inv_chunks.append(1.0 / denom)
    o_all = _dot(jnp.concatenate(p_chunks, axis=0), vall)
    outs = []
    for g in range(GQA_GROUP):
        acc = jnp.zeros((rows, KV_DIM), F32)
        for h in range(N_KV_HEADS):
            c = h * GQA_GROUP + g
            acc = acc + jnp.where(masks_b[h], o_all[c * rows:(c + 1) * rows] * inv_chunks[c], 0.0)
        outs.append(acc)
    return outs


ATTN_Q_BLOCKS = 4


def _attn_prompt_kernel(sink_ref, q_ref, kc_ref, kp_ref, vc_ref, vp_ref, o_ref):
    j = pl.program_id(1)
    kall = jnp.concatenate([kp_ref[...], kc_ref[...]], axis=0).astype(BF16)
    vall = jnp.concatenate([vp_ref[...], vc_ref[...]], axis=0).astype(BF16)
    qi = lax.broadcasted_iota(I32, (WINDOW, 2 * WINDOW), 0)
    kj = lax.broadcasted_iota(I32, (WINDOW, 2 * WINDOW), 1)
    dist = qi + WINDOW - kj
    in_window = (dist >= 0) & (dist <= WINDOW)
    for c in range(ATTN_Q_BLOCKS):
        rows = slice(c * WINDOW, (c + 1) * WINDOW)
        keys = slice(c * WINDOW, (c + 2) * WINDOW)
        valid = in_window & ((kj >= WINDOW) | (j > 0)) if c == 0 else in_window
        q_perm = [q_ref[rows, g * KV_DIM:(g + 1) * KV_DIM] for g in range(GQA_GROUP)]
        outs = _attention_core(q_perm, kall[keys], vall[keys], valid,
                               lambda h, g: sink_ref[h * GQA_GROUP + g], WINDOW)
        for g in range(GQA_GROUP):
            o_ref[rows, g * KV_DIM:(g + 1) * KV_DIM] = outs[g].astype(BF16)


def _attn_prompt(sinks, q, k, v, batch, seq):
    step_rows = ATTN_Q_BLOCKS * WINDOW
    nb = seq // step_rows
    cur = lambda w: pl.BlockSpec((step_rows, w), lambda b, j: (b * nb + j, 0))
    prev = lambda w: pl.BlockSpec(
        (WINDOW, w), lambda b, j: (jnp.maximum((b * nb + j) * ATTN_Q_BLOCKS - 1, 0), 0))
    return pl.pallas_call(
        _attn_prompt_kernel,
        grid=(batch, nb),
        in_specs=[pl.BlockSpec(memory_space=pltpu.SMEM),
                  cur(Q_DIM), cur(KV_DIM), prev(KV_DIM), cur(KV_DIM), prev(KV_DIM)],
        out_specs=cur(Q_DIM),
        out_shape=jax.ShapeDtypeStruct((batch * seq, Q_DIM), BF16),
        compiler_params=_cparams(2),
        name="attn_prompt",
    )(sinks, q, k, k, v, v)


SEQ_PER_STEP = 8


def _attn_sample_kernel(sink_ref, q_ref, kn_ref, vn_ref, kc_ref, vc_ref, o_ref, kbuf, vbuf, *, t_new):
    rows = GQA_GROUP * t_new
    kbuf[WINDOW:2 * WINDOW, :] = jnp.zeros((WINDOW, KV_DIM), F32)
    vbuf[WINDOW:2 * WINDOW, :] = jnp.zeros((WINDOW, KV_DIM), F32)
    ri = lax.broadcasted_iota(I32, (rows, 2 * WINDOW), 0)
    kj = lax.broadcasted_iota(I32, (rows, 2 * WINDOW), 1)
    tq = ri // GQA_GROUP
    valid = (kj >= tq) & (kj <= tq + WINDOW) & (kj < WINDOW + t_new)
    for s in range(SEQ_PER_STEP):
        kbuf[0:WINDOW, :] = kc_ref[s]
        vbuf[0:WINDOW, :] = vc_ref[s]
        kbuf[WINDOW:WINDOW + t_new, :] = kn_ref[s]
        vbuf[WINDOW:WINDOW + t_new, :] = vn_ref[s]
        kall = kbuf[...].astype(BF16)
        vall = vbuf[...].astype(BF16)
        qs = q_ref[s]
        masks_b = _head_masks((rows, KV_DIM))
        zero_b = jnp.zeros((rows, KV_DIM), BF16)
        q_big = jnp.concatenate([jnp.where(masks_b[h], qs, zero_b) for h in range(N_KV_HEADS)], axis=0)
        s_all = lax.dot_general(q_big, kall, (((1,), (1,)), ((), ())), preferred_element_type=F32)
        s_all = s_all * (HEAD_DIM ** -0.5)
        acc = jnp.zeros((rows, KV_DIM), F32)
        p_chunks, inv_chunks = [], []
        for h in range(N_KV_HEADS):
            sc = jnp.where(valid, s_all[h * rows:(h + 1) * rows], NEG_INF)
            sink = sink_ref[h]
            m = jnp.maximum(jnp.max(sc, axis=-1, keepdims=True), sink)
            p = jnp.exp(sc - m)
            denom = jnp.sum(p, axis=-1, keepdims=True) + jnp.exp(sink - m)
            p_chunks.append(p.astype(BF16))
            inv_chunks.append(1.0 / denom)
        o_all = _dot(jnp.concatenate(p_chunks, axis=0), vall)
        for h in range(N_KV_HEADS):
            acc = acc + jnp.where(masks_b[h], o_all[h * rows:(h + 1) * rows] * inv_chunks[h], 0.0)
        o_ref[s] = acc.astype(BF16)


def _attn_sample(sink_rows, q, kn, vn, kc, vc):
    nseq, rows, _ = q.shape
    t_new = kn.shape[1]
    sb = SEQ_PER_STEP
    blk = lambda r: pl.BlockSpec((sb, r, KV_DIM), lambda i: (i, 0, 0))
    return pl.pallas_call(
        functools.partial(_attn_sample_kernel, t_new=t_new),
        grid=(nseq // sb,),
        in_specs=[pl.BlockSpec((N_KV_HEADS, rows, 1), lambda i: (0, 0, 0)),
                  blk(rows), blk(t_new), blk(t_new), blk(WINDOW), blk(WINDOW)],
        out_specs=blk(rows),
        out_shape=jax.ShapeDtypeStruct((nseq, rows, KV_DIM), BF16),
        scratch_shapes=[pltpu.VMEM((2 * WINDOW, KV_DIM), F32), pltpu.VMEM((2 * WINDOW, KV_DIM), F32)],
        compiler_params=_cparams(1),
        name="attn_sample",
    )(sink_rows, q, kn, vn, kc, vc)


def _gelu_tanh(x):
    return 0.5 * x * (1.0 + jnp.tanh(0.7978845608028654 * (x + 0.044715 * x * x * x)))


def _lru_gates(xc, wbd_ref, ba, bx, lam):
    xcb = xc.astype(BF16)
    r_parts, i_parts = [], []
    for gidx in range(LRU_WIDTH // MXU_DIM):
        z = _dot(xcb[:, gidx * MXU_DIM:(gidx + 1) * MXU_DIM], wbd_ref[gidx])
        r_parts.append(z[:, :MXU_DIM])
        i_parts.append(z[:, MXU_DIM:])
    return _gate_math(jnp.concatenate(r_parts, axis=1), jnp.concatenate(i_parts, axis=1), xc, ba, bx, lam)


def _gate_math(zr, zi, xc, ba, bx, lam):
    r = _sigmoid(zr + ba)
    i = _sigmoid(zi + bx)
    softplus_neg_lam = jnp.maximum(-lam, 0.0) + jnp.log1p(jnp.exp(-jnp.abs(lam)))
    log_a = -LRU_C * r * softplus_neg_lam
    a = jnp.exp(log_a)
    y = 1.0 - a * a
    u = jnp.where(y > 0.0, y * lax.rsqrt(y), 0.0) * (i * xc)
    return a, u


LRU_CHUNK = 64


def _lru_chunk(c, xbuf, ybuf, hcar, ol_ref, wc_ref, bc_ref, wbd_ref, ba_ref, bx_ref, lam_ref):
    w = LRU_WIDTH
    n = LRU_CHUNK
    rows = slice(c * n, (c + 1) * n)
    r0 = SUBLANES + c * n
    xc = xbuf[r0:r0 + n, :] * wc_ref[CONV_WIDTH - 1:CONV_WIDTH, :] + bc_ref[...]
    for k in range(1, CONV_WIDTH):
        xc = xc + xbuf[r0 - k:r0 - k + n, :] * wc_ref[CONV_WIDTH - 1 - k:CONV_WIDTH - k, :]
    a, u = _lru_gates(xc, wbd_ref, ba_ref[...], bx_ref[...], lam_ref[...])

    ng = n // SUBLANES
    a3 = a.reshape(ng, SUBLANES, w)
    u3 = u.reshape(ng, SUBLANES, w)
    row = lax.broadcasted_iota(I32, (ng, SUBLANES, w), 1)
    d = 1
    while d < SUBLANES:
        a_s = jnp.where(row >= d, pltpu.roll(a3, d, 1), 1.0)
        u_s = jnp.where(row >= d, pltpu.roll(u3, d, 1), 0.0)
        u3 = a3 * u_s + u3
        a3 = a3 * a_s
        d *= 2
    carry = hcar[...]
    hs = []
    for gi in range(ng):
        hg = a3[gi] * carry + u3[gi]
        hs.append(hg)
        carry = hg[SUBLANES - 1:SUBLANES, :]
    hcar[...] = carry
    h = jnp.concatenate(hs, axis=0)
    ol_ref[rows, :] = (h * _gelu_tanh(ybuf[rows, :])).astype(BF16)


def _inproj_lru_kernel(x_ref, sc_ref, sh_ref, g_ref, cos_ref, sin_ref, w_ref,
                       wc_ref, bc_ref, wbd_ref, ba_ref, bx_ref, lam_ref,
                       q_ref, k_ref, v_ref, sga_ref, sgr_ref, klast_ref, vlast_ref, ol_ref, xtail_ref, hlast_ref,
                       hbuf, xbuf, ybuf, hcar, *, tiles_per_seq):
    t = x_ref.shape[0]

    @pl.when(pl.program_id(0) % tiles_per_seq == 0)
    def _():
        xbuf[0:SUBLANES, :] = jnp.zeros((SUBLANES, LRU_WIDTH), F32)
        hcar[...] = jnp.zeros((1, LRU_WIDTH), F32)

    hbuf[...] = _prenorm(x_ref, g_ref, sc_ref, sh_ref)
    xbuf[SUBLANES:SUBLANES + t, :] = _dot(hbuf[...], w_ref[:, _O3:_O4])
    ybuf[...] = _dot(hbuf[...], w_ref[:, _O4:_O5])

    pieces = _qkv_gate_pieces(hbuf, w_ref, cos_ref, sin_ref, q_ref, k_ref, v_ref, sga_ref, sgr_ref,
                              last_refs=(klast_ref, vlast_ref))
    n_chunks = t // LRU_CHUNK
    per_chunk = -(-len(pieces) // n_chunks)
    for c in range(n_chunks):
        _lru_chunk(c, xbuf, ybuf, hcar, ol_ref, wc_ref, bc_ref, wbd_ref, ba_ref, bx_ref, lam_ref)
        for piece in pieces[c * per_chunk:(c + 1) * per_chunk]:
            piece()

    tail = xbuf[t:t + SUBLANES, :]
    xtail_ref[...] = tail
    xbuf[0:SUBLANES, :] = tail
    hlast_ref[...] = hcar[...]


def _inproj_lru(x, sc, sh, g, cos, sin, w_in, wc, bc, wbd, ba, bx, lam, batch, seq):
    n = x.shape[0]
    tm = min(ROW_TILE, seq)
    tps = seq // tm
    mod_spec = pl.BlockSpec((None, 1, D_MODEL), lambda i: (i // tps, 0, 0))
    row = lambda w: pl.BlockSpec((tm, w), lambda i: (i, 0))
    full = lambda a: pl.BlockSpec(a.shape, lambda i: (0,) * a.ndim)
    per_seq = lambda r, w: pl.BlockSpec((None, r, w), lambda i: (i // tps, 0, 0))
    outs = [(Q_DIM, BF16), (KV_DIM, F32), (KV_DIM, F32), (D_MODEL, BF16), (D_MODEL, BF16)]
    return pl.pallas_call(
        functools.partial(_inproj_lru_kernel, tiles_per_seq=tps),
        grid=(n // tm,),
        in_specs=[row(D_MODEL), mod_spec, mod_spec, full(g),
                  pl.BlockSpec((tm, LANES), lambda i: (i % tps, 0)),
                  pl.BlockSpec((tm, LANES), lambda i: (i % tps, 0)),
                  full(w_in), full(wc), full(bc), full(wbd), full(ba), full(bx), full(lam)],
        out_specs=[row(w) for w, _ in outs] + [per_seq(WINDOW, KV_DIM), per_seq(WINDOW, KV_DIM)]
        + [row(LRU_WIDTH), per_seq(SUBLANES, LRU_WIDTH), per_seq(1, LRU_WIDTH)],
        out_shape=[jax.ShapeDtypeStruct((n, w), dt) for w, dt in outs]
        + [jax.ShapeDtypeStruct((batch, WINDOW, KV_DIM), F32)] * 2
        + [jax.ShapeDtypeStruct((n, LRU_WIDTH), BF16),
           jax.ShapeDtypeStruct((batch, SUBLANES, LRU_WIDTH), F32),
           jax.ShapeDtypeStruct((batch, 1, LRU_WIDTH), F32)],
        scratch_shapes=[pltpu.VMEM((tm, D_MODEL), BF16),
                        pltpu.VMEM((2 * SUBLANES + tm, LRU_WIDTH), F32), pltpu.VMEM((tm, LRU_WIDTH), F32),
                        pltpu.VMEM((1, LRU_WIDTH), F32)],
        compiler_params=_cparams(1, vmem=VMEM_LIMIT_BIG),
        name="in_proj_lru",
    )(x, sc, sh, g, cos, sin, w_in, wc, bc, wbd, ba, bx, lam)


def _lru_sample_kernel(xl_ref, yl_ref, cs_ref, h0_ref, wc_ref, bc_ref, wbd_ref, ba_ref, bx_ref, lam_ref,
                       o_ref, hlast_ref):
    t_new, nseq, w = xl_ref.shape
    xp = [cs_ref[k] for k in range(CONV_WIDTH - 1)] + [xl_ref[k].astype(F32) for k in range(t_new)]
    xcs = []
    for t in range(t_new):
        acc = bc_ref[...] + xp[t] * wc_ref[0:1, :]
        for k in range(1, CONV_WIDTH):
            acc = acc + xp[t + k] * wc_ref[k:k + 1, :]
        xcs.append(acc)
    xc = jnp.concatenate(xcs, axis=0)
    a, u = _lru_gates(xc, wbd_ref, ba_ref[...], bx_ref[...], lam_ref[...])
    h = h0_ref[...]
    for t in range(t_new):
        h = a[t * nseq:(t + 1) * nseq] * h + u[t * nseq:(t + 1) * nseq]
        o_ref[t] = (h * _gelu_tanh(yl_ref[t].astype(F32))).astype(BF16)
    hlast_ref[...] = h


def _lru_sample(xl, yl, cs, h0, wc, bc, wbd, ba, bx, lam):
    t_new, nseq, w = xl.shape
    full = lambda shp: pl.BlockSpec(shp, lambda i: (0,) * len(shp))
    args = (xl, yl, cs, h0, wc, bc, wbd, ba, bx, lam)
    return pl.pallas_call(
        _lru_sample_kernel,
        grid=(1,),
        in_specs=[full(a.shape) for a in args],
        out_specs=[full((t_new, nseq, w)), full((nseq, w))],
        out_shape=[jax.ShapeDtypeStruct((t_new, nseq, w), BF16), jax.ShapeDtypeStruct((nseq, w), F32)],
        compiler_params=_cparams(1),
        name="lru_sample",
    )(*args)


def _post_kernel(oa_ref, ol_ref, sga_ref, sgr_ref, x_ref, gta_ref, scf_ref, shf_ref,
                 wab_ref, wlb_ref, wout_ref, gpm_ref, gpf_ref, wrp_ref, wrt_ref, br_ref, tri_ref, cin_ref,
                 x1_ref, h2_ref, route_ref, stat_ref, cnt_ref, carry):
    i = pl.program_id(0)

    @pl.when(i == 0)
    def _():
        carry[...] = cin_ref[...]

    b_attn = _dot(oa_ref[...], wab_ref[...])
    b_lru = _dot(ol_ref[...], wlb_ref[...])
    merged = sga_ref[...].astype(F32) * b_attn + sgr_ref[...].astype(F32) * b_lru
    mix = _dot(merged.astype(BF16), wout_ref[...])
    x1 = x_ref[...] + gta_ref[...] * _rms(mix, gpm_ref[...])
    x1_ref[...] = x1
    h2 = _rms(x1, gpf_ref[...]) * (1.0 + scf_ref[...]) + shf_ref[...]
    h2_ref[...] = h2.astype(BF16)

    h_hi = h2.astype(BF16)
    h_lo = (h2 - h_hi.astype(F32)).astype(BF16)
    hi_terms = _dot(h_hi, wrp_ref[...])
    logits = (hi_terms[:, :ROUTE_LANES] + (hi_terms[:, ROUTE_LANES:] + _dot(h_lo, wrt_ref[...]))) + br_ref[...]

    tm = logits.shape[0]
    lane = lax.broadcasted_iota(I32, (tm, ROUTE_LANES), 1)
    big = jnp.int32(ROUTE_LANES)
    is_g = lane < N_GROUPS
    lg = jnp.where(is_g, logits, NEG_INF)
    mg = jnp.max(lg, axis=-1, keepdims=True)
    g_star = jnp.min(jnp.where(lg == mg, lane, big), axis=-1, keepdims=True)
    p_star = 1.0 / jnp.sum(jnp.where(is_g, jnp.exp(lg - mg), 0.0), axis=-1, keepdims=True)
    lo = N_GROUPS + g_star * EXPERTS_PER_GROUP
    in_grp = (lane >= lo) & (lane < lo + EXPERTS_PER_GROUP)
    le = jnp.where(in_grp, logits, NEG_INF)
    m1 = jnp.max(le, axis=-1, keepdims=True)
    i1 = jnp.min(jnp.where(le == m1, lane, big), axis=-1, keepdims=True)
    le2 = jnp.where(lane == i1, NEG_INF, le)
    m2 = jnp.max(le2, axis=-1, keepdims=True)
    i2 = jnp.min(jnp.where(le2 == m2, lane, big), axis=-1, keepdims=True)
    e2x = jnp.exp(m2 - m1)
    wsum = 1.0 + e2x
    w1 = (1.0 / wsum) * p_star
    w2 = (e2x / wsum) * p_star

    oh1 = lane == i1
    oh2 = lane == i2
    cnt = jnp.where(oh1 | oh2, 1.0, 0.0)
    excl = _dot(tri_ref[...], cnt.astype(BF16))
    per_e = jnp.sum(cnt, axis=0, keepdims=True)
    pad8 = jnp.floor((per_e + (GRANULE - 1.0)) * (1.0 / GRANULE)) * GRANULE
    incl = jnp.broadcast_to(pad8, (SUBLANES, ROUTE_LANES))
    lane8 = lax.broadcasted_iota(I32, (SUBLANES, ROUTE_LANES), 1)
    d = 1
    while d < ROUTE_LANES:
        incl = incl + jnp.where(lane8 >= d, pltpu.roll(incl, d, 1), 0.0)
        d *= 2
    seg_start = incl[0:1, :] - pad8
    pos = excl + seg_start
    s1 = jnp.sum(jnp.where(oh1, pos, 0.0), axis=-1, keepdims=True)
    s2 = jnp.sum(jnp.where(oh2, pos, 0.0), axis=-1, keepdims=True)
    rec = jnp.where(lane == 0, s1, 0.0)
    rec = jnp.where(lane == 1, s2, rec)
    rec = jnp.where(lane == 2, w1, rec)
    rec = jnp.where(lane == 3, w2, rec)
    route_ref[...] = rec
    srow = lax.broadcasted_iota(I32, (SUBLANES, ROUTE_LANES), 0)
    stat_ref[...] = jnp.where(srow == 0, pad8, jnp.where(srow == 1, carry[...], 0.0))
    carry[...] = carry[...] + pad8
    cnt_ref[...] = carry[...]


def _post(oa, ol, sga, sgr, x, gta, scf, shf, wab, wlb, wout, gpm, gpf, wrh, wrl, br, tri, cin, rows_per_mod):
    n = x.shape[0]
    tm = min(ROW_TILE, n)
    if rows_per_mod:
        tiles_per_mod = rows_per_mod // tm
        mod_spec = pl.BlockSpec((None, 1, D_MODEL), lambda i: (i // tiles_per_mod, 0, 0))
    else:
        mod_spec = pl.BlockSpec((tm, D_MODEL), lambda i: (i, 0))
    row = lambda w: pl.BlockSpec((tm, w), lambda i: (i, 0))
    full = lambda a: pl.BlockSpec(a.shape, lambda i: (0,) * a.ndim)
    return pl.pallas_call(
        _post_kernel,
        grid=(n // tm,),
        in_specs=[row(Q_DIM), row(LRU_WIDTH), row(D_MODEL), row(D_MODEL), row(D_MODEL),
                  mod_spec, mod_spec, mod_spec,
                  full(wab), full(wlb), full(wout), full(gpm), full(gpf), full(wrh), full(wrl), full(br),
                  full(tri), full(cin)],
        out_specs=[row(D_MODEL), row(D_MODEL), row(ROUTE_LANES),
                   pl.BlockSpec((SUBLANES, ROUTE_LANES), lambda i: (i, 0)),
                   pl.BlockSpec((1, ROUTE_LANES), lambda i: (0, 0))],
        out_shape=[jax.ShapeDtypeStruct((n, D_MODEL), F32), jax.ShapeDtypeStruct((n, D_MODEL), BF16),
                   jax.ShapeDtypeStruct((n, ROUTE_LANES), F32),
                   jax.ShapeDtypeStruct((n // tm * SUBLANES, ROUTE_LANES), F32),
                   jax.ShapeDtypeStruct((1, ROUTE_LANES), F32)],
        scratch_shapes=[pltpu.VMEM((1, ROUTE_LANES), F32)],
        compiler_params=_cparams(1),
        name="post_mix",
    )(oa, ol, sga, sgr, x, gta, scf, shf, wab, wlb, wout, gpm, gpf, wrh, wrl, br, tri, cin)


BLOCK_GRANULES = EXPERT_ROWS // GRANULE
BLOCK_SHIFT = BLOCK_GRANULES.bit_length() - 1
assert BLOCK_GRANULES == 1 << BLOCK_SHIFT


def _padded(c):
    return ((c + (BLOCK_GRANULES - 1)) >> BLOCK_SHIFT) << BLOCK_SHIFT


def _sorted_rows(tm):
    r = 2 * tm + N_EXPERTS * (GRANULE - 1)
    return -(-r // MXU_DIM) * MXU_DIM


def _plan_kernel(tot_ref, be_ref, meta_ref, *, n_blocks):
    def fill(j, _):
        be_ref[j] = N_EXPERTS - 1
        return 0
    lax.fori_loop(0, n_blocks, fill, 0)

    def per_expert(e, nb):
        k = _padded(tot_ref[e]) >> BLOCK_SHIFT

        def put(b, _):
            be_ref[nb + b] = e
            return 0
        lax.fori_loop(0, k, put, 0)
        return nb + k
    n_active = lax.fori_loop(0, N_EXPERTS, per_expert, 0)
    meta_ref[0] = n_active


def _plan(totals, n_blocks):
    return pl.pallas_call(
        functools.partial(_plan_kernel, n_blocks=n_blocks),
        in_specs=[pl.BlockSpec(memory_space=pltpu.SMEM)],
        out_specs=[pl.BlockSpec(memory_space=pltpu.SMEM), pl.BlockSpec(memory_space=pltpu.SMEM)],
        out_shape=[jax.ShapeDtypeStruct((n_blocks,), I32), jax.ShapeDtypeStruct((1,), I32)],
        name="moe_plan",
    )(totals)


def _expert_starts(tot_ref, pstart):
    def body(e, acc):
        pstart[e] = acc
        return acc + _padded(tot_ref[e])
    return lax.fori_loop(0, N_EXPERTS, body, 0)


def _granule(ref, g):
    return ref.at[pl.ds(pl.multiple_of(g * GRANULE, GRANULE), GRANULE)]


def _pack_halves(lo_f32, hi_f32):
    return (pltpu.bitcast(lo_f32, U32) >> 16) | (pltpu.bitcast(hi_f32, U32) & jnp.uint32(0xFFFF0000))


def _unpack_halves(packed):
    lo = pltpu.bitcast(packed << 16, F32).astype(BF16)
    hi = pltpu.bitcast(packed & jnp.uint32(0xFFFF0000), F32).astype(BF16)
    return lo, hi


def _sort_kernel(tot_ref, cnt_ref, goff_ref, rec_p_ref, h2_p_ref, rec_s_ref, h2_s_ref,
                 srt_ref, gsrc_ref, gslot_ref, pstart, *, tiles_p, n_sorted, n_slots):
    i = pl.program_id(0)
    from_sample = i >= tiles_p
    rec = jnp.where(from_sample, rec_s_ref[...], rec_p_ref[...])
    h2 = jnp.where(from_sample, h2_s_ref[...], h2_p_ref[...])
    tm = h2.shape[0]

    rec_t = rec.T
    s1 = rec_t[0:1, :].astype(I32)
    s2 = rec_t[1:2, :].astype(I32)
    rows = lax.broadcasted_iota(I32, (n_sorted, tm), 0)
    sel = jnp.where((rows == s1) | (rows == s2), 1.0, 0.0).astype(BF16)
    srt = _dot(sel, h2)
    srt_ref[...] = _pack_halves(srt[:, :HALF_D], srt[:, HALF_D:])

    zero_granule = n_sorted // GRANULE - 1

    @pl.when(i == 0)
    def _():
        used = _expert_starts(tot_ref, pstart)

        def put_zero(g, _):
            gsrc_ref[g] = zero_granule
            return 0

        def per_expert(e, _):
            total = tot_ref[e]
            lax.fori_loop(pstart[e] + total, pstart[e] + _padded(total), put_zero, 0)
            return 0
        lax.fori_loop(0, N_EXPERTS, per_expert, 0)
        lax.fori_loop(used, n_slots // GRANULE, put_zero, 0)

    tile_granules = n_sorted // GRANULE

    def per_expert(e, seg):
        k = cnt_ref[i * N_EXPERTS + e]
        dst = pstart[e] + goff_ref[i * N_EXPERTS + e]
        src = i * tile_granules + seg

        def put(g, _):
            gsrc_ref[dst + g] = src + g
            gslot_ref[src + g] = dst + g
            return 0
        lax.fori_loop(0, k, put, 0)
        return seg + k
    used_in_tile = lax.fori_loop(0, N_EXPERTS, per_expert, 0, unroll=4)

    def put_any(q, _):
        gslot_ref[i * tile_granules + q] = 0
        return 0
    lax.fori_loop(used_in_tile, tile_granules, put_any, 0)


def _sort(totals, cnt8, goff, rec_p, h2_p, rec_s, h2_s, n_slots):
    tm = min(ROW_TILE, h2_p.shape[0])
    assert h2_s.shape[0] % tm == 0
    tiles_p = h2_p.shape[0] // tm
    tiles_s = h2_s.shape[0] // tm
    n_sorted = _sorted_rows(tm)
    row_p = lambda w: pl.BlockSpec((tm, w), lambda i, *_: (jnp.minimum(i, tiles_p - 1), 0))
    row_s = lambda w: pl.BlockSpec((tm, w), lambda i, *_: (jnp.maximum(i - tiles_p, 0), 0))
    return pl.pallas_call(
        functools.partial(_sort_kernel, tiles_p=tiles_p, n_sorted=n_sorted, n_slots=n_slots),
        grid_spec=pltpu.PrefetchScalarGridSpec(
            num_scalar_prefetch=3,
            grid=(tiles_p + tiles_s,),
            in_specs=[row_p(ROUTE_LANES), row_p(D_MODEL), row_s(ROUTE_LANES), row_s(D_MODEL)],
            out_specs=[pl.BlockSpec((n_sorted, HALF_D), lambda i, *_: (i, 0)),
                       pl.BlockSpec(memory_space=pltpu.SMEM), pl.BlockSpec(memory_space=pltpu.SMEM)],
            scratch_shapes=[pltpu.SMEM((N_EXPERTS,), I32)]),
        out_shape=[jax.ShapeDtypeStruct(((tiles_p + tiles_s) * n_sorted, HALF_D), U32),
                   jax.ShapeDtypeStruct((n_slots // GRANULE,), I32),
                   jax.ShapeDtypeStruct(((tiles_p + tiles_s) * n_sorted // GRANULE,), I32)],
        compiler_params=_cparams(1),
        name="moe_sort",
    )(totals, cnt8, goff, rec_p, h2_p, rec_s, h2_s)


def _expert_kernel(be_ref, meta_ref, gsrc_ref, srt_hbm, wg_ref, wu_ref, wd_ref, ys_ref,
                   xbuf, wgb, wub, wdb, sems):
    j = pl.program_id(0)
    n_active = meta_ref[0]
    gran_per_block = BLOCK_GRANULES

    def granule_copy(blk, g, slot):
        return pltpu.make_async_copy(_granule(srt_hbm, gsrc_ref[blk * gran_per_block + g]),
                                     xbuf.at[slot, pl.ds(g * GRANULE, GRANULE)], sems.at[slot])

    def gather(blk, slot):
        for g in range(gran_per_block):
            granule_copy(blk, g, slot).start()

    @pl.when(j == 0)
    def _():
        gather(0, 0)

    @pl.when(j < n_active)
    def _():
        slot = j % 2

        @pl.when(j + 1 < n_active)
        def _():
            gather(j + 1, 1 - slot)

        @pl.when((j == 0) | (be_ref[j] != be_ref[jnp.maximum(j - 1, 0)]))
        def _():
            wgb[...] = wg_ref[...].astype(BF16)
            wub[...] = wu_ref[...].astype(BF16)
            wdb[...] = wd_ref[...].astype(BF16)

        for g in range(gran_per_block):
            granule_copy(j, g, slot).wait()
        x_lo, x_hi = _unpack_halves(xbuf[slot])
        g = _dot(x_lo, wgb[0:HALF_D, :]) + _dot(x_hi, wgb[HALF_D:D_MODEL, :])
        u = _dot(x_lo, wub[0:HALF_D, :]) + _dot(x_hi, wub[HALF_D:D_MODEL, :])
        hmid = (g * _sigmoid(g) * u).astype(BF16)
        y = _dot(hmid, wdb[...])
        ys_ref[...] = _pack_halves(y[:, :HALF_D].astype(BF16).astype(F32), y[:, HALF_D:].astype(BF16).astype(F32))

    @pl.when(j >= meta_ref[0])
    def _():
        ys_ref[...] = jnp.zeros(ys_ref.shape, U32)


def _experts(block_e, meta, gsrc, srt, wg, wu, wd, n_slots):
    n_blocks = n_slots // EXPERT_ROWS
    wspec = lambda shp: pl.BlockSpec(
        (None,) + shp, lambda j, be, meta, gs: (be[jnp.minimum(j, meta[0] - 1)], 0, 0))
    return pl.pallas_call(
        _expert_kernel,
        grid_spec=pltpu.PrefetchScalarGridSpec(
            num_scalar_prefetch=3,
            grid=(n_blocks,),
            in_specs=[pl.BlockSpec(memory_space=pl.ANY),
                      wspec((D_MODEL, D_EXPERT)), wspec((D_MODEL, D_EXPERT)), wspec((D_EXPERT, D_MODEL))],
            out_specs=pl.BlockSpec((EXPERT_ROWS, HALF_D), lambda j, be, meta, gs: (j, 0)),
            scratch_shapes=[pltpu.VMEM((2, EXPERT_ROWS, HALF_D), U32),
                            pltpu.VMEM((D_MODEL, D_EXPERT), BF16), pltpu.VMEM((D_MODEL, D_EXPERT), BF16),
                            pltpu.VMEM((D_EXPERT, D_MODEL), BF16), pltpu.SemaphoreType.DMA((2,))]),
        out_shape=jax.ShapeDtypeStruct((n_slots, HALF_D), U32),
        compiler_params=_cparams(1),
        name="moe_experts",
    )(block_e, meta, gsrc, srt, wg, wu, wd)


def _combine_kernel(gslot_ref, rec_ref, x1_ref, gtf_ref, g_ref, ys_hbm, y_ref, cbuf, sems, *, tile_base, n_tiles):
    i = pl.program_id(0)
    slot = i % 2
    tm = x1_ref.shape[0]
    n_sorted = cbuf.shape[1]
    tile_granules = n_sorted // GRANULE

    def granule_copy(slot_granule, q, slot_):
        return pltpu.make_async_copy(_granule(ys_hbm, slot_granule),
                                     cbuf.at[slot_, pl.ds(q * GRANULE, GRANULE)], sems.at[slot_])

    def gather(tile, slot_):
        for q in range(tile_granules):
            granule_copy(gslot_ref[tile * tile_granules + q], q, slot_).start()

    def drain(slot_):
        for q in range(tile_granules):
            granule_copy(0, q, slot_).wait()

    @pl.when(i == 0)
    def _():
        gather(tile_base, 0)

    gather(tile_base + jnp.minimum(i + 1, n_tiles - 1), 1 - slot)
    drain(slot)

    rec = rec_ref[...]
    s1 = rec[:, 0:1].astype(I32)
    s2 = rec[:, 1:2].astype(I32)
    col = lax.broadcasted_iota(I32, (tm, n_sorted), 1)
    wmat = (jnp.where(col == s1, rec[:, 2:3], 0.0) + jnp.where(col == s2, rec[:, 3:4], 0.0)).astype(BF16)
    y_lo, y_hi = _unpack_halves(cbuf[slot])
    f = jnp.concatenate([_dot(wmat, y_lo), _dot(wmat, y_hi)], axis=1)
    y_ref[...] = x1_ref[...] + gtf_ref[...] * _rms(f, g_ref[...])

    @pl.when(i == n_tiles - 1)
    def _():
        drain(1 - slot)


def _combine(gslot, rec, x1, gtf, g, ys, rows_per_mod, tile_base):
    n = x1.shape[0]
    tm = min(ROW_TILE, n)
    if rows_per_mod:
        tiles_per_mod = rows_per_mod // tm
        mod_spec = pl.BlockSpec((None, 1, D_MODEL), lambda i, *_: (i // tiles_per_mod, 0, 0))
    else:
        mod_spec = pl.BlockSpec((tm, D_MODEL), lambda i, *_: (i, 0))
    row = lambda w: pl.BlockSpec((tm, w), lambda i, *_: (i, 0))
    return pl.pallas_call(
        functools.partial(_combine_kernel, tile_base=tile_base, n_tiles=n // tm),
        grid_spec=pltpu.PrefetchScalarGridSpec(
            num_scalar_prefetch=1,
            grid=(n // tm,),
            in_specs=[row(ROUTE_LANES), row(D_MODEL), mod_spec,
                      pl.BlockSpec((1, D_MODEL), lambda i, *_: (0, 0)),
                      pl.BlockSpec(memory_space=pl.ANY)],
            out_specs=row(D_MODEL),
            scratch_shapes=[pltpu.VMEM((2, _sorted_rows(tm), HALF_D), U32), pltpu.SemaphoreType.DMA((2,))]),
        out_shape=jax.ShapeDtypeStruct((n, D_MODEL), F32),
        compiler_params=_cparams(1),
        name="moe_combine",
    )(gslot, rec, x1, gtf, g, ys)


def _rope_tables(pos):
    half = HEAD_DIM // 2
    inv = jnp.power(jnp.float32(ROPE_THETA), -jnp.arange(half, dtype=F32) / half)
    ang = pos.astype(F32)[:, None] * inv[None, :]
    cos = jnp.cos(ang)
    sin = jnp.sin(ang)
    reps = LANES // HEAD_DIM
    cos_t = jnp.tile(jnp.concatenate([cos, cos], axis=-1), (1, reps))
    sin_t = jnp.tile(jnp.concatenate([-sin, sin], axis=-1), (1, reps))
    return cos_t, sin_t


def _q_perm_index():
    g = jnp.arange(GQA_GROUP)[:, None, None]
    h = jnp.arange(N_KV_HEADS)[None, :, None]
    d = jnp.arange(HEAD_DIM)[None, None, :]
    return ((h * GQA_GROUP + g) * HEAD_DIM + d).reshape(-1)


def _block_diag_gates(w_a, w_x):
    per = MXU_DIM // LRU_BLOCK_W
    groups = LRU_BLOCKS // per

    def bd(w):
        w = w.reshape(groups, per, LRU_BLOCK_W, LRU_BLOCK_W)
        eye = jnp.eye(per, dtype=w.dtype)
        full = jnp.einsum('gpij,pq->gpiqj', w, eye)
        return full.reshape(groups, MXU_DIM, MXU_DIM)
    return jnp.concatenate([bd(w_a), bd(w_x)], axis=-1).astype(BF16)


def _layer_forward(xp, xs_tm, ck, cv, cconv, ch, mod_p, mod_s, p):
    batch, seq, _ = xp.shape
    nseq, _, _, _ = ck.shape
    t_new = xs_tm.shape[0] // nseq
    n_p = batch * seq
    n_s = xs_tm.shape[0]

    perm = _q_perm_index()
    w_in = p['w_in']
    w_in_b = jnp.concatenate([w_in[:, :Q_DIM][:, perm], w_in[:, Q_DIM:]], axis=1).astype(BF16)
    wab = p['w_attn_branch'][perm, :].astype(BF16)
    wlb = p['w_lru_branch'].astype(BF16)
    wout = p['w_out'].astype(BF16)
    wbd = _block_diag_gates(p['w_lru_a'], p['w_lru_x'])
    row = lambda v: v.reshape(1, -1)
    wr = jnp.concatenate([p['w_router_group'], p['w_router_expert'],
                          jnp.zeros((D_MODEL, ROUTE_LANES - N_GROUPS - N_EXPERTS), F32)], axis=1)
    wr_top = wr.astype(BF16)
    wr_pair = jnp.concatenate([wr_top, (wr - wr_top.astype(F32)).astype(BF16)], axis=1)
    br = jnp.concatenate([p['b_router_group'], p['b_router_expert'],
                          jnp.zeros((ROUTE_LANES - N_GROUPS - N_EXPERTS,), F32)]).reshape(1, -1)
    wg = p['w_exp_gate']
    wu = p['w_exp_up']
    wd = p['w_exp_down']

    def mods(mod):
        return [mod[:, k * D_MODEL:(k + 1) * D_MODEL] for k in range(6)]
    sh_a_p, sc_a_p, gt_a_p, sh_f_p, sc_f_p, gt_f_p = [m.reshape(batch, 1, D_MODEL) for m in mods(mod_p)]
    sh_a_s, sc_a_s, gt_a_s, sh_f_s, sc_f_s, gt_f_s = [jnp.tile(m, (t_new, 1)) for m in mods(mod_s)]

    lru_w = (p['w_conv'], row(p['b_conv']), wbd, row(p['b_lru_a']), row(p['b_lru_x']), row(p['lru_lambda']))
    cos_p, sin_p = _rope_tables(jnp.arange(seq, dtype=I32))
    q_p, k_p, v_p, sga_p, sgr_p, klast_p, vlast_p, ol_p, xtail_p, hlast_p = _inproj_lru(
        xp.reshape(n_p, D_MODEL), sc_a_p, sh_a_p, row(p['g_pre_mix']), cos_p, sin_p, w_in_b, *lru_w, batch, seq)
    pos_s = jnp.repeat(PAST_LEN_ + jnp.arange(t_new, dtype=I32), nseq)
    cos_s, sin_s = _rope_tables(pos_s)
    q_s, k_s, v_s, xl_s, yl_s, sga_s, sgr_s = _inproj(
        xs_tm, sc_a_s, sh_a_s, row(p['g_pre_mix']), cos_s, sin_s, w_in_b, rows_per_mod=0,
        pos_tiles=n_s // min(ROW_TILE, n_s))

    sinks_perm = p['sinks']
    oa_p = _attn_prompt(sinks_perm, q_p, k_p, v_p, batch, seq)
    rows = t_new * GQA_GROUP
    q_s3 = q_s.reshape(t_new, nseq, GQA_GROUP, KV_DIM).transpose(1, 0, 2, 3).reshape(nseq, rows, KV_DIM)
    kn = k_s.reshape(t_new, nseq, KV_DIM).transpose(1, 0, 2)
    vn = v_s.reshape(t_new, nseq, KV_DIM).transpose(1, 0, 2)
    kc = ck.reshape(nseq, WINDOW, KV_DIM)
    vc = cv.reshape(nseq, WINDOW, KV_DIM)
    sink_rows = jnp.tile(p['sinks'].reshape(N_KV_HEADS, 1, GQA_GROUP), (1, t_new, 1)).reshape(N_KV_HEADS, rows, 1)
    oa_s3 = _attn_sample(sink_rows, q_s3, kn, vn, kc, vc)
    oa_s = oa_s3.reshape(nseq, t_new, Q_DIM).transpose(1, 0, 2).reshape(n_s, Q_DIM)

    ol_s3, hlast_s = _lru_sample(xl_s.reshape(t_new, nseq, LRU_WIDTH), yl_s.reshape(t_new, nseq, LRU_WIDTH),
                                 cconv.transpose(1, 0, 2), ch, *lru_w)
    ol_s = ol_s3.reshape(n_s, LRU_WIDTH)

    tm_post = min(ROW_TILE, n_p)
    tri = jnp.tril(jnp.ones((tm_post, tm_post), F32), -1).astype(BF16)
    post_w = (wab, wlb, wout, row(p['g_post_mix']), row(p['g_pre_ffn']), wr_pair, wr_top, br)
    zero_cnt = jnp.zeros((1, ROUTE_LANES), F32)
    x1_p, h2_p, route_p, stat_p, cnt_p = _post(oa_p, ol_p, sga_p, sgr_p, xp.reshape(n_p, D_MODEL),
                                               gt_a_p, sc_f_p, sh_f_p, *post_w, tri, zero_cnt, rows_per_mod=seq)
    tm_s = min(ROW_TILE, n_s)
    tri_s = tri if tm_s == tm_post else jnp.tril(jnp.ones((tm_s, tm_s), F32), -1).astype(BF16)
    x1_s, h2_s, route_s, stat_s, cnt_all = _post(oa_s, ol_s, sga_s, sgr_s, xs_tm,
                                                 gt_a_s, sc_f_s, sh_f_s, *post_w, tri_s, cnt_p, rows_per_mod=0)

    e_lanes = slice(N_GROUPS, N_GROUPS + N_EXPERTS)
    to_granules = lambda v: (v.astype(I32) // GRANULE).reshape(-1)
    totals = to_granules(cnt_all[0, e_lanes])
    stats = jnp.concatenate([stat_p, stat_s], axis=0).reshape(-1, SUBLANES, ROUTE_LANES)
    cnt8 = to_granules(stats[:, 0, e_lanes])
    goff = to_granules(stats[:, 1, e_lanes])
    tiles_p = n_p // tm_post
    n_tiles = tiles_p + n_s // tm_s
    max_rows = 2 * (n_p + n_s) + n_tiles * N_EXPERTS * (GRANULE - 1) + N_EXPERTS * (EXPERT_ROWS - GRANULE)
    n_blocks = -(-max_rows // EXPERT_ROWS)
    n_slots = n_blocks * EXPERT_ROWS
    srt, gsrc, gslot = _sort(totals, cnt8, goff, route_p, h2_p, route_s, h2_s, n_slots)
    block_e, meta = _plan(totals, n_blocks)
    ys = _experts(block_e, meta, gsrc, srt, wg, wu, wd, n_slots)
    y_p = _combine(gslot, route_p, x1_p, gt_f_p, row(p['g_post_ffn']), ys, seq, 0)
    y_s = _combine(gslot, route_s, x1_s, gt_f_s, row(p['g_post_ffn']), ys, 0, tiles_p)

    k_new_p = klast_p.reshape(batch, WINDOW, N_KV_HEADS, HEAD_DIM)
    v_new_p = vlast_p.reshape(batch, WINDOW, N_KV_HEADS, HEAD_DIM)
    conv_p = xtail_p[:, -(CONV_WIDTH - 1):]
    h_p = hlast_p.reshape(batch, LRU_WIDTH)
    k_new_s = jnp.concatenate([ck, kn.reshape(nseq, t_new, N_KV_HEADS, HEAD_DIM)], axis=1)[:, -WINDOW:]
    v_new_s = jnp.concatenate([cv, vn.reshape(nseq, t_new, N_KV_HEADS, HEAD_DIM)], axis=1)[:, -WINDOW:]
    xl_s3 = xl_s.reshape(t_new, nseq, LRU_WIDTH).transpose(1, 0, 2).astype(F32)
    conv_s = jnp.concatenate([cconv, xl_s3], axis=1)[:, -(CONV_WIDTH - 1):]
    return (y_p.reshape(batch, seq, D_MODEL), y_s, k_new_p, v_new_p, conv_p, h_p,
            k_new_s, v_new_s, conv_s, hlast_s)


PAST_LEN_ = 16384

PARAM_NAMES = ('w_ada', 'b_ada', 'g_pre_mix', 'g_post_mix', 'g_pre_ffn', 'g_post_ffn', 'w_in', 'sinks',
               'w_conv', 'b_conv', 'w_lru_a', 'b_lru_a', 'w_lru_x', 'b_lru_x', 'lru_lambda',
               'w_attn_branch', 'w_lru_branch', 'w_out', 'w_router_group', 'b_router_group',
               'w_router_expert', 'b_router_expert', 'w_exp_gate', 'w_exp_up', 'w_exp_down')


def kernel(x_prompt, x_sample, cache_k_win, cache_v_win, state_conv, state_h, c_prompt, c_sample, w_ada, b_ada, g_pre_mix, g_post_mix, g_pre_ffn, g_post_ffn, w_in, sinks, w_conv, b_conv, w_lru_a, b_lru_a, w_lru_x, b_lru_x, lru_lambda, w_attn_branch, w_lru_branch, w_out, w_router_group, b_router_group, w_router_expert, b_router_expert, w_exp_gate, w_exp_up, w_exp_down):
    weights = (w_ada, b_ada, g_pre_mix, g_post_mix, g_pre_ffn, g_post_ffn, w_in, sinks,
               w_conv, b_conv, w_lru_a, b_lru_a, w_lru_x, b_lru_x, lru_lambda,
               w_attn_branch, w_lru_branch, w_out, w_router_group, b_router_group,
               w_router_expert, b_router_expert, w_exp_gate, w_exp_up, w_exp_down)
    depth = w_ada.shape[0]
    batch = x_prompt.shape[0]
    nseq, t_new, _ = x_sample.shape
    y_p = x_prompt
    y_s = x_sample.transpose(1, 0, 2).reshape(t_new * nseq, D_MODEL)
    c_all = jnp.concatenate([c_prompt, c_sample], axis=0)
    outs = [[] for _ in range(8)]
    for layer in range(depth):
        p = {name: w[layer] for name, w in zip(PARAM_NAMES, weights)}
        mod = _ada(c_all, p['w_ada'], p['b_ada'].reshape(1, -1))
        res = _layer_forward(y_p, y_s, cache_k_win[layer], cache_v_win[layer], state_conv[layer],
                             state_h[layer], mod[:batch], mod[batch:], p)
        y_p, y_s = res[0], res[1]
        for o, r in zip(outs, res[2:]):
            o.append(r)
    y_sample = y_s.reshape(t_new, nseq, D_MODEL).transpose(1, 0, 2)
    return (y_p, y_sample) + tuple(jnp.stack(o) for o in outs)
```

```python
import functools

import jax
import jax.numpy as jnp
from jax import lax
from jax.experimental import pallas as pl
from jax.experimental.pallas import tpu as pltpu

F32 = jnp.float32
BF16 = jnp.bfloat16
I32 = jnp.int32

D_MODEL = 1024
N_HEADS = 16
HEAD_DIM = 64
N_KV_HEADS = 4
GQA_GROUP = 4
WINDOW = 128
ROPE_THETA = 10000.0
NEG_INF = -1e30
LRU_WIDTH = 1024
LRU_BLOCKS = 16
LRU_BLOCK_W = 64
CONV_WIDTH = 4
LRU_C = 8.0
N_GROUPS = 4
EXPERTS_PER_GROUP = 8
N_EXPERTS = 32
D_EXPERT = 512
MOE_BLOCK = 128
NORM_EPS = 1e-6
Q_DIM = N_HEADS * HEAD_DIM
KV_DIM = N_KV_HEADS * HEAD_DIM
IN_DIM = Q_DIM + 2 * KV_DIM + 2 * LRU_WIDTH + 2 * D_MODEL

LANES = 128
SUBLANES = 8
MXU_DIM = 256
VMEM_LIMIT = 56 * 1024 * 1024
VMEM_LIMIT_BIG = 60 * 1024 * 1024

ROW_TILE = 512
ROUTE_LANES = LANES
GRANULE = SUBLANES
EXPERT_ROWS = 512
HALF_D = D_MODEL // 2
U32 = jnp.uint32


def _cparams(n_axes, vmem=VMEM_LIMIT, flags=None):
    return pltpu.CompilerParams(dimension_semantics=("arbitrary",) * n_axes, vmem_limit_bytes=vmem, flags=flags)


def _rms(x, g):
    ms = jnp.mean(x * x, axis=-1, keepdims=True)
    return x * lax.rsqrt(ms + NORM_EPS) * g


def _sigmoid(x):
    return 1.0 / (1.0 + jnp.exp(-x))


def _dot(a, b):
    return jnp.dot(a, b, preferred_element_type=F32)


def _ada_kernel(c_ref, w_ref, b_ref, o_ref):
    c = c_ref[...]
    s = (c * _sigmoid(c)).astype(BF16)
    o_ref[...] = _dot(s, w_ref[...].astype(BF16)) + b_ref[...]


def _ada(c_all, w_ada, b_ada):
    r = c_all.shape[0]
    n = w_ada.shape[1]
    return pl.pallas_call(
        _ada_kernel,
        grid=(n // D_MODEL,),
        in_specs=[pl.BlockSpec((r, D_MODEL), lambda j: (0, 0)),
                  pl.BlockSpec((D_MODEL, D_MODEL), lambda j: (0, j)),
                  pl.BlockSpec((1, D_MODEL), lambda j: (0, j))],
        out_specs=pl.BlockSpec((r, D_MODEL), lambda j: (0, j)),
        out_shape=jax.ShapeDtypeStruct((r, n), F32),
        compiler_params=_cparams(1),
        name="ada_mod",
    )(c_all, w_ada, b_ada)


_O1 = Q_DIM
_O2 = _O1 + KV_DIM
_O3 = _O2 + KV_DIM
_O4 = _O3 + LRU_WIDTH
_O5 = _O4 + LRU_WIDTH
_O6 = _O5 + D_MODEL


def _prenorm(x_ref, g_ref, sc_ref, sh_ref):
    h = _rms(x_ref[...], g_ref[...]) * (1.0 + sc_ref[...]) + sh_ref[...]
    return h.astype(BF16)


PIECE_COLS = 256


def _plain_pieces(h_ref, w_ref, base, width, store):
    def piece(c0):
        def run():
            store(c0, _dot(h_ref[...], w_ref[:, base + c0:base + c0 + PIECE_COLS]))
        return run
    return [piece(c0) for c0 in range(0, width, PIECE_COLS)]


def _qkv_gate_pieces(h_ref, w_ref, cos_ref, sin_ref, q_ref, k_ref, v_ref, sga_ref, sgr_ref, last_refs=None):
    def rope(t):
        cos = cos_ref[...]
        sin = sin_ref[...]
        lane = lax.broadcasted_iota(I32, cos.shape, 1)
        first_half = (lane % HEAD_DIM) < (HEAD_DIM // 2)
        rot = jnp.where(first_half, pltpu.roll(t, LANES - HEAD_DIM // 2, 1), pltpu.roll(t, HEAD_DIM // 2, 1))
        return t * cos + rot * sin

    def q_piece(c0):
        def run():
            qf = _dot(h_ref[...], w_ref[:, c0:c0 + PIECE_COLS])
            for c in range(PIECE_COLS // LANES):
                q_ref[:, c0 + c * LANES:c0 + (c + 1) * LANES] = rope(qf[:, c * LANES:(c + 1) * LANES]).astype(BF16)
        return run

    def kv_piece():
        kv = _dot(h_ref[...], w_ref[:, _O1:_O3])
        for c in range(KV_DIM // LANES):
            k_ref[:, c * LANES:(c + 1) * LANES] = rope(kv[:, c * LANES:(c + 1) * LANES])
        v_ref[...] = kv[:, KV_DIM:]
        if last_refs is not None:
            t = k_ref.shape[0]
            last_refs[0][...] = k_ref[t - WINDOW:t, :]
            last_refs[1][...] = v_ref[t - WINDOW:t, :]

    def gate_piece(o_ref, base, c0):
        def run():
            z = _dot(h_ref[...], w_ref[:, base + c0:base + c0 + PIECE_COLS])
            o_ref[:, c0:c0 + PIECE_COLS] = _sigmoid(z).astype(BF16)
        return run

    pieces = [q_piece(c0) for c0 in range(0, Q_DIM, PIECE_COLS)] + [kv_piece]
    pieces += [gate_piece(sga_ref, _O5, c0) for c0 in range(0, D_MODEL, PIECE_COLS)]
    pieces += [gate_piece(sgr_ref, _O6, c0) for c0 in range(0, D_MODEL, PIECE_COLS)]
    return pieces


def _inproj_kernel(x_ref, sc_ref, sh_ref, g_ref, cos_ref, sin_ref, w_ref,
                   q_ref, k_ref, v_ref, xl_ref, yl_ref, sga_ref, sgr_ref, hbuf):
    hbuf[...] = _prenorm(x_ref, g_ref, sc_ref, sh_ref)
    xl_ref[...] = _dot(hbuf[...], w_ref[:, _O3:_O4]).astype(BF16)
    yl_ref[...] = _dot(hbuf[...], w_ref[:, _O4:_O5]).astype(BF16)
    for piece in _qkv_gate_pieces(hbuf, w_ref, cos_ref, sin_ref, q_ref, k_ref, v_ref, sga_ref, sgr_ref):
        piece()


def _inproj(x, sc, sh, g, cos, sin, w_in, rows_per_mod, pos_tiles):
    n = x.shape[0]
    tm = min(ROW_TILE, n)
    if rows_per_mod:
        tiles_per_mod = rows_per_mod // tm
        mod_spec = pl.BlockSpec((None, 1, D_MODEL), lambda i: (i // tiles_per_mod, 0, 0))
    else:
        mod_spec = pl.BlockSpec((tm, D_MODEL), lambda i: (i, 0))
    row = lambda w: pl.BlockSpec((tm, w), lambda i: (i, 0))
    outs = [(Q_DIM, BF16), (KV_DIM, F32), (KV_DIM, F32), (LRU_WIDTH, BF16), (LRU_WIDTH, BF16),
            (D_MODEL, BF16), (D_MODEL, BF16)]
    return pl.pallas_call(
        _inproj_kernel,
        grid=(n // tm,),
        in_specs=[row(D_MODEL), mod_spec, mod_spec,
                  pl.BlockSpec((1, D_MODEL), lambda i: (0, 0)),
                  pl.BlockSpec((tm, LANES), lambda i: (i % pos_tiles, 0)),
                  pl.BlockSpec((tm, LANES), lambda i: (i % pos_tiles, 0)),
                  pl.BlockSpec((D_MODEL, IN_DIM), lambda i: (0, 0))],
        out_specs=[row(w) for w, _ in outs],
        out_shape=[jax.ShapeDtypeStruct((n, w), dt) for w, dt in outs],
        scratch_shapes=[pltpu.VMEM((tm, D_MODEL), BF16)],
        compiler_params=_cparams(1),
        name="in_proj",
    )(x, sc, sh, g, cos, sin, w_in)


def _head_masks(shape):
    lane = lax.broadcasted_iota(I32, shape, 1)
    return [(lane // HEAD_DIM) == h for h in range(N_KV_HEADS)]


def _attention_core(q_perm, kall, vall, valid, sink_of, rows):
    masks_b = _head_masks((rows, KV_DIM))
    zero_b = jnp.zeros((rows, KV_DIM), BF16)
    pieces = []
    for h in range(N_KV_HEADS):
        for g in range(GQA_GROUP):
            pieces.append(jnp.where(masks_b[h], q_perm[g], zero_b))
    q_big = jnp.concatenate(pieces, axis=0)
    s_all = lax.dot_general(q_big, kall, (((1,), (1,)), ((), ())), preferred_element_type=F32)
    s_all = s_all * (HEAD_DIM ** -0.5)
    p_chunks, inv_chunks = [], []
    for h in range(N_KV_HEADS):
        for g in range(GQA_GROUP):
            c = h * GQA_GROUP + g
            s = jnp.where(valid, s_all[c * rows:(c + 1) * rows], NEG_INF)
            sink = sink_of(h, g)
            m = jnp.maximum(jnp.max(s, axis=-1, keepdims=True), sink)
            p = jnp.exp(s - m)
            denom = jnp.sum(p, axis=-1, keepdims=True) + jnp.exp(sink - m)
            p_chunks.append(p.astype(BF16))
            inv_chunks.append(1.0 / denom)
    o_all = _dot(jnp.concatenate(p_chunks, axis=0), vall)
    outs = []
    for g in range(GQA_GROUP):
        acc = jnp.zeros((rows, KV_DIM), F32)
        for h in range(N_KV_HEADS):
            c = h * GQA_GROUP + g
            acc = acc + jnp.where(masks_b[h], o_all[c * rows:(c + 1) * rows] * inv_chunks[c], 0.0)
        outs.append(acc)
    return outs


ATTN_Q_BLOCKS = 4


def _attn_prompt_kernel(sink_ref, q_ref, kc_ref, kp_ref, vc_ref, vp_ref, o_ref):
    j = pl.program_id(1)
    kall = jnp.concatenate([kp_ref[...], kc_ref[...]], axis=0).astype(BF16)
    vall = jnp.concatenate([vp_ref[...], vc_ref[...]], axis=0).astype(BF16)
    qi = lax.broadcasted_iota(I32, (WINDOW, 2 * WINDOW), 0)
    kj = lax.broadcasted_iota(I32, (WINDOW, 2 * WINDOW), 1)
    dist = qi + WINDOW - kj
    in_window = (dist >= 0) & (dist <= WINDOW)
    for c in range(ATTN_Q_BLOCKS):
        rows = slice(c * WINDOW, (c + 1) * WINDOW)
        keys = slice(c * WINDOW, (c + 2) * WINDOW)
        valid = in_window & ((kj >= WINDOW) | (j > 0)) if c == 0 else in_window
        q_perm = [q_ref[rows, g * KV_DIM:(g + 1) * KV_DIM] for g in range(GQA_GROUP)]
        outs = _attention_core(q_perm, kall[keys], vall[keys], valid,
                               lambda h, g: sink_ref[h * GQA_GROUP + g], WINDOW)
        for g in range(GQA_GROUP):
            o_ref[rows, g * KV_DIM:(g + 1) * KV_DIM] = outs[g].astype(BF16)


def _attn_prompt(sinks, q, k, v, batch, seq):
    step_rows = ATTN_Q_BLOCKS * WINDOW
    nb = seq // step_rows
    cur = lambda w: pl.BlockSpec((step_rows, w), lambda b, j: (b * nb + j, 0))
    prev = lambda w: pl.BlockSpec(
        (WINDOW, w), lambda b, j: (jnp.maximum((b * nb + j) * ATTN_Q_BLOCKS - 1, 0), 0))
    return pl.pallas_call(
        _attn_prompt_kernel,
        grid=(batch, nb),
        in_specs=[pl.BlockSpec(memory_space=pltpu.SMEM),
                  cur(Q_DIM), cur(KV_DIM), prev(KV_DIM), cur(KV_DIM), prev(KV_DIM)],
        out_specs=cur(Q_DIM),
        out_shape=jax.ShapeDtypeStruct((batch * seq, Q_DIM), BF16),
        compiler_params=_cparams(2),
        name="attn_prompt",
    )(sinks, q, k, k, v, v)


SEQ_PER_STEP = 8


def _attn_sample_kernel(sink_ref, q_ref, kn_ref, vn_ref, kc_ref, vc_ref, o_ref, kbuf, vbuf, *, t_new):
    rows = GQA_GROUP * t_new
    kbuf[WINDOW:2 * WINDOW, :] = jnp.zeros((WINDOW, KV_DIM), F32)
    vbuf[WINDOW:2 * WINDOW, :] = jnp.zeros((WINDOW, KV_DIM), F32)
    ri = lax.broadcasted_iota(I32, (rows, 2 * WINDOW), 0)
    kj = lax.broadcasted_iota(I32, (rows, 2 * WINDOW), 1)
    tq = ri // GQA_GROUP
    valid = (kj >= tq) & (kj <= tq + WINDOW) & (kj < WINDOW + t_new)
    for s in range(SEQ_PER_STEP):
        kbuf[0:WINDOW, :] = kc_ref[s]
        vbuf[0:WINDOW, :] = vc_ref[s]
        kbuf[WINDOW:WINDOW + t_new, :] = kn_ref[s]
        vbuf[WINDOW:WINDOW + t_new, :] = vn_ref[s]
        kall = kbuf[...].astype(BF16)
        vall = vbuf[...].astype(BF16)
        qs = q_ref[s]
        masks_b = _head_masks((rows, KV_DIM))
        zero_b = jnp.zeros((rows, KV_DIM), BF16)
        q_big = jnp.concatenate([jnp.where(masks_b[h], qs, zero_b) for h in range(N_KV_HEADS)], axis=0)
        s_all = lax.dot_general(q_big, kall, (((1,), (1,)), ((), ())), preferred_element_type=F32)
        s_all = s_all * (HEAD_DIM ** -0.5)
        acc = jnp.zeros((rows, KV_DIM), F32)
        p_chunks, inv_chunks = [], []
        for h in range(N_KV_HEADS):
            sc = jnp.where(valid, s_all[h * rows:(h + 1) * rows], NEG_INF)
            sink = sink_ref[h]
            m = jnp.maximum(jnp.max(sc, axis=-1, keepdims=True), sink)
            p = jnp.exp(sc - m)
            denom = jnp.sum(p, axis=-1, keepdims=True) + jnp.exp(sink - m)
            p_chunks.append(p.astype(BF16))
            inv_chunks.append(1.0 / denom)
        o_all = _dot(jnp.concatenate(p_chunks, axis=0), vall)
        for h in range(N_KV_HEADS):
            acc = acc + jnp.where(masks_b[h], o_all[h * rows:(h + 1) * rows] * inv_chunks[h], 0.0)
        o_ref[s] = acc.astype(BF16)


def _attn_sample(sink_rows, q, kn, vn, kc, vc):
    nseq, rows, _ = q.shape
    t_new = kn.shape[1]
    sb = SEQ_PER_STEP
    blk = lambda r: pl.BlockSpec((sb, r, KV_DIM), lambda i: (i, 0, 0))
    return pl.pallas_call(
        functools.partial(_attn_sample_kernel, t_new=t_new),
        grid=(nseq // sb,),
        in_specs=[pl.BlockSpec((N_KV_HEADS, rows, 1), lambda i: (0, 0, 0)),
                  blk(rows), blk(t_new), blk(t_new), blk(WINDOW), blk(WINDOW)],
        out_specs=blk(rows),
        out_shape=jax.ShapeDtypeStruct((nseq, rows, KV_DIM), BF16),
        scratch_shapes=[pltpu.VMEM((2 * WINDOW, KV_DIM), F32), pltpu.VMEM((2 * WINDOW, KV_DIM), F32)],
        compiler_params=_cparams(1),
        name="attn_sample",
    )(sink_rows, q, kn, vn, kc, vc)


def _gelu_tanh(x):
    return 0.5 * x * (1.0 + jnp.tanh(0.7978845608028654 * (x + 0.044715 * x * x * x)))


def _lru_gates(xc, wbd_ref, ba, bx, lam):
    xcb = xc.astype(BF16)
    r_parts, i_parts = [], []
    for gidx in range(LRU_WIDTH // MXU_DIM):
        z = _dot(xcb[:, gidx * MXU_DIM:(gidx + 1) * MXU_DIM], wbd_ref[gidx])
        r_parts.append(z[:, :MXU_DIM])
        i_parts.append(z[:, MXU_DIM:])
    return _gate_math(jnp.concatenate(r_parts, axis=1), jnp.concatenate(i_parts, axis=1), xc, ba, bx, lam)


def _gate_math(zr, zi, xc, ba, bx, lam):
    r = _sigmoid(zr + ba)
    i = _sigmoid(zi + bx)
    softplus_neg_lam = jnp.maximum(-lam, 0.0) + jnp.log1p(jnp.exp(-jnp.abs(lam)))
    log_a = -LRU_C * r * softplus_neg_lam
    a = jnp.exp(log_a)
    y = 1.0 - a * a
    u = jnp.where(y > 0.0, y * lax.rsqrt(y), 0.0) * (i * xc)
    return a, u


LRU_CHUNK = 64


def _lru_chunk(c, xbuf, ybuf, hcar, ol_ref, wc_ref, bc_ref, wbd_ref, ba_ref, bx_ref, lam_ref):
    w = LRU_WIDTH
    n = LRU_CHUNK
    rows = slice(c * n, (c + 1) * n)
    r0 = SUBLANES + c * n
    xc = xbuf[r0:r0 + n, :] * wc_ref[CONV_WIDTH - 1:CONV_WIDTH, :] + bc_ref[...]
    for k in range(1, CONV_WIDTH):
        xc = xc + xbuf[r0 - k:r0 - k + n, :] * wc_ref[CONV_WIDTH - 1 - k:CONV_WIDTH - k, :]
    a, u = _lru_gates(xc, wbd_ref, ba_ref[...], bx_ref[...], lam_ref[...])

    ng = n // SUBLANES
    a3 = a.reshape(ng, SUBLANES, w)
    u3 = u.reshape(ng, SUBLANES, w)
    row = lax.broadcasted_iota(I32, (ng, SUBLANES, w), 1)
    d = 1
    while d < SUBLANES:
        a_s = jnp.where(row >= d, pltpu.roll(a3, d, 1), 1.0)
        u_s = jnp.where(row >= d, pltpu.roll(u3, d, 1), 0.0)
        u3 = a3 * u_s + u3
        a3 = a3 * a_s
        d *= 2
    carry = hcar[...]
    hs = []
    for gi in range(ng):
        hg = a3[gi] * carry + u3[gi]
        hs.append(hg)
        carry = hg[SUBLANES - 1:SUBLANES, :]
    hcar[...] = carry
    h = jnp.concatenate(hs, axis=0)
    ol_ref[rows, :] = (h * _gelu_tanh(ybuf[rows, :])).astype(BF16)


def _inproj_lru_kernel(x_ref, sc_ref, sh_ref, g_ref, cos_ref, sin_ref, w_ref,
                       wc_ref, bc_ref, wbd_ref, ba_ref, bx_ref, lam_ref,
                       q_ref, k_ref, v_ref, sga_ref, sgr_ref, klast_ref, vlast_ref, ol_ref, xtail_ref, hlast_ref,
                       hbuf, xbuf, ybuf, hcar, *, tiles_per_seq):
    t = x_ref.shape[0]

    @pl.when(pl.program_id(0) % tiles_per_seq == 0)
    def _():
        xbuf[0:SUBLANES, :] = jnp.zeros((SUBLANES, LRU_WIDTH), F32)
        hcar[...] = jnp.zeros((1, LRU_WIDTH), F32)

    hbuf[...] = _prenorm(x_ref, g_ref, sc_ref, sh_ref)
    xbuf[SUBLANES:SUBLANES + t, :] = _dot(hbuf[...], w_ref[:, _O3:_O4])
    ybuf[...] = _dot(hbuf[...], w_ref[:, _O4:_O5])

    pieces = _qkv_gate_pieces(hbuf, w_ref, cos_ref, sin_ref, q_ref, k_ref, v_ref, sga_ref, sgr_ref,
                              last_refs=(klast_ref, vlast_ref))
    n_chunks = t // LRU_CHUNK
    per_chunk = -(-len(pieces) // n_chunks)
    for c in range(n_chunks):
        _lru_chunk(c, xbuf, ybuf, hcar, ol_ref, wc_ref, bc_ref, wbd_ref, ba_ref, bx_ref, lam_ref)
        for piece in pieces[c * per_chunk:(c + 1) * per_chunk]:
            piece()

    tail = xbuf[t:t + SUBLANES, :]
    xtail_ref[...] = tail
    xbuf[0:SUBLANES, :] = tail
    hlast_ref[...] = hcar[...]


def _inproj_lru(x, sc, sh, g, cos, sin, w_in, wc, bc, wbd, ba, bx, lam, batch, seq):
    n = x.shape[0]
    tm = min(ROW_TILE, seq)
    tps = seq // tm
    mod_spec = pl.BlockSpec((None, 1, D_MODEL), lambda i: (i // tps, 0, 0))
    row = lambda w: pl.BlockSpec((tm, w), lambda i: (i, 0))
    full = lambda a: pl.BlockSpec(a.shape, lambda i: (0,) * a.ndim)
    per_seq = lambda r, w: pl.BlockSpec((None, r, w), lambda i: (i // tps, 0, 0))
    outs = [(Q_DIM, BF16), (KV_DIM, F32), (KV_DIM, F32), (D_MODEL, BF16), (D_MODEL, BF16)]
    return pl.pallas_call(
        functools.partial(_inproj_lru_kernel, tiles_per_seq=tps),
        grid=(n // tm,),
        in_specs=[row(D_MODEL), mod_spec, mod_spec, full(g),
                  pl.BlockSpec((tm, LANES), lambda i: (i % tps, 0)),
                  pl.BlockSpec((tm, LANES), lambda i: (i % tps, 0)),
                  full(w_in), full(wc), full(bc), full(wbd), full(ba), full(bx), full(lam)],
        out_specs=[row(w) for w, _ in outs] + [per_seq(WINDOW, KV_DIM), per_seq(WINDOW, KV_DIM)]
        + [row(LRU_WIDTH), per_seq(SUBLANES, LRU_WIDTH), per_seq(1, LRU_WIDTH)],
        out_shape=[jax.ShapeDtypeStruct((n, w), dt) for w, dt in outs]
        + [jax.ShapeDtypeStruct((batch, WINDOW, KV_DIM), F32)] * 2
        + [jax.ShapeDtypeStruct((n, LRU_WIDTH), BF16),
           jax.ShapeDtypeStruct((batch, SUBLANES, LRU_WIDTH), F32),
           jax.ShapeDtypeStruct((batch, 1, LRU_WIDTH), F32)],
        scratch_shapes=[pltpu.VMEM((tm, D_MODEL), BF16),
                        pltpu.VMEM((2 * SUBLANES + tm, LRU_WIDTH), F32), pltpu.VMEM((tm, LRU_WIDTH), F32),
                        pltpu.VMEM((1, LRU_WIDTH), F32)],
        compiler_params=_cparams(1, vmem=VMEM_LIMIT_BIG),
        name="in_proj_lru",
    )(x, sc, sh, g, cos, sin, w_in, wc, bc, wbd, ba, bx, lam)


def _lru_sample_kernel(xl_ref, yl_ref, cs_ref, h0_ref, wc_ref, bc_ref, wbd_ref, ba_ref, bx_ref, lam_ref,
                       o_ref, hlast_ref):
    t_new, nseq, w = xl_ref.shape
    xp = [cs_ref[k] for k in range(CONV_WIDTH - 1)] + [xl_ref[k].astype(F32) for k in range(t_new)]
    xcs = []
    for t in range(t_new):
        acc = bc_ref[...] + xp[t] * wc_ref[0:1, :]
        for k in range(1, CONV_WIDTH):
            acc = acc + xp[t + k] * wc_ref[k:k + 1, :]
        xcs.append(acc)
    xc = jnp.concatenate(xcs, axis=0)
    a, u = _lru_gates(xc, wbd_ref, ba_ref[...], bx_ref[...], lam_ref[...])
    h = h0_ref[...]
    for t in range(t_new):
        h = a[t * nseq:(t + 1) * nseq] * h + u[t * nseq:(t + 1) * nseq]
        o_ref[t] = (h * _gelu_tanh(yl_ref[t].astype(F32))).astype(BF16)
    hlast_ref[...] = h


def _lru_sample(xl, yl, cs, h0, wc, bc, wbd, ba, bx, lam):
    t_new, nseq, w = xl.shape
    full = lambda shp: pl.BlockSpec(shp, lambda i: (0,) * len(shp))
    args = (xl, yl, cs, h0, wc, bc, wbd, ba, bx, lam)
    return pl.pallas_call(
        _lru_sample_kernel,
        grid=(1,),
        in_specs=[full(a.shape) for a in args],
        out_specs=[full((t_new, nseq, w)), full((nseq, w))],
        out_shape=[jax.ShapeDtypeStruct((t_new, nseq, w), BF16), jax.ShapeDtypeStruct((nseq, w), F32)],
        compiler_params=_cparams(1),
        name="lru_sample",
    )(*args)


def _post_kernel(oa_ref, ol_ref, sga_ref, sgr_ref, x_ref, gta_ref, scf_ref, shf_ref,
                 wab_ref, wlb_ref, wout_ref, gpm_ref, gpf_ref, wrp_ref, wrt_ref, br_ref, tri_ref, cin_ref,
                 x1_ref, h2_ref, route_ref, stat_ref, cnt_ref, carry):
    i = pl.program_id(0)

    @pl.when(i == 0)
    def _():
        carry[...] = cin_ref[...]

    b_attn = _dot(oa_ref[...], wab_ref[...])
    b_lru = _dot(ol_ref[...], wlb_ref[...])
    merged = sga_ref[...].astype(F32) * b_attn + sgr_ref[...].astype(F32) * b_lru
    mix = _dot(merged.astype(BF16), wout_ref[...])
    x1 = x_ref[...] + gta_ref[...] * _rms(mix, gpm_ref[...])
    x1_ref[...] = x1
    h2 = _rms(x1, gpf_ref[...]) * (1.0 + scf_ref[...]) + shf_ref[...]
    h2_ref[...] = h2.astype(BF16)

    h_hi = h2.astype(BF16)
    h_lo = (h2 - h_hi.astype(F32)).astype(BF16)
    hi_terms = _dot(h_hi, wrp_ref[...])
    logits = (hi_terms[:, :ROUTE_LANES] + (hi_terms[:, ROUTE_LANES:] + _dot(h_lo, wrt_ref[...]))) + br_ref[...]

    tm = logits.shape[0]
    lane = lax.broadcasted_iota(I32, (tm, ROUTE_LANES), 1)
    big = jnp.int32(ROUTE_LANES)
    is_g = lane < N_GROUPS
    lg = jnp.where(is_g, logits, NEG_INF)
    mg = jnp.max(lg, axis=-1, keepdims=True)
    g_star = jnp.min(jnp.where(lg == mg, lane, big), axis=-1, keepdims=True)
    p_star = 1.0 / jnp.sum(jnp.where(is_g, jnp.exp(lg - mg), 0.0), axis=-1, keepdims=True)
    lo = N_GROUPS + g_star * EXPERTS_PER_GROUP
    in_grp = (lane >= lo) & (lane < lo + EXPERTS_PER_GROUP)
    le = jnp.where(in_grp, logits, NEG_INF)
    m1 = jnp.max(le, axis=-1, keepdims=True)
    i1 = jnp.min(jnp.where(le == m1, lane, big), axis=-1, keepdims=True)
    le2 = jnp.where(lane == i1, NEG_INF, le)
    m2 = jnp.max(le2, axis=-1, keepdims=True)
    i2 = jnp.min(jnp.where(le2 == m2, lane, big), axis=-1, keepdims=True)
    e2x = jnp.exp(m2 - m1)
    wsum = 1.0 + e2x
    w1 = (1.0 / wsum) * p_star
    w2 = (e2x / wsum) * p_star

    oh1 = lane == i1
    oh2 = lane == i2
    cnt = jnp.where(oh1 | oh2, 1.0, 0.0)
    excl = _dot(tri_ref[...], cnt.astype(BF16))
    per_e = jnp.sum(cnt, axis=0, keepdims=True)
    pad8 = jnp.floor((per_e + (GRANULE - 1.0)) * (1.0 / GRANULE)) * GRANULE
    incl = jnp.broadcast_to(pad8, (SUBLANES, ROUTE_LANES))
    lane8 = lax.broadcasted_iota(I32, (SUBLANES, ROUTE_LANES), 1)
    d = 1
    while d < ROUTE_LANES:
        incl = incl + jnp.where(lane8 >= d, pltpu.roll(incl, d, 1), 0.0)
        d *= 2
    seg_start = incl[0:1, :] - pad8
    pos = excl + seg_start
    s1 = jnp.sum(jnp.where(oh1, pos, 0.0), axis=-1, keepdims=True)
    s2 = jnp.sum(jnp.where(oh2, pos, 0.0), axis=-1, keepdims=True)
    rec = jnp.where(lane == 0, s1, 0.0)
    rec = jnp.where(lane == 1, s2, rec)
    rec = jnp.where(lane == 2, w1, rec)
    rec = jnp.where(lane == 3, w2, rec)
    route_ref[...] = rec
    srow = lax.broadcasted_iota(I32, (SUBLANES, ROUTE_LANES), 0)
    stat_ref[...] = jnp.where(srow == 0, pad8, jnp.where(srow == 1, carry[...], 0.0))
    carry[...] = carry[...] + pad8
    cnt_ref[...] = carry[...]


def _post(oa, ol, sga, sgr, x, gta, scf, shf, wab, wlb, wout, gpm, gpf, wrh, wrl, br, tri, cin, rows_per_mod):
    n = x.shape[0]
    tm = min(ROW_TILE, n)
    if rows_per_mod:
        tiles_per_mod = rows_per_mod // tm
        mod_spec = pl.BlockSpec((None, 1, D_MODEL), lambda i: (i // tiles_per_mod, 0, 0))
    else:
        mod_spec = pl.BlockSpec((tm, D_MODEL), lambda i: (i, 0))
    row = lambda w: pl.BlockSpec((tm, w), lambda i: (i, 0))
    full = lambda a: pl.BlockSpec(a.shape, lambda i: (0,) * a.ndim)
    return pl.pallas_call(
        _post_kernel,
        grid=(n // tm,),
        in_specs=[row(Q_DIM), row(LRU_WIDTH), row(D_MODEL), row(D_MODEL), row(D_MODEL),
                  mod_spec, mod_spec, mod_spec,
                  full(wab), full(wlb), full(wout), full(gpm), full(gpf), full(wrh), full(wrl), full(br),
                  full(tri), full(cin)],
        out_specs=[row(D_MODEL), row(D_MODEL), row(ROUTE_LANES),
                   pl.BlockSpec((SUBLANES, ROUTE_LANES), lambda i: (i, 0)),
                   pl.BlockSpec((1, ROUTE_LANES), lambda i: (0, 0))],
        out_shape=[jax.ShapeDtypeStruct((n, D_MODEL), F32), jax.ShapeDtypeStruct((n, D_MODEL), BF16),
                   jax.ShapeDtypeStruct((n, ROUTE_LANES), F32),
                   jax.ShapeDtypeStruct((n // tm * SUBLANES, ROUTE_LANES), F32),
                   jax.ShapeDtypeStruct((1, ROUTE_LANES), F32)],
        scratch_shapes=[pltpu.VMEM((1, ROUTE_LANES), F32)],
        compiler_params=_cparams(1),
        name="post_mix",
    )(oa, ol, sga, sgr, x, gta, scf, shf, wab, wlb, wout, gpm, gpf, wrh, wrl, br, tri, cin)


BLOCK_GRANULES = EXPERT_ROWS // GRANULE
BLOCK_SHIFT = BLOCK_GRANULES.bit_length() - 1
assert BLOCK_GRANULES == 1 << BLOCK_SHIFT


def _padded(c):
    return ((c + (BLOCK_GRANULES - 1)) >> BLOCK_SHIFT) << BLOCK_SHIFT


def _sorted_rows(tm):
    r = 2 * tm + N_EXPERTS * (GRANULE - 1)
    return -(-r // MXU_DIM) * MXU_DIM


def _plan_kernel(tot_ref, be_ref, meta_ref, *, n_blocks):
    def fill(j, _):
        be_ref[j] = N_EXPERTS - 1
        return 0
    lax.fori_loop(0, n_blocks, fill, 0)

    def per_expert(e, nb):
        k = _padded(tot_ref[e]) >> BLOCK_SHIFT

        def put(b, _):
            be_ref[nb + b] = e
            return 0
        lax.fori_loop(0, k, put, 0)
        return nb + k
    n_active = lax.fori_loop(0, N_EXPERTS, per_expert, 0)
    meta_ref[0] = n_active


def _plan(totals, n_blocks):
    return pl.pallas_call(
        functools.partial(_plan_kernel, n_blocks=n_blocks),
        in_specs=[pl.BlockSpec(memory_space=pltpu.SMEM)],
        out_specs=[pl.BlockSpec(memory_space=pltpu.SMEM), pl.BlockSpec(memory_space=pltpu.SMEM)],
        out_shape=[jax.ShapeDtypeStruct((n_blocks,), I32), jax.ShapeDtypeStruct((1,), I32)],
        name="moe_plan",
    )(totals)


def _expert_starts(tot_ref, pstart):
    def body(e, acc):
        pstart[e] = acc
        return acc + _padded(tot_ref[e])
    return lax.fori_loop(0, N_EXPERTS, body, 0)


def _granule(ref, g):
    return ref.at[pl.ds(pl.multiple_of(g * GRANULE, GRANULE), GRANULE)]


def _pack_halves(lo_f32, hi_f32):
    return (pltpu.bitcast(lo_f32, U32) >> 16) | (pltpu.bitcast(hi_f32, U32) & jnp.uint32(0xFFFF0000))


def _unpack_halves(packed):
    lo = pltpu.bitcast(packed << 16, F32).astype(BF16)
    hi = pltpu.bitcast(packed & jnp.uint32(0xFFFF0000), F32).astype(BF16)
    return lo, hi


def _sort_kernel(tot_ref, cnt_ref, goff_ref, rec_p_ref, h2_p_ref, rec_s_ref, h2_s_ref,
                 srt_ref, gsrc_ref, gslot_ref, pstart, *, tiles_p, n_sorted, n_slots):
    i = pl.program_id(0)
    from_sample = i >= tiles_p
    rec = jnp.where(from_sample, rec_s_ref[...], rec_p_ref[...])
    h2 = jnp.where(from_sample, h2_s_ref[...], h2_p_ref[...])
    tm = h2.shape[0]

    rec_t = rec.T
    s1 = rec_t[0:1, :].astype(I32)
    s2 = rec_t[1:2, :].astype(I32)
    rows = lax.broadcasted_iota(I32, (n_sorted, tm), 0)
    sel = jnp.where((rows == s1) | (rows == s2), 1.0, 0.0).astype(BF16)
    srt = _dot(sel, h2)
    srt_ref[...] = _pack_halves(srt[:, :HALF_D], srt[:, HALF_D:])

    zero_granule = n_sorted // GRANULE - 1

    @pl.when(i == 0)
    def _():
        used = _expert_starts(tot_ref, pstart)

        def put_zero(g, _):
            gsrc_ref[g] = zero_granule
            return 0

        def per_expert(e, _):
            total = tot_ref[e]
            lax.fori_loop(pstart[e] + total, pstart[e] + _padded(total), put_zero, 0)
            return 0
        lax.fori_loop(0, N_EXPERTS, per_expert, 0)
        lax.fori_loop(used, n_slots // GRANULE, put_zero, 0)

    tile_granules = n_sorted // GRANULE

    def per_expert(e, seg):
        k = cnt_ref[i * N_EXPERTS + e]
        dst = pstart[e] + goff_ref[i * N_EXPERTS + e]
        src = i * tile_granules + seg

        def put(g, _):
            gsrc_ref[dst + g] = src + g
            gslot_ref[src + g] = dst + g
            return 0
        lax.fori_loop(0, k, put, 0)
        return seg + k
    used_in_tile = lax.fori_loop(0, N_EXPERTS, per_expert, 0, unroll=4)

    def put_any(q, _):
        gslot_ref[i * tile_granules + q] = 0
        return 0
    lax.fori_loop(used_in_tile, tile_granules, put_any, 0)


def _sort(totals, cnt8, goff, rec_p, h2_p, rec_s, h2_s, n_slots):
    tm = min(ROW_TILE, h2_p.shape[0])
    assert h2_s.shape[0] % tm == 0
    tiles_p = h2_p.shape[0] // tm
    tiles_s = h2_s.shape[0] // tm
    n_sorted = _sorted_rows(tm)
    row_p = lambda w: pl.BlockSpec((tm, w), lambda i, *_: (jnp.minimum(i, tiles_p - 1), 0))
    row_s = lambda w: pl.BlockSpec((tm, w), lambda i, *_: (jnp.maximum(i - tiles_p, 0), 0))
    return pl.pallas_call(
        functools.partial(_sort_kernel, tiles_p=tiles_p, n_sorted=n_sorted, n_slots=n_slots),
        grid_spec=pltpu.PrefetchScalarGridSpec(
            num_scalar_prefetch=3,
            grid=(tiles_p + tiles_s,),
            in_specs=[row_p(ROUTE_LANES), row_p(D_MODEL), row_s(ROUTE_LANES), row_s(D_MODEL)],
            out_specs=[pl.BlockSpec((n_sorted, HALF_D), lambda i, *_: (i, 0)),
                       pl.BlockSpec(memory_space=pltpu.SMEM), pl.BlockSpec(memory_space=pltpu.SMEM)],
            scratch_shapes=[pltpu.SMEM((N_EXPERTS,), I32)]),
        out_shape=[jax.ShapeDtypeStruct(((tiles_p + tiles_s) * n_sorted, HALF_D), U32),
                   jax.ShapeDtypeStruct((n_slots // GRANULE,), I32),
                   jax.ShapeDtypeStruct(((tiles_p + tiles_s) * n_sorted // GRANULE,), I32)],
        compiler_params=_cparams(1),
        name="moe_sort",
    )(totals, cnt8, goff, rec_p, h2_p, rec_s, h2_s)


def _expert_kernel(be_ref, meta_ref, gsrc_ref, srt_hbm, wg_hbm, wu_hbm, wd_hbm, ys_ref,
                   xbuf, wgs, wus, wds, wgb, wub, wdb, stage, sems, wsems):
    j = pl.program_id(0)
    n_active = meta_ref[0]
    gran_per_block = BLOCK_GRANULES

    def granule_copy(blk, g, slot):
        return pltpu.make_async_copy(_granule(srt_hbm, gsrc_ref[blk * gran_per_block + g]),
                                     xbuf.at[slot, pl.ds(g * GRANULE, GRANULE)], sems.at[slot])

    def gather(blk, slot):
        for g in range(gran_per_block):
            granule_copy(blk, g, slot).start(priority=g % 2)

    def drain(slot):
        for g in range(gran_per_block):
            granule_copy(0, g, slot).wait()

    def weight_copies(e, p):
        return [pltpu.make_async_copy(wg_hbm.at[e], wgs.at[p], wsems.at[p]),
                pltpu.make_async_copy(wu_hbm.at[e], wus.at[p], wsems.at[p]),
                pltpu.make_async_copy(wd_hbm.at[e], wds.at[p], wsems.at[p])]

    @pl.when(j == 0)
    def _():
        gather(0, 0)
        stage[0] = 0
        for cp in weight_copies(be_ref[0], 0):
            cp.start()

    @pl.when(j < n_active)
    def _():
        slot = j % 2
        e = be_ref[j]

        @pl.when((j == 0) | (e != be_ref[jnp.maximum(j - 1, 0)]))
        def _():
            p = stage[0]
            for cp in weight_copies(e, p):
                cp.wait()
            wgb[...] = wgs[p].astype(BF16)
            wub[...] = wus[p].astype(BF16)
            wdb[...] = wds[p].astype(BF16)
            nxt = lax.while_loop(lambda k: (k < n_active) & (be_ref[jnp.minimum(k, n_active - 1)] == e),
                                 lambda k: k + 1, j + 1)

            @pl.when(nxt < n_active)
            def _():
                for cp in weight_copies(be_ref[nxt], 1 - p):
                    cp.start()
            stage[0] = 1 - p

        gather(jnp.minimum(j + 1, n_active - 1), 1 - slot)
        drain(slot)
        x_lo, x_hi = _unpack_halves(xbuf[slot])
        g = _dot(x_lo, wgb[0:HALF_D, :]) + _dot(x_hi, wgb[HALF_D:D_MODEL, :])
        u = _dot(x_lo, wub[0:HALF_D, :]) + _dot(x_hi, wub[HALF_D:D_MODEL, :])
        hmid = (g * _sigmoid(g) * u).astype(BF16)
        y = _dot(hmid, wdb[...])
        ys_ref[...] = _pack_halves(y[:, :HALF_D].astype(BF16).astype(F32), y[:, HALF_D:].astype(BF16).astype(F32))

        @pl.when(j == n_active - 1)
        def _():
            drain(1 - slot)

    @pl.when(j >= meta_ref[0])
    def _():
        ys_ref[...] = jnp.zeros(ys_ref.shape, U32)


def _experts(block_e, meta, gsrc, srt, wg, wu, wd, n_slots):
    n_blocks = n_slots // EXPERT_ROWS
    anyspec = pl.BlockSpec(memory_space=pl.ANY)
    return pl.pallas_call(
        _expert_kernel,
        grid_spec=pltpu.PrefetchScalarGridSpec(
            num_scalar_prefetch=3,
            grid=(n_blocks,),
            in_specs=[anyspec, anyspec, anyspec, anyspec],
            out_specs=pl.BlockSpec((EXPERT_ROWS, HALF_D), lambda j, be, meta, gs: (j, 0)),
            scratch_shapes=[pltpu.VMEM((2, EXPERT_ROWS, HALF_D), U32),
                            pltpu.VMEM((2, D_MODEL, D_EXPERT), F32), pltpu.VMEM((2, D_MODEL, D_EXPERT), F32),
                            pltpu.VMEM((2, D_EXPERT, D_MODEL), F32),
                            pltpu.VMEM((D_MODEL, D_EXPERT), BF16), pltpu.VMEM((D_MODEL, D_EXPERT), BF16),
                            pltpu.VMEM((D_EXPERT, D_MODEL), BF16),
                            pltpu.SMEM((1,), I32), pltpu.SemaphoreType.DMA((2,)), pltpu.SemaphoreType.DMA((2,))]),
        out_shape=jax.ShapeDtypeStruct((n_slots, HALF_D), U32),
        compiler_params=_cparams(1),
        name="moe_experts",
    )(block_e, meta, gsrc, srt, wg, wu, wd)


def _combine_kernel(gslot_ref, rec_ref, x1_ref, gtf_ref, g_ref, ys_hbm, y_ref, cbuf, sems, *, tile_base, n_tiles):
    i = pl.program_id(0)
    slot = i % 2
    tm = x1_ref.shape[0]
    n_sorted = cbuf.shape[1]
    tile_granules = n_sorted // GRANULE

    def granule_copy(slot_granule, q, slot_):
        return pltpu.make_async_copy(_granule(ys_hbm, slot_granule),
                                     cbuf.at[slot_, pl.ds(q * GRANULE, GRANULE)], sems.at[slot_])

    def gather(tile, slot_):
        for q in range(tile_granules):
            granule_copy(gslot_ref[tile * tile_granules + q], q, slot_).start(priority=q % 2)

    def drain(slot_):
        for q in range(tile_granules):
            granule_copy(0, q, slot_).wait()

    @pl.when(i == 0)
    def _():
        gather(tile_base, 0)

    gather(tile_base + jnp.minimum(i + 1, n_tiles - 1), 1 - slot)
    drain(slot)

    rec = rec_ref[...]
    s1 = rec[:, 0:1].astype(I32)
    s2 = rec[:, 1:2].astype(I32)
    col = lax.broadcasted_iota(I32, (tm, n_sorted), 1)
    wmat = (jnp.where(col == s1, rec[:, 2:3], 0.0) + jnp.where(col == s2, rec[:, 3:4], 0.0)).astype(BF16)
    y_lo, y_hi = _unpack_halves(cbuf[slot])
    f = jnp.concatenate([_dot(wmat, y_lo), _dot(wmat, y_hi)], axis=1)
    y_ref[...] = x1_ref[...] + gtf_ref[...] * _rms(f, g_ref[...])

    @pl.when(i == n_tiles - 1)
    def _():
        drain(1 - slot)


def _combine(gslot, rec, x1, gtf, g, ys, rows_per_mod, tile_base):
    n = x1.shape[0]
    tm = min(ROW_TILE, n)
    if rows_per_mod:
        tiles_per_mod = rows_per_mod // tm
        mod_spec = pl.BlockSpec((None, 1, D_MODEL), lambda i, *_: (i // tiles_per_mod, 0, 0))
    else:
        mod_spec = pl.BlockSpec((tm, D_MODEL), lambda i, *_: (i, 0))
    row = lambda w: pl.BlockSpec((tm, w), lambda i, *_: (i, 0))
    return pl.pallas_call(
        functools.partial(_combine_kernel, tile_base=tile_base, n_tiles=n // tm),
        grid_spec=pltpu.PrefetchScalarGridSpec(
            num_scalar_prefetch=1,
            grid=(n // tm,),
            in_specs=[row(ROUTE_LANES), row(D_MODEL), mod_spec,
                      pl.BlockSpec((1, D_MODEL), lambda i, *_: (0, 0)),
                      pl.BlockSpec(memory_space=pl.ANY)],
            out_specs=row(D_MODEL),
            scratch_shapes=[pltpu.VMEM((2, _sorted_rows(tm), HALF_D), U32), pltpu.SemaphoreType.DMA((2,))]),
        out_shape=jax.ShapeDtypeStruct((n, D_MODEL), F32),
        compiler_params=_cparams(1),
        name="moe_combine",
    )(gslot, rec, x1, gtf, g, ys)


def _rope_tables(pos):
    half = HEAD_DIM // 2
    inv = jnp.power(jnp.float32(ROPE_THETA), -jnp.arange(half, dtype=F32) / half)
    ang = pos.astype(F32)[:, None] * inv[None, :]
    cos = jnp.cos(ang)
    sin = jnp.sin(ang)
    reps = LANES // HEAD_DIM
    cos_t = jnp.tile(jnp.concatenate([cos, cos], axis=-1), (1, reps))
    sin_t = jnp.tile(jnp.concatenate([-sin, sin], axis=-1), (1, reps))
    return cos_t, sin_t


def _q_perm_index():
    g = jnp.arange(GQA_GROUP)[:, None, None]
    h = jnp.arange(N_KV_HEADS)[None, :, None]
    d = jnp.arange(HEAD_DIM)[None, None, :]
    return ((h * GQA_GROUP + g) * HEAD_DIM + d).reshape(-1)


def _block_diag_gates(w_a, w_x):
    per = MXU_DIM // LRU_BLOCK_W
    groups = LRU_BLOCKS // per

    def bd(w):
        w = w.reshape(groups, per, LRU_BLOCK_W, LRU_BLOCK_W)
        eye = jnp.eye(per, dtype=w.dtype)
        full = jnp.einsum('gpij,pq->gpiqj', w, eye)
        return full.reshape(groups, MXU_DIM, MXU_DIM)
    return jnp.concatenate([bd(w_a), bd(w_x)], axis=-1).astype(BF16)


def _layer_forward(xp, xs_tm, ck, cv, cconv, ch, mod_p, mod_s, p):
    batch, seq, _ = xp.shape
    nseq, _, _, _ = ck.shape
    t_new = xs_tm.shape[0] // nseq
    n_p = batch * seq
    n_s = xs_tm.shape[0]

    perm = _q_perm_index()
    w_in = p['w_in']
    w_in_b = jnp.concatenate([w_in[:, :Q_DIM][:, perm], w_in[:, Q_DIM:]], axis=1).astype(BF16)
    wab = p['w_attn_branch'][perm, :].astype(BF16)
    wlb = p['w_lru_branch'].astype(BF16)
    wout = p['w_out'].astype(BF16)
    wbd = _block_diag_gates(p['w_lru_a'], p['w_lru_x'])
    row = lambda v: v.reshape(1, -1)
    wr = jnp.concatenate([p['w_router_group'], p['w_router_expert'],
                          jnp.zeros((D_MODEL, ROUTE_LANES - N_GROUPS - N_EXPERTS), F32)], axis=1)
    wr_top = wr.astype(BF16)
    wr_pair = jnp.concatenate([wr_top, (wr - wr_top.astype(F32)).astype(BF16)], axis=1)
    br = jnp.concatenate([p['b_router_group'], p['b_router_expert'],
                          jnp.zeros((ROUTE_LANES - N_GROUPS - N_EXPERTS,), F32)]).reshape(1, -1)
    wg = p['w_exp_gate']
    wu = p['w_exp_up']
    wd = p['w_exp_down']

    def mods(mod):
        return [mod[:, k * D_MODEL:(k + 1) * D_MODEL] for k in range(6)]
    sh_a_p, sc_a_p, gt_a_p, sh_f_p, sc_f_p, gt_f_p = [m.reshape(batch, 1, D_MODEL) for m in mods(mod_p)]
    sh_a_s, sc_a_s, gt_a_s, sh_f_s, sc_f_s, gt_f_s = [jnp.tile(m, (t_new, 1)) for m in mods(mod_s)]

    lru_w = (p['w_conv'], row(p['b_conv']), wbd, row(p['b_lru_a']), row(p['b_lru_x']), row(p['lru_lambda']))
    cos_p, sin_p = _rope_tables(jnp.arange(seq, dtype=I32))
    q_p, k_p, v_p, sga_p, sgr_p, klast_p, vlast_p, ol_p, xtail_p, hlast_p = _inproj_lru(
        xp.reshape(n_p, D_MODEL), sc_a_p, sh_a_p, row(p['g_pre_mix']), cos_p, sin_p, w_in_b, *lru_w, batch, seq)
    pos_s = jnp.repeat(PAST_LEN_ + jnp.arange(t_new, dtype=I32), nseq)
    cos_s, sin_s = _rope_tables(pos_s)
    q_s, k_s, v_s, xl_s, yl_s, sga_s, sgr_s = _inproj(
        xs_tm, sc_a_s, sh_a_s, row(p['g_pre_mix']), cos_s, sin_s, w_in_b, rows_per_mod=0,
        pos_tiles=n_s // min(ROW_TILE, n_s))

    sinks_perm = p['sinks']
    oa_p = _attn_prompt(sinks_perm, q_p, k_p, v_p, batch, seq)
    rows = t_new * GQA_GROUP
    q_s3 = q_s.reshape(t_new, nseq, GQA_GROUP, KV_DIM).transpose(1, 0, 2, 3).reshape(nseq, rows, KV_DIM)
    kn = k_s.reshape(t_new, nseq, KV_DIM).transpose(1, 0, 2)
    vn = v_s.reshape(t_new, nseq, KV_DIM).transpose(1, 0, 2)
    kc = ck.reshape(nseq, WINDOW, KV_DIM)
    vc = cv.reshape(nseq, WINDOW, KV_DIM)
    sink_rows = jnp.tile(p['sinks'].reshape(N_KV_HEADS, 1, GQA_GROUP), (1, t_new, 1)).reshape(N_KV_HEADS, rows, 1)
    oa_s3 = _attn_sample(sink_rows, q_s3, kn, vn, kc, vc)
    oa_s = oa_s3.reshape(nseq, t_new, Q_DIM).transpose(1, 0, 2).reshape(n_s, Q_DIM)

    ol_s3, hlast_s = _lru_sample(xl_s.reshape(t_new, nseq, LRU_WIDTH), yl_s.reshape(t_new, nseq, LRU_WIDTH),
                                 cconv.transpose(1, 0, 2), ch, *lru_w)
    ol_s = ol_s3.reshape(n_s, LRU_WIDTH)

    tm_post = min(ROW_TILE, n_p)
    tri = jnp.tril(jnp.ones((tm_post, tm_post), F32), -1).astype(BF16)
    post_w = (wab, wlb, wout, row(p['g_post_mix']), row(p['g_pre_ffn']), wr_pair, wr_top, br)
    zero_cnt = jnp.zeros((1, ROUTE_LANES), F32)
    x1_p, h2_p, route_p, stat_p, cnt_p = _post(oa_p, ol_p, sga_p, sgr_p, xp.reshape(n_p, D_MODEL),
                                               gt_a_p, sc_f_p, sh_f_p, *post_w, tri, zero_cnt, rows_per_mod=seq)
    tm_s = min(ROW_TILE, n_s)
    tri_s = tri if tm_s == tm_post else jnp.tril(jnp.ones((tm_s, tm_s), F32), -1).astype(BF16)
    x1_s, h2_s, route_s, stat_s, cnt_all = _post(oa_s, ol_s, sga_s, sgr_s, xs_tm,
                                                 gt_a_s, sc_f_s, sh_f_s, *post_w, tri_s, cnt_p, rows_per_mod=0)

    e_lanes = slice(N_GROUPS, N_GROUPS + N_EXPERTS)
    to_granules = lambda v: (v.astype(I32) // GRANULE).reshape(-1)
    totals = to_granules(cnt_all[0, e_lanes])
    stats = jnp.concatenate([stat_p, stat_s], axis=0).reshape(-1, SUBLANES, ROUTE_LANES)
    cnt8 = to_granules(stats[:, 0, e_lanes])
    goff = to_granules(stats[:, 1, e_lanes])
    tiles_p = n_p // tm_post
    n_tiles = tiles_p + n_s // tm_s
    max_rows = 2 * (n_p + n_s) + n_tiles * N_EXPERTS * (GRANULE - 1) + N_EXPERTS * (EXPERT_ROWS - GRANULE)
    n_blocks = -(-max_rows // EXPERT_ROWS)
    n_slots = n_blocks * EXPERT_ROWS
    srt, gsrc, gslot = _sort(totals, cnt8, goff, route_p, h2_p, route_s, h2_s, n_slots)
    block_e, meta = _plan(totals, n_blocks)
    ys = _experts(block_e, meta, gsrc, srt, wg, wu, wd, n_slots)
    y_p = _combine(gslot, route_p, x1_p, gt_f_p, row(p['g_post_ffn']), ys, seq, 0)
    y_s = _combine(gslot, route_s, x1_s, gt_f_s, row(p['g_post_ffn']), ys, 0, tiles_p)

    k_new_p = klast_p.reshape(batch, WINDOW, N_KV_HEADS, HEAD_DIM)
    v_new_p = vlast_p.reshape(batch, WINDOW, N_KV_HEADS, HEAD_DIM)
    conv_p = xtail_p[:, -(CONV_WIDTH - 1):]
    h_p = hlast_p.reshape(batch, LRU_WIDTH)
    k_new_s = jnp.concatenate([ck, kn.reshape(nseq, t_new, N_KV_HEADS, HEAD_DIM)], axis=1)[:, -WINDOW:]
    v_new_s = jnp.concatenate([cv, vn.reshape(nseq, t_new, N_KV_HEADS, HEAD_DIM)], axis=1)[:, -WINDOW:]
    xl_s3 = xl_s.reshape(t_new, nseq, LRU_WIDTH).transpose(1, 0, 2).astype(F32)
    conv_s = jnp.concatenate([cconv, xl_s3], axis=1)[:, -(CONV_WIDTH - 1):]
    return (y_p.reshape(batch, seq, D_MODEL), y_s, k_new_p, v_new_p, conv_p, h_p,
            k_new_s, v_new_s, conv_s, hlast_s)


PAST_LEN_ = 16384

PARAM_NAMES = ('w_ada', 'b_ada', 'g_pre_mix', 'g_post_mix', 'g_pre_ffn', 'g_post_ffn', 'w_in', 'sinks',
               'w_conv', 'b_conv', 'w_lru_a', 'b_lru_a', 'w_lru_x', 'b_lru_x', 'lru_lambda',
               'w_attn_branch', 'w_lru_branch', 'w_out', 'w_router_group', 'b_router_group',
               'w_router_expert', 'b_router_expert', 'w_exp_gate', 'w_exp_up', 'w_exp_down')


def kernel(x_prompt, x_sample, cache_k_win, cache_v_win, state_conv, state_h, c_prompt, c_sample, w_ada, b_ada, g_pre_mix, g_post_mix, g_pre_ffn, g_post_ffn, w_in, sinks, w_conv, b_conv, w_lru_a, b_lru_a, w_lru_x, b_lru_x, lru_lambda, w_attn_branch, w_lru_branch, w_out, w_router_group, b_router_group, w_router_expert, b_router_expert, w_exp_gate, w_exp_up, w_exp_down):
    weights = (w_ada, b_ada, g_pre_mix, g_post_mix, g_pre_ffn, g_post_ffn, w_in, sinks,
               w_conv, b_conv, w_lru_a, b_lru_a, w_lru_x, b_lru_x, lru_lambda,
               w_attn_branch, w_lru_branch, w_out, w_router_group, b_router_group,
               w_router_expert, b_router_expert, w_exp_gate, w_exp_up, w_exp_down)
    depth = w_ada.shape[0]
    batch = x_prompt.shape[0]
    nseq, t_new, _ = x_sample.shape
    y_p = x_prompt
    y_s = x_sample.transpose(1, 0, 2).reshape(t_new * nseq, D_MODEL)
    c_all = jnp.concatenate([c_prompt, c_sample], axis=0)
    outs = [[] for _ in range(8)]
    for layer in range(depth):
        p = {name: w[layer] for name, w in zip(PARAM_NAMES, weights)}
        mod = _ada(c_all, p['w_ada'], p['b_ada'].reshape(1, -1))
        res = _layer_forward(y_p, y_s, cache_k_win[layer], cache_v_win[layer], state_conv[layer],
                             state_h[layer], mod[:batch], mod[batch:], p)
        y_p, y_s = res[0], res[1]
        for o, r in zip(outs, res[2:]):
            o.append(r)
    y_sample = y_s.reshape(t_new, nseq, D_MODEL).transpose(1, 0, 2)
    return (y_p, y_sample) + tuple(jnp.stack(o) for o in outs)
```

```python
import functools

import jax
import jax.numpy as jnp
from jax import lax
from jax.experimental import pallas as pl
from jax.experimental.pallas import tpu as pltpu

F32 = jnp.float32
BF16 = jnp.bfloat16
I32 = jnp.int32

D_MODEL = 1024
N_HEADS = 16
HEAD_DIM = 64
N_KV_HEADS = 4
GQA_GROUP = 4
WINDOW = 128
ROPE_THETA = 10000.0
NEG_INF = -1e30
LRU_WIDTH = 1024
LRU_BLOCKS = 16
LRU_BLOCK_W = 64
CONV_WIDTH = 4
LRU_C = 8.0
N_GROUPS = 4
EXPERTS_PER_GROUP = 8
N_EXPERTS = 32
D_EXPERT = 512
MOE_BLOCK = 128
NORM_EPS = 1e-6
Q_DIM = N_HEADS * HEAD_DIM
KV_DIM = N_KV_HEADS * HEAD_DIM
IN_DIM = Q_DIM + 2 * KV_DIM + 2 * LRU_WIDTH + 2 * D_MODEL

LANES = 128
SUBLANES = 8
MXU_DIM = 256
VMEM_LIMIT = 56 * 1024 * 1024
VMEM_LIMIT_BIG = 60 * 1024 * 1024

ROW_TILE = 512
ROUTE_LANES = LANES
GRANULE = SUBLANES
EXPERT_ROWS = 512
HALF_D = D_MODEL // 2
U32 = jnp.uint32


def _cparams(n_axes, vmem=VMEM_LIMIT, flags=None):
    return pltpu.CompilerParams(dimension_semantics=("arbitrary",) * n_axes, vmem_limit_bytes=vmem, flags=flags)


def _rms(x, g):
    ms = jnp.mean(x * x, axis=-1, keepdims=True)
    return x * lax.rsqrt(ms + NORM_EPS) * g


def _sigmoid(x):
    return 1.0 / (1.0 + jnp.exp(-x))


def _dot(a, b):
    return jnp.dot(a, b, preferred_element_type=F32)


def _ada_kernel(c_ref, w_ref, b_ref, o_ref):
    c = c_ref[...]
    s = (c * _sigmoid(c)).astype(BF16)
    o_ref[...] = _dot(s, w_ref[...].astype(BF16)) + b_ref[...]


def _ada(c_all, w_ada, b_ada):
    r = c_all.shape[0]
    n = w_ada.shape[1]
    return pl.pallas_call(
        _ada_kernel,
        grid=(n // D_MODEL,),
        in_specs=[pl.BlockSpec((r, D_MODEL), lambda j: (0, 0)),
                  pl.BlockSpec((D_MODEL, D_MODEL), lambda j: (0, j)),
                  pl.BlockSpec((1, D_MODEL), lambda j: (0, j))],
        out_specs=pl.BlockSpec((r, D_MODEL), lambda j: (0, j)),
        out_shape=jax.ShapeDtypeStruct((r, n), F32),
        compiler_params=_cparams(1),
        name="ada_mod",
    )(c_all, w_ada, b_ada)


_O1 = Q_DIM
_O2 = _O1 + KV_DIM
_O3 = _O2 + KV_DIM
_O4 = _O3 + LRU_WIDTH
_O5 = _O4 + LRU_WIDTH
_O6 = _O5 + D_MODEL


def _prenorm(x_ref, g_ref, sc_ref, sh_ref):
    h = _rms(x_ref[...], g_ref[...]) * (1.0 + sc_ref[...]) + sh_ref[...]
    return h.astype(BF16)


PIECE_COLS = 256


def _plain_pieces(h_ref, w_ref, base, width, store):
    def piece(c0):
        def run():
            store(c0, _dot(h_ref[...], w_ref[:, base + c0:base + c0 + PIECE_COLS]))
        return run
    return [piece(c0) for c0 in range(0, width, PIECE_COLS)]


def _qkv_gate_pieces(h_ref, w_ref, cos_ref, sin_ref, q_ref, k_ref, v_ref, sga_ref, sgr_ref, last_refs=None):
    def rope(t):
        cos = cos_ref[...]
        sin = sin_ref[...]
        lane = lax.broadcasted_iota(I32, cos.shape, 1)
        first_half = (lane % HEAD_DIM) < (HEAD_DIM // 2)
        rot = jnp.where(first_half, pltpu.roll(t, LANES - HEAD_DIM // 2, 1), pltpu.roll(t, HEAD_DIM // 2, 1))
        return t * cos + rot * sin

    def q_piece(c0):
        def run():
            qf = _dot(h_ref[...], w_ref[:, c0:c0 + PIECE_COLS])
            for c in range(PIECE_COLS // LANES):
                q_ref[:, c0 + c * LANES:c0 + (c + 1) * LANES] = rope(qf[:, c * LANES:(c + 1) * LANES]).astype(BF16)
        return run

    def kv_piece():
        kv = _dot(h_ref[...], w_ref[:, _O1:_O3])
        for c in range(KV_DIM // LANES):
            k_ref[:, c * LANES:(c + 1) * LANES] = rope(kv[:, c * LANES:(c + 1) * LANES])
        v_ref[...] = kv[:, KV_DIM:]
        if last_refs is not None:
            t = k_ref.shape[0]
            last_refs[0][...] = k_ref[t - WINDOW:t, :]
            last_refs[1][...] = v_ref[t - WINDOW:t, :]

    def gate_piece(o_ref, base, c0):
        def run():
            z = _dot(h_ref[...], w_ref[:, base + c0:base + c0 + PIECE_COLS])
            o_ref[:, c0:c0 + PIECE_COLS] = _sigmoid(z).astype(BF16)
        return run

    pieces = [q_piece(c0) for c0 in range(0, Q_DIM, PIECE_COLS)] + [kv_piece]
    pieces += [gate_piece(sga_ref, _O5, c0) for c0 in range(0, D_MODEL, PIECE_COLS)]
    pieces += [gate_piece(sgr_ref, _O6, c0) for c0 in range(0, D_MODEL, PIECE_COLS)]
    return pieces


def _inproj_kernel(x_ref, sc_ref, sh_ref, g_ref, cos_ref, sin_ref, w_ref,
                   q_ref, k_ref, v_ref, xl_ref, yl_ref, sga_ref, sgr_ref, hbuf):
    hbuf[...] = _prenorm(x_ref, g_ref, sc_ref, sh_ref)
    xl_ref[...] = _dot(hbuf[...], w_ref[:, _O3:_O4]).astype(BF16)
    yl_ref[...] = _dot(hbuf[...], w_ref[:, _O4:_O5]).astype(BF16)
    for piece in _qkv_gate_pieces(hbuf, w_ref, cos_ref, sin_ref, q_ref, k_ref, v_ref, sga_ref, sgr_ref):
        piece()


def _inproj(x, sc, sh, g, cos, sin, w_in, rows_per_mod, pos_tiles):
    n = x.shape[0]
    tm = min(ROW_TILE, n)
    if rows_per_mod:
        tiles_per_mod = rows_per_mod // tm
        mod_spec = pl.BlockSpec((None, 1, D_MODEL), lambda i: (i // tiles_per_mod, 0, 0))
    else:
        mod_spec = pl.BlockSpec((tm, D_MODEL), lambda i: (i, 0))
    row = lambda w: pl.BlockSpec((tm, w), lambda i: (i, 0))
    outs = [(Q_DIM, BF16), (KV_DIM, F32), (KV_DIM, F32), (LRU_WIDTH, BF16), (LRU_WIDTH, BF16),
            (D_MODEL, BF16), (D_MODEL, BF16)]
    return pl.pallas_call(
        _inproj_kernel,
        grid=(n // tm,),
        in_specs=[row(D_MODEL), mod_spec, mod_spec,
                  pl.BlockSpec((1, D_MODEL), lambda i: (0, 0)),
                  pl.BlockSpec((tm, LANES), lambda i: (i % pos_tiles, 0)),
                  pl.BlockSpec((tm, LANES), lambda i: (i % pos_tiles, 0)),
                  pl.BlockSpec((D_MODEL, IN_DIM), lambda i: (0, 0))],
        out_specs=[row(w) for w, _ in outs],
        out_shape=[jax.ShapeDtypeStruct((n, w), dt) for w, dt in outs],
        scratch_shapes=[pltpu.VMEM((tm, D_MODEL), BF16)],
        compiler_params=_cparams(1),
        name="in_proj",
    )(x, sc, sh, g, cos, sin, w_in)


def _head_masks(shape):
    lane = lax.broadcasted_iota(I32, shape, 1)
    return [(lane // HEAD_DIM) == h for h in range(N_KV_HEADS)]


def _attention_core(q_perm, kall, vall, valid, sink_of, rows):
    masks_b = _head_masks((rows, KV_DIM))
    zero_b = jnp.zeros((rows, KV_DIM), BF16)
    pieces = []
    for h in range(N_KV_HEADS):
        for g in range(GQA_GROUP):
            pieces.append(jnp.where(masks_b[h], q_perm[g], zero_b))
    q_big = jnp.concatenate(pieces, axis=0)
    s_all = lax.dot_general(q_big, kall, (((1,), (1,)), ((), ())), preferred_element_type=F32)
    s_all = s_all * (HEAD_DIM ** -0.5)
    p_chunks, inv_chunks = [], []
    for h in range(N_KV_HEADS):
        for g in range(GQA_GROUP):
            c = h * GQA_GROUP + g
            s = jnp.where(valid, s_all[c * rows:(c + 1) * rows], NEG_INF)
            sink = sink_of(h, g)
            m = jnp.maximum(jnp.max(s, axis=-1, keepdims=True), sink)
            p = jnp.exp(s - m)
            denom = jnp.sum(p, axis=-1, keepdims=True) + jnp.exp(sink - m)
            p_chunks.append(p.astype(BF16))
            inv_chunks.append(1.0 / denom)
    o_all = _dot(jnp.concatenate(p_chunks, axis=0), vall)
    outs = []
    for g in range(GQA_GROUP):
        acc = jnp.zeros((rows, KV_DIM), F32)
        for h in range(N_KV_HEADS):
            c = h * GQA_GROUP + g
            acc = acc + jnp.where(masks_b[h], o_all[c * rows:(c + 1) * rows] * inv_chunks[c], 0.0)
        outs.append(acc)
    return outs


ATTN_Q_BLOCKS = 4


def _attention_tile(sink_ref, q_ref, kc_ref, kp_ref, vc_ref, vp_ref, o_ref, seq_start, between=None):
    kall = jnp.concatenate([kp_ref[...], kc_ref[...]], axis=0).astype(BF16)
    vall = jnp.concatenate([vp_ref[...], vc_ref[...]], axis=0).astype(BF16)
    qi = lax.broadcasted_iota(I32, (WINDOW, 2 * WINDOW), 0)
    kj = lax.broadcasted_iota(I32, (WINDOW, 2 * WINDOW), 1)
    dist = qi + WINDOW - kj
    in_window = (dist >= 0) & (dist <= WINDOW)
    for c in range(ATTN_Q_BLOCKS):
        rows = slice(c * WINDOW, (c + 1) * WINDOW)
        keys = slice(c * WINDOW, (c + 2) * WINDOW)
        valid = in_window & ((kj >= WINDOW) | jnp.logical_not(seq_start)) if c == 0 else in_window
        q_perm = [q_ref[rows, g * KV_DIM:(g + 1) * KV_DIM] for g in range(GQA_GROUP)]
        outs = _attention_core(q_perm, kall[keys], vall[keys], valid,
                               lambda h, g: sink_ref[h * GQA_GROUP + g], WINDOW)
        for g in range(GQA_GROUP):
            o_ref[rows, g * KV_DIM:(g + 1) * KV_DIM] = outs[g].astype(BF16)
        if between is not None:
            between(c)


def _attn_prompt_kernel(sink_ref, q_ref, kc_ref, kp_ref, vc_ref, vp_ref, o_ref):
    _attention_tile(sink_ref, q_ref, kc_ref, kp_ref, vc_ref, vp_ref, o_ref, pl.program_id(1) == 0)


def _attn_prompt(sinks, q, k, v, batch, seq):
    step_rows = ATTN_Q_BLOCKS * WINDOW
    nb = seq // step_rows
    cur = lambda w: pl.BlockSpec((step_rows, w), lambda b, j: (b * nb + j, 0))
    prev = lambda w: pl.BlockSpec(
        (WINDOW, w), lambda b, j: (jnp.maximum((b * nb + j) * ATTN_Q_BLOCKS - 1, 0), 0))
    return pl.pallas_call(
        _attn_prompt_kernel,
        grid=(batch, nb),
        in_specs=[pl.BlockSpec(memory_space=pltpu.SMEM),
                  cur(Q_DIM), cur(KV_DIM), prev(KV_DIM), cur(KV_DIM), prev(KV_DIM)],
        out_specs=cur(Q_DIM),
        out_shape=jax.ShapeDtypeStruct((batch * seq, Q_DIM), BF16),
        compiler_params=_cparams(2),
        name="attn_prompt",
    )(sinks, q, k, k, v, v)


SEQ_PER_STEP = 8


def _attn_sample_kernel(sink_ref, q_ref, kn_ref, vn_ref, kc_ref, vc_ref, o_ref, kbuf, vbuf, *, t_new):
    rows = GQA_GROUP * t_new
    kbuf[WINDOW:2 * WINDOW, :] = jnp.zeros((WINDOW, KV_DIM), F32)
    vbuf[WINDOW:2 * WINDOW, :] = jnp.zeros((WINDOW, KV_DIM), F32)
    ri = lax.broadcasted_iota(I32, (rows, 2 * WINDOW), 0)
    kj = lax.broadcasted_iota(I32, (rows, 2 * WINDOW), 1)
    tq = ri // GQA_GROUP
    valid = (kj >= tq) & (kj <= tq + WINDOW) & (kj < WINDOW + t_new)
    for s in range(SEQ_PER_STEP):
        kbuf[0:WINDOW, :] = kc_ref[s]
        vbuf[0:WINDOW, :] = vc_ref[s]
        kbuf[WINDOW:WINDOW + t_new, :] = kn_ref[s]
        vbuf[WINDOW:WINDOW + t_new, :] = vn_ref[s]
        kall = kbuf[...].astype(BF16)
        vall = vbuf[...].astype(BF16)
        qs = q_ref[s]
        masks_b = _head_masks((rows, KV_DIM))
        zero_b = jnp.zeros((rows, KV_DIM), BF16)
        q_big = jnp.concatenate([jnp.where(masks_b[h], qs, zero_b) for h in range(N_KV_HEADS)], axis=0)
        s_all = lax.dot_general(q_big, kall, (((1,), (1,)), ((), ())), preferred_element_type=F32)
        s_all = s_all * (HEAD_DIM ** -0.5)
        acc = jnp.zeros((rows, KV_DIM), F32)
        p_chunks, inv_chunks = [], []
        for h in range(N_KV_HEADS):
            sc = jnp.where(valid, s_all[h * rows:(h + 1) * rows], NEG_INF)
            sink = sink_ref[h]
            m = jnp.maximum(jnp.max(sc, axis=-1, keepdims=True), sink)
            p = jnp.exp(sc - m)
            denom = jnp.sum(p, axis=-1, keepdims=True) + jnp.exp(sink - m)
            p_chunks.append(p.astype(BF16))
            inv_chunks.append(1.0 / denom)
        o_all = _dot(jnp.concatenate(p_chunks, axis=0), vall)
        for h in range(N_KV_HEADS):
            acc = acc + jnp.where(masks_b[h], o_all[h * rows:(h + 1) * rows] * inv_chunks[h], 0.0)
        o_ref[s] = acc.astype(BF16)


def _attn_sample(sink_rows, q, kn, vn, kc, vc):
    nseq, rows, _ = q.shape
    t_new = kn.shape[1]
    sb = SEQ_PER_STEP
    blk = lambda r: pl.BlockSpec((sb, r, KV_DIM), lambda i: (i, 0, 0))
    return pl.pallas_call(
        functools.partial(_attn_sample_kernel, t_new=t_new),
        grid=(nseq // sb,),
        in_specs=[pl.BlockSpec((N_KV_HEADS, rows, 1), lambda i: (0, 0, 0)),
                  blk(rows), blk(t_new), blk(t_new), blk(WINDOW), blk(WINDOW)],
        out_specs=blk(rows),
        out_shape=jax.ShapeDtypeStruct((nseq, rows, KV_DIM), BF16),
        scratch_shapes=[pltpu.VMEM((2 * WINDOW, KV_DIM), F32), pltpu.VMEM((2 * WINDOW, KV_DIM), F32)],
        compiler_params=_cparams(1),
        name="attn_sample",
    )(sink_rows, q, kn, vn, kc, vc)


def _gelu_tanh(x):
    return 0.5 * x * (1.0 + jnp.tanh(0.7978845608028654 * (x + 0.044715 * x * x * x)))


def _lru_gates(xc, wbd_ref, ba, bx, lam):
    xcb = xc.astype(BF16)
    r_parts, i_parts = [], []
    for gidx in range(LRU_WIDTH // MXU_DIM):
        z = _dot(xcb[:, gidx * MXU_DIM:(gidx + 1) * MXU_DIM], wbd_ref[gidx])
        r_parts.append(z[:, :MXU_DIM])
        i_parts.append(z[:, MXU_DIM:])
    return _gate_math(jnp.concatenate(r_parts, axis=1), jnp.concatenate(i_parts, axis=1), xc, ba, bx, lam)


def _gate_math(zr, zi, xc, ba, bx, lam):
    r = _sigmoid(zr + ba)
    i = _sigmoid(zi + bx)
    softplus_neg_lam = jnp.maximum(-lam, 0.0) + jnp.log1p(jnp.exp(-jnp.abs(lam)))
    log_a = -LRU_C * r * softplus_neg_lam
    a = jnp.exp(log_a)
    y = 1.0 - a * a
    u = jnp.where(y > 0.0, y * lax.rsqrt(y), 0.0) * (i * xc)
    return a, u


LRU_CHUNK = 64


def _lru_chunk(c, xbuf, ybuf, hcar, ol_ref, wc_ref, bc_ref, wbd_ref, ba_ref, bx_ref, lam_ref):
    w = LRU_WIDTH
    n = LRU_CHUNK
    rows = slice(c * n, (c + 1) * n)
    r0 = SUBLANES + c * n
    xc = xbuf[r0:r0 + n, :] * wc_ref[CONV_WIDTH - 1:CONV_WIDTH, :] + bc_ref[...]
    for k in range(1, CONV_WIDTH):
        xc = xc + xbuf[r0 - k:r0 - k + n, :] * wc_ref[CONV_WIDTH - 1 - k:CONV_WIDTH - k, :]
    a, u = _lru_gates(xc, wbd_ref, ba_ref[...], bx_ref[...], lam_ref[...])

    ng = n // SUBLANES
    a3 = a.reshape(ng, SUBLANES, w)
    u3 = u.reshape(ng, SUBLANES, w)
    row = lax.broadcasted_iota(I32, (ng, SUBLANES, w), 1)
    d = 1
    while d < SUBLANES:
        a_s = jnp.where(row >= d, pltpu.roll(a3, d, 1), 1.0)
        u_s = jnp.where(row >= d, pltpu.roll(u3, d, 1), 0.0)
        u3 = a3 * u_s + u3
        a3 = a3 * a_s
        d *= 2
    carry = hcar[...]
    hs = []
    for gi in range(ng):
        hg = a3[gi] * carry + u3[gi]
        hs.append(hg)
        carry = hg[SUBLANES - 1:SUBLANES, :]
    hcar[...] = carry
    h = jnp.concatenate(hs, axis=0)
    ol_ref[rows, :] = (h * _gelu_tanh(ybuf[rows, :])).astype(BF16)


def _inproj_lru_kernel(x_ref, sc_ref, sh_ref, g_ref, cos_ref, sin_ref, w_ref,
                       wc_ref, bc_ref, wbd_ref, ba_ref, bx_ref, lam_ref,
                       q_ref, k_ref, v_ref, sga_ref, sgr_ref, klast_ref, vlast_ref, ol_ref, xtail_ref, hlast_ref,
                       hbuf, xbuf, ybuf, hcar, *, tiles_per_seq):
    t = x_ref.shape[0]

    @pl.when(pl.program_id(0) % tiles_per_seq == 0)
    def _():
        xbuf[0:SUBLANES, :] = jnp.zeros((SUBLANES, LRU_WIDTH), F32)
        hcar[...] = jnp.zeros((1, LRU_WIDTH), F32)

    hbuf[...] = _prenorm(x_ref, g_ref, sc_ref, sh_ref)

    xbuf[SUBLANES:SUBLANES + t, :] = _dot(hbuf[...], w_ref[:, _O3:_O4])
    ybuf[...] = _dot(hbuf[...], w_ref[:, _O4:_O5])

    pieces = _qkv_gate_pieces(hbuf, w_ref, cos_ref, sin_ref, q_ref, k_ref, v_ref, sga_ref, sgr_ref,
                              last_refs=(klast_ref, vlast_ref))
    n_chunks = t // LRU_CHUNK
    per_chunk = -(-len(pieces) // n_chunks)
    for c in range(n_chunks):
        _lru_chunk(c, xbuf, ybuf, hcar, ol_ref, wc_ref, bc_ref, wbd_ref, ba_ref, bx_ref, lam_ref)
        for piece in pieces[c * per_chunk:(c + 1) * per_chunk]:
            piece()

    tail = xbuf[t:t + SUBLANES, :]
    xtail_ref[...] = tail
    xbuf[0:SUBLANES, :] = tail
    hlast_ref[...] = hcar[...]


def _inproj_lru(x, sc, sh, g, cos, sin, w_in, wc, bc, wbd, ba, bx, lam, batch, seq):
    n = x.shape[0]
    tm = min(ROW_TILE, seq)
    tps = seq // tm
    mod_spec = pl.BlockSpec((None, 1, D_MODEL), lambda i: (i // tps, 0, 0))
    row = lambda w: pl.BlockSpec((tm, w), lambda i: (i, 0))
    full = lambda a: pl.BlockSpec(a.shape, lambda i: (0,) * a.ndim)
    per_seq = lambda r, w: pl.BlockSpec((None, r, w), lambda i: (i // tps, 0, 0))
    outs = [(Q_DIM, BF16), (KV_DIM, F32), (KV_DIM, F32), (D_MODEL, BF16), (D_MODEL, BF16)]
    return pl.pallas_call(
        functools.partial(_inproj_lru_kernel, tiles_per_seq=tps),
        grid=(n // tm,),
        in_specs=[row(D_MODEL), mod_spec, mod_spec, full(g),
                  pl.BlockSpec((tm, LANES), lambda i: (i % tps, 0)),
                  pl.BlockSpec((tm, LANES), lambda i: (i % tps, 0)),
                  full(w_in), full(wc), full(bc), full(wbd), full(ba), full(bx), full(lam)],
        out_specs=[row(w) for w, _ in outs] + [per_seq(WINDOW, KV_DIM), per_seq(WINDOW, KV_DIM)]
        + [row(LRU_WIDTH), per_seq(SUBLANES, LRU_WIDTH), per_seq(1, LRU_WIDTH)],
        out_shape=[jax.ShapeDtypeStruct((n, w), dt) for w, dt in outs]
        + [jax.ShapeDtypeStruct((batch, WINDOW, KV_DIM), F32)] * 2
        + [jax.ShapeDtypeStruct((n, LRU_WIDTH), BF16),
           jax.ShapeDtypeStruct((batch, SUBLANES, LRU_WIDTH), F32),
           jax.ShapeDtypeStruct((batch, 1, LRU_WIDTH), F32)],
        scratch_shapes=[pltpu.VMEM((tm, D_MODEL), BF16),
                        pltpu.VMEM((2 * SUBLANES + tm, LRU_WIDTH), F32), pltpu.VMEM((tm, LRU_WIDTH), F32),
                        pltpu.VMEM((1, LRU_WIDTH), F32)],
        compiler_params=_cparams(1, vmem=VMEM_LIMIT_BIG),
        name="in_proj_lru",
    )(x, sc, sh, g, cos, sin, w_in, wc, bc, wbd, ba, bx, lam)


def _lru_sample_kernel(xl_ref, yl_ref, cs_ref, h0_ref, wc_ref, bc_ref, wbd_ref, ba_ref, bx_ref, lam_ref,
                       o_ref, hlast_ref):
    t_new, nseq, w = xl_ref.shape
    xp = [cs_ref[k] for k in range(CONV_WIDTH - 1)] + [xl_ref[k].astype(F32) for k in range(t_new)]
    xcs = []
    for t in range(t_new):
        acc = bc_ref[...] + xp[t] * wc_ref[0:1, :]
        for k in range(1, CONV_WIDTH):
            acc = acc + xp[t + k] * wc_ref[k:k + 1, :]
        xcs.append(acc)
    xc = jnp.concatenate(xcs, axis=0)
    a, u = _lru_gates(xc, wbd_ref, ba_ref[...], bx_ref[...], lam_ref[...])
    h = h0_ref[...]
    for t in range(t_new):
        h = a[t * nseq:(t + 1) * nseq] * h + u[t * nseq:(t + 1) * nseq]
        o_ref[t] = (h * _gelu_tanh(yl_ref[t].astype(F32))).astype(BF16)
    hlast_ref[...] = h


def _lru_sample(xl, yl, cs, h0, wc, bc, wbd, ba, bx, lam):
    t_new, nseq, w = xl.shape
    full = lambda shp: pl.BlockSpec(shp, lambda i: (0,) * len(shp))
    args = (xl, yl, cs, h0, wc, bc, wbd, ba, bx, lam)
    return pl.pallas_call(
        _lru_sample_kernel,
        grid=(1,),
        in_specs=[full(a.shape) for a in args],
        out_specs=[full((t_new, nseq, w)), full((nseq, w))],
        out_shape=[jax.ShapeDtypeStruct((t_new, nseq, w), BF16), jax.ShapeDtypeStruct((nseq, w), F32)],
        compiler_params=_cparams(1),
        name="lru_sample",
    )(*args)


def _post_kernel(oa_ref, ol_ref, sga_ref, sgr_ref, x_ref, gta_ref, scf_ref, shf_ref,
                 wab_ref, wlb_ref, wout_ref, gpm_ref, gpf_ref, wrp_ref, wrt_ref, br_ref, tri_ref, cin_ref,
                 x1_ref, h2_ref, route_ref, stat_ref, cnt_ref, carry):
    @pl.when(pl.program_id(0) == 0)
    def _():
        carry[...] = cin_ref[...]

    b_attn = _dot(oa_ref[...], wab_ref[...])
    b_lru = _dot(ol_ref[...], wlb_ref[...])
    _post_tail(b_attn, b_lru, sga_ref, sgr_ref, x_ref, gta_ref, scf_ref, shf_ref, wout_ref, gpm_ref, gpf_ref,
               wrp_ref, wrt_ref, br_ref, tri_ref, x1_ref, h2_ref, route_ref, stat_ref, cnt_ref, carry)


def _post_attn_kernel(sink_ref, q_ref, kc_ref, kp_ref, vc_ref, vp_ref,
                      ol_ref, sga_ref, sgr_ref, x_ref, gta_ref, scf_ref, shf_ref,
                      wab_ref, wlb_ref, wout_ref, gpm_ref, gpf_ref, wrp_ref, wrt_ref, br_ref, tri_ref, cin_ref,
                      x1_ref, h2_ref, route_ref, stat_ref, cnt_ref, carry, oabuf, blbuf, *, tiles_per_seq):
    i = pl.program_id(0)

    @pl.when(i == 0)
    def _():
        carry[...] = cin_ref[...]

    cols = D_MODEL // ATTN_Q_BLOCKS

    def lru_piece(c):
        blbuf[:, c * cols:(c + 1) * cols] = _dot(ol_ref[...], wlb_ref[:, c * cols:(c + 1) * cols])

    _attention_tile(sink_ref, q_ref, kc_ref, kp_ref, vc_ref, vp_ref, oabuf, i % tiles_per_seq == 0, lru_piece)
    b_attn = _dot(oabuf[...], wab_ref[...])
    _post_tail(b_attn, blbuf[...], sga_ref, sgr_ref, x_ref, gta_ref, scf_ref, shf_ref, wout_ref, gpm_ref, gpf_ref,
               wrp_ref, wrt_ref, br_ref, tri_ref, x1_ref, h2_ref, route_ref, stat_ref, cnt_ref, carry)


def _post_tail(b_attn, b_lru, sga_ref, sgr_ref, x_ref, gta_ref, scf_ref, shf_ref, wout_ref, gpm_ref, gpf_ref,
               wrp_ref, wrt_ref, br_ref, tri_ref, x1_ref, h2_ref, route_ref, stat_ref, cnt_ref, carry):
    merged = sga_ref[...].astype(F32) * b_attn + sgr_ref[...].astype(F32) * b_lru
    mix = _dot(merged.astype(BF16), wout_ref[...])
    x1 = x_ref[...] + gta_ref[...] * _rms(mix, gpm_ref[...])
    x1_ref[...] = x1
    h2 = _rms(x1, gpf_ref[...]) * (1.0 + scf_ref[...]) + shf_ref[...]
    h2_ref[...] = h2.astype(BF16)

    h_hi = h2.astype(BF16)
    h_lo = (h2 - h_hi.astype(F32)).astype(BF16)
    hi_terms = _dot(h_hi, wrp_ref[...])
    logits = (hi_terms[:, :ROUTE_LANES] + (hi_terms[:, ROUTE_LANES:] + _dot(h_lo, wrt_ref[...]))) + br_ref[...]

    tm = logits.shape[0]
    lane = lax.broadcasted_iota(I32, (tm, ROUTE_LANES), 1)
    big = jnp.int32(ROUTE_LANES)
    is_g = lane < N_GROUPS
    lg = jnp.where(is_g, logits, NEG_INF)
    mg = jnp.max(lg, axis=-1, keepdims=True)
    g_star = jnp.min(jnp.where(lg == mg, lane, big), axis=-1, keepdims=True)
    p_star = 1.0 / jnp.sum(jnp.where(is_g, jnp.exp(lg - mg), 0.0), axis=-1, keepdims=True)
    lo = N_GROUPS + g_star * EXPERTS_PER_GROUP
    in_grp = (lane >= lo) & (lane < lo + EXPERTS_PER_GROUP)
    le = jnp.where(in_grp, logits, NEG_INF)
    m1 = jnp.max(le, axis=-1, keepdims=True)
    i1 = jnp.min(jnp.where(le == m1, lane, big), axis=-1, keepdims=True)
    le2 = jnp.where(lane == i1, NEG_INF, le)
    m2 = jnp.max(le2, axis=-1, keepdims=True)
    i2 = jnp.min(jnp.where(le2 == m2, lane, big), axis=-1, keepdims=True)
    e2x = jnp.exp(m2 - m1)
    wsum = 1.0 + e2x
    w1 = (1.0 / wsum) * p_star
    w2 = (e2x / wsum) * p_star

    oh1 = lane == i1
    oh2 = lane == i2
    cnt = jnp.where(oh1 | oh2, 1.0, 0.0)
    excl = _dot(tri_ref[...], cnt.astype(BF16))
    per_e = jnp.sum(cnt, axis=0, keepdims=True)
    pad8 = jnp.floor((per_e + (GRANULE - 1.0)) * (1.0 / GRANULE)) * GRANULE
    incl = jnp.broadcast_to(pad8, (SUBLANES, ROUTE_LANES))
    lane8 = lax.broadcasted_iota(I32, (SUBLANES, ROUTE_LANES), 1)
    d = 1
    while d < ROUTE_LANES:
        incl = incl + jnp.where(lane8 >= d, pltpu.roll(incl, d, 1), 0.0)
        d *= 2
    seg_start = incl[0:1, :] - pad8
    pos = excl + seg_start
    s1 = jnp.sum(jnp.where(oh1, pos, 0.0), axis=-1, keepdims=True)
    s2 = jnp.sum(jnp.where(oh2, pos, 0.0), axis=-1, keepdims=True)
    rec = jnp.where(lane == 0, s1, 0.0)
    rec = jnp.where(lane == 1, s2, rec)
    rec = jnp.where(lane == 2, w1, rec)
    rec = jnp.where(lane == 3, w2, rec)
    route_ref[...] = rec
    srow = lax.broadcasted_iota(I32, (SUBLANES, ROUTE_LANES), 0)
    stat_ref[...] = jnp.where(srow == 0, pad8, jnp.where(srow == 1, carry[...], 0.0))
    carry[...] = carry[...] + pad8
    cnt_ref[...] = carry[...]


def _post(oa, ol, sga, sgr, x, gta, scf, shf, wab, wlb, wout, gpm, gpf, wrh, wrl, br, tri, cin, rows_per_mod):
    n = x.shape[0]
    tm = min(ROW_TILE, n)
    if rows_per_mod:
        tiles_per_mod = rows_per_mod // tm
        mod_spec = pl.BlockSpec((None, 1, D_MODEL), lambda i: (i // tiles_per_mod, 0, 0))
    else:
        mod_spec = pl.BlockSpec((tm, D_MODEL), lambda i: (i, 0))
    row = lambda w: pl.BlockSpec((tm, w), lambda i: (i, 0))
    full = lambda a: pl.BlockSpec(a.shape, lambda i: (0,) * a.ndim)
    if isinstance(oa, tuple):
        assert tm == ATTN_Q_BLOCKS * WINDOW and rows_per_mod
        sinks, q, k, v = oa
        prev = pl.BlockSpec((WINDOW, KV_DIM), lambda i: (jnp.maximum(i * ATTN_Q_BLOCKS - 1, 0), 0))
        kernel = functools.partial(_post_attn_kernel, tiles_per_seq=rows_per_mod // tm)
        first_specs = [pl.BlockSpec(memory_space=pltpu.SMEM), row(Q_DIM), row(KV_DIM), prev, row(KV_DIM), prev]
        first_args = (sinks, q, k, k, v, v)
        scratch = [pltpu.VMEM((tm, Q_DIM), BF16), pltpu.VMEM((tm, D_MODEL), F32)]
    else:
        kernel, first_specs, first_args, scratch = _post_kernel, [row(Q_DIM)], (oa,), []
    return pl.pallas_call(
        kernel,
        grid=(n // tm,),
        in_specs=first_specs + [row(LRU_WIDTH), row(D_MODEL), row(D_MODEL), row(D_MODEL),
                  mod_spec, mod_spec, mod_spec,
                  full(wab), full(wlb), full(wout), full(gpm), full(gpf), full(wrh), full(wrl), full(br),
                  full(tri), full(cin)],
        out_specs=[row(D_MODEL), row(D_MODEL), row(ROUTE_LANES),
                   pl.BlockSpec((SUBLANES, ROUTE_LANES), lambda i: (i, 0)),
                   pl.BlockSpec((1, ROUTE_LANES), lambda i: (0, 0))],
        out_shape=[jax.ShapeDtypeStruct((n, D_MODEL), F32), jax.ShapeDtypeStruct((n, D_MODEL), BF16),
                   jax.ShapeDtypeStruct((n, ROUTE_LANES), F32),
                   jax.ShapeDtypeStruct((n // tm * SUBLANES, ROUTE_LANES), F32),
                   jax.ShapeDtypeStruct((1, ROUTE_LANES), F32)],
        scratch_shapes=[pltpu.VMEM((1, ROUTE_LANES), F32)] + scratch,
        compiler_params=_cparams(1),
        name="post_mix",
    )(*first_args, ol, sga, sgr, x, gta, scf, shf, wab, wlb, wout, gpm, gpf, wrh, wrl, br, tri, cin)


BLOCK_GRANULES = EXPERT_ROWS // GRANULE
BLOCK_SHIFT = BLOCK_GRANULES.bit_length() - 1
assert BLOCK_GRANULES == 1 << BLOCK_SHIFT


def _padded(c):
    return ((c + (BLOCK_GRANULES - 1)) >> BLOCK_SHIFT) << BLOCK_SHIFT


def _sorted_rows(tm):
    r = 2 * tm + N_EXPERTS * (GRANULE - 1)
    return -(-r // MXU_DIM) * MXU_DIM


def _plan_kernel(tot_ref, be_ref, meta_ref, *, n_blocks):
    def fill(j, _):
        be_ref[j] = N_EXPERTS - 1
        return 0
    lax.fori_loop(0, n_blocks, fill, 0)

    def per_expert(e, nb):
        k = _padded(tot_ref[e]) >> BLOCK_SHIFT

        def put(b, _):
            be_ref[nb + b] = e
            return 0
        lax.fori_loop(0, k, put, 0)
        return nb + k
    n_active = lax.fori_loop(0, N_EXPERTS, per_expert, 0)
    meta_ref[0] = n_active


def _plan(totals, n_blocks):
    return pl.pallas_call(
        functools.partial(_plan_kernel, n_blocks=n_blocks),
        in_specs=[pl.BlockSpec(memory_space=pltpu.SMEM)],
        out_specs=[pl.BlockSpec(memory_space=pltpu.SMEM), pl.BlockSpec(memory_space=pltpu.SMEM)],
        out_shape=[jax.ShapeDtypeStruct((n_blocks,), I32), jax.ShapeDtypeStruct((1,), I32)],
        name="moe_plan",
    )(totals)


def _expert_starts(tot_ref, pstart):
    def body(e, acc):
        pstart[e] = acc
        return acc + _padded(tot_ref[e])
    return lax.fori_loop(0, N_EXPERTS, body, 0)


def _granule(ref, g):
    return ref.at[pl.ds(pl.multiple_of(g * GRANULE, GRANULE), GRANULE)]


def _pack_halves(lo_f32, hi_f32):
    return (pltpu.bitcast(lo_f32, U32) >> 16) | (pltpu.bitcast(hi_f32, U32) & jnp.uint32(0xFFFF0000))


def _unpack_halves(packed):
    lo = pltpu.bitcast(packed << 16, F32).astype(BF16)
    hi = pltpu.bitcast(packed & jnp.uint32(0xFFFF0000), F32).astype(BF16)
    return lo, hi


def _sort_kernel(tot_ref, cnt_ref, goff_ref, rec_p_ref, h2_p_ref, rec_s_ref, h2_s_ref,
                 srt_ref, gsrc_ref, gslot_ref, pstart, *, tiles_p, n_sorted, n_slots):
    i = pl.program_id(0)
    from_sample = i >= tiles_p
    rec = jnp.where(from_sample, rec_s_ref[...], rec_p_ref[...])
    h2 = jnp.where(from_sample, h2_s_ref[...], h2_p_ref[...])
    tm = h2.shape[0]

    rec_t = rec.T
    s1 = rec_t[0:1, :].astype(I32)
    s2 = rec_t[1:2, :].astype(I32)
    rows = lax.broadcasted_iota(I32, (n_sorted, tm), 0)
    sel = jnp.where((rows == s1) | (rows == s2), 1.0, 0.0).astype(BF16)
    srt = _dot(sel, h2)
    srt_ref[...] = _pack_halves(srt[:, :HALF_D], srt[:, HALF_D:])

    zero_granule = n_sorted // GRANULE - 1

    @pl.when(i == 0)
    def _():
        used = _expert_starts(tot_ref, pstart)

        def put_zero(g, _):
            gsrc_ref[g] = zero_granule
            return 0

        def per_expert(e, _):
            total = tot_ref[e]
            lax.fori_loop(pstart[e] + total, pstart[e] + _padded(total), put_zero, 0)
            return 0
        lax.fori_loop(0, N_EXPERTS, per_expert, 0)
        lax.fori_loop(used, n_slots // GRANULE, put_zero, 0)

    tile_granules = n_sorted // GRANULE

    def per_expert(e, seg):
        k = cnt_ref[i * N_EXPERTS + e]
        dst = pstart[e] + goff_ref[i * N_EXPERTS + e]
        src = i * tile_granules + seg

        def put(g, _):
            gsrc_ref[dst + g] = src + g
            gslot_ref[src + g] = dst + g
            return 0
        lax.fori_loop(0, k, put, 0)
        return seg + k
    used_in_tile = lax.fori_loop(0, N_EXPERTS, per_expert, 0, unroll=4)

    def put_any(q, _):
        gslot_ref[i * tile_granules + q] = 0
        return 0
    lax.fori_loop(used_in_tile, tile_granules, put_any, 0)


def _sort(totals, cnt8, goff, rec_p, h2_p, rec_s, h2_s, n_slots):
    tm = min(ROW_TILE, h2_p.shape[0])
    assert h2_s.shape[0] % tm == 0
    tiles_p = h2_p.shape[0] // tm
    tiles_s = h2_s.shape[0] // tm
    n_sorted = _sorted_rows(tm)
    row_p = lambda w: pl.BlockSpec((tm, w), lambda i, *_: (jnp.minimum(i, tiles_p - 1), 0))
    row_s = lambda w: pl.BlockSpec((tm, w), lambda i, *_: (jnp.maximum(i - tiles_p, 0), 0))
    return pl.pallas_call(
        functools.partial(_sort_kernel, tiles_p=tiles_p, n_sorted=n_sorted, n_slots=n_slots),
        grid_spec=pltpu.PrefetchScalarGridSpec(
            num_scalar_prefetch=3,
            grid=(tiles_p + tiles_s,),
            in_specs=[row_p(ROUTE_LANES), row_p(D_MODEL), row_s(ROUTE_LANES), row_s(D_MODEL)],
            out_specs=[pl.BlockSpec((n_sorted, HALF_D), lambda i, *_: (i, 0)),
                       pl.BlockSpec(memory_space=pltpu.SMEM), pl.BlockSpec(memory_space=pltpu.SMEM)],
            scratch_shapes=[pltpu.SMEM((N_EXPERTS,), I32)]),
        out_shape=[jax.ShapeDtypeStruct(((tiles_p + tiles_s) * n_sorted, HALF_D), U32),
                   jax.ShapeDtypeStruct((n_slots // GRANULE,), I32),
                   jax.ShapeDtypeStruct(((tiles_p + tiles_s) * n_sorted // GRANULE,), I32)],
        compiler_params=_cparams(1),
        name="moe_sort",
    )(totals, cnt8, goff, rec_p, h2_p, rec_s, h2_s)


def _expert_kernel(be_ref, meta_ref, gsrc_ref, srt_hbm, wg_hbm, wu_hbm, wd_hbm, ys_ref,
                   xbuf, wgs, wus, wds, wgb, wub, wdb, stage, sems, wsems):
    j = pl.program_id(0)
    n_active = meta_ref[0]
    gran_per_block = BLOCK_GRANULES

    def granule_copy(blk, g, slot):
        return pltpu.make_async_copy(_granule(srt_hbm, gsrc_ref[blk * gran_per_block + g]),
                                     xbuf.at[slot, pl.ds(g * GRANULE, GRANULE)], sems.at[slot])

    def gather(blk, slot):
        for g in range(gran_per_block):
            granule_copy(blk, g, slot).start(priority=g % 2)

    def drain(slot):
        for g in range(gran_per_block):
            granule_copy(0, g, slot).wait()

    def weight_copies(e, p):
        return [pltpu.make_async_copy(wg_hbm.at[e], wgs.at[p], wsems.at[p]),
                pltpu.make_async_copy(wu_hbm.at[e], wus.at[p], wsems.at[p]),
                pltpu.make_async_copy(wd_hbm.at[e], wds.at[p], wsems.at[p])]

    @pl.when(j == 0)
    def _():
        gather(0, 0)
        stage[0] = 0
        for cp in weight_copies(be_ref[0], 0):
            cp.start()

    @pl.when(j < n_active)
    def _():
        slot = j % 2
        e = be_ref[j]

        @pl.when((j == 0) | (e != be_ref[jnp.maximum(j - 1, 0)]))
        def _():
            p = stage[0]
            for cp in weight_copies(e, p):
                cp.wait()
            wgb[...] = wgs[p].astype(BF16)
            wub[...] = wus[p].astype(BF16)
            wdb[...] = wds[p].astype(BF16)
            nxt = lax.while_loop(lambda k: (k < n_active) & (be_ref[jnp.minimum(k, n_active - 1)] == e),
                                 lambda k: k + 1, j + 1)

            @pl.when(nxt < n_active)
            def _():
                for cp in weight_copies(be_ref[nxt], 1 - p):
                    cp.start()
            stage[0] = 1 - p

        gather(jnp.minimum(j + 1, n_active - 1), 1 - slot)
        drain(slot)
        x_lo, x_hi = _unpack_halves(xbuf[slot])
        g = _dot(x_lo, wgb[0:HALF_D, :]) + _dot(x_hi, wgb[HALF_D:D_MODEL, :])
        u = _dot(x_lo, wub[0:HALF_D, :]) + _dot(x_hi, wub[HALF_D:D_MODEL, :])
        hmid = (g * _sigmoid(g) * u).astype(BF16)
        y = _dot(hmid, wdb[...])
        ys_ref[...] = _pack_halves(y[:, :HALF_D].astype(BF16).astype(F32), y[:, HALF_D:].astype(BF16).astype(F32))

        @pl.when(j == n_active - 1)
        def _():
            drain(1 - slot)

    @pl.when(j >= meta_ref[0])
    def _():
        ys_ref[...] = jnp.zeros(ys_ref.shape, U32)


def _experts(block_e, meta, gsrc, srt, wg, wu, wd, n_slots):
    n_blocks = n_slots // EXPERT_ROWS
    anyspec = pl.BlockSpec(memory_space=pl.ANY)
    return pl.pallas_call(
        _expert_kernel,
        grid_spec=pltpu.PrefetchScalarGridSpec(
            num_scalar_prefetch=3,
            grid=(n_blocks,),
            in_specs=[anyspec, anyspec, anyspec, anyspec],
            out_specs=pl.BlockSpec((EXPERT_ROWS, HALF_D), lambda j, be, meta, gs: (j, 0)),
            scratch_shapes=[pltpu.VMEM((2, EXPERT_ROWS, HALF_D), U32),
                            pltpu.VMEM((2, D_MODEL, D_EXPERT), F32), pltpu.VMEM((2, D_MODEL, D_EXPERT), F32),
                            pltpu.VMEM((2, D_EXPERT, D_MODEL), F32),
                            pltpu.VMEM((D_MODEL, D_EXPERT), BF16), pltpu.VMEM((D_MODEL, D_EXPERT), BF16),
                            pltpu.VMEM((D_EXPERT, D_MODEL), BF16),
                            pltpu.SMEM((1,), I32), pltpu.SemaphoreType.DMA((2,)), pltpu.SemaphoreType.DMA((2,))]),
        out_shape=jax.ShapeDtypeStruct((n_slots, HALF_D), U32),
        compiler_params=_cparams(1),
        name="moe_experts",
    )(block_e, meta, gsrc, srt, wg, wu, wd)


def _combine_kernel(gslot_ref, rec_ref, x1_ref, gtf_ref, g_ref, ys_hbm, y_ref, cbuf, sems, *, tile_base, n_tiles):
    i = pl.program_id(0)
    slot = i % 2
    tm = x1_ref.shape[0]
    n_sorted = cbuf.shape[1]
    tile_granules = n_sorted // GRANULE

    def granule_copy(slot_granule, q, slot_):
        return pltpu.make_async_copy(_granule(ys_hbm, slot_granule),
                                     cbuf.at[slot_, pl.ds(q * GRANULE, GRANULE)], sems.at[slot_])

    def gather(tile, slot_):
        for q in range(tile_granules):
            granule_copy(gslot_ref[tile * tile_granules + q], q, slot_).start(priority=q % 2)

    def drain(slot_):
        for q in range(tile_granules):
            granule_copy(0, q, slot_).wait()

    @pl.when(i == 0)
    def _():
        gather(tile_base, 0)

    gather(tile_base + jnp.minimum(i + 1, n_tiles - 1), 1 - slot)
    drain(slot)

    rec = rec_ref[...]
    s1 = rec[:, 0:1].astype(I32)
    s2 = rec[:, 1:2].astype(I32)
    col = lax.broadcasted_iota(I32, (tm, n_sorted), 1)
    wmat = (jnp.where(col == s1, rec[:, 2:3], 0.0) + jnp.where(col == s2, rec[:, 3:4], 0.0)).astype(BF16)
    y_lo, y_hi = _unpack_halves(cbuf[slot])
    f = jnp.concatenate([_dot(wmat, y_lo), _dot(wmat, y_hi)], axis=1)
    y_ref[...] = x1_ref[...] + gtf_ref[...] * _rms(f, g_ref[...])

    @pl.when(i == n_tiles - 1)
    def _():
        drain(1 - slot)


def _combine(gslot, rec, x1, gtf, g, ys, rows_per_mod, tile_base):
    n = x1.shape[0]
    tm = min(ROW_TILE, n)
    if rows_per_mod:
        tiles_per_mod = rows_per_mod // tm
        mod_spec = pl.BlockSpec((None, 1, D_MODEL), lambda i, *_: (i // tiles_per_mod, 0, 0))
    else:
        mod_spec = pl.BlockSpec((tm, D_MODEL), lambda i, *_: (i, 0))
    row = lambda w: pl.BlockSpec((tm, w), lambda i, *_: (i, 0))
    return pl.pallas_call(
        functools.partial(_combine_kernel, tile_base=tile_base, n_tiles=n // tm),
        grid_spec=pltpu.PrefetchScalarGridSpec(
            num_scalar_prefetch=1,
            grid=(n // tm,),
            in_specs=[row(ROUTE_LANES), row(D_MODEL), mod_spec,
                      pl.BlockSpec((1, D_MODEL), lambda i, *_: (0, 0)),
                      pl.BlockSpec(memory_space=pl.ANY)],
            out_specs=row(D_MODEL),
            scratch_shapes=[pltpu.VMEM((2, _sorted_rows(tm), HALF_D), U32), pltpu.SemaphoreType.DMA((2,))]),
        out_shape=jax.ShapeDtypeStruct((n, D_MODEL), F32),
        compiler_params=_cparams(1),
        name="moe_combine",
    )(gslot, rec, x1, gtf, g, ys)


def _rope_tables(pos):
    half = HEAD_DIM // 2
    inv = jnp.power(jnp.float32(ROPE_THETA), -jnp.arange(half, dtype=F32) / half)
    ang = pos.astype(F32)[:, None] * inv[None, :]
    cos = jnp.cos(ang)
    sin = jnp.sin(ang)
    reps = LANES // HEAD_DIM
    cos_t = jnp.tile(jnp.concatenate([cos, cos], axis=-1), (1, reps))
    sin_t = jnp.tile(jnp.concatenate([-sin, sin], axis=-1), (1, reps))
    return cos_t, sin_t


def _q_perm_index():
    g = jnp.arange(GQA_GROUP)[:, None, None]
    h = jnp.arange(N_KV_HEADS)[None, :, None]
    d = jnp.arange(HEAD_DIM)[None, None, :]
    return ((h * GQA_GROUP + g) * HEAD_DIM + d).reshape(-1)


def _block_diag_gates(w_a, w_x):
    per = MXU_DIM // LRU_BLOCK_W
    groups = LRU_BLOCKS // per

    def bd(w):
        w = w.reshape(groups, per, LRU_BLOCK_W, LRU_BLOCK_W)
        eye = jnp.eye(per, dtype=w.dtype)
        full = jnp.einsum('gpij,pq->gpiqj', w, eye)
        return full.reshape(groups, MXU_DIM, MXU_DIM)
    return jnp.concatenate([bd(w_a), bd(w_x)], axis=-1).astype(BF16)


def _layer_forward(xp, xs_tm, ck, cv, cconv, ch, mod_p, mod_s, p):
    batch, seq, _ = xp.shape
    nseq, _, _, _ = ck.shape
    t_new = xs_tm.shape[0] // nseq
    n_p = batch * seq
    n_s = xs_tm.shape[0]

    perm = _q_perm_index()
    w_in = p['w_in']
    w_in_b = jnp.concatenate([w_in[:, :Q_DIM][:, perm], w_in[:, Q_DIM:]], axis=1).astype(BF16)
    wab = p['w_attn_branch'][perm, :].astype(BF16)
    wlb = p['w_lru_branch'].astype(BF16)
    wout = p['w_out'].astype(BF16)
    wbd = _block_diag_gates(p['w_lru_a'], p['w_lru_x'])
    row = lambda v: v.reshape(1, -1)
    wr = jnp.concatenate([p['w_router_group'], p['w_router_expert'],
                          jnp.zeros((D_MODEL, ROUTE_LANES - N_GROUPS - N_EXPERTS), F32)], axis=1)
    wr_top = wr.astype(BF16)
    wr_pair = jnp.concatenate([wr_top, (wr - wr_top.astype(F32)).astype(BF16)], axis=1)
    br = jnp.concatenate([p['b_router_group'], p['b_router_expert'],
                          jnp.zeros((ROUTE_LANES - N_GROUPS - N_EXPERTS,), F32)]).reshape(1, -1)
    wg = p['w_exp_gate']
    wu = p['w_exp_up']
    wd = p['w_exp_down']

    def mods(mod):
        return [mod[:, k * D_MODEL:(k + 1) * D_MODEL] for k in range(6)]
    sh_a_p, sc_a_p, gt_a_p, sh_f_p, sc_f_p, gt_f_p = [m.reshape(batch, 1, D_MODEL) for m in mods(mod_p)]
    sh_a_s, sc_a_s, gt_a_s, sh_f_s, sc_f_s, gt_f_s = [jnp.tile(m, (t_new, 1)) for m in mods(mod_s)]

    lru_w = (p['w_conv'], row(p['b_conv']), wbd, row(p['b_lru_a']), row(p['b_lru_x']), row(p['lru_lambda']))
    cos_p, sin_p = _rope_tables(jnp.arange(seq, dtype=I32))
    q_p, k_p, v_p, sga_p, sgr_p, klast_p, vlast_p, ol_p, xtail_p, hlast_p = _inproj_lru(
        xp.reshape(n_p, D_MODEL), sc_a_p, sh_a_p, row(p['g_pre_mix']), cos_p, sin_p, w_in_b, *lru_w, batch, seq)
    pos_s = jnp.repeat(PAST_LEN_ + jnp.arange(t_new, dtype=I32), nseq)
    cos_s, sin_s = _rope_tables(pos_s)
    q_s, k_s, v_s, xl_s, yl_s, sga_s, sgr_s = _inproj(
        xs_tm, sc_a_s, sh_a_s, row(p['g_pre_mix']), cos_s, sin_s, w_in_b, rows_per_mod=0,
        pos_tiles=n_s // min(ROW_TILE, n_s))

    sinks_perm = p['sinks']
    oa_p = (sinks_perm, q_p, k_p, v_p)
    rows = t_new * GQA_GROUP
    q_s3 = q_s.reshape(t_new, nseq, GQA_GROUP, KV_DIM).transpose(1, 0, 2, 3).reshape(nseq, rows, KV_DIM)
    kn = k_s.reshape(t_new, nseq, KV_DIM).transpose(1, 0, 2)
    vn = v_s.reshape(t_new, nseq, KV_DIM).transpose(1, 0, 2)
    kc = ck.reshape(nseq, WINDOW, KV_DIM)
    vc = cv.reshape(nseq, WINDOW, KV_DIM)
    sink_rows = jnp.tile(p['sinks'].reshape(N_KV_HEADS, 1, GQA_GROUP), (1, t_new, 1)).reshape(N_KV_HEADS, rows, 1)
    oa_s3 = _attn_sample(sink_rows, q_s3, kn, vn, kc, vc)
    oa_s = oa_s3.reshape(nseq, t_new, Q_DIM).transpose(1, 0, 2).reshape(n_s, Q_DIM)

    ol_s3, hlast_s = _lru_sample(xl_s.reshape(t_new, nseq, LRU_WIDTH), yl_s.reshape(t_new, nseq, LRU_WIDTH),
                                 cconv.transpose(1, 0, 2), ch, *lru_w)
    ol_s = ol_s3.reshape(n_s, LRU_WIDTH)

    tm_post = min(ROW_TILE, n_p)
    tri = jnp.tril(jnp.ones((tm_post, tm_post), F32), -1).astype(BF16)
    post_w = (wab, wlb, wout, row(p['g_post_mix']), row(p['g_pre_ffn']), wr_pair, wr_top, br)
    zero_cnt = jnp.zeros((1, ROUTE_LANES), F32)
    x1_p, h2_p, route_p, stat_p, cnt_p = _post(oa_p, ol_p, sga_p, sgr_p, xp.reshape(n_p, D_MODEL),
                                               gt_a_p, sc_f_p, sh_f_p, *post_w, tri, zero_cnt, rows_per_mod=seq)
    tm_s = min(ROW_TILE, n_s)
    tri_s = tri if tm_s == tm_post else jnp.tril(jnp.ones((tm_s, tm_s), F32), -1).astype(BF16)
    x1_s, h2_s, route_s, stat_s, cnt_all = _post(oa_s, ol_s, sga_s, sgr_s, xs_tm,
                                                 gt_a_s, sc_f_s, sh_f_s, *post_w, tri_s, cnt_p, rows_per_mod=0)

    e_lanes = slice(N_GROUPS, N_GROUPS + N_EXPERTS)
    to_granules = lambda v: (v.astype(I32) // GRANULE).reshape(-1)
    totals = to_granules(cnt_all[0, e_lanes])
    stats = jnp.concatenate([stat_p, stat_s], axis=0).reshape(-1, SUBLANES, ROUTE_LANES)
    cnt8 = to_granules(stats[:, 0, e_lanes])
    goff = to_granules(stats[:, 1, e_lanes])
    tiles_p = n_p // tm_post
    n_tiles = tiles_p + n_s // tm_s
    max_rows = 2 * (n_p + n_s) + n_tiles * N_EXPERTS * (GRANULE - 1) + N_EXPERTS * (EXPERT_ROWS - GRANULE)
    n_blocks = -(-max_rows // EXPERT_ROWS)
    n_slots = n_blocks * EXPERT_ROWS
    srt, gsrc, gslot = _sort(totals, cnt8, goff, route_p, h2_p, route_s, h2_s, n_slots)
    block_e, meta = _plan(totals, n_blocks)
    ys = _experts(block_e, meta, gsrc, srt, wg, wu, wd, n_slots)
    y_p = _combine(gslot, route_p, x1_p, gt_f_p, row(p['g_post_ffn']), ys, seq, 0)
    y_s = _combine(gslot, route_s, x1_s, gt_f_s, row(p['g_post_ffn']), ys, 0, tiles_p)

    k_new_p = klast_p.reshape(batch, WINDOW, N_KV_HEADS, HEAD_DIM)
    v_new_p = vlast_p.reshape(batch, WINDOW, N_KV_HEADS, HEAD_DIM)
    conv_p = xtail_p[:, -(CONV_WIDTH - 1):]
    h_p = hlast_p.reshape(batch, LRU_WIDTH)
    k_new_s = jnp.concatenate([ck, kn.reshape(nseq, t_new, N_KV_HEADS, HEAD_DIM)], axis=1)[:, -WINDOW:]
    v_new_s = jnp.concatenate([cv, vn.reshape(nseq, t_new, N_KV_HEADS, HEAD_DIM)], axis=1)[:, -WINDOW:]
    xl_s3 = xl_s.reshape(t_new, nseq, LRU_WIDTH).transpose(1, 0, 2).astype(F32)
    conv_s = jnp.concatenate([cconv, xl_s3], axis=1)[:, -(CONV_WIDTH - 1):]
    return (y_p.reshape(batch, seq, D_MODEL), y_s, k_new_p, v_new_p, conv_p, h_p,
            k_new_s, v_new_s, conv_s, hlast_s)


PAST_LEN_ = 16384

PARAM_NAMES = ('w_ada', 'b_ada', 'g_pre_mix', 'g_post_mix', 'g_pre_ffn', 'g_post_ffn', 'w_in', 'sinks',
               'w_conv', 'b_conv', 'w_lru_a', 'b_lru_a', 'w_lru_x', 'b_lru_x', 'lru_lambda',
               'w_attn_branch', 'w_lru_branch', 'w_out', 'w_router_group', 'b_router_group',
               'w_router_expert', 'b_router_expert', 'w_exp_gate', 'w_exp_up', 'w_exp_down')


def kernel(x_prompt, x_sample, cache_k_win, cache_v_win, state_conv, state_h, c_prompt, c_sample, w_ada, b_ada, g_pre_mix, g_post_mix, g_pre_ffn, g_post_ffn, w_in, sinks, w_conv, b_conv, w_lru_a, b_lru_a, w_lru_x, b_lru_x, lru_lambda, w_attn_branch, w_lru_branch, w_out, w_router_group, b_router_group, w_router_expert, b_router_expert, w_exp_gate, w_exp_up, w_exp_down):
    weights = (w_ada, b_ada, g_pre_mix, g_post_mix, g_pre_ffn, g_post_ffn, w_in, sinks,
               w_conv, b_conv, w_lru_a, b_lru_a, w_lru_x, b_lru_x, lru_lambda,
               w_attn_branch, w_lru_branch, w_out, w_router_group, b_router_group,
               w_router_expert, b_router_expert, w_exp_gate, w_exp_up, w_exp_down)
    depth = w_ada.shape[0]
    batch = x_prompt.shape[0]
    nseq, t_new, _ = x_sample.shape
    y_p = x_prompt
    y_s = x_sample.transpose(1, 0, 2).reshape(t_new * nseq, D_MODEL)
    c_all = jnp.concatenate([c_prompt, c_sample], axis=0)
    outs = [[] for _ in range(8)]
    for layer in range(depth):
        p = {name: w[layer] for name, w in zip(PARAM_NAMES, weights)}
        mod = _ada(c_all, p['w_ada'], p['b_ada'].reshape(1, -1))
        res = _layer_forward(y_p, y_s, cache_k_win[layer], cache_v_win[layer], state_conv[layer],
                             state_h[layer], mod[:batch], mod[batch:], p)
        y_p, y_s = res[0], res[1]
        for o, r in zip(outs, res[2:]):
            o.append(r)
    y_sample = y_s.reshape(t_new, nseq, D_MODEL).transpose(1, 0, 2)
    return (y_p, y_sample) + tuple(jnp.stack(o) for o in outs)
```

```python
import functools

import jax
import jax.numpy as jnp
from jax import lax
from jax.experimental import pallas as pl
from jax.experimental.pallas import tpu as pltpu

F32 = jnp.float32
BF16 = jnp.bfloat16
I32 = jnp.int32

D_MODEL = 1024
N_HEADS = 16
HEAD_DIM = 64
N_KV_HEADS = 4
GQA_GROUP = 4
WINDOW = 128
ROPE_THETA = 10000.0
NEG_INF = -1e30
LRU_WIDTH = 1024
LRU_BLOCKS = 16
LRU_BLOCK_W = 64
CONV_WIDTH = 4
LRU_C = 8.0
N_GROUPS = 4
EXPERTS_PER_GROUP = 8
N_EXPERTS = 32
D_EXPERT = 512
MOE_BLOCK = 128
NORM_EPS = 1e-6
Q_DIM = N_HEADS * HEAD_DIM
KV_DIM = N_KV_HEADS * HEAD_DIM
IN_DIM = Q_DIM + 2 * KV_DIM + 2 * LRU_WIDTH + 2 * D_MODEL

LANES = 128
SUBLANES = 8
MXU_DIM = 256
VMEM_LIMIT = 56 * 1024 * 1024
VMEM_LIMIT_BIG = 60 * 1024 * 1024

ROW_TILE = 512
ROUTE_LANES = LANES
GRANULE = SUBLANES
EXPERT_ROWS = 512
HALF_D = D_MODEL // 2
U32 = jnp.uint32


def _cparams(n_axes, vmem=VMEM_LIMIT, flags=None):
    return pltpu.CompilerParams(dimension_semantics=("arbitrary",) * n_axes, vmem_limit_bytes=vmem, flags=flags)


def _rms(x, g):
    ms = jnp.mean(x * x, axis=-1, keepdims=True)
    return x * lax.rsqrt(ms + NORM_EPS) * g


def _sigmoid(x):
    return 1.0 / (1.0 + jnp.exp(-x))


def _dot(a, b):
    return jnp.dot(a, b, preferred_element_type=F32)


def _ada_kernel(c_ref, w_ref, b_ref, o_ref):
    c = c_ref[...]
    s = (c * _sigmoid(c)).astype(BF16)
    o_ref[...] = _dot(s, w_ref[...].astype(BF16)) + b_ref[...]


def _ada(c_all, w_ada, b_ada):
    r = c_all.shape[0]
    n = w_ada.shape[1]
    return pl.pallas_call(
        _ada_kernel,
        grid=(n // D_MODEL,),
        in_specs=[pl.BlockSpec((r, D_MODEL), lambda j: (0, 0)),
                  pl.BlockSpec((D_MODEL, D_MODEL), lambda j: (0, j)),
                  pl.BlockSpec((1, D_MODEL), lambda j: (0, j))],
        out_specs=pl.BlockSpec((r, D_MODEL), lambda j: (0, j)),
        out_shape=jax.ShapeDtypeStruct((r, n), F32),
        compiler_params=_cparams(1),
        name="ada_mod",
    )(c_all, w_ada, b_ada)


_O1 = Q_DIM
_O2 = _O1 + KV_DIM
_O3 = _O2 + KV_DIM
_O4 = _O3 + LRU_WIDTH
_O5 = _O4 + LRU_WIDTH
_O6 = _O5 + D_MODEL


def _prenorm(x_ref, g_ref, sc_ref, sh_ref):
    h = _rms(x_ref[...], g_ref[...]) * (1.0 + sc_ref[...]) + sh_ref[...]
    return h.astype(BF16)


PIECE_COLS = 256
SOFTMAX_SCALE = HEAD_DIM ** -0.5
assert SOFTMAX_SCALE == 0.125


def _plain_pieces(h_ref, w_ref, base, width, store):
    def piece(c0):
        def run():
            store(c0, _dot(h_ref[...], w_ref[:, base + c0:base + c0 + PIECE_COLS]))
        return run
    return [piece(c0) for c0 in range(0, width, PIECE_COLS)]


def _qkv_gate_pieces(h_ref, w_ref, cos_ref, sin_ref, q_ref, k_ref, v_ref, sga_ref, sgr_ref, last_refs=None):
    def rope(t):
        cos = cos_ref[...]
        sin = sin_ref[...]
        lane = lax.broadcasted_iota(I32, cos.shape, 1)
        first_half = (lane % HEAD_DIM) < (HEAD_DIM // 2)
        rot = jnp.where(first_half, pltpu.roll(t, LANES - HEAD_DIM // 2, 1), pltpu.roll(t, HEAD_DIM // 2, 1))
        return t * cos + rot * sin

    def q_piece(c0):
        def run():
            qf = _dot(h_ref[...], w_ref[:, c0:c0 + PIECE_COLS])
            for c in range(PIECE_COLS // LANES):
                q_ref[:, c0 + c * LANES:c0 + (c + 1) * LANES] = (
                    rope(qf[:, c * LANES:(c + 1) * LANES]) * SOFTMAX_SCALE).astype(BF16)
        return run

    def kv_piece():
        kv = _dot(h_ref[...], w_ref[:, _O1:_O3])
        for c in range(KV_DIM // LANES):
            k_ref[:, c * LANES:(c + 1) * LANES] = rope(kv[:, c * LANES:(c + 1) * LANES])
        v_ref[...] = kv[:, KV_DIM:]
        if last_refs is not None:
            t = k_ref.shape[0]
            last_refs[0][...] = k_ref[t - WINDOW:t, :]
            last_refs[1][...] = v_ref[t - WINDOW:t, :]

    def gate_piece(o_ref, base, c0):
        def run():
            z = _dot(h_ref[...], w_ref[:, base + c0:base + c0 + PIECE_COLS])
            o_ref[:, c0:c0 + PIECE_COLS] = _sigmoid(z).astype(BF16)
        return run

    pieces = [q_piece(c0) for c0 in range(0, Q_DIM, PIECE_COLS)] + [kv_piece]
    pieces += [gate_piece(sga_ref, _O5, c0) for c0 in range(0, D_MODEL, PIECE_COLS)]
    pieces += [gate_piece(sgr_ref, _O6, c0) for c0 in range(0, D_MODEL, PIECE_COLS)]
    return pieces


def _inproj_kernel(x_ref, sc_ref, sh_ref, g_ref, cos_ref, sin_ref, w_ref,
                   q_ref, k_ref, v_ref, xl_ref, yl_ref, sga_ref, sgr_ref, hbuf):
    hbuf[...] = _prenorm(x_ref, g_ref, sc_ref, sh_ref)
    xl_ref[...] = _dot(hbuf[...], w_ref[:, _O3:_O4]).astype(BF16)
    yl_ref[...] = _dot(hbuf[...], w_ref[:, _O4:_O5]).astype(BF16)
    for piece in _qkv_gate_pieces(hbuf, w_ref, cos_ref, sin_ref, q_ref, k_ref, v_ref, sga_ref, sgr_ref):
        piece()


def _inproj(x, sc, sh, g, cos, sin, w_in, rows_per_mod, pos_tiles):
    n = x.shape[0]
    tm = min(ROW_TILE, n)
    if rows_per_mod:
        tiles_per_mod = rows_per_mod // tm
        mod_spec = pl.BlockSpec((None, 1, D_MODEL), lambda i: (i // tiles_per_mod, 0, 0))
    else:
        mod_spec = pl.BlockSpec((tm, D_MODEL), lambda i: (i, 0))
    row = lambda w: pl.BlockSpec((tm, w), lambda i: (i, 0))
    outs = [(Q_DIM, BF16), (KV_DIM, F32), (KV_DIM, F32), (LRU_WIDTH, BF16), (LRU_WIDTH, BF16),
            (D_MODEL, BF16), (D_MODEL, BF16)]
    return pl.pallas_call(
        _inproj_kernel,
        grid=(n // tm,),
        in_specs=[row(D_MODEL), mod_spec, mod_spec,
                  pl.BlockSpec((1, D_MODEL), lambda i: (0, 0)),
                  pl.BlockSpec((tm, LANES), lambda i: (i % pos_tiles, 0)),
                  pl.BlockSpec((tm, LANES), lambda i: (i % pos_tiles, 0)),
                  pl.BlockSpec((D_MODEL, IN_DIM), lambda i: (0, 0))],
        out_specs=[row(w) for w, _ in outs],
        out_shape=[jax.ShapeDtypeStruct((n, w), dt) for w, dt in outs],
        scratch_shapes=[pltpu.VMEM((tm, D_MODEL), BF16)],
        compiler_params=_cparams(1),
        name="in_proj",
    )(x, sc, sh, g, cos, sin, w_in)


def _head_masks(shape):
    lane = lax.broadcasted_iota(I32, shape, 1)
    return [(lane // HEAD_DIM) == h for h in range(N_KV_HEADS)]


def _attention_core(q_perm, kall, vall, valid, sink_of, rows):
    masks_b = _head_masks((rows, KV_DIM))
    zero_b = jnp.zeros((rows, KV_DIM), BF16)
    pieces = []
    for h in range(N_KV_HEADS):
        for g in range(GQA_GROUP):
            pieces.append(jnp.where(masks_b[h], q_perm[g], zero_b))
    q_big = jnp.concatenate(pieces, axis=0)
    s_all = lax.dot_general(q_big, kall, (((1,), (1,)), ((), ())), preferred_element_type=F32)
    p_chunks, inv_chunks = [], []
    for h in range(N_KV_HEADS):
        for g in range(GQA_GROUP):
            c = h * GQA_GROUP + g
            s = jnp.where(valid, s_all[c * rows:(c + 1) * rows], NEG_INF)
            sink = sink_of(h, g)
            m = jnp.maximum(jnp.max(s, axis=-1, keepdims=True), sink)
            p = jnp.exp(s - m)
            denom = jnp.sum(p, axis=-1, keepdims=True) + jnp.exp(sink - m)
            p_chunks.append(p.astype(BF16))
            inv_chunks.append(1.0 / denom)
    o_all = _dot(jnp.concatenate(p_chunks, axis=0), vall)
    outs = []
    for g in range(GQA_GROUP):
        c = (N_KV_HEADS - 1) * GQA_GROUP + g
        o_sel = o_all[c * rows:(c + 1) * rows]
        inv_sel = inv_chunks[c]
        for h in reversed(range(N_KV_HEADS - 1)):
            c = h * GQA_GROUP + g
            o_sel = jnp.where(masks_b[h], o_all[c * rows:(c + 1) * rows], o_sel)
            inv_sel = jnp.where(masks_b[h], inv_chunks[c], inv_sel)
        outs.append(o_sel * inv_sel)
    return outs


ATTN_Q_BLOCKS = 4


def _attention_tile(sink_ref, q_ref, kc_ref, kp_ref, vc_ref, vp_ref, o_ref, seq_start, between=None):
    kall = jnp.concatenate([kp_ref[...], kc_ref[...]], axis=0).astype(BF16)
    vall = jnp.concatenate([vp_ref[...], vc_ref[...]], axis=0).astype(BF16)
    qi = lax.broadcasted_iota(I32, (WINDOW, 2 * WINDOW), 0)
    kj = lax.broadcasted_iota(I32, (WINDOW, 2 * WINDOW), 1)
    dist = qi + WINDOW - kj
    in_window = (dist >= 0) & (dist <= WINDOW)
    for c in range(ATTN_Q_BLOCKS):
        rows = slice(c * WINDOW, (c + 1) * WINDOW)
        keys = slice(c * WINDOW, (c + 2) * WINDOW)
        valid = in_window & ((kj >= WINDOW) | jnp.logical_not(seq_start)) if c == 0 else in_window
        q_perm = [q_ref[rows, g * KV_DIM:(g + 1) * KV_DIM] for g in range(GQA_GROUP)]
        outs = _attention_core(q_perm, kall[keys], vall[keys], valid,
                               lambda h, g: sink_ref[h * GQA_GROUP + g], WINDOW)
        for g in range(GQA_GROUP):
            o_ref[rows, g * KV_DIM:(g + 1) * KV_DIM] = outs[g].astype(BF16)
        if between is not None:
            between(c)


def _attn_prompt_kernel(sink_ref, q_ref, kc_ref, kp_ref, vc_ref, vp_ref, o_ref):
    _attention_tile(sink_ref, q_ref, kc_ref, kp_ref, vc_ref, vp_ref, o_ref, pl.program_id(1) == 0)


def _attn_prompt(sinks, q, k, v, batch, seq):
    step_rows = ATTN_Q_BLOCKS * WINDOW
    nb = seq // step_rows
    cur = lambda w: pl.BlockSpec((step_rows, w), lambda b, j: (b * nb + j, 0))
    prev = lambda w: pl.BlockSpec(
        (WINDOW, w), lambda b, j: (jnp.maximum((b * nb + j) * ATTN_Q_BLOCKS - 1, 0), 0))
    return pl.pallas_call(
        _attn_prompt_kernel,
        grid=(batch, nb),
        in_specs=[pl.BlockSpec(memory_space=pltpu.SMEM),
                  cur(Q_DIM), cur(KV_DIM), prev(KV_DIM), cur(KV_DIM), prev(KV_DIM)],
        out_specs=cur(Q_DIM),
        out_shape=jax.ShapeDtypeStruct((batch * seq, Q_DIM), BF16),
        compiler_params=_cparams(2),
        name="attn_prompt",
    )(sinks, q, k, k, v, v)


SEQ_PER_STEP = 8


def _attn_sample_kernel(sink_ref, q_ref, kn_ref, vn_ref, kc_ref, vc_ref, o_ref, kbuf, vbuf, *, t_new):
    rows = GQA_GROUP * t_new

    @pl.when(pl.program_id(0) == 0)
    def _():
        kbuf[...] = jnp.zeros(kbuf.shape, F32)
        vbuf[...] = jnp.zeros(vbuf.shape, F32)

    ri = lax.broadcasted_iota(I32, (rows, 2 * WINDOW), 0)
    kj = lax.broadcasted_iota(I32, (rows, 2 * WINDOW), 1)
    tq = ri // GQA_GROUP
    valid = (kj >= tq) & (kj <= tq + WINDOW) & (kj < WINDOW + t_new)
    for s in range(SEQ_PER_STEP):
        kbuf[s, 0:t_new, :] = kn_ref[s]
        vbuf[s, 0:t_new, :] = vn_ref[s]
        kall = jnp.concatenate([kc_ref[s], kbuf[s]], axis=0).astype(BF16)
        vall = jnp.concatenate([vc_ref[s], vbuf[s]], axis=0).astype(BF16)
        qs = q_ref[s]
        masks_b = _head_masks((rows, KV_DIM))
        zero_b = jnp.zeros((rows, KV_DIM), BF16)
        q_big = jnp.concatenate([jnp.where(masks_b[h], qs, zero_b) for h in range(N_KV_HEADS)], axis=0)
        s_all = lax.dot_general(q_big, kall, (((1,), (1,)), ((), ())), preferred_element_type=F32)
        acc = jnp.zeros((rows, KV_DIM), F32)
        p_chunks, inv_chunks = [], []
        for h in range(N_KV_HEADS):
            sc = jnp.where(valid, s_all[h * rows:(h + 1) * rows], NEG_INF)
            sink = sink_ref[h]
            m = jnp.maximum(jnp.max(sc, axis=-1, keepdims=True), sink)
            p = jnp.exp(sc - m)
            denom = jnp.sum(p, axis=-1, keepdims=True) + jnp.exp(sink - m)
            p_chunks.append(p.astype(BF16))
            inv_chunks.append(1.0 / denom)
        o_all = _dot(jnp.concatenate(p_chunks, axis=0), vall)
        for h in range(N_KV_HEADS):
            acc = acc + jnp.where(masks_b[h], o_all[h * rows:(h + 1) * rows] * inv_chunks[h], 0.0)
        o_ref[s] = acc.astype(BF16)


def _attn_sample(sink_rows, q, kn, vn, kc, vc):
    nseq, rows, _ = q.shape
    t_new = kn.shape[1]
    sb = SEQ_PER_STEP
    blk = lambda r: pl.BlockSpec((sb, r, KV_DIM), lambda i: (i, 0, 0))
    return pl.pallas_call(
        functools.partial(_attn_sample_kernel, t_new=t_new),
        grid=(nseq // sb,),
        in_specs=[pl.BlockSpec((N_KV_HEADS, rows, 1), lambda i: (0, 0, 0)),
                  blk(rows), blk(t_new), blk(t_new), blk(WINDOW), blk(WINDOW)],
        out_specs=blk(rows),
        out_shape=jax.ShapeDtypeStruct((nseq, rows, KV_DIM), BF16),
        scratch_shapes=[pltpu.VMEM((sb, WINDOW, KV_DIM), F32), pltpu.VMEM((sb, WINDOW, KV_DIM), F32)],
        compiler_params=_cparams(1),
        name="attn_sample",
    )(sink_rows, q, kn, vn, kc, vc)


def _gelu_tanh(x):
    return 0.5 * x * (1.0 + jnp.tanh(0.7978845608028654 * (x + 0.044715 * x * x * x)))


def _lru_gates(xc, wbd_ref, ba, bx, lam):
    xcb = xc.astype(BF16)
    r_parts, i_parts = [], []
    for gidx in range(LRU_WIDTH // MXU_DIM):
        z = _dot(xcb[:, gidx * MXU_DIM:(gidx + 1) * MXU_DIM], wbd_ref[gidx])
        r_parts.append(z[:, :MXU_DIM])
        i_parts.append(z[:, MXU_DIM:])
    return _gate_math(jnp.concatenate(r_parts, axis=1), jnp.concatenate(i_parts, axis=1), xc, ba, bx, lam)


def _gate_math(zr, zi, xc, ba, bx, lam):
    r = _sigmoid(zr + ba)
    i = _sigmoid(zi + bx)
    softplus_neg_lam = jnp.maximum(-lam, 0.0) + jnp.log1p(jnp.exp(-jnp.abs(lam)))
    log_a = -LRU_C * r * softplus_neg_lam
    a = jnp.exp(log_a)
    y = 1.0 - a * a
    u = jnp.where(y > 0.0, y * lax.rsqrt(y), 0.0) * (i * xc)
    return a, u


LRU_CHUNK = 64


def _lru_chunk(c, xbuf, ybuf, hcar, ol_ref, wc_ref, bc_ref, wbd_ref, ba_ref, bx_ref, lam_ref):
    w = LRU_WIDTH
    n = LRU_CHUNK
    rows = slice(c * n, (c + 1) * n)
    r0 = SUBLANES + c * n
    xc = xbuf[r0:r0 + n, :] * wc_ref[CONV_WIDTH - 1:CONV_WIDTH, :] + bc_ref[...]
    for k in range(1, CONV_WIDTH):
        xc = xc + xbuf[r0 - k:r0 - k + n, :] * wc_ref[CONV_WIDTH - 1 - k:CONV_WIDTH - k, :]
    a, u = _lru_gates(xc, wbd_ref, ba_ref[...], bx_ref[...], lam_ref[...])

    ng = n // SUBLANES
    a3 = a.reshape(ng, SUBLANES, w)
    u3 = u.reshape(ng, SUBLANES, w)
    row = lax.broadcasted_iota(I32, (ng, SUBLANES, w), 1)
    d = 1
    while d < SUBLANES:
        a_s = jnp.where(row >= d, pltpu.roll(a3, d, 1), 1.0)
        u_s = jnp.where(row >= d, pltpu.roll(u3, d, 1), 0.0)
        u3 = a3 * u_s + u3
        a3 = a3 * a_s
        d *= 2
    carry = hcar[...]
    hs = []
    for gi in range(ng):
        hg = a3[gi] * carry + u3[gi]
        hs.append(hg)
        carry = hg[SUBLANES - 1:SUBLANES, :]
    hcar[...] = carry
    h = jnp.concatenate(hs, axis=0)
    ol_ref[rows, :] = (h * _gelu_tanh(ybuf[rows, :])).astype(BF16)


def _inproj_lru_kernel(x_ref, sc_ref, sh_ref, g_ref, cos_ref, sin_ref, w_ref,
                       wc_ref, bc_ref, wbd_ref, ba_ref, bx_ref, lam_ref,
                       q_ref, k_ref, v_ref, sga_ref, sgr_ref, klast_ref, vlast_ref, ol_ref, xtail_ref, hlast_ref,
                       hbuf, xbuf, ybuf, hcar, *, tiles_per_seq):
    t = x_ref.shape[0]

    @pl.when(pl.program_id(0) % tiles_per_seq == 0)
    def _():
        xbuf[0:SUBLANES, :] = jnp.zeros((SUBLANES, LRU_WIDTH), F32)
        hcar[...] = jnp.zeros((1, LRU_WIDTH), F32)

    hbuf[...] = _prenorm(x_ref, g_ref, sc_ref, sh_ref)

    xbuf[SUBLANES:SUBLANES + t, :] = _dot(hbuf[...], w_ref[:, _O3:_O4])
    ybuf[...] = _dot(hbuf[...], w_ref[:, _O4:_O5])

    pieces = _qkv_gate_pieces(hbuf, w_ref, cos_ref, sin_ref, q_ref, k_ref, v_ref, sga_ref, sgr_ref,
                              last_refs=(klast_ref, vlast_ref))
    n_chunks = t // LRU_CHUNK
    per_chunk = -(-len(pieces) // n_chunks)
    for c in range(n_chunks):
        _lru_chunk(c, xbuf, ybuf, hcar, ol_ref, wc_ref, bc_ref, wbd_ref, ba_ref, bx_ref, lam_ref)
        for piece in pieces[c * per_chunk:(c + 1) * per_chunk]:
            piece()

    tail = xbuf[t:t + SUBLANES, :]
    xtail_ref[...] = tail
    xbuf[0:SUBLANES, :] = tail
    hlast_ref[...] = hcar[...]


def _inproj_lru(x, sc, sh, g, cos, sin, w_in, wc, bc, wbd, ba, bx, lam, batch, seq):
    n = x.shape[0]
    tm = min(ROW_TILE, seq)
    tps = seq // tm
    mod_spec = pl.BlockSpec((None, 1, D_MODEL), lambda i: (i // tps, 0, 0))
    row = lambda w: pl.BlockSpec((tm, w), lambda i: (i, 0))
    full = lambda a: pl.BlockSpec(a.shape, lambda i: (0,) * a.ndim)
    per_seq = lambda r, w: pl.BlockSpec((None, r, w), lambda i: (i // tps, 0, 0))
    outs = [(Q_DIM, BF16), (KV_DIM, F32), (KV_DIM, F32), (D_MODEL, BF16), (D_MODEL, BF16)]
    return pl.pallas_call(
        functools.partial(_inproj_lru_kernel, tiles_per_seq=tps),
        grid=(n // tm,),
        in_specs=[row(D_MODEL), mod_spec, mod_spec, full(g),
                  pl.BlockSpec((tm, LANES), lambda i: (i % tps, 0)),
                  pl.BlockSpec((tm, LANES), lambda i: (i % tps, 0)),
                  full(w_in), full(wc), full(bc), full(wbd), full(ba), full(bx), full(lam)],
        out_specs=[row(w) for w, _ in outs] + [per_seq(WINDOW, KV_DIM), per_seq(WINDOW, KV_DIM)]
        + [row(LRU_WIDTH), per_seq(SUBLANES, LRU_WIDTH), per_seq(1, LRU_WIDTH)],
        out_shape=[jax.ShapeDtypeStruct((n, w), dt) for w, dt in outs]
        + [jax.ShapeDtypeStruct((batch, WINDOW, KV_DIM), F32)] * 2
        + [jax.ShapeDtypeStruct((n, LRU_WIDTH), BF16),
           jax.ShapeDtypeStruct((batch, SUBLANES, LRU_WIDTH), F32),
           jax.ShapeDtypeStruct((batch, 1, LRU_WIDTH), F32)],
        scratch_shapes=[pltpu.VMEM((tm, D_MODEL), BF16),
                        pltpu.VMEM((2 * SUBLANES + tm, LRU_WIDTH), F32), pltpu.VMEM((tm, LRU_WIDTH), F32),
                        pltpu.VMEM((1, LRU_WIDTH), F32)],
        compiler_params=_cparams(1, vmem=VMEM_LIMIT_BIG),
        name="in_proj_lru",
    )(x, sc, sh, g, cos, sin, w_in, wc, bc, wbd, ba, bx, lam)


def _lru_sample_kernel(xl_ref, yl_ref, cs_ref, h0_ref, wc_ref, bc_ref, wbd_ref, ba_ref, bx_ref, lam_ref,
                       o_ref, hlast_ref):
    t_new, nseq, w = xl_ref.shape
    xp = [cs_ref[k] for k in range(CONV_WIDTH - 1)] + [xl_ref[k].astype(F32) for k in range(t_new)]
    xcs = []
    for t in range(t_new):
        acc = bc_ref[...] + xp[t] * wc_ref[0:1, :]
        for k in range(1, CONV_WIDTH):
            acc = acc + xp[t + k] * wc_ref[k:k + 1, :]
        xcs.append(acc)
    xc = jnp.concatenate(xcs, axis=0)
    a, u = _lru_gates(xc, wbd_ref, ba_ref[...], bx_ref[...], lam_ref[...])
    h = h0_ref[...]
    for t in range(t_new):
        h = a[t * nseq:(t + 1) * nseq] * h + u[t * nseq:(t + 1) * nseq]
        o_ref[t] = (h * _gelu_tanh(yl_ref[t].astype(F32))).astype(BF16)
    hlast_ref[...] = h


def _lru_sample(xl, yl, cs, h0, wc, bc, wbd, ba, bx, lam):
    t_new, nseq, w = xl.shape
    full = lambda shp: pl.BlockSpec(shp, lambda i: (0,) * len(shp))
    args = (xl, yl, cs, h0, wc, bc, wbd, ba, bx, lam)
    return pl.pallas_call(
        _lru_sample_kernel,
        grid=(1,),
        in_specs=[full(a.shape) for a in args],
        out_specs=[full((t_new, nseq, w)), full((nseq, w))],
        out_shape=[jax.ShapeDtypeStruct((t_new, nseq, w), BF16), jax.ShapeDtypeStruct((nseq, w), F32)],
        compiler_params=_cparams(1),
        name="lru_sample",
    )(*args)


def _post_kernel(oa_ref, ol_ref, sga_ref, sgr_ref, x_ref, gta_ref, scf_ref, shf_ref,
                 wab_ref, wlb_ref, wout_ref, gpm_ref, gpf_ref, wrp_ref, wrt_ref, br_ref, tri_ref, cin_ref,
                 x1_ref, h2_ref, route_ref, stat_ref, cnt_ref, carry):
    @pl.when(pl.program_id(0) == 0)
    def _():
        carry[...] = cin_ref[...]

    b_attn = _dot(oa_ref[...], wab_ref[...])
    b_lru = _dot(ol_ref[...], wlb_ref[...])
    _post_tail(b_attn, b_lru, sga_ref, sgr_ref, x_ref, gta_ref, scf_ref, shf_ref, wout_ref, gpm_ref, gpf_ref,
               wrp_ref, wrt_ref, br_ref, tri_ref, x1_ref, h2_ref, route_ref, stat_ref, cnt_ref, carry)


def _post_attn_kernel(sink_ref, q_ref, kc_ref, kp_ref, vc_ref, vp_ref,
                      ol_ref, sga_ref, sgr_ref, x_ref, gta_ref, scf_ref, shf_ref,
                      wab_ref, wlb_ref, wout_ref, gpm_ref, gpf_ref, wrp_ref, wrt_ref, br_ref, tri_ref, cin_ref,
                      x1_ref, h2_ref, route_ref, stat_ref, cnt_ref, carry, oabuf, blbuf, *, tiles_per_seq):
    i = pl.program_id(0)

    @pl.when(i == 0)
    def _():
        carry[...] = cin_ref[...]

    cols = D_MODEL // ATTN_Q_BLOCKS

    def lru_piece(c):
        blbuf[:, c * cols:(c + 1) * cols] = _dot(ol_ref[...], wlb_ref[:, c * cols:(c + 1) * cols])

    _attention_tile(sink_ref, q_ref, kc_ref, kp_ref, vc_ref, vp_ref, oabuf, i % tiles_per_seq == 0, lru_piece)
    b_attn = _dot(oabuf[...], wab_ref[...])
    _post_tail(b_attn, blbuf[...], sga_ref, sgr_ref, x_ref, gta_ref, scf_ref, shf_ref, wout_ref, gpm_ref, gpf_ref,
               wrp_ref, wrt_ref, br_ref, tri_ref, x1_ref, h2_ref, route_ref, stat_ref, cnt_ref, carry)


def _post_tail(b_attn, b_lru, sga_ref, sgr_ref, x_ref, gta_ref, scf_ref, shf_ref, wout_ref, gpm_ref, gpf_ref,
               wrp_ref, wrt_ref, br_ref, tri_ref, x1_ref, h2_ref, route_ref, stat_ref, cnt_ref, carry):
    merged = sga_ref[...].astype(F32) * b_attn + sgr_ref[...].astype(F32) * b_lru
    mix = _dot(merged.astype(BF16), wout_ref[...])
    x1 = x_ref[...] + gta_ref[...] * _rms(mix, gpm_ref[...])
    x1_ref[...] = x1
    h2 = _rms(x1, gpf_ref[...]) * (1.0 + scf_ref[...]) + shf_ref[...]
    h2_ref[...] = h2.astype(BF16)

    h_hi = h2.astype(BF16)
    h_lo = (h2 - h_hi.astype(F32)).astype(BF16)
    hi_terms = _dot(h_hi, wrp_ref[...])
    logits = (hi_terms[:, :ROUTE_LANES] + (hi_terms[:, ROUTE_LANES:] + _dot(h_lo, wrt_ref[...]))) + br_ref[...]

    tm = logits.shape[0]
    lane = lax.broadcasted_iota(I32, (tm, ROUTE_LANES), 1)
    big = jnp.int32(ROUTE_LANES)
    is_g = lane < N_GROUPS
    lg = jnp.where(is_g, logits, NEG_INF)
    mg = jnp.max(lg, axis=-1, keepdims=True)
    g_star = jnp.min(jnp.where(lg == mg, lane, big), axis=-1, keepdims=True)
    p_star = 1.0 / jnp.sum(jnp.where(is_g, jnp.exp(lg - mg), 0.0), axis=-1, keepdims=True)
    lo = N_GROUPS + g_star * EXPERTS_PER_GROUP
    in_grp = (lane >= lo) & (lane < lo + EXPERTS_PER_GROUP)
    le = jnp.where(in_grp, logits, NEG_INF)
    m1 = jnp.max(le, axis=-1, keepdims=True)
    i1 = jnp.min(jnp.where(le == m1, lane, big), axis=-1, keepdims=True)
    le2 = jnp.where(lane == i1, NEG_INF, le)
    m2 = jnp.max(le2, axis=-1, keepdims=True)
    i2 = jnp.min(jnp.where(le2 == m2, lane, big), axis=-1, keepdims=True)
    e2x = jnp.exp(m2 - m1)
    wsum = 1.0 + e2x
    w1 = (1.0 / wsum) * p_star
    w2 = (e2x / wsum) * p_star

    oh1 = lane == i1
    oh2 = lane == i2
    cnt = jnp.where(oh1 | oh2, 1.0, 0.0)
    excl = _dot(tri_ref[...], cnt.astype(BF16))
    per_e = jnp.sum(cnt, axis=0, keepdims=True)
    pad8 = jnp.floor((per_e + (GRANULE - 1.0)) * (1.0 / GRANULE)) * GRANULE
    incl = jnp.broadcast_to(pad8, (SUBLANES, ROUTE_LANES))
    lane8 = lax.broadcasted_iota(I32, (SUBLANES, ROUTE_LANES), 1)
    d = 1
    while d < ROUTE_LANES:
        incl = incl + jnp.where(lane8 >= d, pltpu.roll(incl, d, 1), 0.0)
        d *= 2
    seg_start = incl[0:1, :] - pad8
    pos = excl + seg_start
    s1 = jnp.sum(jnp.where(oh1, pos, 0.0), axis=-1, keepdims=True)
    s2 = jnp.sum(jnp.where(oh2, pos, 0.0), axis=-1, keepdims=True)
    rec = jnp.where(lane == 0, s1, 0.0)
    rec = jnp.where(lane == 1, s2, rec)
    rec = jnp.where(lane == 2, w1, rec)
    rec = jnp.where(lane == 3, w2, rec)
    route_ref[...] = rec
    srow = lax.broadcasted_iota(I32, (SUBLANES, ROUTE_LANES), 0)
    stat_ref[...] = jnp.where(srow == 0, pad8, jnp.where(srow == 1, carry[...], 0.0))
    carry[...] = carry[...] + pad8
    cnt_ref[...] = carry[...]


def _post(oa, ol, sga, sgr, x, gta, scf, shf, wab, wlb, wout, gpm, gpf, wrh, wrl, br, tri, cin, rows_per_mod):
    n = x.shape[0]
    tm = min(ROW_TILE, n)
    if rows_per_mod:
        tiles_per_mod = rows_per_mod // tm
        mod_spec = pl.BlockSpec((None, 1, D_MODEL), lambda i: (i // tiles_per_mod, 0, 0))
    else:
        mod_spec = pl.BlockSpec((tm, D_MODEL), lambda i: (i, 0))
    row = lambda w: pl.BlockSpec((tm, w), lambda i: (i, 0))
    full = lambda a: pl.BlockSpec(a.shape, lambda i: (0,) * a.ndim)
    if isinstance(oa, tuple):
        assert tm == ATTN_Q_BLOCKS * WINDOW and rows_per_mod
        sinks, q, k, v = oa
        prev = pl.BlockSpec((WINDOW, KV_DIM), lambda i: (jnp.maximum(i * ATTN_Q_BLOCKS - 1, 0), 0))
        kernel = functools.partial(_post_attn_kernel, tiles_per_seq=rows_per_mod // tm)
        first_specs = [pl.BlockSpec(memory_space=pltpu.SMEM), row(Q_DIM), row(KV_DIM), prev, row(KV_DIM), prev]
        first_args = (sinks, q, k, k, v, v)
        scratch = [pltpu.VMEM((tm, Q_DIM), BF16), pltpu.VMEM((tm, D_MODEL), F32)]
    else:
        kernel, first_specs, first_args, scratch = _post_kernel, [row(Q_DIM)], (oa,), []
    return pl.pallas_call(
        kernel,
        grid=(n // tm,),
        in_specs=first_specs + [row(LRU_WIDTH), row(D_MODEL), row(D_MODEL), row(D_MODEL),
                  mod_spec, mod_spec, mod_spec,
                  full(wab), full(wlb), full(wout), full(gpm), full(gpf), full(wrh), full(wrl), full(br),
                  full(tri), full(cin)],
        out_specs=[row(D_MODEL), row(D_MODEL), row(ROUTE_LANES),
                   pl.BlockSpec((SUBLANES, ROUTE_LANES), lambda i: (i, 0)),
                   pl.BlockSpec((1, ROUTE_LANES), lambda i: (0, 0))],
        out_shape=[jax.ShapeDtypeStruct((n, D_MODEL), F32), jax.ShapeDtypeStruct((n, D_MODEL), BF16),
                   jax.ShapeDtypeStruct((n, ROUTE_LANES), F32),
                   jax.ShapeDtypeStruct((n // tm * SUBLANES, ROUTE_LANES), F32),
                   jax.ShapeDtypeStruct((1, ROUTE_LANES), F32)],
        scratch_shapes=[pltpu.VMEM((1, ROUTE_LANES), F32)] + scratch,
        compiler_params=_cparams(1),
        name="post_mix",
    )(*first_args, ol, sga, sgr, x, gta, scf, shf, wab, wlb, wout, gpm, gpf, wrh, wrl, br, tri, cin)


BLOCK_GRANULES = EXPERT_ROWS // GRANULE
BLOCK_SHIFT = BLOCK_GRANULES.bit_length() - 1
assert BLOCK_GRANULES == 1 << BLOCK_SHIFT


def _padded(c):
    return ((c + (BLOCK_GRANULES - 1)) >> BLOCK_SHIFT) << BLOCK_SHIFT


def _sorted_rows(tm):
    r = 2 * tm + N_EXPERTS * (GRANULE - 1)
    return -(-r // MXU_DIM) * MXU_DIM


def _plan_kernel(tot_ref, be_ref, meta_ref, *, n_blocks):
    def fill(j, _):
        be_ref[j] = N_EXPERTS - 1
        return 0
    lax.fori_loop(0, n_blocks, fill, 0)

    def per_expert(e, nb):
        k = _padded(tot_ref[e]) >> BLOCK_SHIFT

        def put(b, _):
            be_ref[nb + b] = e
            return 0
        lax.fori_loop(0, k, put, 0)
        return nb + k
    n_active = lax.fori_loop(0, N_EXPERTS, per_expert, 0)
    meta_ref[0] = n_active


def _plan(totals, n_blocks):
    return pl.pallas_call(
        functools.partial(_plan_kernel, n_blocks=n_blocks),
        in_specs=[pl.BlockSpec(memory_space=pltpu.SMEM)],
        out_specs=[pl.BlockSpec(memory_space=pltpu.SMEM), pl.BlockSpec(memory_space=pltpu.SMEM)],
        out_shape=[jax.ShapeDtypeStruct((n_blocks,), I32), jax.ShapeDtypeStruct((1,), I32)],
        name="moe_plan",
    )(totals)


def _expert_starts(tot_ref, pstart):
    def body(e, acc):
        pstart[e] = acc
        return acc + _padded(tot_ref[e])
    return lax.fori_loop(0, N_EXPERTS, body, 0)


def _granule(ref, g):
    return ref.at[pl.ds(pl.multiple_of(g * GRANULE, GRANULE), GRANULE)]


def _pack_halves(lo_f32, hi_f32):
    return (pltpu.bitcast(lo_f32, U32) >> 16) | (pltpu.bitcast(hi_f32, U32) & jnp.uint32(0xFFFF0000))


def _unpack_halves(packed):
    lo = pltpu.bitcast(packed << 16, F32).astype(BF16)
    hi = pltpu.bitcast(packed & jnp.uint32(0xFFFF0000), F32).astype(BF16)
    return lo, hi


def _sort_kernel(tot_ref, cnt_ref, goff_ref, rec_p_ref, h2_p_ref, rec_s_ref, h2_s_ref,
                 srt_ref, gsrc_ref, gslot_ref, pstart, *, tiles_p, n_sorted, n_slots):
    i = pl.program_id(0)
    from_sample = i >= tiles_p
    rec = jnp.where(from_sample, rec_s_ref[...], rec_p_ref[...])
    h2 = jnp.where(from_sample, h2_s_ref[...], h2_p_ref[...])
    tm = h2.shape[0]

    rec_t = rec.T
    s1 = rec_t[0:1, :].astype(I32)
    s2 = rec_t[1:2, :].astype(I32)
    rows = lax.broadcasted_iota(I32, (n_sorted, tm), 0)
    sel = jnp.where((rows == s1) | (rows == s2), 1.0, 0.0).astype(BF16)
    srt = _dot(sel, h2)
    srt_ref[...] = _pack_halves(srt[:, :HALF_D], srt[:, HALF_D:])

    zero_granule = n_sorted // GRANULE - 1

    @pl.when(i == 0)
    def _():
        used = _expert_starts(tot_ref, pstart)

        def put_zero(g, _):
            gsrc_ref[g] = zero_granule
            return 0

        def per_expert(e, _):
            total = tot_ref[e]
            lax.fori_loop(pstart[e] + total, pstart[e] + _padded(total), put_zero, 0)
            return 0
        lax.fori_loop(0, N_EXPERTS, per_expert, 0)
        lax.fori_loop(used, n_slots // GRANULE, put_zero, 0)

    tile_granules = n_sorted // GRANULE

    def per_expert(e, seg):
        k = cnt_ref[i * N_EXPERTS + e]
        dst = pstart[e] + goff_ref[i * N_EXPERTS + e]
        src = i * tile_granules + seg

        def put(g, _):
            gsrc_ref[dst + g] = src + g
            gslot_ref[src + g] = dst + g
            return 0
        lax.fori_loop(0, k, put, 0)
        return seg + k
    used_in_tile = lax.fori_loop(0, N_EXPERTS, per_expert, 0, unroll=4)

    def put_any(q, _):
        gslot_ref[i * tile_granules + q] = 0
        return 0
    lax.fori_loop(used_in_tile, tile_granules, put_any, 0)


def _sort(totals, cnt8, goff, rec_p, h2_p, rec_s, h2_s, n_slots):
    tm = min(ROW_TILE, h2_p.shape[0])
    assert h2_s.shape[0] % tm == 0
    tiles_p = h2_p.shape[0] // tm
    tiles_s = h2_s.shape[0] // tm
    n_sorted = _sorted_rows(tm)
    row_p = lambda w: pl.BlockSpec((tm, w), lambda i, *_: (jnp.minimum(i, tiles_p - 1), 0))
    row_s = lambda w: pl.BlockSpec((tm, w), lambda i, *_: (jnp.maximum(i - tiles_p, 0), 0))
    return pl.pallas_call(
        functools.partial(_sort_kernel, tiles_p=tiles_p, n_sorted=n_sorted, n_slots=n_slots),
        grid_spec=pltpu.PrefetchScalarGridSpec(
            num_scalar_prefetch=3,
            grid=(tiles_p + tiles_s,),
            in_specs=[row_p(ROUTE_LANES), row_p(D_MODEL), row_s(ROUTE_LANES), row_s(D_MODEL)],
            out_specs=[pl.BlockSpec((n_sorted, HALF_D), lambda i, *_: (i, 0)),
                       pl.BlockSpec(memory_space=pltpu.SMEM), pl.BlockSpec(memory_space=pltpu.SMEM)],
            scratch_shapes=[pltpu.SMEM((N_EXPERTS,), I32)]),
        out_shape=[jax.ShapeDtypeStruct(((tiles_p + tiles_s) * n_sorted, HALF_D), U32),
                   jax.ShapeDtypeStruct((n_slots // GRANULE,), I32),
                   jax.ShapeDtypeStruct(((tiles_p + tiles_s) * n_sorted // GRANULE,), I32)],
        compiler_params=_cparams(1),
        name="moe_sort",
    )(totals, cnt8, goff, rec_p, h2_p, rec_s, h2_s)


def _expert_kernel(be_ref, meta_ref, gsrc_ref, srt_hbm, wg_hbm, wu_hbm, wd_hbm, ys_ref,
                   xbuf, wgs, wus, wds, wgb, wub, wdb, stage, sems, wsems):
    j = pl.program_id(0)
    n_active = meta_ref[0]
    gran_per_block = BLOCK_GRANULES

    def granule_copy(blk, g, slot):
        return pltpu.make_async_copy(_granule(srt_hbm, gsrc_ref[blk * gran_per_block + g]),
                                     xbuf.at[slot, pl.ds(g * GRANULE, GRANULE)], sems.at[slot])

    def gather(blk, slot):
        for g in range(gran_per_block):
            granule_copy(blk, g, slot).start(priority=g % 2)

    def drain(slot):
        for g in range(gran_per_block):
            granule_copy(0, g, slot).wait()

    def weight_copies(e, p):
        return [pltpu.make_async_copy(wg_hbm.at[e], wgs.at[p], wsems.at[p]),
                pltpu.make_async_copy(wu_hbm.at[e], wus.at[p], wsems.at[p]),
                pltpu.make_async_copy(wd_hbm.at[e], wds.at[p], wsems.at[p])]

    @pl.when(j == 0)
    def _():
        gather(0, 0)
        stage[0] = 0
        for cp in weight_copies(be_ref[0], 0):
            cp.start()

    @pl.when(j < n_active)
    def _():
        slot = j % 2
        e = be_ref[j]

        @pl.when((j == 0) | (e != be_ref[jnp.maximum(j - 1, 0)]))
        def _():
            p = stage[0]
            for cp in weight_copies(e, p):
                cp.wait()
            wgb[...] = wgs[p].astype(BF16)
            wub[...] = wus[p].astype(BF16)
            wdb[...] = wds[p].astype(BF16)
            nxt = lax.while_loop(lambda k: (k < n_active) & (be_ref[jnp.minimum(k, n_active - 1)] == e),
                                 lambda k: k + 1, j + 1)

            @pl.when(nxt < n_active)
            def _():
                for cp in weight_copies(be_ref[nxt], 1 - p):
                    cp.start()
            stage[0] = 1 - p

        gather(jnp.minimum(j + 1, n_active - 1), 1 - slot)
        drain(slot)
        x_lo, x_hi = _unpack_halves(xbuf[slot])
        g = _dot(x_lo, wgb[0:HALF_D, :]) + _dot(x_hi, wgb[HALF_D:D_MODEL, :])
        u = _dot(x_lo, wub[0:HALF_D, :]) + _dot(x_hi, wub[HALF_D:D_MODEL, :])
        hmid = (g * _sigmoid(g) * u).astype(BF16)
        y = _dot(hmid, wdb[...])
        ys_ref[...] = _pack_halves(y[:, :HALF_D].astype(BF16).astype(F32), y[:, HALF_D:].astype(BF16).astype(F32))

        @pl.when(j == n_active - 1)
        def _():
            drain(1 - slot)

    @pl.when(j >= meta_ref[0])
    def _():
        ys_ref[...] = jnp.zeros(ys_ref.shape, U32)


def _experts(block_e, meta, gsrc, srt, wg, wu, wd, n_slots):
    n_blocks = n_slots // EXPERT_ROWS
    anyspec = pl.BlockSpec(memory_space=pl.ANY)
    return pl.pallas_call(
        _expert_kernel,
        grid_spec=pltpu.PrefetchScalarGridSpec(
            num_scalar_prefetch=3,
            grid=(n_blocks,),
            in_specs=[anyspec, anyspec, anyspec, anyspec],
            out_specs=pl.BlockSpec((EXPERT_ROWS, HALF_D), lambda j, be, meta, gs: (j, 0)),
            scratch_shapes=[pltpu.VMEM((2, EXPERT_ROWS, HALF_D), U32),
                            pltpu.VMEM((2, D_MODEL, D_EXPERT), F32), pltpu.VMEM((2, D_MODEL, D_EXPERT), F32),
                            pltpu.VMEM((2, D_EXPERT, D_MODEL), F32),
                            pltpu.VMEM((D_MODEL, D_EXPERT), BF16), pltpu.VMEM((D_MODEL, D_EXPERT), BF16),
                            pltpu.VMEM((D_EXPERT, D_MODEL), BF16),
                            pltpu.SMEM((1,), I32), pltpu.SemaphoreType.DMA((2,)), pltpu.SemaphoreType.DMA((2,))]),
        out_shape=jax.ShapeDtypeStruct((n_slots, HALF_D), U32),
        compiler_params=_cparams(1),
        name="moe_experts",
    )(block_e, meta, gsrc, srt, wg, wu, wd)


def _combine_kernel(gslot_ref, rec_ref, x1_ref, gtf_ref, g_ref, ys_hbm, y_ref, cbuf, sems, *, tile_base, n_tiles):
    i = pl.program_id(0)
    slot = i % 2
    tm = x1_ref.shape[0]
    n_sorted = cbuf.shape[1]
    tile_granules = n_sorted // GRANULE

    def granule_copy(slot_granule, q, slot_):
        return pltpu.make_async_copy(_granule(ys_hbm, slot_granule),
                                     cbuf.at[slot_, pl.ds(q * GRANULE, GRANULE)], sems.at[slot_])

    def gather(tile, slot_):
        for q in range(tile_granules):
            granule_copy(gslot_ref[tile * tile_granules + q], q, slot_).start(priority=q % 2)

    def drain(slot_):
        for q in range(tile_granules):
            granule_copy(0, q, slot_).wait()

    @pl.when(i == 0)
    def _():
        gather(tile_base, 0)

    gather(tile_base + jnp.minimum(i + 1, n_tiles - 1), 1 - slot)
    drain(slot)

    rec = rec_ref[...]
    s1 = rec[:, 0:1].astype(I32)
    s2 = rec[:, 1:2].astype(I32)
    col = lax.broadcasted_iota(I32, (tm, n_sorted), 1)
    wmat = (jnp.where(col == s1, rec[:, 2:3], 0.0) + jnp.where(col == s2, rec[:, 3:4], 0.0)).astype(BF16)
    y_lo, y_hi = _unpack_halves(cbuf[slot])
    f = jnp.concatenate([_dot(wmat, y_lo), _dot(wmat, y_hi)], axis=1)
    y_ref[...] = x1_ref[...] + gtf_ref[...] * _rms(f, g_ref[...])

    @pl.when(i == n_tiles - 1)
    def _():
        drain(1 - slot)


def _combine(gslot, rec, x1, gtf, g, ys, rows_per_mod, tile_base):
    n = x1.shape[0]
    tm = min(ROW_TILE, n)
    if rows_per_mod:
        tiles_per_mod = rows_per_mod // tm
        mod_spec = pl.BlockSpec((None, 1, D_MODEL), lambda i, *_: (i // tiles_per_mod, 0, 0))
    else:
        mod_spec = pl.BlockSpec((tm, D_MODEL), lambda i, *_: (i, 0))
    row = lambda w: pl.BlockSpec((tm, w), lambda i, *_: (i, 0))
    return pl.pallas_call(
        functools.partial(_combine_kernel, tile_base=tile_base, n_tiles=n // tm),
        grid_spec=pltpu.PrefetchScalarGridSpec(
            num_scalar_prefetch=1,
            grid=(n // tm,),
            in_specs=[row(ROUTE_LANES), row(D_MODEL), mod_spec,
                      pl.BlockSpec((1, D_MODEL), lambda i, *_: (0, 0)),
                      pl.BlockSpec(memory_space=pl.ANY)],
            out_specs=row(D_MODEL),
            scratch_shapes=[pltpu.VMEM((2, _sorted_rows(tm), HALF_D), U32), pltpu.SemaphoreType.DMA((2,))]),
        out_shape=jax.ShapeDtypeStruct((n, D_MODEL), F32),
        compiler_params=_cparams(1),
        name="moe_combine",
    )(gslot, rec, x1, gtf, g, ys)


def _rope_tables(pos):
    half = HEAD_DIM // 2
    inv = jnp.power(jnp.float32(ROPE_THETA), -jnp.arange(half, dtype=F32) / half)
    ang = pos.astype(F32)[:, None] * inv[None, :]
    cos = jnp.cos(ang)
    sin = jnp.sin(ang)
    reps = LANES // HEAD_DIM
    cos_t = jnp.tile(jnp.concatenate([cos, cos], axis=-1), (1, reps))
    sin_t = jnp.tile(jnp.concatenate([-sin, sin], axis=-1), (1, reps))
    return cos_t, sin_t


def _q_perm_index():
    g = jnp.arange(GQA_GROUP)[:, None, None]
    h = jnp.arange(N_KV_HEADS)[None, :, None]
    d = jnp.arange(HEAD_DIM)[None, None, :]
    return ((h * GQA_GROUP + g) * HEAD_DIM + d).reshape(-1)


def _block_diag_gates(w_a, w_x):
    per = MXU_DIM // LRU_BLOCK_W
    groups = LRU_BLOCKS // per

    def bd(w):
        w = w.reshape(groups, per, LRU_BLOCK_W, LRU_BLOCK_W)
        eye = jnp.eye(per, dtype=w.dtype)
        full = jnp.einsum('gpij,pq->gpiqj', w, eye)
        return full.reshape(groups, MXU_DIM, MXU_DIM)
    return jnp.concatenate([bd(w_a), bd(w_x)], axis=-1).astype(BF16)


def _layer_forward(xp, xs_tm, ck, cv, cconv, ch, mod_p, mod_s, p):
    batch, seq, _ = xp.shape
    nseq, _, _, _ = ck.shape
    t_new = xs_tm.shape[0] // nseq
    n_p = batch * seq
    n_s = xs_tm.shape[0]

    perm = _q_perm_index()
    w_in = p['w_in']
    w_in_b = jnp.concatenate([w_in[:, :Q_DIM][:, perm], w_in[:, Q_DIM:]], axis=1).astype(BF16)
    wab = p['w_attn_branch'][perm, :].astype(BF16)
    wlb = p['w_lru_branch'].astype(BF16)
    wout = p['w_out'].astype(BF16)
    wbd = _block_diag_gates(p['w_lru_a'], p['w_lru_x'])
    row = lambda v: v.reshape(1, -1)
    wr = jnp.concatenate([p['w_router_group'], p['w_router_expert'],
                          jnp.zeros((D_MODEL, ROUTE_LANES - N_GROUPS - N_EXPERTS), F32)], axis=1)
    wr_top = wr.astype(BF16)
    wr_pair = jnp.concatenate([wr_top, (wr - wr_top.astype(F32)).astype(BF16)], axis=1)
    br = jnp.concatenate([p['b_router_group'], p['b_router_expert'],
                          jnp.zeros((ROUTE_LANES - N_GROUPS - N_EXPERTS,), F32)]).reshape(1, -1)
    wg = p['w_exp_gate']
    wu = p['w_exp_up']
    wd = p['w_exp_down']

    def mods(mod):
        return [mod[:, k * D_MODEL:(k + 1) * D_MODEL] for k in range(6)]
    sh_a_p, sc_a_p, gt_a_p, sh_f_p, sc_f_p, gt_f_p = [m.reshape(batch, 1, D_MODEL) for m in mods(mod_p)]
    sh_a_s, sc_a_s, gt_a_s, sh_f_s, sc_f_s, gt_f_s = [jnp.tile(m, (t_new, 1)) for m in mods(mod_s)]

    lru_w = (p['w_conv'], row(p['b_conv']), wbd, row(p['b_lru_a']), row(p['b_lru_x']), row(p['lru_lambda']))
    cos_p, sin_p = _rope_tables(jnp.arange(seq, dtype=I32))
    q_p, k_p, v_p, sga_p, sgr_p, klast_p, vlast_p, ol_p, xtail_p, hlast_p = _inproj_lru(
        xp.reshape(n_p, D_MODEL), sc_a_p, sh_a_p, row(p['g_pre_mix']), cos_p, sin_p, w_in_b, *lru_w, batch, seq)
    pos_s = jnp.repeat(PAST_LEN_ + jnp.arange(t_new, dtype=I32), nseq)
    cos_s, sin_s = _rope_tables(pos_s)
    q_s, k_s, v_s, xl_s, yl_s, sga_s, sgr_s = _inproj(
        xs_tm, sc_a_s, sh_a_s, row(p['g_pre_mix']), cos_s, sin_s, w_in_b, rows_per_mod=0,
        pos_tiles=n_s // min(ROW_TILE, n_s))

    sinks_perm = p['sinks']
    oa_p = (sinks_perm, q_p, k_p, v_p)
    rows = t_new * GQA_GROUP
    q_s3 = q_s.reshape(t_new, nseq, GQA_GROUP, KV_DIM).transpose(1, 0, 2, 3).reshape(nseq, rows, KV_DIM)
    kn = k_s.reshape(t_new, nseq, KV_DIM).transpose(1, 0, 2)
    vn = v_s.reshape(t_new, nseq, KV_DIM).transpose(1, 0, 2)
    kc = ck.reshape(nseq, WINDOW, KV_DIM)
    vc = cv.reshape(nseq, WINDOW, KV_DIM)
    sink_rows = jnp.tile(p['sinks'].reshape(N_KV_HEADS, 1, GQA_GROUP), (1, t_new, 1)).reshape(N_KV_HEADS, rows, 1)
    oa_s3 = _attn_sample(sink_rows, q_s3, kn, vn, kc, vc)
    oa_s = oa_s3.reshape(nseq, t_new, Q_DIM).transpose(1, 0, 2).reshape(n_s, Q_DIM)

    ol_s3, hlast_s = _lru_sample(xl_s.reshape(t_new, nseq, LRU_WIDTH), yl_s.reshape(t_new, nseq, LRU_WIDTH),
                                 cconv.transpose(1, 0, 2), ch, *lru_w)
    ol_s = ol_s3.reshape(n_s, LRU_WIDTH)

    tm_post = min(ROW_TILE, n_p)
    tri = jnp.tril(jnp.ones((tm_post, tm_post), F32), -1).astype(BF16)
    post_w = (wab, wlb, wout, row(p['g_post_mix']), row(p['g_pre_ffn']), wr_pair, wr_top, br)
    zero_cnt = jnp.zeros((1, ROUTE_LANES), F32)
    x1_p, h2_p, route_p, stat_p, cnt_p = _post(oa_p, ol_p, sga_p, sgr_p, xp.reshape(n_p, D_MODEL),
                                               gt_a_p, sc_f_p, sh_f_p, *post_w, tri, zero_cnt, rows_per_mod=seq)
    tm_s = min(ROW_TILE, n_s)
    tri_s = tri if tm_s == tm_post else jnp.tril(jnp.ones((tm_s, tm_s), F32), -1).astype(BF16)
    x1_s, h2_s, route_s, stat_s, cnt_all = _post(oa_s, ol_s, sga_s, sgr_s, xs_tm,
                                                 gt_a_s, sc_f_s, sh_f_s, *post_w, tri_s, cnt_p, rows_per_mod=0)

    e_lanes = slice(N_GROUPS, N_GROUPS + N_EXPERTS)
    to_granules = lambda v: (v.astype(I32) // GRANULE).reshape(-1)
    totals = to_granules(cnt_all[0, e_lanes])
    stats = jnp.concatenate([stat_p, stat_s], axis=0).reshape(-1, SUBLANES, ROUTE_LANES)
    cnt8 = to_granules(stats[:, 0, e_lanes])
    goff = to_granules(stats[:, 1, e_lanes])
    tiles_p = n_p // tm_post
    n_tiles = tiles_p + n_s // tm_s
    max_rows = 2 * (n_p + n_s) + n_tiles * N_EXPERTS * (GRANULE - 1) + N_EXPERTS * (EXPERT_ROWS - GRANULE)
    n_blocks = -(-max_rows // EXPERT_ROWS)
    n_slots = n_blocks * EXPERT_ROWS
    srt, gsrc, gslot = _sort(totals, cnt8, goff, route_p, h2_p, route_s, h2_s, n_slots)
    block_e, meta = _plan(totals, n_blocks)
    ys = _experts(block_e, meta, gsrc, srt, wg, wu, wd, n_slots)
    y_p = _combine(gslot, route_p, x1_p, gt_f_p, row(p['g_post_ffn']), ys, seq, 0)
    y_s = _combine(gslot, route_s, x1_s, gt_f_s, row(p['g_post_ffn']), ys, 0, tiles_p)

    k_new_p = klast_p.reshape(batch, WINDOW, N_KV_HEADS, HEAD_DIM)
    v_new_p = vlast_p.reshape(batch, WINDOW, N_KV_HEADS, HEAD_DIM)
    conv_p = xtail_p[:, -(CONV_WIDTH - 1):]
    h_p = hlast_p.reshape(batch, LRU_WIDTH)
    k_new_s = jnp.concatenate([ck, kn.reshape(nseq, t_new, N_KV_HEADS, HEAD_DIM)], axis=1)[:, -WINDOW:]
    v_new_s = jnp.concatenate([cv, vn.reshape(nseq, t_new, N_KV_HEADS, HEAD_DIM)], axis=1)[:, -WINDOW:]
    xl_s3 = xl_s.reshape(t_new, nseq, LRU_WIDTH).transpose(1, 0, 2).astype(F32)
    conv_s = jnp.concatenate([cconv, xl_s3], axis=1)[:, -(CONV_WIDTH - 1):]
    return (y_p.reshape(batch, seq, D_MODEL), y_s, k_new_p, v_new_p, conv_p, h_p,
            k_new_s, v_new_s, conv_s, hlast_s)


PAST_LEN_ = 16384

PARAM_NAMES = ('w_ada', 'b_ada', 'g_pre_mix', 'g_post_mix', 'g_pre_ffn', 'g_post_ffn', 'w_in', 'sinks',
               'w_conv', 'b_conv', 'w_lru_a', 'b_lru_a', 'w_lru_x', 'b_lru_x', 'lru_lambda',
               'w_attn_branch', 'w_lru_branch', 'w_out', 'w_router_group', 'b_router_group',
               'w_router_expert', 'b_router_expert', 'w_exp_gate', 'w_exp_up', 'w_exp_down')


def kernel(x_prompt, x_sample, cache_k_win, cache_v_win, state_conv, state_h, c_prompt, c_sample, w_ada, b_ada, g_pre_mix, g_post_mix, g_pre_ffn, g_post_ffn, w_in, sinks, w_conv, b_conv, w_lru_a, b_lru_a, w_lru_x, b_lru_x, lru_lambda, w_attn_branch, w_lru_branch, w_out, w_router_group, b_router_group, w_router_expert, b_router_expert, w_exp_gate, w_exp_up, w_exp_down):
    weights = (w_ada, b_ada, g_pre_mix, g_post_mix, g_pre_ffn, g_post_ffn, w_in, sinks,
               w_conv, b_conv, w_lru_a, b_lru_a, w_lru_x, b_lru_x, lru_lambda,
               w_attn_branch, w_lru_branch, w_out, w_router_group, b_router_group,
               w_router_expert, b_router_expert, w_exp_gate, w_exp_up, w_exp_down)
    depth = w_ada.shape[0]
    batch = x_prompt.shape[0]
    nseq, t_new, _ = x_sample.shape
    y_p = x_prompt
    y_s = x_sample.transpose(1, 0, 2).reshape(t_new * nseq, D_MODEL)
    c_all = jnp.concatenate([c_prompt, c_sample], axis=0)
    outs = [[] for _ in range(8)]
    for layer in range(depth):
        p = {name: w[layer] for name, w in zip(PARAM_NAMES, weights)}
        mod = _ada(c_all, p['w_ada'], p['b_ada'].reshape(1, -1))
        res = _layer_forward(y_p, y_s, cache_k_win[layer], cache_v_win[layer], state_conv[layer],
                             state_h[layer], mod[:batch], mod[batch:], p)
        y_p, y_s = res[0], res[1]
        for o, r in zip(outs, res[2:]):
            o.append(r)
    y_sample = y_s.reshape(t_new, nseq, D_MODEL).transpose(1, 0, 2)
    return (y_p, y_sample) + tuple(jnp.stack(o) for o in outs)
```

```python
import functools

import jax
import jax.numpy as jnp
from jax import lax
from jax.experimental import pallas as pl
from jax.experimental.pallas import tpu as pltpu

F32 = jnp.float32
BF16 = jnp.bfloat16
I32 = jnp.int32

D_MODEL = 1024
N_HEADS = 16
HEAD_DIM = 64
N_KV_HEADS = 4
GQA_GROUP = 4
WINDOW = 128
ROPE_THETA = 10000.0
NEG_INF = -1e30
LRU_WIDTH = 1024
LRU_BLOCKS = 16
LRU_BLOCK_W = 64
CONV_WIDTH = 4
LRU_C = 8.0
N_GROUPS = 4
EXPERTS_PER_GROUP = 8
N_EXPERTS = 32
D_EXPERT = 512
MOE_BLOCK = 128
NORM_EPS = 1e-6
Q_DIM = N_HEADS * HEAD_DIM
KV_DIM = N_KV_HEADS * HEAD_DIM
IN_DIM = Q_DIM + 2 * KV_DIM + 2 * LRU_WIDTH + 2 * D_MODEL

LANES = 128
SUBLANES = 8
MXU_DIM = 256
VMEM_LIMIT = 56 * 1024 * 1024
VMEM_LIMIT_BIG = 60 * 1024 * 1024

ROW_TILE = 512
ROUTE_LANES = LANES
GRANULE = SUBLANES
EXPERT_ROWS = 512
HALF_D = D_MODEL // 2
U32 = jnp.uint32


def _cparams(n_axes, vmem=VMEM_LIMIT, flags=None):
    return pltpu.CompilerParams(dimension_semantics=("arbitrary",) * n_axes, vmem_limit_bytes=vmem, flags=flags)


def _rms(x, g):
    ms = jnp.mean(x * x, axis=-1, keepdims=True)
    return x * lax.rsqrt(ms + NORM_EPS) * g


def _sigmoid(x):
    return 1.0 / (1.0 + jnp.exp(-x))


def _dot(a, b):
    return jnp.dot(a, b, preferred_element_type=F32)


def _ada_kernel(c_ref, w_ref, b_ref, o_ref):
    c = c_ref[...]
    s = (c * _sigmoid(c)).astype(BF16)
    o_ref[...] = _dot(s, w_ref[...].astype(BF16)) + b_ref[...]


def _ada(c_all, w_ada, b_ada):
    r = c_all.shape[0]
    n = w_ada.shape[1]
    return pl.pallas_call(
        _ada_kernel,
        grid=(n // D_MODEL,),
        in_specs=[pl.BlockSpec((r, D_MODEL), lambda j: (0, 0)),
                  pl.BlockSpec((D_MODEL, D_MODEL), lambda j: (0, j)),
                  pl.BlockSpec((1, D_MODEL), lambda j: (0, j))],
        out_specs=pl.BlockSpec((r, D_MODEL), lambda j: (0, j)),
        out_shape=jax.ShapeDtypeStruct((r, n), F32),
        compiler_params=_cparams(1),
        name="ada_mod",
    )(c_all, w_ada, b_ada)


_O1 = Q_DIM
_O2 = _O1 + KV_DIM
_O3 = _O2 + KV_DIM
_O4 = _O3 + LRU_WIDTH
_O5 = _O4 + LRU_WIDTH
_O6 = _O5 + D_MODEL


def _prenorm(x_ref, g_ref, sc_ref, sh_ref):
    h = _rms(x_ref[...], g_ref[...]) * (1.0 + sc_ref[...]) + sh_ref[...]
    return h.astype(BF16)


PIECE_COLS = 256


def _plain_pieces(h_ref, w_ref, base, width, store):
    def piece(c0):
        def run():
            store(c0, _dot(h_ref[...], w_ref[:, base + c0:base + c0 + PIECE_COLS]))
        return run
    return [piece(c0) for c0 in range(0, width, PIECE_COLS)]


def _qkv_gate_pieces(h_ref, w_ref, cos_ref, sin_ref, q_ref, k_ref, v_ref, sga_ref, sgr_ref, last_refs=None):
    def rope(t):
        cos = cos_ref[...]
        sin = sin_ref[...]
        lane = lax.broadcasted_iota(I32, cos.shape, 1)
        first_half = (lane % HEAD_DIM) < (HEAD_DIM // 2)
        rot = jnp.where(first_half, pltpu.roll(t, LANES - HEAD_DIM // 2, 1), pltpu.roll(t, HEAD_DIM // 2, 1))
        return t * cos + rot * sin

    def q_piece(c0):
        def run():
            qf = _dot(h_ref[...], w_ref[:, c0:c0 + PIECE_COLS])
            for c in range(PIECE_COLS // LANES):
                q_ref[:, c0 + c * LANES:c0 + (c + 1) * LANES] = rope(qf[:, c * LANES:(c + 1) * LANES]).astype(BF16)
        return run

    def kv_piece():
        kv = _dot(h_ref[...], w_ref[:, _O1:_O3])
        for c in range(KV_DIM // LANES):
            k_ref[:, c * LANES:(c + 1) * LANES] = rope(kv[:, c * LANES:(c + 1) * LANES])
        v_ref[...] = kv[:, KV_DIM:]
        if last_refs is not None:
            t = k_ref.shape[0]
            last_refs[0][...] = k_ref[t - WINDOW:t, :]
            last_refs[1][...] = v_ref[t - WINDOW:t, :]

    def gate_piece(o_ref, base, c0):
        def run():
            z = _dot(h_ref[...], w_ref[:, base + c0:base + c0 + PIECE_COLS])
            o_ref[:, c0:c0 + PIECE_COLS] = _sigmoid(z).astype(BF16)
        return run

    pieces = [q_piece(c0) for c0 in range(0, Q_DIM, PIECE_COLS)] + [kv_piece]
    pieces += [gate_piece(sga_ref, _O5, c0) for c0 in range(0, D_MODEL, PIECE_COLS)]
    pieces += [gate_piece(sgr_ref, _O6, c0) for c0 in range(0, D_MODEL, PIECE_COLS)]
    return pieces


def _inproj_kernel(x_ref, sc_ref, sh_ref, g_ref, cos_ref, sin_ref, w_ref,
                   q_ref, k_ref, v_ref, xl_ref, yl_ref, sga_ref, sgr_ref, hbuf):
    hbuf[...] = _prenorm(x_ref, g_ref, sc_ref, sh_ref)
    xl_ref[...] = _dot(hbuf[...], w_ref[:, _O3:_O4]).astype(BF16)
    yl_ref[...] = _dot(hbuf[...], w_ref[:, _O4:_O5]).astype(BF16)
    for piece in _qkv_gate_pieces(hbuf, w_ref, cos_ref, sin_ref, q_ref, k_ref, v_ref, sga_ref, sgr_ref):
        piece()


def _inproj(x, sc, sh, g, cos, sin, w_in, rows_per_mod, pos_tiles):
    n = x.shape[0]
    tm = min(ROW_TILE, n)
    if rows_per_mod:
        tiles_per_mod = rows_per_mod // tm
        mod_spec = pl.BlockSpec((None, 1, D_MODEL), lambda i: (i // tiles_per_mod, 0, 0))
    else:
        mod_spec = pl.BlockSpec((tm, D_MODEL), lambda i: (i, 0))
    row = lambda w: pl.BlockSpec((tm, w), lambda i: (i, 0))
    outs = [(Q_DIM, BF16), (KV_DIM, F32), (KV_DIM, F32), (LRU_WIDTH, BF16), (LRU_WIDTH, BF16),
            (D_MODEL, BF16), (D_MODEL, BF16)]
    return pl.pallas_call(
        _inproj_kernel,
        grid=(n // tm,),
        in_specs=[row(D_MODEL), mod_spec, mod_spec,
                  pl.BlockSpec((1, D_MODEL), lambda i: (0, 0)),
                  pl.BlockSpec((tm, LANES), lambda i: (i % pos_tiles, 0)),
                  pl.BlockSpec((tm, LANES), lambda i: (i % pos_tiles, 0)),
                  pl.BlockSpec((D_MODEL, IN_DIM), lambda i: (0, 0))],
        out_specs=[row(w) for w, _ in outs],
        out_shape=[jax.ShapeDtypeStruct((n, w), dt) for w, dt in outs],
        scratch_shapes=[pltpu.VMEM((tm, D_MODEL), BF16)],
        compiler_params=_cparams(1),
        name="in_proj",
    )(x, sc, sh, g, cos, sin, w_in)


def _head_masks(shape):
    lane = lax.broadcasted_iota(I32, shape, 1)
    return [(lane // HEAD_DIM) == h for h in range(N_KV_HEADS)]


def _attention_core(q_perm, kall, vall, valid, sink_of, rows):
    masks_b = _head_masks((rows, KV_DIM))
    zero_b = jnp.zeros((rows, KV_DIM), BF16)
    pieces = []
    for h in range(N_KV_HEADS):
        for g in range(GQA_GROUP):
            pieces.append(jnp.where(masks_b[h], q_perm[g], zero_b))
    q_big = jnp.concatenate(pieces, axis=0)
    s_all = lax.dot_general(q_big, kall, (((1,), (1,)), ((), ())), preferred_element_type=F32)
    s_all = s_all * (HEAD_DIM ** -0.5)
    p_chunks, inv_chunks = [], []
    for h in range(N_KV_HEADS):
        for g in range(GQA_GROUP):
            c = h * GQA_GROUP + g
            s = jnp.where(valid, s_all[c * rows:(c + 1) * rows], NEG_INF)
            sink = sink_of(h, g)
            m = jnp.maximum(jnp.max(s, axis=-1, keepdims=True), sink)
            p = jnp.exp(s - m)
            denom = jnp.sum(p, axis=-1, keepdims=True) + jnp.exp(sink - m)
            p_chunks.append(p.astype(BF16))
            inv_chunks.append(1.0 / denom)
    o_all = _dot(jnp.concatenate(p_chunks, axis=0), vall)
    outs = []
    for g in range(GQA_GROUP):
        acc = jnp.zeros((rows, KV_DIM), F32)
        for h in range(N_KV_HEADS):
            c = h * GQA_GROUP + g
            acc = acc + jnp.where(masks_b[h], o_all[c * rows:(c + 1) * rows] * inv_chunks[c], 0.0)
        outs.append(acc)
    return outs


ATTN_Q_BLOCKS = 4


def _attention_tile(sink_ref, q_ref, kc_ref, kp_ref, vc_ref, vp_ref, o_ref, seq_start, between=None):
    kall = jnp.concatenate([kp_ref[...], kc_ref[...]], axis=0).astype(BF16)
    vall = jnp.concatenate([vp_ref[...], vc_ref[...]], axis=0).astype(BF16)
    qi = lax.broadcasted_iota(I32, (WINDOW, 2 * WINDOW), 0)
    kj = lax.broadcasted_iota(I32, (WINDOW, 2 * WINDOW), 1)
    dist = qi + WINDOW - kj
    in_window = (dist >= 0) & (dist <= WINDOW)
    for c in range(ATTN_Q_BLOCKS):
        rows = slice(c * WINDOW, (c + 1) * WINDOW)
        keys = slice(c * WINDOW, (c + 2) * WINDOW)
        valid = in_window & ((kj >= WINDOW) | jnp.logical_not(seq_start)) if c == 0 else in_window
        q_perm = [q_ref[rows, g * KV_DIM:(g + 1) * KV_DIM] for g in range(GQA_GROUP)]
        outs = _attention_core(q_perm, kall[keys], vall[keys], valid,
                               lambda h, g: sink_ref[h * GQA_GROUP + g], WINDOW)
        for g in range(GQA_GROUP):
            o_ref[rows, g * KV_DIM:(g + 1) * KV_DIM] = outs[g].astype(BF16)
        if between is not None:
            between(c)


def _attn_prompt_kernel(sink_ref, q_ref, kc_ref, kp_ref, vc_ref, vp_ref, o_ref):
    _attention_tile(sink_ref, q_ref, kc_ref, kp_ref, vc_ref, vp_ref, o_ref, pl.program_id(1) == 0)


def _attn_prompt(sinks, q, k, v, batch, seq):
    step_rows = ATTN_Q_BLOCKS * WINDOW
    nb = seq // step_rows
    cur = lambda w: pl.BlockSpec((step_rows, w), lambda b, j: (b * nb + j, 0))
    prev = lambda w: pl.BlockSpec(
        (WINDOW, w), lambda b, j: (jnp.maximum((b * nb + j) * ATTN_Q_BLOCKS - 1, 0), 0))
    return pl.pallas_call(
        _attn_prompt_kernel,
        grid=(batch, nb),
        in_specs=[pl.BlockSpec(memory_space=pltpu.SMEM),
                  cur(Q_DIM), cur(KV_DIM), prev(KV_DIM), cur(KV_DIM), prev(KV_DIM)],
        out_specs=cur(Q_DIM),
        out_shape=jax.ShapeDtypeStruct((batch * seq, Q_DIM), BF16),
        compiler_params=_cparams(2),
        name="attn_prompt",
    )(sinks, q, k, k, v, v)


SEQ_PER_STEP = 8


def _attn_sample_kernel(sink_ref, q_ref, kn_ref, vn_ref, kc_ref, vc_ref, o_ref, kbuf, vbuf, *, t_new):
    rows = GQA_GROUP * t_new
    kbuf[...] = jnp.zeros(kbuf.shape, F32)
    vbuf[...] = jnp.zeros(vbuf.shape, F32)
    ri = lax.broadcasted_iota(I32, (rows, 2 * WINDOW), 0)
    kj = lax.broadcasted_iota(I32, (rows, 2 * WINDOW), 1)
    tq = ri // GQA_GROUP
    valid = (kj >= tq) & (kj <= tq + WINDOW) & (kj < WINDOW + t_new)
    nt_dims = (((1,), (1,)), ((), ()))
    for s in range(SEQ_PER_STEP):
        kbuf[0:t_new, :] = kn_ref[s]
        vbuf[0:t_new, :] = vn_ref[s]
        k_cache_t = kc_ref[s].astype(BF16)
        v_cache_t = vc_ref[s].astype(BF16)
        k_new = kbuf[...].astype(BF16)
        v_new = vbuf[...].astype(BF16)
        qs = q_ref[s]
        masks_b = _head_masks((rows, KV_DIM))
        zero_b = jnp.zeros((rows, KV_DIM), BF16)
        q_big = jnp.concatenate([jnp.where(masks_b[h], qs, zero_b) for h in range(N_KV_HEADS)], axis=0)
        s_all = jnp.concatenate([_dot(q_big, k_cache_t),
                                 lax.dot_general(q_big, k_new, nt_dims, preferred_element_type=F32)], axis=1)
        s_all = s_all * (HEAD_DIM ** -0.5)
        acc = jnp.zeros((rows, KV_DIM), F32)
        p_chunks, inv_chunks = [], []
        for h in range(N_KV_HEADS):
            sc = jnp.where(valid, s_all[h * rows:(h + 1) * rows], NEG_INF)
            sink = sink_ref[h]
            m = jnp.maximum(jnp.max(sc, axis=-1, keepdims=True), sink)
            p = jnp.exp(sc - m)
            denom = jnp.sum(p, axis=-1, keepdims=True) + jnp.exp(sink - m)
            p_chunks.append(p.astype(BF16))
            inv_chunks.append(1.0 / denom)
        p_all = jnp.concatenate(p_chunks, axis=0)
        o_all = (lax.dot_general(p_all[:, :WINDOW], v_cache_t, nt_dims, preferred_element_type=F32)
                 + _dot(p_all[:, WINDOW:], v_new))
        for h in range(N_KV_HEADS):
            acc = acc + jnp.where(masks_b[h], o_all[h * rows:(h + 1) * rows] * inv_chunks[h], 0.0)
        o_ref[s] = acc.astype(BF16)


def _attn_sample(sink_rows, q, kn, vn, kc, vc):
    nseq, rows, _ = q.shape
    t_new = kn.shape[1]
    sb = SEQ_PER_STEP
    blk = lambda r: pl.BlockSpec((sb, r, KV_DIM), lambda i: (i, 0, 0))
    cache = pl.BlockSpec((sb, KV_DIM, WINDOW), lambda i: (i, 0, 0))
    return pl.pallas_call(
        functools.partial(_attn_sample_kernel, t_new=t_new),
        grid=(nseq // sb,),
        in_specs=[pl.BlockSpec((N_KV_HEADS, rows, 1), lambda i: (0, 0, 0)),
                  blk(rows), blk(t_new), blk(t_new), cache, cache],
        out_specs=blk(rows),
        out_shape=jax.ShapeDtypeStruct((nseq, rows, KV_DIM), BF16),
        scratch_shapes=[pltpu.VMEM((WINDOW, KV_DIM), F32), pltpu.VMEM((WINDOW, KV_DIM), F32)],
        compiler_params=_cparams(1),
        name="attn_sample",
    )(sink_rows, q, kn, vn, kc, vc)


def _gelu_tanh(x):
    return 0.5 * x * (1.0 + jnp.tanh(0.7978845608028654 * (x + 0.044715 * x * x * x)))


def _lru_gates(xc, wbd_ref, ba, bx, lam):
    xcb = xc.astype(BF16)
    r_parts, i_parts = [], []
    for gidx in range(LRU_WIDTH // MXU_DIM):
        z = _dot(xcb[:, gidx * MXU_DIM:(gidx + 1) * MXU_DIM], wbd_ref[gidx])
        r_parts.append(z[:, :MXU_DIM])
        i_parts.append(z[:, MXU_DIM:])
    return _gate_math(jnp.concatenate(r_parts, axis=1), jnp.concatenate(i_parts, axis=1), xc, ba, bx, lam)


def _gate_math(zr, zi, xc, ba, bx, lam):
    r = _sigmoid(zr + ba)
    i = _sigmoid(zi + bx)
    softplus_neg_lam = jnp.maximum(-lam, 0.0) + jnp.log1p(jnp.exp(-jnp.abs(lam)))
    log_a = -LRU_C * r * softplus_neg_lam
    a = jnp.exp(log_a)
    y = 1.0 - a * a
    u = jnp.where(y > 0.0, y * lax.rsqrt(y), 0.0) * (i * xc)
    return a, u


LRU_CHUNK = 64


def _lru_chunk(c, xbuf, ybuf, hcar, ol_ref, wc_ref, bc_ref, wbd_ref, ba_ref, bx_ref, lam_ref):
    w = LRU_WIDTH
    n = LRU_CHUNK
    rows = slice(c * n, (c + 1) * n)
    r0 = SUBLANES + c * n
    xc = xbuf[r0:r0 + n, :] * wc_ref[CONV_WIDTH - 1:CONV_WIDTH, :] + bc_ref[...]
    for k in range(1, CONV_WIDTH):
        xc = xc + xbuf[r0 - k:r0 - k + n, :] * wc_ref[CONV_WIDTH - 1 - k:CONV_WIDTH - k, :]
    a, u = _lru_gates(xc, wbd_ref, ba_ref[...], bx_ref[...], lam_ref[...])

    ng = n // SUBLANES
    a3 = a.reshape(ng, SUBLANES, w)
    u3 = u.reshape(ng, SUBLANES, w)
    row = lax.broadcasted_iota(I32, (ng, SUBLANES, w), 1)
    d = 1
    while d < SUBLANES:
        a_s = jnp.where(row >= d, pltpu.roll(a3, d, 1), 1.0)
        u_s = jnp.where(row >= d, pltpu.roll(u3, d, 1), 0.0)
        u3 = a3 * u_s + u3
        a3 = a3 * a_s
        d *= 2
    carry = hcar[...]
    hs = []
    for gi in range(ng):
        hg = a3[gi] * carry + u3[gi]
        hs.append(hg)
        carry = hg[SUBLANES - 1:SUBLANES, :]
    hcar[...] = carry
    h = jnp.concatenate(hs, axis=0)
    ol_ref[rows, :] = (h * _gelu_tanh(ybuf[rows, :])).astype(BF16)


def _inproj_lru_kernel(x_ref, sc_ref, sh_ref, g_ref, cos_ref, sin_ref, w_ref,
                       wc_ref, bc_ref, wbd_ref, ba_ref, bx_ref, lam_ref,
                       q_ref, k_ref, v_ref, sga_ref, sgr_ref, klast_ref, vlast_ref, ol_ref, xtail_ref, hlast_ref,
                       hbuf, xbuf, ybuf, hcar, *, tiles_per_seq):
    t = x_ref.shape[0]

    @pl.when(pl.program_id(0) % tiles_per_seq == 0)
    def _():
        xbuf[0:SUBLANES, :] = jnp.zeros((SUBLANES, LRU_WIDTH), F32)
        hcar[...] = jnp.zeros((1, LRU_WIDTH), F32)

    hbuf[...] = _prenorm(x_ref, g_ref, sc_ref, sh_ref)

    xbuf[SUBLANES:SUBLANES + t, :] = _dot(hbuf[...], w_ref[:, _O3:_O4])
    ybuf[...] = _dot(hbuf[...], w_ref[:, _O4:_O5])

    pieces = _qkv_gate_pieces(hbuf, w_ref, cos_ref, sin_ref, q_ref, k_ref, v_ref, sga_ref, sgr_ref,
                              last_refs=(klast_ref, vlast_ref))
    n_chunks = t // LRU_CHUNK
    per_chunk = -(-len(pieces) // n_chunks)
    for c in range(n_chunks):
        _lru_chunk(c, xbuf, ybuf, hcar, ol_ref, wc_ref, bc_ref, wbd_ref, ba_ref, bx_ref, lam_ref)
        for piece in pieces[c * per_chunk:(c + 1) * per_chunk]:
            piece()

    tail = xbuf[t:t + SUBLANES, :]
    xtail_ref[...] = tail
    xbuf[0:SUBLANES, :] = tail
    hlast_ref[...] = hcar[...]


def _inproj_lru(x, sc, sh, g, cos, sin, w_in, wc, bc, wbd, ba, bx, lam, batch, seq):
    n = x.shape[0]
    tm = min(ROW_TILE, seq)
    tps = seq // tm
    mod_spec = pl.BlockSpec((None, 1, D_MODEL), lambda i: (i // tps, 0, 0))
    row = lambda w: pl.BlockSpec((tm, w), lambda i: (i, 0))
    full = lambda a: pl.BlockSpec(a.shape, lambda i: (0,) * a.ndim)
    per_seq = lambda r, w: pl.BlockSpec((None, r, w), lambda i: (i // tps, 0, 0))
    outs = [(Q_DIM, BF16), (KV_DIM, F32), (KV_DIM, F32), (D_MODEL, BF16), (D_MODEL, BF16)]
    return pl.pallas_call(
        functools.partial(_inproj_lru_kernel, tiles_per_seq=tps),
        grid=(n // tm,),
        in_specs=[row(D_MODEL), mod_spec, mod_spec, full(g),
                  pl.BlockSpec((tm, LANES), lambda i: (i % tps, 0)),
                  pl.BlockSpec((tm, LANES), lambda i: (i % tps, 0)),
                  full(w_in), full(wc), full(bc), full(wbd), full(ba), full(bx), full(lam)],
        out_specs=[row(w) for w, _ in outs] + [per_seq(WINDOW, KV_DIM), per_seq(WINDOW, KV_DIM)]
        + [row(LRU_WIDTH), per_seq(SUBLANES, LRU_WIDTH), per_seq(1, LRU_WIDTH)],
        out_shape=[jax.ShapeDtypeStruct((n, w), dt) for w, dt in outs]
        + [jax.ShapeDtypeStruct((batch, WINDOW, KV_DIM), F32)] * 2
        + [jax.ShapeDtypeStruct((n, LRU_WIDTH), BF16),
           jax.ShapeDtypeStruct((batch, SUBLANES, LRU_WIDTH), F32),
           jax.ShapeDtypeStruct((batch, 1, LRU_WIDTH), F32)],
        scratch_shapes=[pltpu.VMEM((tm, D_MODEL), BF16),
                        pltpu.VMEM((2 * SUBLANES + tm, LRU_WIDTH), F32), pltpu.VMEM((tm, LRU_WIDTH), F32),
                        pltpu.VMEM((1, LRU_WIDTH), F32)],
        compiler_params=_cparams(1, vmem=VMEM_LIMIT_BIG),
        name="in_proj_lru",
    )(x, sc, sh, g, cos, sin, w_in, wc, bc, wbd, ba, bx, lam)


def _lru_sample_kernel(xl_ref, yl_ref, cs_ref, h0_ref, wc_ref, bc_ref, wbd_ref, ba_ref, bx_ref, lam_ref,
                       o_ref, hlast_ref):
    t_new, nseq, w = xl_ref.shape
    xp = [cs_ref[k] for k in range(CONV_WIDTH - 1)] + [xl_ref[k].astype(F32) for k in range(t_new)]
    xcs = []
    for t in range(t_new):
        acc = bc_ref[...] + xp[t] * wc_ref[0:1, :]
        for k in range(1, CONV_WIDTH):
            acc = acc + xp[t + k] * wc_ref[k:k + 1, :]
        xcs.append(acc)
    xc = jnp.concatenate(xcs, axis=0)
    a, u = _lru_gates(xc, wbd_ref, ba_ref[...], bx_ref[...], lam_ref[...])
    h = h0_ref[...]
    for t in range(t_new):
        h = a[t * nseq:(t + 1) * nseq] * h + u[t * nseq:(t + 1) * nseq]
        o_ref[t] = (h * _gelu_tanh(yl_ref[t].astype(F32))).astype(BF16)
    hlast_ref[...] = h


def _lru_sample(xl, yl, cs, h0, wc, bc, wbd, ba, bx, lam):
    t_new, nseq, w = xl.shape
    full = lambda shp: pl.BlockSpec(shp, lambda i: (0,) * len(shp))
    args = (xl, yl, cs, h0, wc, bc, wbd, ba, bx, lam)
    return pl.pallas_call(
        _lru_sample_kernel,
        grid=(1,),
        in_specs=[full(a.shape) for a in args],
        out_specs=[full((t_new, nseq, w)), full((nseq, w))],
        out_shape=[jax.ShapeDtypeStruct((t_new, nseq, w), BF16), jax.ShapeDtypeStruct((nseq, w), F32)],
        compiler_params=_cparams(1),
        name="lru_sample",
    )(*args)


def _post_kernel(oa_ref, ol_ref, sga_ref, sgr_ref, x_ref, gta_ref, scf_ref, shf_ref,
                 wab_ref, wlb_ref, wout_ref, gpm_ref, gpf_ref, wrp_ref, wrt_ref, br_ref, tri_ref, cin_ref,
                 x1_ref, h2_ref, route_ref, stat_ref, cnt_ref, carry):
    @pl.when(pl.program_id(0) == 0)
    def _():
        carry[...] = cin_ref[...]

    b_attn = _dot(oa_ref[...], wab_ref[...])
    b_lru = _dot(ol_ref[...], wlb_ref[...])
    _post_tail(b_attn, b_lru, sga_ref, sgr_ref, x_ref, gta_ref, scf_ref, shf_ref, wout_ref, gpm_ref, gpf_ref,
               wrp_ref, wrt_ref, br_ref, tri_ref, x1_ref, h2_ref, route_ref, stat_ref, cnt_ref, carry)


def _post_attn_kernel(sink_ref, q_ref, kc_ref, kp_ref, vc_ref, vp_ref,
                      ol_ref, sga_ref, sgr_ref, x_ref, gta_ref, scf_ref, shf_ref,
                      wab_ref, wlb_ref, wout_ref, gpm_ref, gpf_ref, wrp_ref, wrt_ref, br_ref, tri_ref, cin_ref,
                      x1_ref, h2_ref, route_ref, stat_ref, cnt_ref, carry, oabuf, blbuf, *, tiles_per_seq):
    i = pl.program_id(0)

    @pl.when(i == 0)
    def _():
        carry[...] = cin_ref[...]

    cols = D_MODEL // ATTN_Q_BLOCKS

    def lru_piece(c):
        blbuf[:, c * cols:(c + 1) * cols] = _dot(ol_ref[...], wlb_ref[:, c * cols:(c + 1) * cols])

    _attention_tile(sink_ref, q_ref, kc_ref, kp_ref, vc_ref, vp_ref, oabuf, i % tiles_per_seq == 0, lru_piece)
    b_attn = _dot(oabuf[...], wab_ref[...])
    _post_tail(b_attn, blbuf[...], sga_ref, sgr_ref, x_ref, gta_ref, scf_ref, shf_ref, wout_ref, gpm_ref, gpf_ref,
               wrp_ref, wrt_ref, br_ref, tri_ref, x1_ref, h2_ref, route_ref, stat_ref, cnt_ref, carry)


def _post_tail(b_attn, b_lru, sga_ref, sgr_ref, x_ref, gta_ref, scf_ref, shf_ref, wout_ref, gpm_ref, gpf_ref,
               wrp_ref, wrt_ref, br_ref, tri_ref, x1_ref, h2_ref, route_ref, stat_ref, cnt_ref, carry):
    merged = sga_ref[...].astype(F32) * b_attn + sgr_ref[...].astype(F32) * b_lru
    mix = _dot(merged.astype(BF16), wout_ref[...])
    x1 = x_ref[...] + gta_ref[...] * _rms(mix, gpm_ref[...])
    x1_ref[...] = x1
    h2 = _rms(x1, gpf_ref[...]) * (1.0 + scf_ref[...]) + shf_ref[...]
    h2_ref[...] = h2.astype(BF16)

    h_hi = h2.astype(BF16)
    h_lo = (h2 - h_hi.astype(F32)).astype(BF16)
    hi_terms = _dot(h_hi, wrp_ref[...])
    logits = (hi_terms[:, :ROUTE_LANES] + (hi_terms[:, ROUTE_LANES:] + _dot(h_lo, wrt_ref[...]))) + br_ref[...]

    tm = logits.shape[0]
    lane = lax.broadcasted_iota(I32, (tm, ROUTE_LANES), 1)
    big = jnp.int32(ROUTE_LANES)
    is_g = lane < N_GROUPS
    lg = jnp.where(is_g, logits, NEG_INF)
    mg = jnp.max(lg, axis=-1, keepdims=True)
    g_star = jnp.min(jnp.where(lg == mg, lane, big), axis=-1, keepdims=True)
    p_star = 1.0 / jnp.sum(jnp.where(is_g, jnp.exp(lg - mg), 0.0), axis=-1, keepdims=True)
    lo = N_GROUPS + g_star * EXPERTS_PER_GROUP
    in_grp = (lane >= lo) & (lane < lo + EXPERTS_PER_GROUP)
    le = jnp.where(in_grp, logits, NEG_INF)
    m1 = jnp.max(le, axis=-1, keepdims=True)
    i1 = jnp.min(jnp.where(le == m1, lane, big), axis=-1, keepdims=True)
    le2 = jnp.where(lane == i1, NEG_INF, le)
    m2 = jnp.max(le2, axis=-1, keepdims=True)
    i2 = jnp.min(jnp.where(le2 == m2, lane, big), axis=-1, keepdims=True)
    e2x = jnp.exp(m2 - m1)
    wsum = 1.0 + e2x
    w1 = (1.0 / wsum) * p_star
    w2 = (e2x / wsum) * p_star

    oh1 = lane == i1
    oh2 = lane == i2
    cnt = jnp.where(oh1 | oh2, 1.0, 0.0)
    excl = _dot(tri_ref[...], cnt.astype(BF16))
    per_e = jnp.sum(cnt, axis=0, keepdims=True)
    pad8 = jnp.floor((per_e + (GRANULE - 1.0)) * (1.0 / GRANULE)) * GRANULE
    incl = jnp.broadcast_to(pad8, (SUBLANES, ROUTE_LANES))
    lane8 = lax.broadcasted_iota(I32, (SUBLANES, ROUTE_LANES), 1)
    d = 1
    while d < ROUTE_LANES:
        incl = incl + jnp.where(lane8 >= d, pltpu.roll(incl, d, 1), 0.0)
        d *= 2
    seg_start = incl[0:1, :] - pad8
    pos = excl + seg_start
    s1 = jnp.sum(jnp.where(oh1, pos, 0.0), axis=-1, keepdims=True)
    s2 = jnp.sum(jnp.where(oh2, pos, 0.0), axis=-1, keepdims=True)
    rec = jnp.where(lane == 0, s1, 0.0)
    rec = jnp.where(lane == 1, s2, rec)
    rec = jnp.where(lane == 2, w1, rec)
    rec = jnp.where(lane == 3, w2, rec)
    route_ref[...] = rec
    srow = lax.broadcasted_iota(I32, (SUBLANES, ROUTE_LANES), 0)
    stat_ref[...] = jnp.where(srow == 0, pad8, jnp.where(srow == 1, carry[...], 0.0))
    carry[...] = carry[...] + pad8
    cnt_ref[...] = carry[...]


def _post(oa, ol, sga, sgr, x, gta, scf, shf, wab, wlb, wout, gpm, gpf, wrh, wrl, br, tri, cin, rows_per_mod):
    n = x.shape[0]
    tm = min(ROW_TILE, n)
    if rows_per_mod:
        tiles_per_mod = rows_per_mod // tm
        mod_spec = pl.BlockSpec((None, 1, D_MODEL), lambda i: (i // tiles_per_mod, 0, 0))
    else:
        mod_spec = pl.BlockSpec((tm, D_MODEL), lambda i: (i, 0))
    row = lambda w: pl.BlockSpec((tm, w), lambda i: (i, 0))
    full = lambda a: pl.BlockSpec(a.shape, lambda i: (0,) * a.ndim)
    if isinstance(oa, tuple):
        assert tm == ATTN_Q_BLOCKS * WINDOW and rows_per_mod
        sinks, q, k, v = oa
        prev = pl.BlockSpec((WINDOW, KV_DIM), lambda i: (jnp.maximum(i * ATTN_Q_BLOCKS - 1, 0), 0))
        kernel = functools.partial(_post_attn_kernel, tiles_per_seq=rows_per_mod // tm)
        first_specs = [pl.BlockSpec(memory_space=pltpu.SMEM), row(Q_DIM), row(KV_DIM), prev, row(KV_DIM), prev]
        first_args = (sinks, q, k, k, v, v)
        scratch = [pltpu.VMEM((tm, Q_DIM), BF16), pltpu.VMEM((tm, D_MODEL), F32)]
    else:
        kernel, first_specs, first_args, scratch = _post_kernel, [row(Q_DIM)], (oa,), []
    return pl.pallas_call(
        kernel,
        grid=(n // tm,),
        in_specs=first_specs + [row(LRU_WIDTH), row(D_MODEL), row(D_MODEL), row(D_MODEL),
                  mod_spec, mod_spec, mod_spec,
                  full(wab), full(wlb), full(wout), full(gpm), full(gpf), full(wrh), full(wrl), full(br),
                  full(tri), full(cin)],
        out_specs=[row(D_MODEL), row(D_MODEL), row(ROUTE_LANES),
                   pl.BlockSpec((SUBLANES, ROUTE_LANES), lambda i: (i, 0)),
                   pl.BlockSpec((1, ROUTE_LANES), lambda i: (0, 0))],
        out_shape=[jax.ShapeDtypeStruct((n, D_MODEL), F32), jax.ShapeDtypeStruct((n, D_MODEL), BF16),
                   jax.ShapeDtypeStruct((n, ROUTE_LANES), F32),
                   jax.ShapeDtypeStruct((n // tm * SUBLANES, ROUTE_LANES), F32),
                   jax.ShapeDtypeStruct((1, ROUTE_LANES), F32)],
        scratch_shapes=[pltpu.VMEM((1, ROUTE_LANES), F32)] + scratch,
        compiler_params=_cparams(1),
        name="post_mix",
    )(*first_args, ol, sga, sgr, x, gta, scf, shf, wab, wlb, wout, gpm, gpf, wrh, wrl, br, tri, cin)


BLOCK_GRANULES = EXPERT_ROWS // GRANULE
BLOCK_SHIFT = BLOCK_GRANULES.bit_length() - 1
assert BLOCK_GRANULES == 1 << BLOCK_SHIFT


def _padded(c):
    return ((c + (BLOCK_GRANULES - 1)) >> BLOCK_SHIFT) << BLOCK_SHIFT


def _sorted_rows(tm):
    r = 2 * tm + N_EXPERTS * (GRANULE - 1)
    return -(-r // MXU_DIM) * MXU_DIM


def _plan_kernel(tot_ref, be_ref, meta_ref, *, n_blocks):
    def fill(j, _):
        be_ref[j] = N_EXPERTS - 1
        return 0
    lax.fori_loop(0, n_blocks, fill, 0)

    def per_expert(e, nb):
        k = _padded(tot_ref[e]) >> BLOCK_SHIFT

        def put(b, _):
            be_ref[nb + b] = e
            return 0
        lax.fori_loop(0, k, put, 0)
        return nb + k
    n_active = lax.fori_loop(0, N_EXPERTS, per_expert, 0)
    meta_ref[0] = n_active


def _plan(totals, n_blocks):
    return pl.pallas_call(
        functools.partial(_plan_kernel, n_blocks=n_blocks),
        in_specs=[pl.BlockSpec(memory_space=pltpu.SMEM)],
        out_specs=[pl.BlockSpec(memory_space=pltpu.SMEM), pl.BlockSpec(memory_space=pltpu.SMEM)],
        out_shape=[jax.ShapeDtypeStruct((n_blocks,), I32), jax.ShapeDtypeStruct((1,), I32)],
        name="moe_plan",
    )(totals)


def _expert_starts(tot_ref, pstart):
    def body(e, acc):
        pstart[e] = acc
        return acc + _padded(tot_ref[e])
    return lax.fori_loop(0, N_EXPERTS, body, 0)


def _granule(ref, g):
    return ref.at[pl.ds(pl.multiple_of(g * GRANULE, GRANULE), GRANULE)]


def _pack_halves(lo_f32, hi_f32):
    return (pltpu.bitcast(lo_f32, U32) >> 16) | (pltpu.bitcast(hi_f32, U32) & jnp.uint32(0xFFFF0000))


def _unpack_halves(packed):
    lo = pltpu.bitcast(packed << 16, F32).astype(BF16)
    hi = pltpu.bitcast(packed & jnp.uint32(0xFFFF0000), F32).astype(BF16)
    return lo, hi


def _sort_kernel(tot_ref, cnt_ref, goff_ref, rec_p_ref, h2_p_ref, rec_s_ref, h2_s_ref,
                 srt_ref, gsrc_ref, gslot_ref, pstart, *, tiles_p, n_sorted, n_slots):
    i = pl.program_id(0)
    from_sample = i >= tiles_p
    rec = jnp.where(from_sample, rec_s_ref[...], rec_p_ref[...])
    h2 = jnp.where(from_sample, h2_s_ref[...], h2_p_ref[...])
    tm = h2.shape[0]

    rec_t = rec.T
    s1 = rec_t[0:1, :].astype(I32)
    s2 = rec_t[1:2, :].astype(I32)
    rows = lax.broadcasted_iota(I32, (n_sorted, tm), 0)
    sel = jnp.where((rows == s1) | (rows == s2), 1.0, 0.0).astype(BF16)
    srt = _dot(sel, h2)
    srt_ref[...] = _pack_halves(srt[:, :HALF_D], srt[:, HALF_D:])

    zero_granule = n_sorted // GRANULE - 1

    @pl.when(i == 0)
    def _():
        used = _expert_starts(tot_ref, pstart)

        def put_zero(g, _):
            gsrc_ref[g] = zero_granule
            return 0

        def per_expert(e, _):
            total = tot_ref[e]
            lax.fori_loop(pstart[e] + total, pstart[e] + _padded(total), put_zero, 0)
            return 0
        lax.fori_loop(0, N_EXPERTS, per_expert, 0)
        lax.fori_loop(used, n_slots // GRANULE, put_zero, 0)

    tile_granules = n_sorted // GRANULE

    def per_expert(e, seg):
        k = cnt_ref[i * N_EXPERTS + e]
        dst = pstart[e] + goff_ref[i * N_EXPERTS + e]
        src = i * tile_granules + seg

        def put(g, _):
            gsrc_ref[dst + g] = src + g
            gslot_ref[src + g] = dst + g
            return 0
        lax.fori_loop(0, k, put, 0)
        return seg + k
    used_in_tile = lax.fori_loop(0, N_EXPERTS, per_expert, 0, unroll=4)

    def put_any(q, _):
        gslot_ref[i * tile_granules + q] = 0
        return 0
    lax.fori_loop(used_in_tile, tile_granules, put_any, 0)


def _sort(totals, cnt8, goff, rec_p, h2_p, rec_s, h2_s, n_slots):
    tm = min(ROW_TILE, h2_p.shape[0])
    assert h2_s.shape[0] % tm == 0
    tiles_p = h2_p.shape[0] // tm
    tiles_s = h2_s.shape[0] // tm
    n_sorted = _sorted_rows(tm)
    row_p = lambda w: pl.BlockSpec((tm, w), lambda i, *_: (jnp.minimum(i, tiles_p - 1), 0))
    row_s = lambda w: pl.BlockSpec((tm, w), lambda i, *_: (jnp.maximum(i - tiles_p, 0), 0))
    return pl.pallas_call(
        functools.partial(_sort_kernel, tiles_p=tiles_p, n_sorted=n_sorted, n_slots=n_slots),
        grid_spec=pltpu.PrefetchScalarGridSpec(
            num_scalar_prefetch=3,
            grid=(tiles_p + tiles_s,),
            in_specs=[row_p(ROUTE_LANES), row_p(D_MODEL), row_s(ROUTE_LANES), row_s(D_MODEL)],
            out_specs=[pl.BlockSpec((n_sorted, HALF_D), lambda i, *_: (i, 0)),
                       pl.BlockSpec(memory_space=pltpu.SMEM), pl.BlockSpec(memory_space=pltpu.SMEM)],
            scratch_shapes=[pltpu.SMEM((N_EXPERTS,), I32)]),
        out_shape=[jax.ShapeDtypeStruct(((tiles_p + tiles_s) * n_sorted, HALF_D), U32),
                   jax.ShapeDtypeStruct((n_slots // GRANULE,), I32),
                   jax.ShapeDtypeStruct(((tiles_p + tiles_s) * n_sorted // GRANULE,), I32)],
        compiler_params=_cparams(1),
        name="moe_sort",
    )(totals, cnt8, goff, rec_p, h2_p, rec_s, h2_s)


def _expert_kernel(be_ref, meta_ref, gsrc_ref, srt_hbm, wg_hbm, wu_hbm, wd_hbm, ys_ref,
                   xbuf, wgs, wus, wds, wgb, wub, wdb, stage, sems, wsems):
    j = pl.program_id(0)
    n_active = meta_ref[0]
    gran_per_block = BLOCK_GRANULES

    def granule_copy(blk, g, slot):
        return pltpu.make_async_copy(_granule(srt_hbm, gsrc_ref[blk * gran_per_block + g]),
                                     xbuf.at[slot, pl.ds(g * GRANULE, GRANULE)], sems.at[slot])

    def gather(blk, slot):
        for g in range(gran_per_block):
            granule_copy(blk, g, slot).start(priority=g % 2)

    def drain(slot):
        for g in range(gran_per_block):
            granule_copy(0, g, slot).wait()

    def weight_copies(e, p):
        return [pltpu.make_async_copy(wg_hbm.at[e], wgs.at[p], wsems.at[p]),
                pltpu.make_async_copy(wu_hbm.at[e], wus.at[p], wsems.at[p]),
                pltpu.make_async_copy(wd_hbm.at[e], wds.at[p], wsems.at[p])]

    @pl.when(j == 0)
    def _():
        gather(0, 0)
        stage[0] = 0
        for cp in weight_copies(be_ref[0], 0):
            cp.start()

    @pl.when(j < n_active)
    def _():
        slot = j % 2
        e = be_ref[j]

        @pl.when((j == 0) | (e != be_ref[jnp.maximum(j - 1, 0)]))
        def _():
            p = stage[0]
            for cp in weight_copies(e, p):
                cp.wait()
            wgb[...] = wgs[p].astype(BF16)
            wub[...] = wus[p].astype(BF16)
            wdb[...] = wds[p].astype(BF16)
            nxt = lax.while_loop(lambda k: (k < n_active) & (be_ref[jnp.minimum(k, n_active - 1)] == e),
                                 lambda k: k + 1, j + 1)

            @pl.when(nxt < n_active)
            def _():
                for cp in weight_copies(be_ref[nxt], 1 - p):
                    cp.start()
            stage[0] = 1 - p

        gather(jnp.minimum(j + 1, n_active - 1), 1 - slot)
        drain(slot)
        x_lo, x_hi = _unpack_halves(xbuf[slot])
        g = _dot(x_lo, wgb[0:HALF_D, :]) + _dot(x_hi, wgb[HALF_D:D_MODEL, :])
        u = _dot(x_lo, wub[0:HALF_D, :]) + _dot(x_hi, wub[HALF_D:D_MODEL, :])
        hmid = (g * _sigmoid(g) * u).astype(BF16)
        y = _dot(hmid, wdb[...])
        ys_ref[...] = _pack_halves(y[:, :HALF_D].astype(BF16).astype(F32), y[:, HALF_D:].astype(BF16).astype(F32))

        @pl.when(j == n_active - 1)
        def _():
            drain(1 - slot)

    @pl.when(j >= meta_ref[0])
    def _():
        ys_ref[...] = jnp.zeros(ys_ref.shape, U32)


def _experts(block_e, meta, gsrc, srt, wg, wu, wd, n_slots):
    n_blocks = n_slots // EXPERT_ROWS
    anyspec = pl.BlockSpec(memory_space=pl.ANY)
    return pl.pallas_call(
        _expert_kernel,
        grid_spec=pltpu.PrefetchScalarGridSpec(
            num_scalar_prefetch=3,
            grid=(n_blocks,),
            in_specs=[anyspec, anyspec, anyspec, anyspec],
            out_specs=pl.BlockSpec((EXPERT_ROWS, HALF_D), lambda j, be, meta, gs: (j, 0)),
            scratch_shapes=[pltpu.VMEM((2, EXPERT_ROWS, HALF_D), U32),
                            pltpu.VMEM((2, D_MODEL, D_EXPERT), F32), pltpu.VMEM((2, D_MODEL, D_EXPERT), F32),
                            pltpu.VMEM((2, D_EXPERT, D_MODEL), F32),
                            pltpu.VMEM((D_MODEL, D_EXPERT), BF16), pltpu.VMEM((D_MODEL, D_EXPERT), BF16),
                            pltpu.VMEM((D_EXPERT, D_MODEL), BF16),
                            pltpu.SMEM((1,), I32), pltpu.SemaphoreType.DMA((2,)), pltpu.SemaphoreType.DMA((2,))]),
        out_shape=jax.ShapeDtypeStruct((n_slots, HALF_D), U32),
        compiler_params=_cparams(1),
        name="moe_experts",
    )(block_e, meta, gsrc, srt, wg, wu, wd)


def _combine_kernel(gslot_ref, rec_ref, x1_ref, gtf_ref, g_ref, ys_hbm, y_ref, cbuf, sems, *, tile_base, n_tiles):
    i = pl.program_id(0)
    slot = i % 2
    tm = x1_ref.shape[0]
    n_sorted = cbuf.shape[1]
    tile_granules = n_sorted // GRANULE

    def granule_copy(slot_granule, q, slot_):
        return pltpu.make_async_copy(_granule(ys_hbm, slot_granule),
                                     cbuf.at[slot_, pl.ds(q * GRANULE, GRANULE)], sems.at[slot_])

    def gather(tile, slot_):
        for q in range(tile_granules):
            granule_copy(gslot_ref[tile * tile_granules + q], q, slot_).start(priority=q % 2)

    def drain(slot_):
        for q in range(tile_granules):
            granule_copy(0, q, slot_).wait()

    @pl.when(i == 0)
    def _():
        gather(tile_base, 0)

    gather(tile_base + jnp.minimum(i + 1, n_tiles - 1), 1 - slot)
    drain(slot)

    rec = rec_ref[...]
    s1 = rec[:, 0:1].astype(I32)
    s2 = rec[:, 1:2].astype(I32)
    col = lax.broadcasted_iota(I32, (tm, n_sorted), 1)
    wmat = (jnp.where(col == s1, rec[:, 2:3], 0.0) + jnp.where(col == s2, rec[:, 3:4], 0.0)).astype(BF16)
    y_lo, y_hi = _unpack_halves(cbuf[slot])
    f = jnp.concatenate([_dot(wmat, y_lo), _dot(wmat, y_hi)], axis=1)
    y_ref[...] = x1_ref[...] + gtf_ref[...] * _rms(f, g_ref[...])

    @pl.when(i == n_tiles - 1)
    def _():
        drain(1 - slot)


def _combine(gslot, rec, x1, gtf, g, ys, rows_per_mod, tile_base):
    n = x1.shape[0]
    tm = min(ROW_TILE, n)
    if rows_per_mod:
        tiles_per_mod = rows_per_mod // tm
        mod_spec = pl.BlockSpec((None, 1, D_MODEL), lambda i, *_: (i // tiles_per_mod, 0, 0))
    else:
        mod_spec = pl.BlockSpec((tm, D_MODEL), lambda i, *_: (i, 0))
    row = lambda w: pl.BlockSpec((tm, w), lambda i, *_: (i, 0))
    return pl.pallas_call(
        functools.partial(_combine_kernel, tile_base=tile_base, n_tiles=n // tm),
        grid_spec=pltpu.PrefetchScalarGridSpec(
            num_scalar_prefetch=1,
            grid=(n // tm,),
            in_specs=[row(ROUTE_LANES), row(D_MODEL), mod_spec,
                      pl.BlockSpec((1, D_MODEL), lambda i, *_: (0, 0)),
                      pl.BlockSpec(memory_space=pl.ANY)],
            out_specs=row(D_MODEL),
            scratch_shapes=[pltpu.VMEM((2, _sorted_rows(tm), HALF_D), U32), pltpu.SemaphoreType.DMA((2,))]),
        out_shape=jax.ShapeDtypeStruct((n, D_MODEL), F32),
        compiler_params=_cparams(1),
        name="moe_combine",
    )(gslot, rec, x1, gtf, g, ys)


def _rope_tables(pos):
    half = HEAD_DIM // 2
    inv = jnp.power(jnp.float32(ROPE_THETA), -jnp.arange(half, dtype=F32) / half)
    ang = pos.astype(F32)[:, None] * inv[None, :]
    cos = jnp.cos(ang)
    sin = jnp.sin(ang)
    reps = LANES // HEAD_DIM
    cos_t = jnp.tile(jnp.concatenate([cos, cos], axis=-1), (1, reps))
    sin_t = jnp.tile(jnp.concatenate([-sin, sin], axis=-1), (1, reps))
    return cos_t, sin_t


def _q_perm_index():
    g = jnp.arange(GQA_GROUP)[:, None, None]
    h = jnp.arange(N_KV_HEADS)[None, :, None]
    d = jnp.arange(HEAD_DIM)[None, None, :]
    return ((h * GQA_GROUP + g) * HEAD_DIM + d).reshape(-1)


def _block_diag_gates(w_a, w_x):
    per = MXU_DIM // LRU_BLOCK_W
    groups = LRU_BLOCKS // per

    def bd(w):
        w = w.reshape(groups, per, LRU_BLOCK_W, LRU_BLOCK_W)
        eye = jnp.eye(per, dtype=w.dtype)
        full = jnp.einsum('gpij,pq->gpiqj', w, eye)
        return full.reshape(groups, MXU_DIM, MXU_DIM)
    return jnp.concatenate([bd(w_a), bd(w_x)], axis=-1).astype(BF16)


def _layer_forward(xp, xs_tm, ck, cv, cconv, ch, mod_p, mod_s, p):
    batch, seq, _ = xp.shape
    nseq, _, _, _ = ck.shape
    t_new = xs_tm.shape[0] // nseq
    n_p = batch * seq
    n_s = xs_tm.shape[0]

    perm = _q_perm_index()
    w_in = p['w_in']
    w_in_b = jnp.concatenate([w_in[:, :Q_DIM][:, perm], w_in[:, Q_DIM:]], axis=1).astype(BF16)
    wab = p['w_attn_branch'][perm, :].astype(BF16)
    wlb = p['w_lru_branch'].astype(BF16)
    wout = p['w_out'].astype(BF16)
    wbd = _block_diag_gates(p['w_lru_a'], p['w_lru_x'])
    row = lambda v: v.reshape(1, -1)
    wr = jnp.concatenate([p['w_router_group'], p['w_router_expert'],
                          jnp.zeros((D_MODEL, ROUTE_LANES - N_GROUPS - N_EXPERTS), F32)], axis=1)
    wr_top = wr.astype(BF16)
    wr_pair = jnp.concatenate([wr_top, (wr - wr_top.astype(F32)).astype(BF16)], axis=1)
    br = jnp.concatenate([p['b_router_group'], p['b_router_expert'],
                          jnp.zeros((ROUTE_LANES - N_GROUPS - N_EXPERTS,), F32)]).reshape(1, -1)
    wg = p['w_exp_gate']
    wu = p['w_exp_up']
    wd = p['w_exp_down']

    def mods(mod):
        return [mod[:, k * D_MODEL:(k + 1) * D_MODEL] for k in range(6)]
    sh_a_p, sc_a_p, gt_a_p, sh_f_p, sc_f_p, gt_f_p = [m.reshape(batch, 1, D_MODEL) for m in mods(mod_p)]
    sh_a_s, sc_a_s, gt_a_s, sh_f_s, sc_f_s, gt_f_s = [jnp.tile(m, (t_new, 1)) for m in mods(mod_s)]

    lru_w = (p['w_conv'], row(p['b_conv']), wbd, row(p['b_lru_a']), row(p['b_lru_x']), row(p['lru_lambda']))
    cos_p, sin_p = _rope_tables(jnp.arange(seq, dtype=I32))
    q_p, k_p, v_p, sga_p, sgr_p, klast_p, vlast_p, ol_p, xtail_p, hlast_p = _inproj_lru(
        xp.reshape(n_p, D_MODEL), sc_a_p, sh_a_p, row(p['g_pre_mix']), cos_p, sin_p, w_in_b, *lru_w, batch, seq)
    pos_s = jnp.repeat(PAST_LEN_ + jnp.arange(t_new, dtype=I32), nseq)
    cos_s, sin_s = _rope_tables(pos_s)
    q_s, k_s, v_s, xl_s, yl_s, sga_s, sgr_s = _inproj(
        xs_tm, sc_a_s, sh_a_s, row(p['g_pre_mix']), cos_s, sin_s, w_in_b, rows_per_mod=0,
        pos_tiles=n_s // min(ROW_TILE, n_s))

    sinks_perm = p['sinks']
    oa_p = (sinks_perm, q_p, k_p, v_p)
    rows = t_new * GQA_GROUP
    q_s3 = q_s.reshape(t_new, nseq, GQA_GROUP, KV_DIM).transpose(1, 0, 2, 3).reshape(nseq, rows, KV_DIM)
    kn = k_s.reshape(t_new, nseq, KV_DIM).transpose(1, 0, 2)
    vn = v_s.reshape(t_new, nseq, KV_DIM).transpose(1, 0, 2)
    kc = ck.transpose(0, 2, 3, 1).reshape(nseq, KV_DIM, WINDOW)
    vc = cv.transpose(0, 2, 3, 1).reshape(nseq, KV_DIM, WINDOW)
    sink_rows = jnp.tile(p['sinks'].reshape(N_KV_HEADS, 1, GQA_GROUP), (1, t_new, 1)).reshape(N_KV_HEADS, rows, 1)
    oa_s3 = _attn_sample(sink_rows, q_s3, kn, vn, kc, vc)
    oa_s = oa_s3.reshape(nseq, t_new, Q_DIM).transpose(1, 0, 2).reshape(n_s, Q_DIM)

    ol_s3, hlast_s = _lru_sample(xl_s.reshape(t_new, nseq, LRU_WIDTH), yl_s.reshape(t_new, nseq, LRU_WIDTH),
                                 cconv.transpose(1, 0, 2), ch, *lru_w)
    ol_s = ol_s3.reshape(n_s, LRU_WIDTH)

    tm_post = min(ROW_TILE, n_p)
    tri = jnp.tril(jnp.ones((tm_post, tm_post), F32), -1).astype(BF16)
    post_w = (wab, wlb, wout, row(p['g_post_mix']), row(p['g_pre_ffn']), wr_pair, wr_top, br)
    zero_cnt = jnp.zeros((1, ROUTE_LANES), F32)
    x1_p, h2_p, route_p, stat_p, cnt_p = _post(oa_p, ol_p, sga_p, sgr_p, xp.reshape(n_p, D_MODEL),
                                               gt_a_p, sc_f_p, sh_f_p, *post_w, tri, zero_cnt, rows_per_mod=seq)
    tm_s = min(ROW_TILE, n_s)
    tri_s = tri if tm_s == tm_post else jnp.tril(jnp.ones((tm_s, tm_s), F32), -1).astype(BF16)
    x1_s, h2_s, route_s, stat_s, cnt_all = _post(oa_s, ol_s, sga_s, sgr_s, xs_tm,
                                                 gt_a_s, sc_f_s, sh_f_s, *post_w, tri_s, cnt_p, rows_per_mod=0)

    e_lanes = slice(N_GROUPS, N_GROUPS + N_EXPERTS)
    to_granules = lambda v: (v.astype(I32) // GRANULE).reshape(-1)
    totals = to_granules(cnt_all[0, e_lanes])
    stats = jnp.concatenate([stat_p, stat_s], axis=0).reshape(-1, SUBLANES, ROUTE_LANES)
    cnt8 = to_granules(stats[:, 0, e_lanes])
    goff = to_granules(stats[:, 1, e_lanes])
    tiles_p = n_p // tm_post
    n_tiles = tiles_p + n_s // tm_s
    max_rows = 2 * (n_p + n_s) + n_tiles * N_EXPERTS * (GRANULE - 1) + N_EXPERTS * (EXPERT_ROWS - GRANULE)
    n_blocks = -(-max_rows // EXPERT_ROWS)
    n_slots = n_blocks * EXPERT_ROWS
    srt, gsrc, gslot = _sort(totals, cnt8, goff, route_p, h2_p, route_s, h2_s, n_slots)
    block_e, meta = _plan(totals, n_blocks)
    ys = _experts(block_e, meta, gsrc, srt, wg, wu, wd, n_slots)
    y_p = _combine(gslot, route_p, x1_p, gt_f_p, row(p['g_post_ffn']), ys, seq, 0)
    y_s = _combine(gslot, route_s, x1_s, gt_f_s, row(p['g_post_ffn']), ys, 0, tiles_p)

    k_new_p = klast_p.reshape(batch, WINDOW, N_KV_HEADS, HEAD_DIM)
    v_new_p = vlast_p.reshape(batch, WINDOW, N_KV_HEADS, HEAD_DIM)
    conv_p = xtail_p[:, -(CONV_WIDTH - 1):]
    h_p = hlast_p.reshape(batch, LRU_WIDTH)
    k_new_s = jnp.concatenate([ck, kn.reshape(nseq, t_new, N_KV_HEADS, HEAD_DIM)], axis=1)[:, -WINDOW:]
    v_new_s = jnp.concatenate([cv, vn.reshape(nseq, t_new, N_KV_HEADS, HEAD_DIM)], axis=1)[:, -WINDOW:]
    xl_s3 = xl_s.reshape(t_new, nseq, LRU_WIDTH).transpose(1, 0, 2).astype(F32)
    conv_s = jnp.concatenate([cconv, xl_s3], axis=1)[:, -(CONV_WIDTH - 1):]
    return (y_p.reshape(batch, seq, D_MODEL), y_s, k_new_p, v_new_p, conv_p, h_p,
            k_new_s, v_new_s, conv_s, hlast_s)


PAST_LEN_ = 16384

PARAM_NAMES = ('w_ada', 'b_ada', 'g_pre_mix', 'g_post_mix', 'g_pre_ffn', 'g_post_ffn', 'w_in', 'sinks',
               'w_conv', 'b_conv', 'w_lru_a', 'b_lru_a', 'w_lru_x', 'b_lru_x', 'lru_lambda',
               'w_attn_branch', 'w_lru_branch', 'w_out', 'w_router_group', 'b_router_group',
               'w_router_expert', 'b_router_expert', 'w_exp_gate', 'w_exp_up', 'w_exp_down')


def kernel(x_prompt, x_sample, cache_k_win, cache_v_win, state_conv, state_h, c_prompt, c_sample, w_ada, b_ada, g_pre_mix, g_post_mix, g_pre_ffn, g_post_ffn, w_in, sinks, w_conv, b_conv, w_lru_a, b_lru_a, w_lru_x, b_lru_x, lru_lambda, w_attn_branch, w_lru_branch, w_out, w_router_group, b_router_group, w_router_expert, b_router_expert, w_exp_gate, w_exp_up, w_exp_down):
    weights = (w_ada, b_ada, g_pre_mix, g_post_mix, g_pre_ffn, g_post_ffn, w_in, sinks,
               w_conv, b_conv, w_lru_a, b_lru_a, w_lru_x, b_lru_x, lru_lambda,
               w_attn_branch, w_lru_branch, w_out, w_router_group, b_router_group,
               w_router_expert, b_router_expert, w_exp_gate, w_exp_up, w_exp_down)
    depth = w_ada.shape[0]
    batch = x_prompt.shape[0]
    nseq, t_new, _ = x_sample.shape
    y_p = x_prompt
    y_s = x_sample.transpose(1, 0, 2).reshape(t_new * nseq, D_MODEL)
    c_all = jnp.concatenate([c_prompt, c_sample], axis=0)
    outs = [[] for _ in range(8)]
    for layer in range(depth):
        p = {name: w[layer] for name, w in zip(PARAM_NAMES, weights)}
        mod = _ada(c_all, p['w_ada'], p['b_ada'].reshape(1, -1))
        res = _layer_forward(y_p, y_s, cache_k_win[layer], cache_v_win[layer], state_conv[layer],
                             state_h[layer], mod[:batch], mod[batch:], p)
        y_p, y_s = res[0], res[1]
        for o, r in zip(outs, res[2:]):
            o.append(r)
    y_sample = y_s.reshape(t_new, nseq, D_MODEL).transpose(1, 0, 2)
    return (y_p, y_sample) + tuple(jnp.stack(o) for o in outs)
```

```python
import functools

import jax
import jax.numpy as jnp
from jax import lax
from jax.experimental import pallas as pl
from jax.experimental.pallas import tpu as pltpu

F32 = jnp.float32
BF16 = jnp.bfloat16
I32 = jnp.int32

D_MODEL = 1024
N_HEADS = 16
HEAD_DIM = 64
N_KV_HEADS = 4
GQA_GROUP = 4
WINDOW = 128
ROPE_THETA = 10000.0
NEG_INF = -1e30
LRU_WIDTH = 1024
LRU_BLOCKS = 16
LRU_BLOCK_W = 64
CONV_WIDTH = 4
LRU_C = 8.0
N_GROUPS = 4
EXPERTS_PER_GROUP = 8
N_EXPERTS = 32
D_EXPERT = 512
MOE_BLOCK = 128
NORM_EPS = 1e-6
Q_DIM = N_HEADS * HEAD_DIM
KV_DIM = N_KV_HEADS * HEAD_DIM
IN_DIM = Q_DIM + 2 * KV_DIM + 2 * LRU_WIDTH + 2 * D_MODEL

LANES = 128
SUBLANES = 8
MXU_DIM = 256
VMEM_LIMIT = 56 * 1024 * 1024
VMEM_LIMIT_BIG = 60 * 1024 * 1024

ROW_TILE = 512
ROUTE_LANES = LANES
GRANULE = SUBLANES
EXPERT_ROWS = 512
HALF_D = D_MODEL // 2
U32 = jnp.uint32


def _cparams(n_axes, vmem=VMEM_LIMIT, flags=None):
    return pltpu.CompilerParams(dimension_semantics=("arbitrary",) * n_axes, vmem_limit_bytes=vmem, flags=flags)


def _rms(x, g):
    ms = jnp.mean(x * x, axis=-1, keepdims=True)
    return x * lax.rsqrt(ms + NORM_EPS) * g


def _sigmoid(x):
    return 1.0 / (1.0 + jnp.exp(-x))


def _dot(a, b):
    return jnp.dot(a, b, preferred_element_type=F32)


def _ada_kernel(c_ref, w_ref, b_ref, o_ref):
    c = c_ref[...]
    s = (c * _sigmoid(c)).astype(BF16)
    o_ref[...] = _dot(s, w_ref[...].astype(BF16)) + b_ref[...]


def _ada(c_all, w_ada, b_ada):
    r = c_all.shape[0]
    n = w_ada.shape[1]
    return pl.pallas_call(
        _ada_kernel,
        grid=(n // D_MODEL,),
        in_specs=[pl.BlockSpec((r, D_MODEL), lambda j: (0, 0)),
                  pl.BlockSpec((D_MODEL, D_MODEL), lambda j: (0, j)),
                  pl.BlockSpec((1, D_MODEL), lambda j: (0, j))],
        out_specs=pl.BlockSpec((r, D_MODEL), lambda j: (0, j)),
        out_shape=jax.ShapeDtypeStruct((r, n), F32),
        compiler_params=_cparams(1),
        name="ada_mod",
    )(c_all, w_ada, b_ada)


_O1 = Q_DIM
_O2 = _O1 + KV_DIM
_O3 = _O2 + KV_DIM
_O4 = _O3 + LRU_WIDTH
_O5 = _O4 + LRU_WIDTH
_O6 = _O5 + D_MODEL


def _prenorm(x_ref, g_ref, sc_ref, sh_ref):
    h = _rms(x_ref[...], g_ref[...]) * (1.0 + sc_ref[...]) + sh_ref[...]
    return h.astype(BF16)


PIECE_COLS = 256


def _plain_pieces(h_ref, w_ref, base, width, store):
    def piece(c0):
        def run():
            store(c0, _dot(h_ref[...], w_ref[:, base + c0:base + c0 + PIECE_COLS]))
        return run
    return [piece(c0) for c0 in range(0, width, PIECE_COLS)]


def _qkv_gate_pieces(h_ref, w_ref, cos_ref, sin_ref, q_ref, k_ref, v_ref, sga_ref, sgr_ref, last_refs=None):
    def rope(t):
        cos = cos_ref[...]
        sin = sin_ref[...]
        lane = lax.broadcasted_iota(I32, cos.shape, 1)
        first_half = (lane % HEAD_DIM) < (HEAD_DIM // 2)
        rot = jnp.where(first_half, pltpu.roll(t, LANES - HEAD_DIM // 2, 1), pltpu.roll(t, HEAD_DIM // 2, 1))
        return t * cos + rot * sin

    def q_piece(c0):
        def run():
            qf = _dot(h_ref[...], w_ref[:, c0:c0 + PIECE_COLS])
            for c in range(PIECE_COLS // LANES):
                q_ref[:, c0 + c * LANES:c0 + (c + 1) * LANES] = rope(qf[:, c * LANES:(c + 1) * LANES]).astype(BF16)
        return run

    def kv_piece():
        kv = _dot(h_ref[...], w_ref[:, _O1:_O3])
        for c in range(KV_DIM // LANES):
            k_ref[:, c * LANES:(c + 1) * LANES] = rope(kv[:, c * LANES:(c + 1) * LANES])
        v_ref[...] = kv[:, KV_DIM:]
        if last_refs is not None:
            t = k_ref.shape[0]
            last_refs[0][...] = k_ref[t - WINDOW:t, :]
            last_refs[1][...] = v_ref[t - WINDOW:t, :]

    def gate_piece(o_ref, base, c0):
        def run():
            z = _dot(h_ref[...], w_ref[:, base + c0:base + c0 + PIECE_COLS])
            o_ref[:, c0:c0 + PIECE_COLS] = _sigmoid(z).astype(BF16)
        return run

    pieces = [q_piece(c0) for c0 in range(0, Q_DIM, PIECE_COLS)] + [kv_piece]
    pieces += [gate_piece(sga_ref, _O5, c0) for c0 in range(0, D_MODEL, PIECE_COLS)]
    pieces += [gate_piece(sgr_ref, _O6, c0) for c0 in range(0, D_MODEL, PIECE_COLS)]
    return pieces


def _inproj_kernel(x_ref, sc_ref, sh_ref, g_ref, cos_ref, sin_ref, w_ref,
                   q_ref, k_ref, v_ref, xl_ref, yl_ref, sga_ref, sgr_ref, hbuf):
    hbuf[...] = _prenorm(x_ref, g_ref, sc_ref, sh_ref)
    xl_ref[...] = _dot(hbuf[...], w_ref[:, _O3:_O4]).astype(BF16)
    yl_ref[...] = _dot(hbuf[...], w_ref[:, _O4:_O5]).astype(BF16)
    for piece in _qkv_gate_pieces(hbuf, w_ref, cos_ref, sin_ref, q_ref, k_ref, v_ref, sga_ref, sgr_ref):
        piece()


def _inproj(x, sc, sh, g, cos, sin, w_in, rows_per_mod, pos_tiles):
    n = x.shape[0]
    tm = min(ROW_TILE, n)
    if rows_per_mod:
        tiles_per_mod = rows_per_mod // tm
        mod_spec = pl.BlockSpec((None, 1, D_MODEL), lambda i: (i // tiles_per_mod, 0, 0))
    else:
        mod_spec = pl.BlockSpec((tm, D_MODEL), lambda i: (i, 0))
    row = lambda w: pl.BlockSpec((tm, w), lambda i: (i, 0))
    outs = [(Q_DIM, BF16), (KV_DIM, F32), (KV_DIM, F32), (LRU_WIDTH, BF16), (LRU_WIDTH, BF16),
            (D_MODEL, BF16), (D_MODEL, BF16)]
    return pl.pallas_call(
        _inproj_kernel,
        grid=(n // tm,),
        in_specs=[row(D_MODEL), mod_spec, mod_spec,
                  pl.BlockSpec((1, D_MODEL), lambda i: (0, 0)),
                  pl.BlockSpec((tm, LANES), lambda i: (i % pos_tiles, 0)),
                  pl.BlockSpec((tm, LANES), lambda i: (i % pos_tiles, 0)),
                  pl.BlockSpec((D_MODEL, IN_DIM), lambda i: (0, 0))],
        out_specs=[row(w) for w, _ in outs],
        out_shape=[jax.ShapeDtypeStruct((n, w), dt) for w, dt in outs],
        scratch_shapes=[pltpu.VMEM((tm, D_MODEL), BF16)],
        compiler_params=_cparams(1),
        name="in_proj",
    )(x, sc, sh, g, cos, sin, w_in)


def _head_masks(shape):
    lane = lax.broadcasted_iota(I32, shape, 1)
    return [(lane // HEAD_DIM) == h for h in range(N_KV_HEADS)]


def _attention_core(q_perm, kall, vall, valid, sink_of, rows):
    masks_b = _head_masks((rows, KV_DIM))
    zero_b = jnp.zeros((rows, KV_DIM), BF16)
    pieces = []
    for h in range(N_KV_HEADS):
        for g in range(GQA_GROUP):
            pieces.append(jnp.where(masks_b[h], q_perm[g], zero_b))
    q_big = jnp.concatenate(pieces, axis=0)
    s_all = lax.dot_general(q_big, kall, (((1,), (1,)), ((), ())), preferred_element_type=F32)
    s_all = s_all * (HEAD_DIM ** -0.5)
    p_chunks, inv_chunks = [], []
    for h in range(N_KV_HEADS):
        for g in range(GQA_GROUP):
            c = h * GQA_GROUP + g
            s = jnp.where(valid, s_all[c * rows:(c + 1) * rows], NEG_INF)
            sink = sink_of(h, g)
            m = jnp.maximum(jnp.max(s, axis=-1, keepdims=True), sink)
            p = jnp.exp(s - m)
            denom = jnp.sum(p, axis=-1, keepdims=True) + jnp.exp(sink - m)
            p_chunks.append(p.astype(BF16))
            inv_chunks.append(1.0 / denom)
    o_all = _dot(jnp.concatenate(p_chunks, axis=0), vall)
    outs = []
    for g in range(GQA_GROUP):
        acc = jnp.zeros((rows, KV_DIM), F32)
        for h in range(N_KV_HEADS):
            c = h * GQA_GROUP + g
            acc = acc + jnp.where(masks_b[h], o_all[c * rows:(c + 1) * rows] * inv_chunks[c], 0.0)
        outs.append(acc)
    return outs


ATTN_Q_BLOCKS = 4


def _attention_tile(sink_ref, q_ref, kc_ref, kp_ref, vc_ref, vp_ref, o_ref, seq_start, between=None):
    kall = jnp.concatenate([kp_ref[...], kc_ref[...]], axis=0).astype(BF16)
    vall = jnp.concatenate([vp_ref[...], vc_ref[...]], axis=0).astype(BF16)
    qi = lax.broadcasted_iota(I32, (WINDOW, 2 * WINDOW), 0)
    kj = lax.broadcasted_iota(I32, (WINDOW, 2 * WINDOW), 1)
    dist = qi + WINDOW - kj
    in_window = (dist >= 0) & (dist <= WINDOW)
    for c in range(ATTN_Q_BLOCKS):
        rows = slice(c * WINDOW, (c + 1) * WINDOW)
        keys = slice(c * WINDOW, (c + 2) * WINDOW)
        valid = in_window & ((kj >= WINDOW) | jnp.logical_not(seq_start)) if c == 0 else in_window
        q_perm = [q_ref[rows, g * KV_DIM:(g + 1) * KV_DIM] for g in range(GQA_GROUP)]
        outs = _attention_core(q_perm, kall[keys], vall[keys], valid,
                               lambda h, g: sink_ref[h * GQA_GROUP + g], WINDOW)
        for g in range(GQA_GROUP):
            o_ref[rows, g * KV_DIM:(g + 1) * KV_DIM] = outs[g].astype(BF16)
        if between is not None:
            between(c)


def _attn_prompt_kernel(sink_ref, q_ref, kc_ref, kp_ref, vc_ref, vp_ref, o_ref):
    _attention_tile(sink_ref, q_ref, kc_ref, kp_ref, vc_ref, vp_ref, o_ref, pl.program_id(1) == 0)


def _attn_prompt(sinks, q, k, v, batch, seq):
    step_rows = ATTN_Q_BLOCKS * WINDOW
    nb = seq // step_rows
    cur = lambda w: pl.BlockSpec((step_rows, w), lambda b, j: (b * nb + j, 0))
    prev = lambda w: pl.BlockSpec(
        (WINDOW, w), lambda b, j: (jnp.maximum((b * nb + j) * ATTN_Q_BLOCKS - 1, 0), 0))
    return pl.pallas_call(
        _attn_prompt_kernel,
        grid=(batch, nb),
        in_specs=[pl.BlockSpec(memory_space=pltpu.SMEM),
                  cur(Q_DIM), cur(KV_DIM), prev(KV_DIM), cur(KV_DIM), prev(KV_DIM)],
        out_specs=cur(Q_DIM),
        out_shape=jax.ShapeDtypeStruct((batch * seq, Q_DIM), BF16),
        compiler_params=_cparams(2),
        name="attn_prompt",
    )(sinks, q, k, k, v, v)


SEQ_PER_STEP = 8


def _attn_sample_kernel(sink_ref, q_ref, kn_ref, vn_ref, kc_ref, vc_ref, o_ref, kbuf, vbuf, *, t_new):
    rows = GQA_GROUP * t_new
    kbuf[...] = jnp.zeros(kbuf.shape, F32)
    vbuf[...] = jnp.zeros(vbuf.shape, F32)
    ri = lax.broadcasted_iota(I32, (rows, 2 * WINDOW), 0)
    kj = lax.broadcasted_iota(I32, (rows, 2 * WINDOW), 1)
    tq = ri // GQA_GROUP
    valid = (kj >= tq) & (kj <= tq + WINDOW) & (kj < WINDOW + t_new)
    nt_dims = (((1,), (1,)), ((), ()))
    for s in range(SEQ_PER_STEP):
        kbuf[0:t_new, :] = kn_ref[s]
        vbuf[0:t_new, :] = vn_ref[s]
        k_cache_t = kc_ref[s].astype(BF16)
        v_cache_t = vc_ref[s].astype(BF16)
        k_new = kbuf[...].astype(BF16)
        v_new = vbuf[...].astype(BF16)
        qs = q_ref[s]
        masks_b = _head_masks((rows, KV_DIM))
        zero_b = jnp.zeros((rows, KV_DIM), BF16)
        q_big = jnp.concatenate([jnp.where(masks_b[h], qs, zero_b) for h in range(N_KV_HEADS)], axis=0)
        s_all = jnp.concatenate([_dot(q_big, k_cache_t),
                                 lax.dot_general(q_big, k_new, nt_dims, preferred_element_type=F32)], axis=1)
        s_all = s_all * (HEAD_DIM ** -0.5)
        acc = jnp.zeros((rows, KV_DIM), F32)
        p_chunks, inv_chunks = [], []
        for h in range(N_KV_HEADS):
            sc = jnp.where(valid, s_all[h * rows:(h + 1) * rows], NEG_INF)
            sink = sink_ref[h]
            m = jnp.maximum(jnp.max(sc, axis=-1, keepdims=True), sink)
            p = jnp.exp(sc - m)
            denom = jnp.sum(p, axis=-1, keepdims=True) + jnp.exp(sink - m)
            p_chunks.append(p.astype(BF16))
            inv_chunks.append(1.0 / denom)
        p_all = jnp.concatenate(p_chunks, axis=0)
        o_all = (lax.dot_general(p_all[:, :WINDOW], v_cache_t, nt_dims, preferred_element_type=F32)
                 + _dot(p_all[:, WINDOW:], v_new))
        for h in range(N_KV_HEADS):
            acc = acc + jnp.where(masks_b[h], o_all[h * rows:(h + 1) * rows] * inv_chunks[h], 0.0)
        o_ref[s] = acc.astype(BF16)


def _attn_sample(sink_rows, q, kn, vn, kc, vc):
    nseq, rows, _ = q.shape
    t_new = kn.shape[1]
    sb = SEQ_PER_STEP
    blk = lambda r: pl.BlockSpec((sb, r, KV_DIM), lambda i: (i, 0, 0))
    cache = pl.BlockSpec((sb, KV_DIM, WINDOW), lambda i: (i, 0, 0))
    return pl.pallas_call(
        functools.partial(_attn_sample_kernel, t_new=t_new),
        grid=(nseq // sb,),
        in_specs=[pl.BlockSpec((N_KV_HEADS, rows, 1), lambda i: (0, 0, 0)),
                  blk(rows), blk(t_new), blk(t_new), cache, cache],
        out_specs=blk(rows),
        out_shape=jax.ShapeDtypeStruct((nseq, rows, KV_DIM), BF16),
        scratch_shapes=[pltpu.VMEM((WINDOW, KV_DIM), F32), pltpu.VMEM((WINDOW, KV_DIM), F32)],
        compiler_params=_cparams(1),
        name="attn_sample",
    )(sink_rows, q, kn, vn, kc, vc)


def _gelu_tanh(x):
    return 0.5 * x * (1.0 + jnp.tanh(0.7978845608028654 * (x + 0.044715 * x * x * x)))


def _lru_gates(xc, wbd_ref, ba, bx, lam):
    xcb = xc.astype(BF16)
    r_parts, i_parts = [], []
    for gidx in range(LRU_WIDTH // MXU_DIM):
        z = _dot(xcb[:, gidx * MXU_DIM:(gidx + 1) * MXU_DIM], wbd_ref[gidx])
        r_parts.append(z[:, :MXU_DIM])
        i_parts.append(z[:, MXU_DIM:])
    return _gate_math(jnp.concatenate(r_parts, axis=1), jnp.concatenate(i_parts, axis=1), xc, ba, bx, lam)


def _gate_math(zr, zi, xc, ba, bx, lam):
    r = _sigmoid(zr + ba)
    i = _sigmoid(zi + bx)
    softplus_neg_lam = jnp.maximum(-lam, 0.0) + jnp.log1p(jnp.exp(-jnp.abs(lam)))
    log_a = -LRU_C * r * softplus_neg_lam
    a = jnp.exp(log_a)
    y = 1.0 - a * a
    u = jnp.where(y > 0.0, y * lax.rsqrt(y), 0.0) * (i * xc)
    return a, u


LRU_CHUNK = 64


def _lru_chunk(c, xbuf, ybuf, hcar, ol_ref, wc_ref, bc_ref, wbd_ref, ba_ref, bx_ref, lam_ref):
    w = LRU_WIDTH
    n = LRU_CHUNK
    rows = slice(c * n, (c + 1) * n)
    r0 = SUBLANES + c * n
    xc = xbuf[r0:r0 + n, :] * wc_ref[CONV_WIDTH - 1:CONV_WIDTH, :] + bc_ref[...]
    for k in range(1, CONV_WIDTH):
        xc = xc + xbuf[r0 - k:r0 - k + n, :] * wc_ref[CONV_WIDTH - 1 - k:CONV_WIDTH - k, :]
    a, u = _lru_gates(xc, wbd_ref, ba_ref[...], bx_ref[...], lam_ref[...])

    ng = n // SUBLANES
    a3 = a.reshape(ng, SUBLANES, w)
    u3 = u.reshape(ng, SUBLANES, w)
    row = lax.broadcasted_iota(I32, (ng, SUBLANES, w), 1)
    d = 1
    while d < SUBLANES:
        a_s = jnp.where(row >= d, pltpu.roll(a3, d, 1), 1.0)
        u_s = jnp.where(row >= d, pltpu.roll(u3, d, 1), 0.0)
        u3 = a3 * u_s + u3
        a3 = a3 * a_s
        d *= 2
    carry = hcar[...]
    hs = []
    for gi in range(ng):
        hg = a3[gi] * carry + u3[gi]
        hs.append(hg)
        carry = hg[SUBLANES - 1:SUBLANES, :]
    hcar[...] = carry
    h = jnp.concatenate(hs, axis=0)
    ol_ref[rows, :] = (h * _gelu_tanh(ybuf[rows, :])).astype(BF16)


def _inproj_lru_kernel(x_ref, sc_ref, sh_ref, g_ref, cos_ref, sin_ref, w_ref,
                       wc_ref, bc_ref, wbd_ref, ba_ref, bx_ref, lam_ref,
                       q_ref, k_ref, v_ref, sga_ref, sgr_ref, klast_ref, vlast_ref, ol_ref, xtail_ref, hlast_ref,
                       hbuf, xbuf, ybuf, hcar, *, tiles_per_seq):
    t = x_ref.shape[0]

    @pl.when(pl.program_id(0) % tiles_per_seq == 0)
    def _():
        xbuf[0:SUBLANES, :] = jnp.zeros((SUBLANES, LRU_WIDTH), F32)
        hcar[...] = jnp.zeros((1, LRU_WIDTH), F32)

    hbuf[...] = _prenorm(x_ref, g_ref, sc_ref, sh_ref)

    xbuf[SUBLANES:SUBLANES + t, :] = _dot(hbuf[...], w_ref[:, _O3:_O4])
    ybuf[...] = _dot(hbuf[...], w_ref[:, _O4:_O5])

    pieces = _qkv_gate_pieces(hbuf, w_ref, cos_ref, sin_ref, q_ref, k_ref, v_ref, sga_ref, sgr_ref,
                              last_refs=(klast_ref, vlast_ref))
    n_chunks = t // LRU_CHUNK
    per_chunk = -(-len(pieces) // n_chunks)
    for c in range(n_chunks):
        _lru_chunk(c, xbuf, ybuf, hcar, ol_ref, wc_ref, bc_ref, wbd_ref, ba_ref, bx_ref, lam_ref)
        for piece in pieces[c * per_chunk:(c + 1) * per_chunk]:
            piece()

    tail = xbuf[t:t + SUBLANES, :]
    xtail_ref[...] = tail
    xbuf[0:SUBLANES, :] = tail
    hlast_ref[...] = hcar[...]


def _inproj_lru(x, sc, sh, g, cos, sin, w_in, wc, bc, wbd, ba, bx, lam, batch, seq):
    n = x.shape[0]
    tm = min(ROW_TILE, seq)
    tps = seq // tm
    mod_spec = pl.BlockSpec((None, 1, D_MODEL), lambda i: (i // tps, 0, 0))
    row = lambda w: pl.BlockSpec((tm, w), lambda i: (i, 0))
    full = lambda a: pl.BlockSpec(a.shape, lambda i: (0,) * a.ndim)
    per_seq = lambda r, w: pl.BlockSpec((None, r, w), lambda i: (i // tps, 0, 0))
    outs = [(Q_DIM, BF16), (KV_DIM, F32), (KV_DIM, F32), (D_MODEL, BF16), (D_MODEL, BF16)]
    return pl.pallas_call(
        functools.partial(_inproj_lru_kernel, tiles_per_seq=tps),
        grid=(n // tm,),
        in_specs=[row(D_MODEL), mod_spec, mod_spec, full(g),
                  pl.BlockSpec((tm, LANES), lambda i: (i % tps, 0)),
                  pl.BlockSpec((tm, LANES), lambda i: (i % tps, 0)),
                  full(w_in), full(wc), full(bc), full(wbd), full(ba), full(bx), full(lam)],
        out_specs=[row(w) for w, _ in outs] + [per_seq(WINDOW, KV_DIM), per_seq(WINDOW, KV_DIM)]
        + [row(LRU_WIDTH), per_seq(SUBLANES, LRU_WIDTH), per_seq(1, LRU_WIDTH)],
        out_shape=[jax.ShapeDtypeStruct((n, w), dt) for w, dt in outs]
        + [jax.ShapeDtypeStruct((batch, WINDOW, KV_DIM), F32)] * 2
        + [jax.ShapeDtypeStruct((n, LRU_WIDTH), BF16),
           jax.ShapeDtypeStruct((batch, SUBLANES, LRU_WIDTH), F32),
           jax.ShapeDtypeStruct((batch, 1, LRU_WIDTH), F32)],
        scratch_shapes=[pltpu.VMEM((tm, D_MODEL), BF16),
                        pltpu.VMEM((2 * SUBLANES + tm, LRU_WIDTH), F32), pltpu.VMEM((tm, LRU_WIDTH), F32),
                        pltpu.VMEM((1, LRU_WIDTH), F32)],
        compiler_params=_cparams(1, vmem=VMEM_LIMIT_BIG),
        name="in_proj_lru",
    )(x, sc, sh, g, cos, sin, w_in, wc, bc, wbd, ba, bx, lam)


def _lru_sample_kernel(xl_ref, yl_ref, cs_ref, h0_ref, wc_ref, bc_ref, wbd_ref, ba_ref, bx_ref, lam_ref,
                       o_ref, hlast_ref):
    t_new, nseq, w = xl_ref.shape
    xp = [cs_ref[k] for k in range(CONV_WIDTH - 1)] + [xl_ref[k].astype(F32) for k in range(t_new)]
    xcs = []
    for t in range(t_new):
        acc = bc_ref[...] + xp[t] * wc_ref[0:1, :]
        for k in range(1, CONV_WIDTH):
            acc = acc + xp[t + k] * wc_ref[k:k + 1, :]
        xcs.append(acc)
    xc = jnp.concatenate(xcs, axis=0)
    a, u = _lru_gates(xc, wbd_ref, ba_ref[...], bx_ref[...], lam_ref[...])
    h = h0_ref[...]
    for t in range(t_new):
        h = a[t * nseq:(t + 1) * nseq] * h + u[t * nseq:(t + 1) * nseq]
        o_ref[t] = (h * _gelu_tanh(yl_ref[t].astype(F32))).astype(BF16)
    hlast_ref[...] = h


def _lru_sample(xl, yl, cs, h0, wc, bc, wbd, ba, bx, lam):
    t_new, nseq, w = xl.shape
    full = lambda shp: pl.BlockSpec(shp, lambda i: (0,) * len(shp))
    args = (xl, yl, cs, h0, wc, bc, wbd, ba, bx, lam)
    return pl.pallas_call(
        _lru_sample_kernel,
        grid=(1,),
        in_specs=[full(a.shape) for a in args],
        out_specs=[full((t_new, nseq, w)), full((nseq, w))],
        out_shape=[jax.ShapeDtypeStruct((t_new, nseq, w), BF16), jax.ShapeDtypeStruct((nseq, w), F32)],
        compiler_params=_cparams(1),
        name="lru_sample",
    )(*args)


def _post_kernel(oa_ref, ol_ref, sga_ref, sgr_ref, x_ref, gta_ref, scf_ref, shf_ref,
                 wab_ref, wlb_ref, wout_ref, gpm_ref, gpf_ref, wrp_ref, wrt_ref, br_ref, tri_ref, cin_ref,
                 x1_ref, h2_ref, route_ref, stat_ref, cnt_ref, carry):
    @pl.when(pl.program_id(0) == 0)
    def _():
        carry[...] = cin_ref[...]

    b_attn = _dot(oa_ref[...], wab_ref[...])
    b_lru = _dot(ol_ref[...], wlb_ref[...])
    _post_tail(b_attn, b_lru, sga_ref, sgr_ref, x_ref, gta_ref, scf_ref, shf_ref, wout_ref, gpm_ref, gpf_ref,
               wrp_ref, wrt_ref, br_ref, tri_ref, x1_ref, h2_ref, route_ref, stat_ref, cnt_ref, carry)


def _post_attn_kernel(sink_ref, q_ref, kc_ref, kp_ref, vc_ref, vp_ref,
                      ol_ref, sga_ref, sgr_ref, x_ref, gta_ref, scf_ref, shf_ref,
                      wab_ref, wlb_ref, wout_ref, gpm_ref, gpf_ref, wrp_ref, wrt_ref, br_ref, tri_ref, cin_ref,
                      x1_ref, h2_ref, route_ref, stat_ref, cnt_ref, carry, oabuf, blbuf, *, tiles_per_seq):
    i = pl.program_id(0)

    @pl.when(i == 0)
    def _():
        carry[...] = cin_ref[...]

    cols = D_MODEL // ATTN_Q_BLOCKS

    def lru_piece(c):
        blbuf[:, c * cols:(c + 1) * cols] = _dot(ol_ref[...], wlb_ref[:, c * cols:(c + 1) * cols])

    _attention_tile(sink_ref, q_ref, kc_ref, kp_ref, vc_ref, vp_ref, oabuf, i % tiles_per_seq == 0, lru_piece)
    b_attn = _dot(oabuf[...], wab_ref[...])
    _post_tail(b_attn, blbuf[...], sga_ref, sgr_ref, x_ref, gta_ref, scf_ref, shf_ref, wout_ref, gpm_ref, gpf_ref,
               wrp_ref, wrt_ref, br_ref, tri_ref, x1_ref, h2_ref, route_ref, stat_ref, cnt_ref, carry)


def _post_tail(b_attn, b_lru, sga_ref, sgr_ref, x_ref, gta_ref, scf_ref, shf_ref, wout_ref, gpm_ref, gpf_ref,
               wrp_ref, wrt_ref, br_ref, tri_ref, x1_ref, h2_ref, route_ref, stat_ref, cnt_ref, carry):
    merged = sga_ref[...].astype(F32) * b_attn + sgr_ref[...].astype(F32) * b_lru
    mix = _dot(merged.astype(BF16), wout_ref[...])
    x1 = x_ref[...] + gta_ref[...] * _rms(mix, gpm_ref[...])
    x1_ref[...] = x1
    h2 = _rms(x1, gpf_ref[...]) * (1.0 + scf_ref[...]) + shf_ref[...]
    h2_ref[...] = h2.astype(BF16)

    h_hi = h2.astype(BF16)
    h_lo = (h2 - h_hi.astype(F32)).astype(BF16)
    hi_terms = _dot(h_hi, wrp_ref[...])
    logits = (hi_terms[:, :ROUTE_LANES] + (hi_terms[:, ROUTE_LANES:] + _dot(h_lo, wrt_ref[...]))) + br_ref[...]

    tm = logits.shape[0]
    lane = lax.broadcasted_iota(I32, (tm, ROUTE_LANES), 1)
    big = jnp.int32(ROUTE_LANES)
    is_g = lane < N_GROUPS
    lg = jnp.where(is_g, logits, NEG_INF)
    mg = jnp.max(lg, axis=-1, keepdims=True)
    g_star = jnp.min(jnp.where(lg == mg, lane, big), axis=-1, keepdims=True)
    p_star = 1.0 / jnp.sum(jnp.where(is_g, jnp.exp(lg - mg), 0.0), axis=-1, keepdims=True)
    lo = N_GROUPS + g_star * EXPERTS_PER_GROUP
    in_grp = (lane >= lo) & (lane < lo + EXPERTS_PER_GROUP)
    le = jnp.where(in_grp, logits, NEG_INF)
    m1 = jnp.max(le, axis=-1, keepdims=True)
    i1 = jnp.min(jnp.where(le == m1, lane, big), axis=-1, keepdims=True)
    le2 = jnp.where(lane == i1, NEG_INF, le)
    m2 = jnp.max(le2, axis=-1, keepdims=True)
    i2 = jnp.min(jnp.where(le2 == m2, lane, big), axis=-1, keepdims=True)
    e2x = jnp.exp(m2 - m1)
    wsum = 1.0 + e2x
    w1 = (1.0 / wsum) * p_star
    w2 = (e2x / wsum) * p_star

    oh1 = lane == i1
    oh2 = lane == i2
    cnt = jnp.where(oh1 | oh2, 1.0, 0.0)
    excl = _dot(tri_ref[...], cnt.astype(BF16))
    per_e = jnp.sum(cnt, axis=0, keepdims=True)
    pad8 = jnp.floor((per_e + (GRANULE - 1.0)) * (1.0 / GRANULE)) * GRANULE
    incl = jnp.broadcast_to(pad8, (SUBLANES, ROUTE_LANES))
    lane8 = lax.broadcasted_iota(I32, (SUBLANES, ROUTE_LANES), 1)
    d = 1
    while d < ROUTE_LANES:
        incl = incl + jnp.where(lane8 >= d, pltpu.roll(incl, d, 1), 0.0)
        d *= 2
    seg_start = incl[0:1, :] - pad8
    pos = excl + seg_start
    s1 = jnp.sum(jnp.where(oh1, pos, 0.0), axis=-1, keepdims=True)
    s2 = jnp.sum(jnp.where(oh2, pos, 0.0), axis=-1, keepdims=True)
    rec = jnp.where(lane == 0, s1, 0.0)
    rec = jnp.where(lane == 1, s2, rec)
    rec = jnp.where(lane == 2, w1, rec)
    rec = jnp.where(lane == 3, w2, rec)
    route_ref[...] = rec
    srow = lax.broadcasted_iota(I32, (SUBLANES, ROUTE_LANES), 0)
    stat_ref[...] = jnp.where(srow == 0, pad8, jnp.where(srow == 1, carry[...], 0.0))
    carry[...] = carry[...] + pad8
    cnt_ref[...] = carry[...]


def _post(oa, ol, sga, sgr, x, gta, scf, shf, wab, wlb, wout, gpm, gpf, wrh, wrl, br, tri, cin, rows_per_mod):
    n = x.shape[0]
    tm = min(ROW_TILE, n)
    if rows_per_mod:
        tiles_per_mod = rows_per_mod // tm
        mod_spec = pl.BlockSpec((None, 1, D_MODEL), lambda i: (i // tiles_per_mod, 0, 0))
    else:
        mod_spec = pl.BlockSpec((tm, D_MODEL), lambda i: (i, 0))
    row = lambda w: pl.BlockSpec((tm, w), lambda i: (i, 0))
    full = lambda a: pl.BlockSpec(a.shape, lambda i: (0,) * a.ndim)
    if isinstance(oa, tuple):
        assert tm == ATTN_Q_BLOCKS * WINDOW and rows_per_mod
        sinks, q, k, v = oa
        prev = pl.BlockSpec((WINDOW, KV_DIM), lambda i: (jnp.maximum(i * ATTN_Q_BLOCKS - 1, 0), 0))
        kernel = functools.partial(_post_attn_kernel, tiles_per_seq=rows_per_mod // tm)
        first_specs = [pl.BlockSpec(memory_space=pltpu.SMEM), row(Q_DIM), row(KV_DIM), prev, row(KV_DIM), prev]
        first_args = (sinks, q, k, k, v, v)
        scratch = [pltpu.VMEM((tm, Q_DIM), BF16), pltpu.VMEM((tm, D_MODEL), F32)]
    else:
        kernel, first_specs, first_args, scratch = _post_kernel, [row(Q_DIM)], (oa,), []
    return pl.pallas_call(
        kernel,
        grid=(n // tm,),
        in_specs=first_specs + [row(LRU_WIDTH), row(D_MODEL), row(D_MODEL), row(D_MODEL),
                  mod_spec, mod_spec, mod_spec,
                  full(wab), full(wlb), full(wout), full(gpm), full(gpf), full(wrh), full(wrl), full(br),
                  full(tri), full(cin)],
        out_specs=[row(D_MODEL), row(D_MODEL), row(ROUTE_LANES),
                   pl.BlockSpec((SUBLANES, ROUTE_LANES), lambda i: (i, 0)),
                   pl.BlockSpec((1, ROUTE_LANES), lambda i: (0, 0))],
        out_shape=[jax.ShapeDtypeStruct((n, D_MODEL), F32), jax.ShapeDtypeStruct((n, D_MODEL), BF16),
                   jax.ShapeDtypeStruct((n, ROUTE_LANES), F32),
                   jax.ShapeDtypeStruct((n // tm * SUBLANES, ROUTE_LANES), F32),
                   jax.ShapeDtypeStruct((1, ROUTE_LANES), F32)],
        scratch_shapes=[pltpu.VMEM((1, ROUTE_LANES), F32)] + scratch,
        compiler_params=_cparams(1),
        name="post_mix",
    )(*first_args, ol, sga, sgr, x, gta, scf, shf, wab, wlb, wout, gpm, gpf, wrh, wrl, br, tri, cin)


BLOCK_GRANULES = EXPERT_ROWS // GRANULE
BLOCK_SHIFT = BLOCK_GRANULES.bit_length() - 1
assert BLOCK_GRANULES == 1 << BLOCK_SHIFT


def _padded(c):
    return ((c + (BLOCK_GRANULES - 1)) >> BLOCK_SHIFT) << BLOCK_SHIFT


def _sorted_rows(tm):
    r = 2 * tm + N_EXPERTS * (GRANULE - 1)
    return -(-r // MXU_DIM) * MXU_DIM


def _plan_kernel(tot_ref, be_ref, meta_ref, *, n_blocks):
    def fill(j, _):
        be_ref[j] = N_EXPERTS - 1
        return 0
    lax.fori_loop(0, n_blocks, fill, 0)

    def per_expert(e, nb):
        k = _padded(tot_ref[e]) >> BLOCK_SHIFT

        def put(b, _):
            be_ref[nb + b] = e
            return 0
        lax.fori_loop(0, k, put, 0)
        return nb + k
    n_active = lax.fori_loop(0, N_EXPERTS, per_expert, 0)
    meta_ref[0] = n_active


def _plan(totals, n_blocks):
    return pl.pallas_call(
        functools.partial(_plan_kernel, n_blocks=n_blocks),
        in_specs=[pl.BlockSpec(memory_space=pltpu.SMEM)],
        out_specs=[pl.BlockSpec(memory_space=pltpu.SMEM), pl.BlockSpec(memory_space=pltpu.SMEM)],
        out_shape=[jax.ShapeDtypeStruct((n_blocks,), I32), jax.ShapeDtypeStruct((1,), I32)],
        name="moe_plan",
    )(totals)


def _expert_starts(tot_ref, pstart):
    def body(e, acc):
        pstart[e] = acc
        return acc + _padded(tot_ref[e])
    return lax.fori_loop(0, N_EXPERTS, body, 0)


def _granule(ref, g):
    return ref.at[pl.ds(pl.multiple_of(g * GRANULE, GRANULE), GRANULE)]


def _pack_halves(lo_f32, hi_f32):
    return (pltpu.bitcast(lo_f32, U32) >> 16) | (pltpu.bitcast(hi_f32, U32) & jnp.uint32(0xFFFF0000))


def _unpack_halves(packed):
    lo = pltpu.bitcast(packed << 16, F32).astype(BF16)
    hi = pltpu.bitcast(packed & jnp.uint32(0xFFFF0000), F32).astype(BF16)
    return lo, hi


def _sort_kernel(tot_ref, cnt_ref, goff_ref, rec_p_ref, h2_p_ref, rec_s_ref, h2_s_ref,
                 srt_ref, gsrc_ref, gslot_ref, pstart, *, tiles_p, n_sorted, n_slots):
    i = pl.program_id(0)
    from_sample = i >= tiles_p
    rec = jnp.where(from_sample, rec_s_ref[...], rec_p_ref[...])
    h2 = jnp.where(from_sample, h2_s_ref[...], h2_p_ref[...])
    tm = h2.shape[0]

    rec_t = rec.T
    s1 = rec_t[0:1, :].astype(I32)
    s2 = rec_t[1:2, :].astype(I32)
    rows = lax.broadcasted_iota(I32, (n_sorted, tm), 0)
    sel = jnp.where((rows == s1) | (rows == s2), 1.0, 0.0).astype(BF16)
    srt = _dot(sel, h2)
    srt_ref[...] = _pack_halves(srt[:, :HALF_D], srt[:, HALF_D:])

    zero_granule = n_sorted // GRANULE - 1

    @pl.when(i == 0)
    def _():
        used = _expert_starts(tot_ref, pstart)

        def put_zero(g, _):
            gsrc_ref[g] = zero_granule
            return 0

        def per_expert(e, _):
            total = tot_ref[e]
            lax.fori_loop(pstart[e] + total, pstart[e] + _padded(total), put_zero, 0)
            return 0
        lax.fori_loop(0, N_EXPERTS, per_expert, 0)
        lax.fori_loop(used, n_slots // GRANULE, put_zero, 0)

    tile_granules = n_sorted // GRANULE

    def per_expert(e, seg):
        k = cnt_ref[i * N_EXPERTS + e]
        dst = pstart[e] + goff_ref[i * N_EXPERTS + e]
        src = i * tile_granules + seg

        def put(g, _):
            gsrc_ref[dst + g] = src + g
            gslot_ref[src + g] = dst + g
            return 0
        lax.fori_loop(0, k, put, 0)
        return seg + k
    used_in_tile = lax.fori_loop(0, N_EXPERTS, per_expert, 0, unroll=4)

    def put_any(q, _):
        gslot_ref[i * tile_granules + q] = 0
        return 0
    lax.fori_loop(used_in_tile, tile_granules, put_any, 0)


def _sort(totals, cnt8, goff, rec_p, h2_p, rec_s, h2_s, n_slots):
    tm = min(ROW_TILE, h2_p.shape[0])
    assert h2_s.shape[0] % tm == 0
    tiles_p = h2_p.shape[0] // tm
    tiles_s = h2_s.shape[0] // tm
    n_sorted = _sorted_rows(tm)
    row_p = lambda w: pl.BlockSpec((tm, w), lambda i, *_: (jnp.minimum(i, tiles_p - 1), 0))
    row_s = lambda w: pl.BlockSpec((tm, w), lambda i, *_: (jnp.maximum(i - tiles_p, 0), 0))
    return pl.pallas_call(
        functools.partial(_sort_kernel, tiles_p=tiles_p, n_sorted=n_sorted, n_slots=n_slots),
        grid_spec=pltpu.PrefetchScalarGridSpec(
            num_scalar_prefetch=3,
            grid=(tiles_p + tiles_s,),
            in_specs=[row_p(ROUTE_LANES), row_p(D_MODEL), row_s(ROUTE_LANES), row_s(D_MODEL)],
            out_specs=[pl.BlockSpec((n_sorted, HALF_D), lambda i, *_: (i, 0)),
                       pl.BlockSpec(memory_space=pltpu.SMEM), pl.BlockSpec(memory_space=pltpu.SMEM)],
            scratch_shapes=[pltpu.SMEM((N_EXPERTS,), I32)]),
        out_shape=[jax.ShapeDtypeStruct(((tiles_p + tiles_s) * n_sorted, HALF_D), U32),
                   jax.ShapeDtypeStruct((n_slots // GRANULE,), I32),
                   jax.ShapeDtypeStruct(((tiles_p + tiles_s) * n_sorted // GRANULE,), I32)],
        compiler_params=_cparams(1),
        name="moe_sort",
    )(totals, cnt8, goff, rec_p, h2_p, rec_s, h2_s)


def _expert_kernel(be_ref, meta_ref, gsrc_ref, srt_hbm, wg_hbm, wu_hbm, wd_hbm, ys_ref,
                   xbuf, wgs, wus, wds, wgb, wub, wdb, stage, sems, wsems):
    j = pl.program_id(0)
    n_active = meta_ref[0]
    gran_per_block = BLOCK_GRANULES

    def granule_copy(blk, g, slot):
        return pltpu.make_async_copy(_granule(srt_hbm, gsrc_ref[blk * gran_per_block + g]),
                                     xbuf.at[slot, pl.ds(g * GRANULE, GRANULE)], sems.at[slot])

    def gather(blk, slot):
        for g in range(gran_per_block):
            granule_copy(blk, g, slot).start(priority=g % 2)

    def drain(slot):
        for g in range(gran_per_block):
            granule_copy(0, g, slot).wait()

    def weight_copies(e, p):
        return [pltpu.make_async_copy(wg_hbm.at[e], wgs.at[p], wsems.at[p]),
                pltpu.make_async_copy(wu_hbm.at[e], wus.at[p], wsems.at[p]),
                pltpu.make_async_copy(wd_hbm.at[e], wds.at[p], wsems.at[p])]

    @pl.when(j == 0)
    def _():
        gather(0, 0)
        stage[0] = 0
        for cp in weight_copies(be_ref[0], 0):
            cp.start()

    @pl.when(j < n_active)
    def _():
        slot = j % 2
        e = be_ref[j]

        @pl.when((j == 0) | (e != be_ref[jnp.maximum(j - 1, 0)]))
        def _():
            p = stage[0]
            for cp in weight_copies(e, p):
                cp.wait()
            wgb[...] = wgs[p].astype(BF16)
            wub[...] = wus[p].astype(BF16)
            wdb[...] = wds[p].astype(BF16)
            nxt = lax.while_loop(lambda k: (k < n_active) & (be_ref[jnp.minimum(k, n_active - 1)] == e),
                                 lambda k: k + 1, j + 1)

            @pl.when(nxt < n_active)
            def _():
                for cp in weight_copies(be_ref[nxt], 1 - p):
                    cp.start()
            stage[0] = 1 - p

        gather(jnp.minimum(j + 1, n_active - 1), 1 - slot)
        drain(slot)
        x_lo, x_hi = _unpack_halves(xbuf[slot])
        g = _dot(x_lo, wgb[0:HALF_D, :]) + _dot(x_hi, wgb[HALF_D:D_MODEL, :])
        u = _dot(x_lo, wub[0:HALF_D, :]) + _dot(x_hi, wub[HALF_D:D_MODEL, :])
        hmid = (g * _sigmoid(g) * u).astype(BF16)
        y = _dot(hmid, wdb[...])
        ys_ref[...] = _pack_halves(y[:, :HALF_D].astype(BF16).astype(F32), y[:, HALF_D:].astype(BF16).astype(F32))

        @pl.when(j == n_active - 1)
        def _():
            drain(1 - slot)

    @pl.when(j >= meta_ref[0])
    def _():
        ys_ref[...] = jnp.zeros(ys_ref.shape, U32)


def _experts(block_e, meta, gsrc, srt, wg, wu, wd, n_slots):
    n_blocks = n_slots // EXPERT_ROWS
    anyspec = pl.BlockSpec(memory_space=pl.ANY)
    return pl.pallas_call(
        _expert_kernel,
        grid_spec=pltpu.PrefetchScalarGridSpec(
            num_scalar_prefetch=3,
            grid=(n_blocks,),
            in_specs=[anyspec, anyspec, anyspec, anyspec],
            out_specs=pl.BlockSpec((EXPERT_ROWS, HALF_D), lambda j, be, meta, gs: (j, 0)),
            scratch_shapes=[pltpu.VMEM((2, EXPERT_ROWS, HALF_D), U32),
                            pltpu.VMEM((2, D_MODEL, D_EXPERT), F32), pltpu.VMEM((2, D_MODEL, D_EXPERT), F32),
                            pltpu.VMEM((2, D_EXPERT, D_MODEL), F32),
                            pltpu.VMEM((D_MODEL, D_EXPERT), BF16), pltpu.VMEM((D_MODEL, D_EXPERT), BF16),
                            pltpu.VMEM((D_EXPERT, D_MODEL), BF16),
                            pltpu.SMEM((1,), I32), pltpu.SemaphoreType.DMA((2,)), pltpu.SemaphoreType.DMA((2,))]),
        out_shape=jax.ShapeDtypeStruct((n_slots, HALF_D), U32),
        compiler_params=_cparams(1),
        name="moe_experts",
    )(block_e, meta, gsrc, srt, wg, wu, wd)


def _combine_kernel(gslot_ref, rec_ref, x1_ref, gtf_ref, g_ref, ys_hbm, y_ref, cbuf, sems, *, tile_base, n_tiles):
    i = pl.program_id(0)
    slot = i % 2
    tm = x1_ref.shape[0]
    n_sorted = cbuf.shape[1]
    tile_granules = n_sorted // GRANULE

    def granule_copy(slot_granule, q, slot_):
        return pltpu.make_async_copy(_granule(ys_hbm, slot_granule),
                                     cbuf.at[slot_, pl.ds(q * GRANULE, GRANULE)], sems.at[slot_])

    def gather(tile, slot_):
        for q in range(tile_granules):
            granule_copy(gslot_ref[tile * tile_granules + q], q, slot_).start(priority=q % 2)

    def drain(slot_):
        for q in range(tile_granules):
            granule_copy(0, q, slot_).wait()

    @pl.when(i == 0)
    def _():
        gather(tile_base, 0)

    gather(tile_base + jnp.minimum(i + 1, n_tiles - 1), 1 - slot)
    drain(slot)

    rec = rec_ref[...]
    s1 = rec[:, 0:1].astype(I32)
    s2 = rec[:, 1:2].astype(I32)
    col = lax.broadcasted_iota(I32, (tm, n_sorted), 1)
    wmat = (jnp.where(col == s1, rec[:, 2:3], 0.0) + jnp.where(col == s2, rec[:, 3:4], 0.0)).astype(BF16)
    y_lo, y_hi = _unpack_halves(cbuf[slot])
    f = jnp.concatenate([_dot(wmat, y_lo), _dot(wmat, y_hi)], axis=1)
    y_ref[...] = x1_ref[...] + gtf_ref[...] * _rms(f, g_ref[...])

    @pl.when(i == n_tiles - 1)
    def _():
        drain(1 - slot)


def _combine(gslot, rec, x1, gtf, g, ys, rows_per_mod, tile_base):
    n = x1.shape[0]
    tm = min(ROW_TILE, n)
    if rows_per_mod:
        tiles_per_mod = rows_per_mod // tm
        mod_spec = pl.BlockSpec((None, 1, D_MODEL), lambda i, *_: (i // tiles_per_mod, 0, 0))
    else:
        mod_spec = pl.BlockSpec((tm, D_MODEL), lambda i, *_: (i, 0))
    row = lambda w: pl.BlockSpec((tm, w), lambda i, *_: (i, 0))
    return pl.pallas_call(
        functools.partial(_combine_kernel, tile_base=tile_base, n_tiles=n // tm),
        grid_spec=pltpu.PrefetchScalarGridSpec(
            num_scalar_prefetch=1,
            grid=(n // tm,),
            in_specs=[row(ROUTE_LANES), row(D_MODEL), mod_spec,
                      pl.BlockSpec((1, D_MODEL), lambda i, *_: (0, 0)),
                      pl.BlockSpec(memory_space=pl.ANY)],
            out_specs=row(D_MODEL),
            scratch_shapes=[pltpu.VMEM((2, _sorted_rows(tm), HALF_D), U32), pltpu.SemaphoreType.DMA((2,))]),
        out_shape=jax.ShapeDtypeStruct((n, D_MODEL), F32),
        compiler_params=_cparams(1),
        name="moe_combine",
    )(gslot, rec, x1, gtf, g, ys)


def _rope_tables(pos):
    half = HEAD_DIM // 2
    inv = jnp.power(jnp.float32(ROPE_THETA), -jnp.arange(half, dtype=F32) / half)
    ang = pos.astype(F32)[:, None] * inv[None, :]
    cos = jnp.cos(ang)
    sin = jnp.sin(ang)
    reps = LANES // HEAD_DIM
    cos_t = jnp.tile(jnp.concatenate([cos, cos], axis=-1), (1, reps))
    sin_t = jnp.tile(jnp.concatenate([-sin, sin], axis=-1), (1, reps))
    return cos_t, sin_t


def _block_diag_gates(w_a, w_x):
    per = MXU_DIM // LRU_BLOCK_W
    groups = LRU_BLOCKS // per

    def bd(w):
        w = w.reshape(groups, per, LRU_BLOCK_W, LRU_BLOCK_W)
        eye = jnp.eye(per, dtype=w.dtype)
        full = jnp.einsum('gpij,pq->gpiqj', w, eye)
        return full.reshape(groups, MXU_DIM, MXU_DIM)
    return jnp.concatenate([bd(w_a), bd(w_x)], axis=-1).astype(BF16)


def _layer_forward(xp, xs_tm, ck, cv, cconv, ch, mod_p, mod_s, p):
    batch, seq, _ = xp.shape
    nseq, _, _, _ = ck.shape
    t_new = xs_tm.shape[0] // nseq
    n_p = batch * seq
    n_s = xs_tm.shape[0]

    w_in = p['w_in']
    wq = w_in[:, :Q_DIM].astype(BF16).reshape(D_MODEL, N_KV_HEADS, GQA_GROUP, HEAD_DIM)
    w_in_b = jnp.concatenate([wq.transpose(0, 2, 1, 3).reshape(D_MODEL, Q_DIM), w_in[:, Q_DIM:].astype(BF16)], axis=1)
    wab = (p['w_attn_branch'].astype(BF16).reshape(N_KV_HEADS, GQA_GROUP, HEAD_DIM, D_MODEL)
           .transpose(1, 0, 2, 3).reshape(Q_DIM, D_MODEL))
    wlb = p['w_lru_branch'].astype(BF16)
    wout = p['w_out'].astype(BF16)
    wbd = _block_diag_gates(p['w_lru_a'], p['w_lru_x'])
    row = lambda v: v.reshape(1, -1)
    wr = jnp.concatenate([p['w_router_group'], p['w_router_expert'],
                          jnp.zeros((D_MODEL, ROUTE_LANES - N_GROUPS - N_EXPERTS), F32)], axis=1)
    wr_top = wr.astype(BF16)
    wr_pair = jnp.concatenate([wr_top, (wr - wr_top.astype(F32)).astype(BF16)], axis=1)
    br = jnp.concatenate([p['b_router_group'], p['b_router_expert'],
                          jnp.zeros((ROUTE_LANES - N_GROUPS - N_EXPERTS,), F32)]).reshape(1, -1)
    wg = p['w_exp_gate']
    wu = p['w_exp_up']
    wd = p['w_exp_down']

    def mods(mod):
        return [mod[:, k * D_MODEL:(k + 1) * D_MODEL] for k in range(6)]
    sh_a_p, sc_a_p, gt_a_p, sh_f_p, sc_f_p, gt_f_p = [m.reshape(batch, 1, D_MODEL) for m in mods(mod_p)]
    sh_a_s, sc_a_s, gt_a_s, sh_f_s, sc_f_s, gt_f_s = [jnp.tile(m, (t_new, 1)) for m in mods(mod_s)]

    lru_w = (p['w_conv'], row(p['b_conv']), wbd, row(p['b_lru_a']), row(p['b_lru_x']), row(p['lru_lambda']))
    cos_p, sin_p = _rope_tables(jnp.arange(seq, dtype=I32))
    q_p, k_p, v_p, sga_p, sgr_p, klast_p, vlast_p, ol_p, xtail_p, hlast_p = _inproj_lru(
        xp.reshape(n_p, D_MODEL), sc_a_p, sh_a_p, row(p['g_pre_mix']), cos_p, sin_p, w_in_b, *lru_w, batch, seq)
    pos_s = jnp.repeat(PAST_LEN_ + jnp.arange(t_new, dtype=I32), nseq)
    cos_s, sin_s = _rope_tables(pos_s)
    q_s, k_s, v_s, xl_s, yl_s, sga_s, sgr_s = _inproj(
        xs_tm, sc_a_s, sh_a_s, row(p['g_pre_mix']), cos_s, sin_s, w_in_b, rows_per_mod=0,
        pos_tiles=n_s // min(ROW_TILE, n_s))

    sinks_perm = p['sinks']
    oa_p = (sinks_perm, q_p, k_p, v_p)
    rows = t_new * GQA_GROUP
    q_s3 = q_s.reshape(t_new, nseq, GQA_GROUP, KV_DIM).transpose(1, 0, 2, 3).reshape(nseq, rows, KV_DIM)
    kn = k_s.reshape(t_new, nseq, KV_DIM).transpose(1, 0, 2)
    vn = v_s.reshape(t_new, nseq, KV_DIM).transpose(1, 0, 2)
    kc = ck.transpose(0, 2, 3, 1).reshape(nseq, KV_DIM, WINDOW)
    vc = cv.transpose(0, 2, 3, 1).reshape(nseq, KV_DIM, WINDOW)
    sink_rows = jnp.tile(p['sinks'].reshape(N_KV_HEADS, 1, GQA_GROUP), (1, t_new, 1)).reshape(N_KV_HEADS, rows, 1)
    oa_s3 = _attn_sample(sink_rows, q_s3, kn, vn, kc, vc)
    oa_s = oa_s3.reshape(nseq, t_new, Q_DIM).transpose(1, 0, 2).reshape(n_s, Q_DIM)

    ol_s3, hlast_s = _lru_sample(xl_s.reshape(t_new, nseq, LRU_WIDTH), yl_s.reshape(t_new, nseq, LRU_WIDTH),
                                 cconv.transpose(1, 0, 2), ch, *lru_w)
    ol_s = ol_s3.reshape(n_s, LRU_WIDTH)

    tm_post = min(ROW_TILE, n_p)
    tri = jnp.tril(jnp.ones((tm_post, tm_post), F32), -1).astype(BF16)
    post_w = (wab, wlb, wout, row(p['g_post_mix']), row(p['g_pre_ffn']), wr_pair, wr_top, br)
    zero_cnt = jnp.zeros((1, ROUTE_LANES), F32)
    x1_p, h2_p, route_p, stat_p, cnt_p = _post(oa_p, ol_p, sga_p, sgr_p, xp.reshape(n_p, D_MODEL),
                                               gt_a_p, sc_f_p, sh_f_p, *post_w, tri, zero_cnt, rows_per_mod=seq)
    tm_s = min(ROW_TILE, n_s)
    tri_s = tri if tm_s == tm_post else jnp.tril(jnp.ones((tm_s, tm_s), F32), -1).astype(BF16)
    x1_s, h2_s, route_s, stat_s, cnt_all = _post(oa_s, ol_s, sga_s, sgr_s, xs_tm,
                                                 gt_a_s, sc_f_s, sh_f_s, *post_w, tri_s, cnt_p, rows_per_mod=0)

    e_lanes = slice(N_GROUPS, N_GROUPS + N_EXPERTS)
    to_granules = lambda v: (v.astype(I32) // GRANULE).reshape(-1)
    totals = to_granules(cnt_all[0, e_lanes])
    stats = jnp.concatenate([stat_p, stat_s], axis=0).reshape(-1, SUBLANES, ROUTE_LANES)
    cnt8 = to_granules(stats[:, 0, e_lanes])
    goff = to_granules(stats[:, 1, e_lanes])
    tiles_p = n_p // tm_post
    n_tiles = tiles_p + n_s // tm_s
    max_rows = 2 * (n_p + n_s) + n_tiles * N_EXPERTS * (GRANULE - 1) + N_EXPERTS * (EXPERT_ROWS - GRANULE)
    n_blocks = -(-max_rows // EXPERT_ROWS)
    n_slots = n_blocks * EXPERT_ROWS
    srt, gsrc, gslot = _sort(totals, cnt8, goff, route_p, h2_p, route_s, h2_s, n_slots)
    block_e, meta = _plan(totals, n_blocks)
    ys = _experts(block_e, meta, gsrc, srt, wg, wu, wd, n_slots)
    y_p = _combine(gslot, route_p, x1_p, gt_f_p, row(p['g_post_ffn']), ys, seq, 0)
    y_s = _combine(gslot, route_s, x1_s, gt_f_s, row(p['g_post_ffn']), ys, 0, tiles_p)

    k_new_p = klast_p.reshape(batch, WINDOW, N_KV_HEADS, HEAD_DIM)
    v_new_p = vlast_p.reshape(batch, WINDOW, N_KV_HEADS, HEAD_DIM)
    conv_p = xtail_p[:, -(CONV_WIDTH - 1):]
    h_p = hlast_p.reshape(batch, LRU_WIDTH)
    k_new_s = jnp.concatenate([ck, kn.reshape(nseq, t_new, N_KV_HEADS, HEAD_DIM)], axis=1)[:, -WINDOW:]
    v_new_s = jnp.concatenate([cv, vn.reshape(nseq, t_new, N_KV_HEADS, HEAD_DIM)], axis=1)[:, -WINDOW:]
    xl_s3 = xl_s.reshape(t_new, nseq, LRU_WIDTH).transpose(1, 0, 2).astype(F32)
    conv_s = jnp.concatenate([cconv, xl_s3], axis=1)[:, -(CONV_WIDTH - 1):]
    return (y_p.reshape(batch, seq, D_MODEL), y_s, k_new_p, v_new_p, conv_p, h_p,
            k_new_s, v_new_s, conv_s, hlast_s)


PAST_LEN_ = 16384

PARAM_NAMES = ('w_ada', 'b_ada', 'g_pre_mix', 'g_post_mix', 'g_pre_ffn', 'g_post_ffn', 'w_in', 'sinks',
               'w_conv', 'b_conv', 'w_lru_a', 'b_lru_a', 'w_lru_x', 'b_lru_x', 'lru_lambda',
               'w_attn_branch', 'w_lru_branch', 'w_out', 'w_router_group', 'b_router_group',
               'w_router_expert', 'b_router_expert', 'w_exp_gate', 'w_exp_up', 'w_exp_down')


def kernel(x_prompt, x_sample, cache_k_win, cache_v_win, state_conv, state_h, c_prompt, c_sample, w_ada, b_ada, g_pre_mix, g_post_mix, g_pre_ffn, g_post_ffn, w_in, sinks, w_conv, b_conv, w_lru_a, b_lru_a, w_lru_x, b_lru_x, lru_lambda, w_attn_branch, w_lru_branch, w_out, w_router_group, b_router_group, w_router_expert, b_router_expert, w_exp_gate, w_exp_up, w_exp_down):
    weights = (w_ada, b_ada, g_pre_mix, g_post_mix, g_pre_ffn, g_post_ffn, w_in, sinks,
               w_conv, b_conv, w_lru_a, b_lru_a, w_lru_x, b_lru_x, lru_lambda,
               w_attn_branch, w_lru_branch, w_out, w_router_group, b_router_group,
               w_router_expert, b_router_expert, w_exp_gate, w_exp_up, w_exp_down)
    depth = w_ada.shape[0]
    batch = x_prompt.shape[0]
    nseq, t_new, _ = x_sample.shape
    y_p = x_prompt
    y_s = x_sample.transpose(1, 0, 2).reshape(t_new * nseq, D_MODEL)
    c_all = jnp.concatenate([c_prompt, c_sample], axis=0)
    outs = [[] for _ in range(8)]
    for layer in range(depth):
        p = {name: w[layer] for name, w in zip(PARAM_NAMES, weights)}
        mod = _ada(c_all, p['w_ada'], p['b_ada'].reshape(1, -1))
        res = _layer_forward(y_p, y_s, cache_k_win[layer], cache_v_win[layer], state_conv[layer],
                             state_h[layer], mod[:batch], mod[batch:], p)
        y_p, y_s = res[0], res[1]
        for o, r in zip(outs, res[2:]):
            o.append(r)
    y_sample = y_s.reshape(t_new, nseq, D_MODEL).transpose(1, 0, 2)
    return (y_p, y_sample) + tuple(jnp.stack(o) for o in outs)
```

```python
import functools

import jax
import jax.numpy as jnp
from jax import lax
from jax.experimental import pallas as pl
from jax.experimental.pallas import tpu as pltpu

F32 = jnp.float32
BF16 = jnp.bfloat16
I32 = jnp.int32

D_MODEL = 1024
N_HEADS = 16
HEAD_DIM = 64
N_KV_HEADS = 4
GQA_GROUP = 4
WINDOW = 128
ROPE_THETA = 10000.0
NEG_INF = -1e30
LRU_WIDTH = 1024
LRU_BLOCKS = 16
LRU_BLOCK_W = 64
CONV_WIDTH = 4
LRU_C = 8.0
N_GROUPS = 4
EXPERTS_PER_GROUP = 8
N_EXPERTS = 32
D_EXPERT = 512
MOE_BLOCK = 128
NORM_EPS = 1e-6
Q_DIM = N_HEADS * HEAD_DIM
KV_DIM = N_KV_HEADS * HEAD_DIM
IN_DIM = Q_DIM + 2 * KV_DIM + 2 * LRU_WIDTH + 2 * D_MODEL

LANES = 128
SUBLANES = 8
MXU_DIM = 256
VMEM_LIMIT = 56 * 1024 * 1024
VMEM_LIMIT_BIG = 60 * 1024 * 1024

ROW_TILE = 512
ROUTE_LANES = LANES
GRANULE = SUBLANES
EXPERT_ROWS = 512
HALF_D = D_MODEL // 2
U32 = jnp.uint32


def _cparams(n_axes, vmem=VMEM_LIMIT):
    return pltpu.CompilerParams(dimension_semantics=("arbitrary",) * n_axes, vmem_limit_bytes=vmem)


def _rms(x, g):
    ms = jnp.mean(x * x, axis=-1, keepdims=True)
    return x * lax.rsqrt(ms + NORM_EPS) * g


def _sigmoid(x):
    return 1.0 / (1.0 + jnp.exp(-x))


def _dot(a, b):
    return jnp.dot(a, b, preferred_element_type=F32)


def _ada_kernel(c_ref, w_ref, b_ref, o_ref):
    c = c_ref[...]
    s = (c * _sigmoid(c)).astype(BF16)
    o_ref[...] = _dot(s, w_ref[...].astype(BF16)) + b_ref[...]


def _ada(c_all, w_ada, b_ada):
    r = c_all.shape[0]
    n = w_ada.shape[1]
    return pl.pallas_call(
        _ada_kernel,
        grid=(n // D_MODEL,),
        in_specs=[pl.BlockSpec((r, D_MODEL), lambda j: (0, 0)),
                  pl.BlockSpec((D_MODEL, D_MODEL), lambda j: (0, j)),
                  pl.BlockSpec((1, D_MODEL), lambda j: (0, j))],
        out_specs=pl.BlockSpec((r, D_MODEL), lambda j: (0, j)),
        out_shape=jax.ShapeDtypeStruct((r, n), F32),
        compiler_params=_cparams(1),
        name="ada_mod",
    )(c_all, w_ada, b_ada)


_O1 = Q_DIM
_O2 = _O1 + KV_DIM
_O3 = _O2 + KV_DIM
_O4 = _O3 + LRU_WIDTH
_O5 = _O4 + LRU_WIDTH
_O6 = _O5 + D_MODEL


def _prenorm(x_ref, g_ref, sc_ref, sh_ref):
    h = _rms(x_ref[...], g_ref[...]) * (1.0 + sc_ref[...]) + sh_ref[...]
    return h.astype(BF16)


PIECE_COLS = 256


def _qkv_gate_pieces(h_ref, w_ref, cos_ref, sin_ref, q_ref, k_ref, v_ref, sga_ref, sgr_ref, last_refs=None):
    def rope(t):
        cos = cos_ref[...]
        sin = sin_ref[...]
        lane = lax.broadcasted_iota(I32, cos.shape, 1)
        first_half = (lane % HEAD_DIM) < (HEAD_DIM // 2)
        rot = jnp.where(first_half, pltpu.roll(t, LANES - HEAD_DIM // 2, 1), pltpu.roll(t, HEAD_DIM // 2, 1))
        return t * cos + rot * sin

    def q_piece(c0):
        def run():
            qf = _dot(h_ref[...], w_ref[:, c0:c0 + PIECE_COLS])
            for c in range(PIECE_COLS // LANES):
                q_ref[:, c0 + c * LANES:c0 + (c + 1) * LANES] = rope(qf[:, c * LANES:(c + 1) * LANES]).astype(BF16)
        return run

    def kv_piece():
        kv = _dot(h_ref[...], w_ref[:, _O1:_O3])
        for c in range(KV_DIM // LANES):
            k_ref[:, c * LANES:(c + 1) * LANES] = rope(kv[:, c * LANES:(c + 1) * LANES])
        v_ref[...] = kv[:, KV_DIM:]
        if last_refs is not None:
            t = k_ref.shape[0]
            last_refs[0][...] = k_ref[t - WINDOW:t, :]
            last_refs[1][...] = v_ref[t - WINDOW:t, :]

    def gate_piece(o_ref, base, c0):
        def run():
            z = _dot(h_ref[...], w_ref[:, base + c0:base + c0 + PIECE_COLS])
            o_ref[:, c0:c0 + PIECE_COLS] = _sigmoid(z).astype(BF16)
        return run

    pieces = [q_piece(c0) for c0 in range(0, Q_DIM, PIECE_COLS)] + [kv_piece]
    pieces += [gate_piece(sga_ref, _O5, c0) for c0 in range(0, D_MODEL, PIECE_COLS)]
    pieces += [gate_piece(sgr_ref, _O6, c0) for c0 in range(0, D_MODEL, PIECE_COLS)]
    return pieces


def _inproj_kernel(x_ref, sc_ref, sh_ref, g_ref, cos_ref, sin_ref, w_ref,
                   q_ref, k_ref, v_ref, xl_ref, yl_ref, sga_ref, sgr_ref, hbuf):
    hbuf[...] = _prenorm(x_ref, g_ref, sc_ref, sh_ref)
    xl_ref[...] = _dot(hbuf[...], w_ref[:, _O3:_O4]).astype(BF16)
    yl_ref[...] = _dot(hbuf[...], w_ref[:, _O4:_O5]).astype(BF16)
    for piece in _qkv_gate_pieces(hbuf, w_ref, cos_ref, sin_ref, q_ref, k_ref, v_ref, sga_ref, sgr_ref):
        piece()


def _inproj(x, sc, sh, g, cos, sin, w_in, rows_per_mod, pos_tiles):
    n = x.shape[0]
    tm = min(ROW_TILE, n)
    if rows_per_mod:
        tiles_per_mod = rows_per_mod // tm
        mod_spec = pl.BlockSpec((None, 1, D_MODEL), lambda i: (i // tiles_per_mod, 0, 0))
    else:
        mod_spec = pl.BlockSpec((tm, D_MODEL), lambda i: (i, 0))
    row = lambda w: pl.BlockSpec((tm, w), lambda i: (i, 0))
    outs = [(Q_DIM, BF16), (KV_DIM, F32), (KV_DIM, F32), (LRU_WIDTH, BF16), (LRU_WIDTH, BF16),
            (D_MODEL, BF16), (D_MODEL, BF16)]
    return pl.pallas_call(
        _inproj_kernel,
        grid=(n // tm,),
        in_specs=[row(D_MODEL), mod_spec, mod_spec,
                  pl.BlockSpec((1, D_MODEL), lambda i: (0, 0)),
                  pl.BlockSpec((tm, LANES), lambda i: (i % pos_tiles, 0)),
                  pl.BlockSpec((tm, LANES), lambda i: (i % pos_tiles, 0)),
                  pl.BlockSpec((D_MODEL, IN_DIM), lambda i: (0, 0))],
        out_specs=[row(w) for w, _ in outs],
        out_shape=[jax.ShapeDtypeStruct((n, w), dt) for w, dt in outs],
        scratch_shapes=[pltpu.VMEM((tm, D_MODEL), BF16)],
        compiler_params=_cparams(1),
        name="in_proj",
    )(x, sc, sh, g, cos, sin, w_in)


def _head_masks(shape):
    lane = lax.broadcasted_iota(I32, shape, 1)
    return [(lane // HEAD_DIM) == h for h in range(N_KV_HEADS)]


def _attention_core(q_perm, kall, vall, valid, sink_of, rows):
    masks_b = _head_masks((rows, KV_DIM))
    zero_b = jnp.zeros((rows, KV_DIM), BF16)
    pieces = []
    for h in range(N_KV_HEADS):
        for g in range(GQA_GROUP):
            pieces.append(jnp.where(masks_b[h], q_perm[g], zero_b))
    q_big = jnp.concatenate(pieces, axis=0)
    s_all = lax.dot_general(q_big, kall, (((1,), (1,)), ((), ())), preferred_element_type=F32)
    s_all = s_all * (HEAD_DIM ** -0.5)
    p_chunks, inv_chunks = [], []
    for h in range(N_KV_HEADS):
        for g in range(GQA_GROUP):
            c = h * GQA_GROUP + g
            s = jnp.where(valid, s_all[c * rows:(c + 1) * rows], NEG_INF)
            sink = sink_of(h, g)
            m = jnp.maximum(jnp.max(s, axis=-1, keepdims=True), sink)
            p = jnp.exp(s - m)
            denom = jnp.sum(p, axis=-1, keepdims=True) + jnp.exp(sink - m)
            p_chunks.append(p.astype(BF16))
            inv_chunks.append(1.0 / denom)
    o_all = _dot(jnp.concatenate(p_chunks, axis=0), vall)
    outs = []
    for g in range(GQA_GROUP):
        acc = jnp.zeros((rows, KV_DIM), F32)
        for h in range(N_KV_HEADS):
            c = h * GQA_GROUP + g
            acc = acc + jnp.where(masks_b[h], o_all[c * rows:(c + 1) * rows] * inv_chunks[c], 0.0)
        outs.append(acc)
    return outs


ATTN_Q_BLOCKS = 4


def _attention_tile(sink_ref, q_ref, kc_ref, kp_ref, vc_ref, vp_ref, o_ref, seq_start, between=None):
    kall = jnp.concatenate([kp_ref[...], kc_ref[...]], axis=0).astype(BF16)
    vall = jnp.concatenate([vp_ref[...], vc_ref[...]], axis=0).astype(BF16)
    qi = lax.broadcasted_iota(I32, (WINDOW, 2 * WINDOW), 0)
    kj = lax.broadcasted_iota(I32, (WINDOW, 2 * WINDOW), 1)
    dist = qi + WINDOW - kj
    in_window = (dist >= 0) & (dist <= WINDOW)
    for c in range(ATTN_Q_BLOCKS):
        rows = slice(c * WINDOW, (c + 1) * WINDOW)
        keys = slice(c * WINDOW, (c + 2) * WINDOW)
        valid = in_window & ((kj >= WINDOW) | jnp.logical_not(seq_start)) if c == 0 else in_window
        q_perm = [q_ref[rows, g * KV_DIM:(g + 1) * KV_DIM] for g in range(GQA_GROUP)]
        outs = _attention_core(q_perm, kall[keys], vall[keys], valid,
                               lambda h, g: sink_ref[h * GQA_GROUP + g], WINDOW)
        for g in range(GQA_GROUP):
            o_ref[rows, g * KV_DIM:(g + 1) * KV_DIM] = outs[g].astype(BF16)
        if between is not None:
            between(c)


SEQ_PER_STEP = 8


def _attn_sample_kernel(sink_ref, q_ref, kn_ref, vn_ref, kc_ref, vc_ref, o_ref, kbuf, vbuf, *, t_new):
    rows = GQA_GROUP * t_new
    kbuf[...] = jnp.zeros(kbuf.shape, F32)
    vbuf[...] = jnp.zeros(vbuf.shape, F32)
    ri = lax.broadcasted_iota(I32, (rows, 2 * WINDOW), 0)
    kj = lax.broadcasted_iota(I32, (rows, 2 * WINDOW), 1)
    tq = ri // GQA_GROUP
    valid = (kj >= tq) & (kj <= tq + WINDOW) & (kj < WINDOW + t_new)
    nt_dims = (((1,), (1,)), ((), ()))
    for s in range(SEQ_PER_STEP):
        kbuf[0:t_new, :] = kn_ref[s]
        vbuf[0:t_new, :] = vn_ref[s]
        k_cache_t = kc_ref[s].astype(BF16)
        v_cache_t = vc_ref[s].astype(BF16)
        k_new = kbuf[...].astype(BF16)
        v_new = vbuf[...].astype(BF16)
        qs = q_ref[s]
        masks_b = _head_masks((rows, KV_DIM))
        zero_b = jnp.zeros((rows, KV_DIM), BF16)
        q_big = jnp.concatenate([jnp.where(masks_b[h], qs, zero_b) for h in range(N_KV_HEADS)], axis=0)
        s_all = jnp.concatenate([_dot(q_big, k_cache_t),
                                 lax.dot_general(q_big, k_new, nt_dims, preferred_element_type=F32)], axis=1)
        s_all = s_all * (HEAD_DIM ** -0.5)
        acc = jnp.zeros((rows, KV_DIM), F32)
        p_chunks, inv_chunks = [], []
        for h in range(N_KV_HEADS):
            sc = jnp.where(valid, s_all[h * rows:(h + 1) * rows], NEG_INF)
            sink = sink_ref[h]
            m = jnp.maximum(jnp.max(sc, axis=-1, keepdims=True), sink)
            p = jnp.exp(sc - m)
            denom = jnp.sum(p, axis=-1, keepdims=True) + jnp.exp(sink - m)
            p_chunks.append(p.astype(BF16))
            inv_chunks.append(1.0 / denom)
        p_all = jnp.concatenate(p_chunks, axis=0)
        o_all = (lax.dot_general(p_all[:, :WINDOW], v_cache_t, nt_dims, preferred_element_type=F32)
                 + _dot(p_all[:, WINDOW:], v_new))
        for h in range(N_KV_HEADS):
            acc = acc + jnp.where(masks_b[h], o_all[h * rows:(h + 1) * rows] * inv_chunks[h], 0.0)
        o_ref[s] = acc.astype(BF16)


def _attn_sample(sink_rows, q, kn, vn, kc, vc):
    nseq, rows, _ = q.shape
    t_new = kn.shape[1]
    sb = SEQ_PER_STEP
    blk = lambda r: pl.BlockSpec((sb, r, KV_DIM), lambda i: (i, 0, 0))
    cache = pl.BlockSpec((sb, KV_DIM, WINDOW), lambda i: (i, 0, 0))
    return pl.pallas_call(
        functools.partial(_attn_sample_kernel, t_new=t_new),
        grid=(nseq // sb,),
        in_specs=[pl.BlockSpec((N_KV_HEADS, rows, 1), lambda i: (0, 0, 0)),
                  blk(rows), blk(t_new), blk(t_new), cache, cache],
        out_specs=blk(rows),
        out_shape=jax.ShapeDtypeStruct((nseq, rows, KV_DIM), BF16),
        scratch_shapes=[pltpu.VMEM((WINDOW, KV_DIM), F32), pltpu.VMEM((WINDOW, KV_DIM), F32)],
        compiler_params=_cparams(1),
        name="attn_sample",
    )(sink_rows, q, kn, vn, kc, vc)


def _gelu_tanh(x):
    return 0.5 * x * (1.0 + jnp.tanh(0.7978845608028654 * (x + 0.044715 * x * x * x)))


def _lru_gates(xc, wbd_ref, ba, bx, lam):
    xcb = xc.astype(BF16)
    r_parts, i_parts = [], []
    for gidx in range(LRU_WIDTH // MXU_DIM):
        z = _dot(xcb[:, gidx * MXU_DIM:(gidx + 1) * MXU_DIM], wbd_ref[gidx])
        r_parts.append(z[:, :MXU_DIM])
        i_parts.append(z[:, MXU_DIM:])
    return _gate_math(jnp.concatenate(r_parts, axis=1), jnp.concatenate(i_parts, axis=1), xc, ba, bx, lam)


def _gate_math(zr, zi, xc, ba, bx, lam):
    r = _sigmoid(zr + ba)
    i = _sigmoid(zi + bx)
    softplus_neg_lam = jnp.maximum(-lam, 0.0) + jnp.log1p(jnp.exp(-jnp.abs(lam)))
    log_a = -LRU_C * r * softplus_neg_lam
    a = jnp.exp(log_a)
    y = 1.0 - a * a
    u = jnp.where(y > 0.0, y * lax.rsqrt(y), 0.0) * (i * xc)
    return a, u


LRU_CHUNK = 64


def _lru_chunk(c, xbuf, ybuf, hcar, ol_ref, wc_ref, bc_ref, wbd_ref, ba_ref, bx_ref, lam_ref):
    w = LRU_WIDTH
    n = LRU_CHUNK
    rows = slice(c * n, (c + 1) * n)
    r0 = SUBLANES + c * n
    xc = xbuf[r0:r0 + n, :] * wc_ref[CONV_WIDTH - 1:CONV_WIDTH, :] + bc_ref[...]
    for k in range(1, CONV_WIDTH):
        xc = xc + xbuf[r0 - k:r0 - k + n, :] * wc_ref[CONV_WIDTH - 1 - k:CONV_WIDTH - k, :]
    a, u = _lru_gates(xc, wbd_ref, ba_ref[...], bx_ref[...], lam_ref[...])

    ng = n // SUBLANES
    a3 = a.reshape(ng, SUBLANES, w)
    u3 = u.reshape(ng, SUBLANES, w)
    row = lax.broadcasted_iota(I32, (ng, SUBLANES, w), 1)
    d = 1
    while d < SUBLANES:
        a_s = jnp.where(row >= d, pltpu.roll(a3, d, 1), 1.0)
        u_s = jnp.where(row >= d, pltpu.roll(u3, d, 1), 0.0)
        u3 = a3 * u_s + u3
        a3 = a3 * a_s
        d *= 2
    carry = hcar[...]
    hs = []
    for gi in range(ng):
        hg = a3[gi] * carry + u3[gi]
        hs.append(hg)
        carry = hg[SUBLANES - 1:SUBLANES, :]
    hcar[...] = carry
    h = jnp.concatenate(hs, axis=0)
    ol_ref[rows, :] = (h * _gelu_tanh(ybuf[rows, :])).astype(BF16)


def _inproj_lru_kernel(x_ref, sc_ref, sh_ref, g_ref, cos_ref, sin_ref, w_ref,
                       wc_ref, bc_ref, wbd_ref, ba_ref, bx_ref, lam_ref,
                       q_ref, k_ref, v_ref, sga_ref, sgr_ref, klast_ref, vlast_ref, ol_ref, xtail_ref, hlast_ref,
                       hbuf, xbuf, ybuf, hcar, *, tiles_per_seq):
    t = x_ref.shape[0]

    @pl.when(pl.program_id(0) % tiles_per_seq == 0)
    def _():
        xbuf[0:SUBLANES, :] = jnp.zeros((SUBLANES, LRU_WIDTH), F32)
        hcar[...] = jnp.zeros((1, LRU_WIDTH), F32)

    hbuf[...] = _prenorm(x_ref, g_ref, sc_ref, sh_ref)

    xbuf[SUBLANES:SUBLANES + t, :] = _dot(hbuf[...], w_ref[:, _O3:_O4])
    ybuf[...] = _dot(hbuf[...], w_ref[:, _O4:_O5])

    pieces = _qkv_gate_pieces(hbuf, w_ref, cos_ref, sin_ref, q_ref, k_ref, v_ref, sga_ref, sgr_ref,
                              last_refs=(klast_ref, vlast_ref))
    n_chunks = t // LRU_CHUNK
    per_chunk = -(-len(pieces) // n_chunks)
    for c in range(n_chunks):
        _lru_chunk(c, xbuf, ybuf, hcar, ol_ref, wc_ref, bc_ref, wbd_ref, ba_ref, bx_ref, lam_ref)
        for piece in pieces[c * per_chunk:(c + 1) * per_chunk]:
            piece()

    tail = xbuf[t:t + SUBLANES, :]
    xtail_ref[...] = tail
    xbuf[0:SUBLANES, :] = tail
    hlast_ref[...] = hcar[...]


def _inproj_lru(x, sc, sh, g, cos, sin, w_in, wc, bc, wbd, ba, bx, lam, batch, seq):
    n = x.shape[0]
    tm = min(ROW_TILE, seq)
    tps = seq // tm
    mod_spec = pl.BlockSpec((None, 1, D_MODEL), lambda i: (i // tps, 0, 0))
    row = lambda w: pl.BlockSpec((tm, w), lambda i: (i, 0))
    full = lambda a: pl.BlockSpec(a.shape, lambda i: (0,) * a.ndim)
    per_seq = lambda r, w: pl.BlockSpec((None, r, w), lambda i: (i // tps, 0, 0))
    outs = [(Q_DIM, BF16), (KV_DIM, F32), (KV_DIM, F32), (D_MODEL, BF16), (D_MODEL, BF16)]
    return pl.pallas_call(
        functools.partial(_inproj_lru_kernel, tiles_per_seq=tps),
        grid=(n // tm,),
        in_specs=[row(D_MODEL), mod_spec, mod_spec, full(g),
                  pl.BlockSpec((tm, LANES), lambda i: (i % tps, 0)),
                  pl.BlockSpec((tm, LANES), lambda i: (i % tps, 0)),
                  full(w_in), full(wc), full(bc), full(wbd), full(ba), full(bx), full(lam)],
        out_specs=[row(w) for w, _ in outs] + [per_seq(WINDOW, KV_DIM), per_seq(WINDOW, KV_DIM)]
        + [row(LRU_WIDTH), per_seq(SUBLANES, LRU_WIDTH), per_seq(1, LRU_WIDTH)],
        out_shape=[jax.ShapeDtypeStruct((n, w), dt) for w, dt in outs]
        + [jax.ShapeDtypeStruct((batch, WINDOW, KV_DIM), F32)] * 2
        + [jax.ShapeDtypeStruct((n, LRU_WIDTH), BF16),
           jax.ShapeDtypeStruct((batch, SUBLANES, LRU_WIDTH), F32),
           jax.ShapeDtypeStruct((batch, 1, LRU_WIDTH), F32)],
        scratch_shapes=[pltpu.VMEM((tm, D_MODEL), BF16),
                        pltpu.VMEM((2 * SUBLANES + tm, LRU_WIDTH), F32), pltpu.VMEM((tm, LRU_WIDTH), F32),
                        pltpu.VMEM((1, LRU_WIDTH), F32)],
        compiler_params=_cparams(1, vmem=VMEM_LIMIT_BIG),
        name="in_proj_lru",
    )(x, sc, sh, g, cos, sin, w_in, wc, bc, wbd, ba, bx, lam)


def _lru_sample_kernel(xl_ref, yl_ref, cs_ref, h0_ref, wc_ref, bc_ref, wbd_ref, ba_ref, bx_ref, lam_ref,
                       o_ref, hlast_ref):
    t_new, nseq, w = xl_ref.shape
    xp = [cs_ref[k] for k in range(CONV_WIDTH - 1)] + [xl_ref[k].astype(F32) for k in range(t_new)]
    xcs = []
    for t in range(t_new):
        acc = bc_ref[...] + xp[t] * wc_ref[0:1, :]
        for k in range(1, CONV_WIDTH):
            acc = acc + xp[t + k] * wc_ref[k:k + 1, :]
        xcs.append(acc)
    xc = jnp.concatenate(xcs, axis=0)
    a, u = _lru_gates(xc, wbd_ref, ba_ref[...], bx_ref[...], lam_ref[...])
    h = h0_ref[...]
    for t in range(t_new):
        h = a[t * nseq:(t + 1) * nseq] * h + u[t * nseq:(t + 1) * nseq]
        o_ref[t] = (h * _gelu_tanh(yl_ref[t].astype(F32))).astype(BF16)
    hlast_ref[...] = h


def _lru_sample(xl, yl, cs, h0, wc, bc, wbd, ba, bx, lam):
    t_new, nseq, w = xl.shape
    full = lambda shp: pl.BlockSpec(shp, lambda i: (0,) * len(shp))
    args = (xl, yl, cs, h0, wc, bc, wbd, ba, bx, lam)
    return pl.pallas_call(
        _lru_sample_kernel,
        grid=(1,),
        in_specs=[full(a.shape) for a in args],
        out_specs=[full((t_new, nseq, w)), full((nseq, w))],
        out_shape=[jax.ShapeDtypeStruct((t_new, nseq, w), BF16), jax.ShapeDtypeStruct((nseq, w), F32)],
        compiler_params=_cparams(1),
        name="lru_sample",
    )(*args)


def _post_kernel(oa_ref, ol_ref, sga_ref, sgr_ref, x_ref, gta_ref, scf_ref, shf_ref,
                 wab_ref, wlb_ref, wout_ref, gpm_ref, gpf_ref, wrp_ref, wrt_ref, br_ref, tri_ref, cin_ref,
                 x1_ref, h2_ref, route_ref, stat_ref, cnt_ref, carry):
    @pl.when(pl.program_id(0) == 0)
    def _():
        carry[...] = cin_ref[...]

    b_attn = _dot(oa_ref[...], wab_ref[...])
    b_lru = _dot(ol_ref[...], wlb_ref[...])
    _post_tail(b_attn, b_lru, sga_ref, sgr_ref, x_ref, gta_ref, scf_ref, shf_ref, wout_ref, gpm_ref, gpf_ref,
               wrp_ref, wrt_ref, br_ref, tri_ref, x1_ref, h2_ref, route_ref, stat_ref, cnt_ref, carry)


def _post_attn_kernel(sink_ref, q_ref, kc_ref, kp_ref, vc_ref, vp_ref,
                      ol_ref, sga_ref, sgr_ref, x_ref, gta_ref, scf_ref, shf_ref,
                      wab_ref, wlb_ref, wout_ref, gpm_ref, gpf_ref, wrp_ref, wrt_ref, br_ref, tri_ref, cin_ref,
                      x1_ref, h2_ref, route_ref, stat_ref, cnt_ref, carry, oabuf, blbuf, *, tiles_per_seq):
    i = pl.program_id(0)

    @pl.when(i == 0)
    def _():
        carry[...] = cin_ref[...]

    cols = D_MODEL // ATTN_Q_BLOCKS

    def lru_piece(c):
        blbuf[:, c * cols:(c + 1) * cols] = _dot(ol_ref[...], wlb_ref[:, c * cols:(c + 1) * cols])

    _attention_tile(sink_ref, q_ref, kc_ref, kp_ref, vc_ref, vp_ref, oabuf, i % tiles_per_seq == 0, lru_piece)
    b_attn = _dot(oabuf[...], wab_ref[...])
    _post_tail(b_attn, blbuf[...], sga_ref, sgr_ref, x_ref, gta_ref, scf_ref, shf_ref, wout_ref, gpm_ref, gpf_ref,
               wrp_ref, wrt_ref, br_ref, tri_ref, x1_ref, h2_ref, route_ref, stat_ref, cnt_ref, carry)


def _post_tail(b_attn, b_lru, sga_ref, sgr_ref, x_ref, gta_ref, scf_ref, shf_ref, wout_ref, gpm_ref, gpf_ref,
               wrp_ref, wrt_ref, br_ref, tri_ref, x1_ref, h2_ref, route_ref, stat_ref, cnt_ref, carry):
    merged = sga_ref[...].astype(F32) * b_attn + sgr_ref[...].astype(F32) * b_lru
    mix = _dot(merged.astype(BF16), wout_ref[...])
    x1 = x_ref[...] + gta_ref[...] * _rms(mix, gpm_ref[...])
    x1_ref[...] = x1
    h2 = _rms(x1, gpf_ref[...]) * (1.0 + scf_ref[...]) + shf_ref[...]
    h2_ref[...] = h2.astype(BF16)

    h_hi = h2.astype(BF16)
    h_lo = (h2 - h_hi.astype(F32)).astype(BF16)
    hi_terms = _dot(h_hi, wrp_ref[...])
    logits = (hi_terms[:, :ROUTE_LANES] + (hi_terms[:, ROUTE_LANES:] + _dot(h_lo, wrt_ref[...]))) + br_ref[...]

    tm = logits.shape[0]
    lane = lax.broadcasted_iota(I32, (tm, ROUTE_LANES), 1)
    big = jnp.int32(ROUTE_LANES)
    is_g = lane < N_GROUPS
    lg = jnp.where(is_g, logits, NEG_INF)
    mg = jnp.max(lg, axis=-1, keepdims=True)
    g_star = jnp.min(jnp.where(lg == mg, lane, big), axis=-1, keepdims=True)
    p_star = 1.0 / jnp.sum(jnp.where(is_g, jnp.exp(lg - mg), 0.0), axis=-1, keepdims=True)
    lo = N_GROUPS + g_star * EXPERTS_PER_GROUP
    in_grp = (lane >= lo) & (lane < lo + EXPERTS_PER_GROUP)
    le = jnp.where(in_grp, logits, NEG_INF)
    m1 = jnp.max(le, axis=-1, keepdims=True)
    i1 = jnp.min(jnp.where(le == m1, lane, big), axis=-1, keepdims=True)
    le2 = jnp.where(lane == i1, NEG_INF, le)
    m2 = jnp.max(le2, axis=-1, keepdims=True)
    i2 = jnp.min(jnp.where(le2 == m2, lane, big), axis=-1, keepdims=True)
    e2x = jnp.exp(m2 - m1)
    wsum = 1.0 + e2x
    w1 = (1.0 / wsum) * p_star
    w2 = (e2x / wsum) * p_star

    oh1 = lane == i1
    oh2 = lane == i2
    cnt = jnp.where(oh1 | oh2, 1.0, 0.0)
    excl = _dot(tri_ref[...], cnt.astype(BF16))
    per_e = jnp.sum(cnt, axis=0, keepdims=True)
    pad8 = jnp.floor((per_e + (GRANULE - 1.0)) * (1.0 / GRANULE)) * GRANULE
    incl = jnp.broadcast_to(pad8, (SUBLANES, ROUTE_LANES))
    lane8 = lax.broadcasted_iota(I32, (SUBLANES, ROUTE_LANES), 1)
    d = 1
    while d < ROUTE_LANES:
        incl = incl + jnp.where(lane8 >= d, pltpu.roll(incl, d, 1), 0.0)
        d *= 2
    seg_start = incl[0:1, :] - pad8
    pos = excl + seg_start
    s1 = jnp.sum(jnp.where(oh1, pos, 0.0), axis=-1, keepdims=True)
    s2 = jnp.sum(jnp.where(oh2, pos, 0.0), axis=-1, keepdims=True)
    rec = jnp.where(lane == 0, s1, 0.0)
    rec = jnp.where(lane == 1, s2, rec)
    rec = jnp.where(lane == 2, w1, rec)
    rec = jnp.where(lane == 3, w2, rec)
    route_ref[...] = rec
    srow = lax.broadcasted_iota(I32, (SUBLANES, ROUTE_LANES), 0)
    stat_ref[...] = jnp.where(srow == 0, pad8, jnp.where(srow == 1, carry[...], 0.0))
    carry[...] = carry[...] + pad8
    cnt_ref[...] = carry[...]


def _post(oa, ol, sga, sgr, x, gta, scf, shf, wab, wlb, wout, gpm, gpf, wrh, wrl, br, tri, cin, rows_per_mod):
    n = x.shape[0]
    tm = min(ROW_TILE, n)
    if rows_per_mod:
        tiles_per_mod = rows_per_mod // tm
        mod_spec = pl.BlockSpec((None, 1, D_MODEL), lambda i: (i // tiles_per_mod, 0, 0))
    else:
        mod_spec = pl.BlockSpec((tm, D_MODEL), lambda i: (i, 0))
    row = lambda w: pl.BlockSpec((tm, w), lambda i: (i, 0))
    full = lambda a: pl.BlockSpec(a.shape, lambda i: (0,) * a.ndim)
    if isinstance(oa, tuple):
        assert tm == ATTN_Q_BLOCKS * WINDOW and rows_per_mod
        sinks, q, k, v = oa
        prev = pl.BlockSpec((WINDOW, KV_DIM), lambda i: (jnp.maximum(i * ATTN_Q_BLOCKS - 1, 0), 0))
        kernel = functools.partial(_post_attn_kernel, tiles_per_seq=rows_per_mod // tm)
        first_specs = [pl.BlockSpec(memory_space=pltpu.SMEM), row(Q_DIM), row(KV_DIM), prev, row(KV_DIM), prev]
        first_args = (sinks, q, k, k, v, v)
        scratch = [pltpu.VMEM((tm, Q_DIM), BF16), pltpu.VMEM((tm, D_MODEL), F32)]
    else:
        kernel, first_specs, first_args, scratch = _post_kernel, [row(Q_DIM)], (oa,), []
    return pl.pallas_call(
        kernel,
        grid=(n // tm,),
        in_specs=first_specs + [row(LRU_WIDTH), row(D_MODEL), row(D_MODEL), row(D_MODEL),
                  mod_spec, mod_spec, mod_spec,
                  full(wab), full(wlb), full(wout), full(gpm), full(gpf), full(wrh), full(wrl), full(br),
                  full(tri), full(cin)],
        out_specs=[row(D_MODEL), row(D_MODEL), row(ROUTE_LANES),
                   pl.BlockSpec((SUBLANES, ROUTE_LANES), lambda i: (i, 0)),
                   pl.BlockSpec((1, ROUTE_LANES), lambda i: (0, 0))],
        out_shape=[jax.ShapeDtypeStruct((n, D_MODEL), F32), jax.ShapeDtypeStruct((n, D_MODEL), BF16),
                   jax.ShapeDtypeStruct((n, ROUTE_LANES), F32),
                   jax.ShapeDtypeStruct((n // tm * SUBLANES, ROUTE_LANES), F32),
                   jax.ShapeDtypeStruct((1, ROUTE_LANES), F32)],
        scratch_shapes=[pltpu.VMEM((1, ROUTE_LANES), F32)] + scratch,
        compiler_params=_cparams(1),
        name="post_mix",
    )(*first_args, ol, sga, sgr, x, gta, scf, shf, wab, wlb, wout, gpm, gpf, wrh, wrl, br, tri, cin)


BLOCK_GRANULES = EXPERT_ROWS // GRANULE
BLOCK_SHIFT = BLOCK_GRANULES.bit_length() - 1
assert BLOCK_GRANULES == 1 << BLOCK_SHIFT


def _padded(c):
    return ((c + (BLOCK_GRANULES - 1)) >> BLOCK_SHIFT) << BLOCK_SHIFT


def _sorted_rows(tm):
    r = 2 * tm + N_EXPERTS * (GRANULE - 1)
    return -(-r // MXU_DIM) * MXU_DIM


def _plan_kernel(tot_ref, be_ref, meta_ref, *, n_blocks):
    def fill(j, _):
        be_ref[j] = N_EXPERTS - 1
        return 0
    lax.fori_loop(0, n_blocks, fill, 0)

    def per_expert(e, nb):
        k = _padded(tot_ref[e]) >> BLOCK_SHIFT

        def put(b, _):
            be_ref[nb + b] = e
            return 0
        lax.fori_loop(0, k, put, 0)
        return nb + k
    n_active = lax.fori_loop(0, N_EXPERTS, per_expert, 0)
    meta_ref[0] = n_active


def _plan(totals, n_blocks):
    return pl.pallas_call(
        functools.partial(_plan_kernel, n_blocks=n_blocks),
        in_specs=[pl.BlockSpec(memory_space=pltpu.SMEM)],
        out_specs=[pl.BlockSpec(memory_space=pltpu.SMEM), pl.BlockSpec(memory_space=pltpu.SMEM)],
        out_shape=[jax.ShapeDtypeStruct((n_blocks,), I32), jax.ShapeDtypeStruct((1,), I32)],
        name="moe_plan",
    )(totals)


def _expert_starts(tot_ref, pstart):
    def body(e, acc):
        pstart[e] = acc
        return acc + _padded(tot_ref[e])
    return lax.fori_loop(0, N_EXPERTS, body, 0)


def _granule(ref, g):
    return ref.at[pl.ds(pl.multiple_of(g * GRANULE, GRANULE), GRANULE)]


def _pack_halves(lo_f32, hi_f32):
    return (pltpu.bitcast(lo_f32, U32) >> 16) | (pltpu.bitcast(hi_f32, U32) & jnp.uint32(0xFFFF0000))


def _unpack_halves(packed):
    lo = pltpu.bitcast(packed << 16, F32).astype(BF16)
    hi = pltpu.bitcast(packed & jnp.uint32(0xFFFF0000), F32).astype(BF16)
    return lo, hi


def _sort_kernel(tot_ref, cnt_ref, goff_ref, rec_p_ref, h2_p_ref, rec_s_ref, h2_s_ref,
                 srt_ref, gsrc_ref, gslot_ref, pstart, *, tiles_p, n_sorted, n_slots):
    i = pl.program_id(0)
    from_sample = i >= tiles_p
    rec = jnp.where(from_sample, rec_s_ref[...], rec_p_ref[...])
    h2 = jnp.where(from_sample, h2_s_ref[...], h2_p_ref[...])
    tm = h2.shape[0]

    rec_t = rec.T
    s1 = rec_t[0:1, :].astype(I32)
    s2 = rec_t[1:2, :].astype(I32)
    rows = lax.broadcasted_iota(I32, (n_sorted, tm), 0)
    sel = jnp.where((rows == s1) | (rows == s2), 1.0, 0.0).astype(BF16)
    srt = _dot(sel, h2)
    srt_ref[...] = _pack_halves(srt[:, :HALF_D], srt[:, HALF_D:])

    zero_granule = n_sorted // GRANULE - 1

    @pl.when(i == 0)
    def _():
        used = _expert_starts(tot_ref, pstart)

        def put_zero(g, _):
            gsrc_ref[g] = zero_granule
            return 0

        def per_expert(e, _):
            total = tot_ref[e]
            lax.fori_loop(pstart[e] + total, pstart[e] + _padded(total), put_zero, 0)
            return 0
        lax.fori_loop(0, N_EXPERTS, per_expert, 0)
        lax.fori_loop(used, n_slots // GRANULE, put_zero, 0)

    tile_granules = n_sorted // GRANULE

    def per_expert(e, seg):
        k = cnt_ref[i * N_EXPERTS + e]
        dst = pstart[e] + goff_ref[i * N_EXPERTS + e]
        src = i * tile_granules + seg

        def put(g, _):
            gsrc_ref[dst + g] = src + g
            gslot_ref[src + g] = dst + g
            return 0
        lax.fori_loop(0, k, put, 0)
        return seg + k
    used_in_tile = lax.fori_loop(0, N_EXPERTS, per_expert, 0, unroll=4)

    def put_any(q, _):
        gslot_ref[i * tile_granules + q] = 0
        return 0
    lax.fori_loop(used_in_tile, tile_granules, put_any, 0)


def _sort(totals, cnt8, goff, rec_p, h2_p, rec_s, h2_s, n_slots):
    tm = min(ROW_TILE, h2_p.shape[0])
    assert h2_s.shape[0] % tm == 0
    tiles_p = h2_p.shape[0] // tm
    tiles_s = h2_s.shape[0] // tm
    n_sorted = _sorted_rows(tm)
    row_p = lambda w: pl.BlockSpec((tm, w), lambda i, *_: (jnp.minimum(i, tiles_p - 1), 0))
    row_s = lambda w: pl.BlockSpec((tm, w), lambda i, *_: (jnp.maximum(i - tiles_p, 0), 0))
    return pl.pallas_call(
        functools.partial(_sort_kernel, tiles_p=tiles_p, n_sorted=n_sorted, n_slots=n_slots),
        grid_spec=pltpu.PrefetchScalarGridSpec(
            num_scalar_prefetch=3,
            grid=(tiles_p + tiles_s,),
            in_specs=[row_p(ROUTE_LANES), row_p(D_MODEL), row_s(ROUTE_LANES), row_s(D_MODEL)],
            out_specs=[pl.BlockSpec((n_sorted, HALF_D), lambda i, *_: (i, 0)),
                       pl.BlockSpec(memory_space=pltpu.SMEM), pl.BlockSpec(memory_space=pltpu.SMEM)],
            scratch_shapes=[pltpu.SMEM((N_EXPERTS,), I32)]),
        out_shape=[jax.ShapeDtypeStruct(((tiles_p + tiles_s) * n_sorted, HALF_D), U32),
                   jax.ShapeDtypeStruct((n_slots // GRANULE,), I32),
                   jax.ShapeDtypeStruct(((tiles_p + tiles_s) * n_sorted // GRANULE,), I32)],
        compiler_params=_cparams(1),
        name="moe_sort",
    )(totals, cnt8, goff, rec_p, h2_p, rec_s, h2_s)


def _expert_kernel(be_ref, meta_ref, gsrc_ref, srt_hbm, wg_hbm, wu_hbm, wd_hbm, ys_ref,
                   xbuf, wgs, wus, wds, wgb, wub, wdb, stage, sems, wsems):
    j = pl.program_id(0)
    n_active = meta_ref[0]
    gran_per_block = BLOCK_GRANULES

    def granule_copy(blk, g, slot):
        return pltpu.make_async_copy(_granule(srt_hbm, gsrc_ref[blk * gran_per_block + g]),
                                     xbuf.at[slot, pl.ds(g * GRANULE, GRANULE)], sems.at[slot])

    def gather(blk, slot):
        for g in range(gran_per_block):
            granule_copy(blk, g, slot).start(priority=g % 2)

    def drain(slot):
        for g in range(gran_per_block):
            granule_copy(0, g, slot).wait()

    def weight_copies(e, p):
        return [pltpu.make_async_copy(wg_hbm.at[e], wgs.at[p], wsems.at[p]),
                pltpu.make_async_copy(wu_hbm.at[e], wus.at[p], wsems.at[p]),
                pltpu.make_async_copy(wd_hbm.at[e], wds.at[p], wsems.at[p])]

    @pl.when(j == 0)
    def _():
        gather(0, 0)
        stage[0] = 0
        for cp in weight_copies(be_ref[0], 0):
            cp.start()

    @pl.when(j < n_active)
    def _():
        slot = j % 2
        e = be_ref[j]

        @pl.when((j == 0) | (e != be_ref[jnp.maximum(j - 1, 0)]))
        def _():
            p = stage[0]
            for cp in weight_copies(e, p):
                cp.wait()
            wgb[...] = wgs[p].astype(BF16)
            wub[...] = wus[p].astype(BF16)
            wdb[...] = wds[p].astype(BF16)
            nxt = lax.while_loop(lambda k: (k < n_active) & (be_ref[jnp.minimum(k, n_active - 1)] == e),
                                 lambda k: k + 1, j + 1)

            @pl.when(nxt < n_active)
            def _():
                for cp in weight_copies(be_ref[nxt], 1 - p):
                    cp.start()
            stage[0] = 1 - p

        gather(jnp.minimum(j + 1, n_active - 1), 1 - slot)
        drain(slot)
        x_lo, x_hi = _unpack_halves(xbuf[slot])
        g = _dot(x_lo, wgb[0:HALF_D, :]) + _dot(x_hi, wgb[HALF_D:D_MODEL, :])
        u = _dot(x_lo, wub[0:HALF_D, :]) + _dot(x_hi, wub[HALF_D:D_MODEL, :])
        hmid = (g * _sigmoid(g) * u).astype(BF16)
        y = _dot(hmid, wdb[...])
        ys_ref[...] = _pack_halves(y[:, :HALF_D].astype(BF16).astype(F32), y[:, HALF_D:].astype(BF16).astype(F32))

        @pl.when(j == n_active - 1)
        def _():
            drain(1 - slot)

    @pl.when(j >= meta_ref[0])
    def _():
        ys_ref[...] = jnp.zeros(ys_ref.shape, U32)


def _experts(block_e, meta, gsrc, srt, wg, wu, wd, n_slots):
    n_blocks = n_slots // EXPERT_ROWS
    anyspec = pl.BlockSpec(memory_space=pl.ANY)
    return pl.pallas_call(
        _expert_kernel,
        grid_spec=pltpu.PrefetchScalarGridSpec(
            num_scalar_prefetch=3,
            grid=(n_blocks,),
            in_specs=[anyspec, anyspec, anyspec, anyspec],
            out_specs=pl.BlockSpec((EXPERT_ROWS, HALF_D), lambda j, be, meta, gs: (j, 0)),
            scratch_shapes=[pltpu.VMEM((2, EXPERT_ROWS, HALF_D), U32),
                            pltpu.VMEM((2, D_MODEL, D_EXPERT), F32), pltpu.VMEM((2, D_MODEL, D_EXPERT), F32),
                            pltpu.VMEM((2, D_EXPERT, D_MODEL), F32),
                            pltpu.VMEM((D_MODEL, D_EXPERT), BF16), pltpu.VMEM((D_MODEL, D_EXPERT), BF16),
                            pltpu.VMEM((D_EXPERT, D_MODEL), BF16),
                            pltpu.SMEM((1,), I32), pltpu.SemaphoreType.DMA((2,)), pltpu.SemaphoreType.DMA((2,))]),
        out_shape=jax.ShapeDtypeStruct((n_slots, HALF_D), U32),
        compiler_params=_cparams(1),
        name="moe_experts",
    )(block_e, meta, gsrc, srt, wg, wu, wd)


def _combine_kernel(gslot_ref, rec_ref, x1_ref, gtf_ref, g_ref, ys_hbm, y_ref, cbuf, sems, *, tile_base, n_tiles):
    i = pl.program_id(0)
    slot = i % 2
    tm = x1_ref.shape[0]
    n_sorted = cbuf.shape[1]
    tile_granules = n_sorted // GRANULE

    def granule_copy(slot_granule, q, slot_):
        return pltpu.make_async_copy(_granule(ys_hbm, slot_granule),
                                     cbuf.at[slot_, pl.ds(q * GRANULE, GRANULE)], sems.at[slot_])

    def gather(tile, slot_):
        for q in range(tile_granules):
            granule_copy(gslot_ref[tile * tile_granules + q], q, slot_).start(priority=q % 2)

    def drain(slot_):
        for q in range(tile_granules):
            granule_copy(0, q, slot_).wait()

    @pl.when(i == 0)
    def _():
        gather(tile_base, 0)

    gather(tile_base + jnp.minimum(i + 1, n_tiles - 1), 1 - slot)
    drain(slot)

    rec = rec_ref[...]
    s1 = rec[:, 0:1].astype(I32)
    s2 = rec[:, 1:2].astype(I32)
    col = lax.broadcasted_iota(I32, (tm, n_sorted), 1)
    wmat = (jnp.where(col == s1, rec[:, 2:3], 0.0) + jnp.where(col == s2, rec[:, 3:4], 0.0)).astype(BF16)
    y_lo, y_hi = _unpack_halves(cbuf[slot])
    f = jnp.concatenate([_dot(wmat, y_lo), _dot(wmat, y_hi)], axis=1)
    y_ref[...] = x1_ref[...] + gtf_ref[...] * _rms(f, g_ref[...])

    @pl.when(i == n_tiles - 1)
    def _():
        drain(1 - slot)


def _combine(gslot, rec, x1, gtf, g, ys, rows_per_mod, tile_base):
    n = x1.shape[0]
    tm = min(ROW_TILE, n)
    if rows_per_mod:
        tiles_per_mod = rows_per_mod // tm
        mod_spec = pl.BlockSpec((None, 1, D_MODEL), lambda i, *_: (i // tiles_per_mod, 0, 0))
    else:
        mod_spec = pl.BlockSpec((tm, D_MODEL), lambda i, *_: (i, 0))
    row = lambda w: pl.BlockSpec((tm, w), lambda i, *_: (i, 0))
    return pl.pallas_call(
        functools.partial(_combine_kernel, tile_base=tile_base, n_tiles=n // tm),
        grid_spec=pltpu.PrefetchScalarGridSpec(
            num_scalar_prefetch=1,
            grid=(n // tm,),
            in_specs=[row(ROUTE_LANES), row(D_MODEL), mod_spec,
                      pl.BlockSpec((1, D_MODEL), lambda i, *_: (0, 0)),
                      pl.BlockSpec(memory_space=pl.ANY)],
            out_specs=row(D_MODEL),
            scratch_shapes=[pltpu.VMEM((2, _sorted_rows(tm), HALF_D), U32), pltpu.SemaphoreType.DMA((2,))]),
        out_shape=jax.ShapeDtypeStruct((n, D_MODEL), F32),
        compiler_params=_cparams(1),
        name="moe_combine",
    )(gslot, rec, x1, gtf, g, ys)


def _rope_tables(pos):
    half = HEAD_DIM // 2
    inv = jnp.power(jnp.float32(ROPE_THETA), -jnp.arange(half, dtype=F32) / half)
    ang = pos.astype(F32)[:, None] * inv[None, :]
    cos = jnp.cos(ang)
    sin = jnp.sin(ang)
    reps = LANES // HEAD_DIM
    cos_t = jnp.tile(jnp.concatenate([cos, cos], axis=-1), (1, reps))
    sin_t = jnp.tile(jnp.concatenate([-sin, sin], axis=-1), (1, reps))
    return cos_t, sin_t


def _block_diag_gates(w_a, w_x):
    per = MXU_DIM // LRU_BLOCK_W
    groups = LRU_BLOCKS // per

    def bd(w):
        w = w.reshape(groups, per, LRU_BLOCK_W, LRU_BLOCK_W)
        eye = jnp.eye(per, dtype=w.dtype)
        full = jnp.einsum('gpij,pq->gpiqj', w, eye)
        return full.reshape(groups, MXU_DIM, MXU_DIM)
    return jnp.concatenate([bd(w_a), bd(w_x)], axis=-1).astype(BF16)


def _layer_forward(xp, xs_tm, ck, cv, cconv, ch, mod_p, mod_s, p):
    batch, seq, _ = xp.shape
    nseq, _, _, _ = ck.shape
    t_new = xs_tm.shape[0] // nseq
    n_p = batch * seq
    n_s = xs_tm.shape[0]

    w_in = p['w_in']
    wq = w_in[:, :Q_DIM].astype(BF16).reshape(D_MODEL, N_KV_HEADS, GQA_GROUP, HEAD_DIM)
    w_in_b = jnp.concatenate([wq.transpose(0, 2, 1, 3).reshape(D_MODEL, Q_DIM), w_in[:, Q_DIM:].astype(BF16)], axis=1)
    wab = (p['w_attn_branch'].astype(BF16).reshape(N_KV_HEADS, GQA_GROUP, HEAD_DIM, D_MODEL)
           .transpose(1, 0, 2, 3).reshape(Q_DIM, D_MODEL))
    wlb = p['w_lru_branch'].astype(BF16)
    wout = p['w_out'].astype(BF16)
    wbd = _block_diag_gates(p['w_lru_a'], p['w_lru_x'])
    row = lambda v: v.reshape(1, -1)
    wr = jnp.concatenate([p['w_router_group'], p['w_router_expert'],
                          jnp.zeros((D_MODEL, ROUTE_LANES - N_GROUPS - N_EXPERTS), F32)], axis=1)
    wr_top = wr.astype(BF16)
    wr_pair = jnp.concatenate([wr_top, (wr - wr_top.astype(F32)).astype(BF16)], axis=1)
    br = jnp.concatenate([p['b_router_group'], p['b_router_expert'],
                          jnp.zeros((ROUTE_LANES - N_GROUPS - N_EXPERTS,), F32)]).reshape(1, -1)
    wg = p['w_exp_gate']
    wu = p['w_exp_up']
    wd = p['w_exp_down']

    def mods(mod):
        return [mod[:, k * D_MODEL:(k + 1) * D_MODEL] for k in range(6)]
    sh_a_p, sc_a_p, gt_a_p, sh_f_p, sc_f_p, gt_f_p = [m.reshape(batch, 1, D_MODEL) for m in mods(mod_p)]
    sh_a_s, sc_a_s, gt_a_s, sh_f_s, sc_f_s, gt_f_s = [jnp.tile(m, (t_new, 1)) for m in mods(mod_s)]

    lru_w = (p['w_conv'], row(p['b_conv']), wbd, row(p['b_lru_a']), row(p['b_lru_x']), row(p['lru_lambda']))
    cos_p, sin_p = _rope_tables(jnp.arange(seq, dtype=I32))
    q_p, k_p, v_p, sga_p, sgr_p, klast_p, vlast_p, ol_p, xtail_p, hlast_p = _inproj_lru(
        xp.reshape(n_p, D_MODEL), sc_a_p, sh_a_p, row(p['g_pre_mix']), cos_p, sin_p, w_in_b, *lru_w, batch, seq)
    pos_s = jnp.repeat(PAST_LEN_ + jnp.arange(t_new, dtype=I32), nseq)
    cos_s, sin_s = _rope_tables(pos_s)
    q_s, k_s, v_s, xl_s, yl_s, sga_s, sgr_s = _inproj(
        xs_tm, sc_a_s, sh_a_s, row(p['g_pre_mix']), cos_s, sin_s, w_in_b, rows_per_mod=0,
        pos_tiles=n_s // min(ROW_TILE, n_s))

    sinks_perm = p['sinks']
    oa_p = (sinks_perm, q_p, k_p, v_p)
    rows = t_new * GQA_GROUP
    q_s3 = q_s.reshape(t_new, nseq, GQA_GROUP, KV_DIM).transpose(1, 0, 2, 3).reshape(nseq, rows, KV_DIM)
    kn = k_s.reshape(t_new, nseq, KV_DIM).transpose(1, 0, 2)
    vn = v_s.reshape(t_new, nseq, KV_DIM).transpose(1, 0, 2)
    kc = ck.transpose(0, 2, 3, 1).reshape(nseq, KV_DIM, WINDOW)
    vc = cv.transpose(0, 2, 3, 1).reshape(nseq, KV_DIM, WINDOW)
    sink_rows = jnp.tile(p['sinks'].reshape(N_KV_HEADS, 1, GQA_GROUP), (1, t_new, 1)).reshape(N_KV_HEADS, rows, 1)
    oa_s3 = _attn_sample(sink_rows, q_s3, kn, vn, kc, vc)
    oa_s = oa_s3.reshape(nseq, t_new, Q_DIM).transpose(1, 0, 2).reshape(n_s, Q_DIM)

    ol_s3, hlast_s = _lru_sample(xl_s.reshape(t_new, nseq, LRU_WIDTH), yl_s.reshape(t_new, nseq, LRU_WIDTH),
                                 cconv.transpose(1, 0, 2), ch, *lru_w)
    ol_s = ol_s3.reshape(n_s, LRU_WIDTH)

    tm_post = min(ROW_TILE, n_p)
    tri = jnp.tril(jnp.ones((tm_post, tm_post), F32), -1).astype(BF16)
    post_w = (wab, wlb, wout, row(p['g_post_mix']), row(p['g_pre_ffn']), wr_pair, wr_top, br)
    zero_cnt = jnp.zeros((1, ROUTE_LANES), F32)
    x1_p, h2_p, route_p, stat_p, cnt_p = _post(oa_p, ol_p, sga_p, sgr_p, xp.reshape(n_p, D_MODEL),
                                               gt_a_p, sc_f_p, sh_f_p, *post_w, tri, zero_cnt, rows_per_mod=seq)
    tm_s = min(ROW_TILE, n_s)
    tri_s = tri if tm_s == tm_post else jnp.tril(jnp.ones((tm_s, tm_s), F32), -1).astype(BF16)
    x1_s, h2_s, route_s, stat_s, cnt_all = _post(oa_s, ol_s, sga_s, sgr_s, xs_tm,
                                                 gt_a_s, sc_f_s, sh_f_s, *post_w, tri_s, cnt_p, rows_per_mod=0)

    e_lanes = slice(N_GROUPS, N_GROUPS + N_EXPERTS)
    to_granules = lambda v: (v.astype(I32) // GRANULE).reshape(-1)
    totals = to_granules(cnt_all[0, e_lanes])
    stats = jnp.concatenate([stat_p, stat_s], axis=0).reshape(-1, SUBLANES, ROUTE_LANES)
    cnt8 = to_granules(stats[:, 0, e_lanes])
    goff = to_granules(stats[:, 1, e_lanes])
    tiles_p = n_p // tm_post
    n_tiles = tiles_p + n_s // tm_s
    max_rows = 2 * (n_p + n_s) + n_tiles * N_EXPERTS * (GRANULE - 1) + N_EXPERTS * (EXPERT_ROWS - GRANULE)
    n_blocks = -(-max_rows // EXPERT_ROWS)
    n_slots = n_blocks * EXPERT_ROWS
    srt, gsrc, gslot = _sort(totals, cnt8, goff, route_p, h2_p, route_s, h2_s, n_slots)
    block_e, meta = _plan(totals, n_blocks)
    ys = _experts(block_e, meta, gsrc, srt, wg, wu, wd, n_slots)
    y_p = _combine(gslot, route_p, x1_p, gt_f_p, row(p['g_post_ffn']), ys, seq, 0)
    y_s = _combine(gslot, route_s, x1_s, gt_f_s, row(p['g_post_ffn']), ys, 0, tiles_p)

    k_new_p = klast_p.reshape(batch, WINDOW, N_KV_HEADS, HEAD_DIM)
    v_new_p = vlast_p.reshape(batch, WINDOW, N_KV_HEADS, HEAD_DIM)
    conv_p = xtail_p[:, -(CONV_WIDTH - 1):]
    h_p = hlast_p.reshape(batch, LRU_WIDTH)
    k_new_s = jnp.concatenate([ck, kn.reshape(nseq, t_new, N_KV_HEADS, HEAD_DIM)], axis=1)[:, -WINDOW:]
    v_new_s = jnp.concatenate([cv, vn.reshape(nseq, t_new, N_KV_HEADS, HEAD_DIM)], axis=1)[:, -WINDOW:]
    xl_s3 = xl_s.reshape(t_new, nseq, LRU_WIDTH).transpose(1, 0, 2).astype(F32)
    conv_s = jnp.concatenate([cconv, xl_s3], axis=1)[:, -(CONV_WIDTH - 1):]
    return (y_p.reshape(batch, seq, D_MODEL), y_s, k_new_p, v_new_p, conv_p, h_p,
            k_new_s, v_new_s, conv_s, hlast_s)


PAST_LEN_ = 16384

PARAM_NAMES = ('w_ada', 'b_ada', 'g_pre_mix', 'g_post_mix', 'g_pre_ffn', 'g_post_ffn', 'w_in', 'sinks',
               'w_conv', 'b_conv', 'w_lru_a', 'b_lru_a', 'w_lru_x', 'b_lru_x', 'lru_lambda',
               'w_attn_branch', 'w_lru_branch', 'w_out', 'w_router_group', 'b_router_group',
               'w_router_expert', 'b_router_expert', 'w_exp_gate', 'w_exp_up', 'w_exp_down')


def kernel(x_prompt, x_sample, cache_k_win, cache_v_win, state_conv, state_h, c_prompt, c_sample, w_ada, b_ada, g_pre_mix, g_post_mix, g_pre_ffn, g_post_ffn, w_in, sinks, w_conv, b_conv, w_lru_a, b_lru_a, w_lru_x, b_lru_x, lru_lambda, w_attn_branch, w_lru_branch, w_out, w_router_group, b_router_group, w_router_expert, b_router_expert, w_exp_gate, w_exp_up, w_exp_down):
    weights = (w_ada, b_ada, g_pre_mix, g_post_mix, g_pre_ffn, g_post_ffn, w_in, sinks,
               w_conv, b_conv, w_lru_a, b_lru_a, w_lru_x, b_lru_x, lru_lambda,
               w_attn_branch, w_lru_branch, w_out, w_router_group, b_router_group,
               w_router_expert, b_router_expert, w_exp_gate, w_exp_up, w_exp_down)
    depth = w_ada.shape[0]
    batch = x_prompt.shape[0]
    nseq, t_new, _ = x_sample.shape
    y_p = x_prompt
    y_s = x_sample.transpose(1, 0, 2).reshape(t_new * nseq, D_MODEL)
    c_all = jnp.concatenate([c_prompt, c_sample], axis=0)
    outs = [[] for _ in range(8)]
    for layer in range(depth):
        p = {name: w[layer] for name, w in zip(PARAM_NAMES, weights)}
        mod = _ada(c_all, p['w_ada'], p['b_ada'].reshape(1, -1))
        res = _layer_forward(y_p, y_s, cache_k_win[layer], cache_v_win[layer], state_conv[layer],
                             state_h[layer], mod[:batch], mod[batch:], p)
        y_p, y_s = res[0], res[1]
        for o, r in zip(outs, res[2:]):
            o.append(r)
    y_sample = y_s.reshape(t_new, nseq, D_MODEL).transpose(1, 0, 2)
    return (y_p, y_sample) + tuple(jnp.stack(o) for o in outs)
```

```python
import functools

import jax
import jax.numpy as jnp
from jax import lax
from jax.experimental import pallas as pl
from jax.experimental.pallas import tpu as pltpu

F32 = jnp.float32
BF16 = jnp.bfloat16
I32 = jnp.int32

D_MODEL = 1024
N_HEADS = 16
HEAD_DIM = 64
N_KV_HEADS = 4
GQA_GROUP = 4
WINDOW = 128
ROPE_THETA = 10000.0
NEG_INF = -1e30
LRU_WIDTH = 1024
LRU_BLOCKS = 16
LRU_BLOCK_W = 64
CONV_WIDTH = 4
LRU_C = 8.0
N_GROUPS = 4
EXPERTS_PER_GROUP = 8
N_EXPERTS = 32
D_EXPERT = 512
MOE_BLOCK = 128
NORM_EPS = 1e-6
Q_DIM = N_HEADS * HEAD_DIM
KV_DIM = N_KV_HEADS * HEAD_DIM
IN_DIM = Q_DIM + 2 * KV_DIM + 2 * LRU_WIDTH + 2 * D_MODEL

LANES = 128
SUBLANES = 8
MXU_DIM = 256
VMEM_LIMIT = 56 * 1024 * 1024
VMEM_LIMIT_BIG = 60 * 1024 * 1024

ROW_TILE = 512
ROUTE_LANES = LANES
GRANULE = SUBLANES
EXPERT_ROWS = 512
HALF_D = D_MODEL // 2
U32 = jnp.uint32


def _cparams(n_axes, vmem=VMEM_LIMIT):
    return pltpu.CompilerParams(dimension_semantics=("arbitrary",) * n_axes, vmem_limit_bytes=vmem)


def _rms(x, g):
    ms = jnp.mean(x * x, axis=-1, keepdims=True)
    return x * lax.rsqrt(ms + NORM_EPS) * g


def _sigmoid(x):
    return 1.0 / (1.0 + jnp.exp(-x))


def _dot(a, b):
    return jnp.dot(a, b, preferred_element_type=F32)


def _ada_kernel(c_ref, w_ref, b_ref, o_ref):
    c = c_ref[...]
    s = (c * _sigmoid(c)).astype(BF16)
    o_ref[...] = _dot(s, w_ref[...].astype(BF16)) + b_ref[...]


def _ada(c_all, w_ada, b_ada):
    r = c_all.shape[0]
    n = w_ada.shape[1]
    return pl.pallas_call(
        _ada_kernel,
        grid=(n // D_MODEL,),
        in_specs=[pl.BlockSpec((r, D_MODEL), lambda j: (0, 0)),
                  pl.BlockSpec((D_MODEL, D_MODEL), lambda j: (0, j)),
                  pl.BlockSpec((1, D_MODEL), lambda j: (0, j))],
        out_specs=pl.BlockSpec((r, D_MODEL), lambda j: (0, j)),
        out_shape=jax.ShapeDtypeStruct((r, n), F32),
        compiler_params=_cparams(1),
        name="ada_mod",
    )(c_all, w_ada, b_ada)


_O1 = Q_DIM
_O2 = _O1 + KV_DIM
_O3 = _O2 + KV_DIM
_O4 = _O3 + LRU_WIDTH
_O5 = _O4 + LRU_WIDTH
_O6 = _O5 + D_MODEL


def _prenorm(x_ref, g_ref, sc_ref, sh_ref):
    h = _rms(x_ref[...], g_ref[...]) * (1.0 + sc_ref[...]) + sh_ref[...]
    return h.astype(BF16)


PIECE_COLS = 256


def _qkv_gate_pieces(h_ref, w_ref, cos_ref, sin_ref, q_ref, k_ref, v_ref, sga_ref, sgr_ref, last_refs=None):
    def rope(t):
        cos = cos_ref[...]
        sin = sin_ref[...]
        lane = lax.broadcasted_iota(I32, cos.shape, 1)
        first_half = (lane % HEAD_DIM) < (HEAD_DIM // 2)
        rot = jnp.where(first_half, pltpu.roll(t, LANES - HEAD_DIM // 2, 1), pltpu.roll(t, HEAD_DIM // 2, 1))
        return t * cos + rot * sin

    def q_piece(c0):
        def run():
            qf = _dot(h_ref[...], w_ref[:, c0:c0 + PIECE_COLS])
            for c in range(PIECE_COLS // LANES):
                q_ref[:, c0 + c * LANES:c0 + (c + 1) * LANES] = rope(qf[:, c * LANES:(c + 1) * LANES]).astype(BF16)
        return run

    def kv_piece():
        kv = _dot(h_ref[...], w_ref[:, _O1:_O3])
        for c in range(KV_DIM // LANES):
            k_ref[:, c * LANES:(c + 1) * LANES] = rope(kv[:, c * LANES:(c + 1) * LANES])
        v_ref[...] = kv[:, KV_DIM:]
        if last_refs is not None:
            t = k_ref.shape[0]
            last_refs[0][...] = k_ref[t - WINDOW:t, :]
            last_refs[1][...] = v_ref[t - WINDOW:t, :]

    def gate_piece(o_ref, base, c0):
        def run():
            z = _dot(h_ref[...], w_ref[:, base + c0:base + c0 + PIECE_COLS])
            o_ref[:, c0:c0 + PIECE_COLS] = _sigmoid(z).astype(BF16)
        return run

    pieces = [q_piece(c0) for c0 in range(0, Q_DIM, PIECE_COLS)] + [kv_piece]
    pieces += [gate_piece(sga_ref, _O5, c0) for c0 in range(0, D_MODEL, PIECE_COLS)]
    pieces += [gate_piece(sgr_ref, _O6, c0) for c0 in range(0, D_MODEL, PIECE_COLS)]
    return pieces


def _inproj_kernel(x_ref, sc_ref, sh_ref, g_ref, cos_ref, sin_ref, w_ref,
                   q_ref, k_ref, v_ref, xl_ref, yl_ref, sga_ref, sgr_ref, hbuf):
    hbuf[...] = _prenorm(x_ref, g_ref, sc_ref, sh_ref)
    xl_ref[...] = _dot(hbuf[...], w_ref[:, _O3:_O4]).astype(BF16)
    yl_ref[...] = _dot(hbuf[...], w_ref[:, _O4:_O5]).astype(BF16)
    for piece in _qkv_gate_pieces(hbuf, w_ref, cos_ref, sin_ref, q_ref, k_ref, v_ref, sga_ref, sgr_ref):
        piece()


def _inproj(x, sc, sh, g, cos, sin, w_in, rows_per_mod, pos_tiles):
    n = x.shape[0]
    tm = min(ROW_TILE, n)
    if rows_per_mod:
        tiles_per_mod = rows_per_mod // tm
        mod_spec = pl.BlockSpec((None, 1, D_MODEL), lambda i: (i // tiles_per_mod, 0, 0))
    else:
        mod_spec = pl.BlockSpec((tm, D_MODEL), lambda i: (i, 0))
    row = lambda w: pl.BlockSpec((tm, w), lambda i: (i, 0))
    outs = [(Q_DIM, BF16), (KV_DIM, F32), (KV_DIM, F32), (LRU_WIDTH, BF16), (LRU_WIDTH, BF16),
            (D_MODEL, BF16), (D_MODEL, BF16)]
    return pl.pallas_call(
        _inproj_kernel,
        grid=(n // tm,),
        in_specs=[row(D_MODEL), mod_spec, mod_spec,
                  pl.BlockSpec((1, D_MODEL), lambda i: (0, 0)),
                  pl.BlockSpec((tm, LANES), lambda i: (i % pos_tiles, 0)),
                  pl.BlockSpec((tm, LANES), lambda i: (i % pos_tiles, 0)),
                  pl.BlockSpec((D_MODEL, IN_DIM), lambda i: (0, 0))],
        out_specs=[row(w) for w, _ in outs],
        out_shape=[jax.ShapeDtypeStruct((n, w), dt) for w, dt in outs],
        scratch_shapes=[pltpu.VMEM((tm, D_MODEL), BF16)],
        compiler_params=_cparams(1),
        name="in_proj",
    )(x, sc, sh, g, cos, sin, w_in)


def _head_masks(shape):
    lane = lax.broadcasted_iota(I32, shape, 1)
    return [(lane // HEAD_DIM) == h for h in range(N_KV_HEADS)]


def _attention_core(q_perm, kall, vall, valid, sink_of, rows):
    masks_b = _head_masks((rows, KV_DIM))
    zero_b = jnp.zeros((rows, KV_DIM), BF16)
    pieces = []
    for h in range(N_KV_HEADS):
        for g in range(GQA_GROUP):
            pieces.append(jnp.where(masks_b[h], q_perm[g], zero_b))
    q_big = jnp.concatenate(pieces, axis=0)
    s_all = lax.dot_general(q_big, kall, (((1,), (1,)), ((), ())), preferred_element_type=F32)
    s_all = s_all * (HEAD_DIM ** -0.5)
    p_chunks, inv_chunks = [], []
    for h in range(N_KV_HEADS):
        for g in range(GQA_GROUP):
            c = h * GQA_GROUP + g
            s = jnp.where(valid, s_all[c * rows:(c + 1) * rows], NEG_INF)
            sink = sink_of(h, g)
            m = jnp.maximum(jnp.max(s, axis=-1, keepdims=True), sink)
            p = jnp.exp(s - m)
            denom = jnp.sum(p, axis=-1, keepdims=True) + jnp.exp(sink - m)
            p_chunks.append(p.astype(BF16))
            inv_chunks.append(1.0 / denom)
    o_all = _dot(jnp.concatenate(p_chunks, axis=0), vall)
    outs = []
    for g in range(GQA_GROUP):
        acc = jnp.zeros((rows, KV_DIM), F32)
        for h in range(N_KV_HEADS):
            c = h * GQA_GROUP + g
            acc = acc + jnp.where(masks_b[h], o_all[c * rows:(c + 1) * rows] * inv_chunks[c], 0.0)
        outs.append(acc)
    return outs


ATTN_Q_BLOCKS = 4


def _attention_tile(sink_ref, q_ref, kc_ref, kp_ref, vc_ref, vp_ref, o_ref, seq_start, between=None):
    kall = jnp.concatenate([kp_ref[...], kc_ref[...]], axis=0).astype(BF16)
    vall = jnp.concatenate([vp_ref[...], vc_ref[...]], axis=0).astype(BF16)
    qi = lax.broadcasted_iota(I32, (WINDOW, 2 * WINDOW), 0)
    kj = lax.broadcasted_iota(I32, (WINDOW, 2 * WINDOW), 1)
    dist = qi + WINDOW - kj
    in_window = (dist >= 0) & (dist <= WINDOW)
    for c in range(ATTN_Q_BLOCKS):
        rows = slice(c * WINDOW, (c + 1) * WINDOW)
        keys = slice(c * WINDOW, (c + 2) * WINDOW)
        valid = in_window & ((kj >= WINDOW) | jnp.logical_not(seq_start)) if c == 0 else in_window
        q_perm = [q_ref[rows, g * KV_DIM:(g + 1) * KV_DIM] for g in range(GQA_GROUP)]
        outs = _attention_core(q_perm, kall[keys], vall[keys], valid,
                               lambda h, g: sink_ref[h * GQA_GROUP + g], WINDOW)
        for g in range(GQA_GROUP):
            o_ref[rows, g * KV_DIM:(g + 1) * KV_DIM] = outs[g].astype(BF16)
        if between is not None:
            between(c)


SEQ_PER_STEP = 8


def _attn_sample_kernel(sink_ref, q_ref, kn_ref, vn_ref, kc_ref, vc_ref, o_ref, kbuf, vbuf, *, t_new):
    rows = GQA_GROUP * t_new
    kbuf[...] = jnp.zeros(kbuf.shape, F32)
    vbuf[...] = jnp.zeros(vbuf.shape, F32)
    ri = lax.broadcasted_iota(I32, (rows, 2 * WINDOW), 0)
    kj = lax.broadcasted_iota(I32, (rows, 2 * WINDOW), 1)
    tq = ri // GQA_GROUP
    valid = (kj >= tq) & (kj <= tq + WINDOW) & (kj < WINDOW + t_new)
    nt_dims = (((1,), (1,)), ((), ()))
    for s in range(SEQ_PER_STEP):
        kbuf[0:t_new, :] = kn_ref[s]
        vbuf[0:t_new, :] = vn_ref[s]
        k_cache_t = kc_ref[s].astype(BF16)
        v_cache_t = vc_ref[s].astype(BF16)
        k_new = kbuf[...].astype(BF16)
        v_new = vbuf[...].astype(BF16)
        qs = q_ref[s]
        masks_b = _head_masks((rows, KV_DIM))
        zero_b = jnp.zeros((rows, KV_DIM), BF16)
        q_big = jnp.concatenate([jnp.where(masks_b[h], qs, zero_b) for h in range(N_KV_HEADS)], axis=0)
        s_all = jnp.concatenate([_dot(q_big, k_cache_t),
                                 lax.dot_general(q_big, k_new, nt_dims, preferred_element_type=F32)], axis=1)
        s_all = s_all * (HEAD_DIM ** -0.5)
        acc = jnp.zeros((rows, KV_DIM), F32)
        p_chunks, inv_chunks = [], []
        for h in range(N_KV_HEADS):
            sc = jnp.where(valid, s_all[h * rows:(h + 1) * rows], NEG_INF)
            sink = sink_ref[h]
            m = jnp.maximum(jnp.max(sc, axis=-1, keepdims=True), sink)
            p = jnp.exp(sc - m)
            denom = jnp.sum(p, axis=-1, keepdims=True) + jnp.exp(sink - m)
            p_chunks.append(p.astype(BF16))
            inv_chunks.append(1.0 / denom)
        p_all = jnp.concatenate(p_chunks, axis=0)
        o_all = (lax.dot_general(p_all[:, :WINDOW], v_cache_t, nt_dims, preferred_element_type=F32)
                 + _dot(p_all[:, WINDOW:], v_new))
        for h in range(N_KV_HEADS):
            acc = acc + jnp.where(masks_b[h], o_all[h * rows:(h + 1) * rows] * inv_chunks[h], 0.0)
        o_ref[s] = acc.astype(BF16)


def _attn_sample(sink_rows, q, kn, vn, kc, vc):
    nseq, rows, _ = q.shape
    t_new = kn.shape[1]
    sb = SEQ_PER_STEP
    blk = lambda r: pl.BlockSpec((sb, r, KV_DIM), lambda i: (i, 0, 0))
    cache = pl.BlockSpec((sb, KV_DIM, WINDOW), lambda i: (i, 0, 0))
    return pl.pallas_call(
        functools.partial(_attn_sample_kernel, t_new=t_new),
        grid=(nseq // sb,),
        in_specs=[pl.BlockSpec((N_KV_HEADS, rows, 1), lambda i: (0, 0, 0)),
                  blk(rows), blk(t_new), blk(t_new), cache, cache],
        out_specs=blk(rows),
        out_shape=jax.ShapeDtypeStruct((nseq, rows, KV_DIM), BF16),
        scratch_shapes=[pltpu.VMEM((WINDOW, KV_DIM), F32), pltpu.VMEM((WINDOW, KV_DIM), F32)],
        compiler_params=_cparams(1),
        name="attn_sample",
    )(sink_rows, q, kn, vn, kc, vc)


def _gelu_tanh(x):
    return 0.5 * x * (1.0 + jnp.tanh(0.7978845608028654 * (x + 0.044715 * x * x * x)))


def _lru_gates(xc, wbd_ref, ba, bx, lam):
    xcb = xc.astype(BF16)
    r_parts, i_parts = [], []
    for gidx in range(LRU_WIDTH // MXU_DIM):
        z = _dot(xcb[:, gidx * MXU_DIM:(gidx + 1) * MXU_DIM], wbd_ref[gidx])
        r_parts.append(z[:, :MXU_DIM])
        i_parts.append(z[:, MXU_DIM:])
    return _gate_math(jnp.concatenate(r_parts, axis=1), jnp.concatenate(i_parts, axis=1), xc, ba, bx, lam)


def _gate_math(zr, zi, xc, ba, bx, lam):
    r = _sigmoid(zr + ba)
    i = _sigmoid(zi + bx)
    softplus_neg_lam = jnp.maximum(-lam, 0.0) + jnp.log1p(jnp.exp(-jnp.abs(lam)))
    log_a = -LRU_C * r * softplus_neg_lam
    a = jnp.exp(log_a)
    y = 1.0 - a * a
    u = jnp.where(y > 0.0, y * lax.rsqrt(y), 0.0) * (i * xc)
    return a, u


LRU_CHUNK = 64


def _lru_chunk(c, xbuf, ybuf, hcar, ol_ref, wc_ref, bc_ref, wbd_ref, ba_ref, bx_ref, lam_ref):
    w = LRU_WIDTH
    n = LRU_CHUNK
    rows = slice(c * n, (c + 1) * n)
    r0 = SUBLANES + c * n
    xc = xbuf[r0:r0 + n, :] * wc_ref[CONV_WIDTH - 1:CONV_WIDTH, :] + bc_ref[...]
    for k in range(1, CONV_WIDTH):
        xc = xc + xbuf[r0 - k:r0 - k + n, :] * wc_ref[CONV_WIDTH - 1 - k:CONV_WIDTH - k, :]
    a, u = _lru_gates(xc, wbd_ref, ba_ref[...], bx_ref[...], lam_ref[...])

    ng = n // SUBLANES
    a3 = a.reshape(ng, SUBLANES, w)
    u3 = u.reshape(ng, SUBLANES, w)
    row = lax.broadcasted_iota(I32, (ng, SUBLANES, w), 1)
    d = 1
    while d < SUBLANES:
        a_s = jnp.where(row >= d, pltpu.roll(a3, d, 1), 1.0)
        u_s = jnp.where(row >= d, pltpu.roll(u3, d, 1), 0.0)
        u3 = a3 * u_s + u3
        a3 = a3 * a_s
        d *= 2
    carry = hcar[...]
    hs = []
    for gi in range(ng):
        hg = a3[gi] * carry + u3[gi]
        hs.append(hg)
        carry = hg[SUBLANES - 1:SUBLANES, :]
    hcar[...] = carry
    h = jnp.concatenate(hs, axis=0)
    ol_ref[rows, :] = (h * _gelu_tanh(ybuf[rows, :])).astype(BF16)


def _inproj_lru_kernel(x_ref, sc_ref, sh_ref, g_ref, cos_ref, sin_ref, w_ref,
                       wc_ref, bc_ref, wbd_ref, ba_ref, bx_ref, lam_ref,
                       q_ref, k_ref, v_ref, sga_ref, sgr_ref, klast_ref, vlast_ref, ol_ref, xtail_ref, hlast_ref,
                       hbuf, xbuf, ybuf, hcar, *, tiles_per_seq):
    t = x_ref.shape[0]

    @pl.when(pl.program_id(0) % tiles_per_seq == 0)
    def _():
        xbuf[0:SUBLANES, :] = jnp.zeros((SUBLANES, LRU_WIDTH), F32)
        hcar[...] = jnp.zeros((1, LRU_WIDTH), F32)

    hbuf[...] = _prenorm(x_ref, g_ref, sc_ref, sh_ref)

    xbuf[SUBLANES:SUBLANES + t, :] = _dot(hbuf[...], w_ref[:, _O3:_O4])
    ybuf[...] = _dot(hbuf[...], w_ref[:, _O4:_O5])

    pieces = _qkv_gate_pieces(hbuf, w_ref, cos_ref, sin_ref, q_ref, k_ref, v_ref, sga_ref, sgr_ref,
                              last_refs=(klast_ref, vlast_ref))
    n_chunks = t // LRU_CHUNK
    per_chunk = -(-len(pieces) // n_chunks)
    for c in range(n_chunks):
        _lru_chunk(c, xbuf, ybuf, hcar, ol_ref, wc_ref, bc_ref, wbd_ref, ba_ref, bx_ref, lam_ref)
        for piece in pieces[c * per_chunk:(c + 1) * per_chunk]:
            piece()

    tail = xbuf[t:t + SUBLANES, :]
    xtail_ref[...] = tail
    xbuf[0:SUBLANES, :] = tail
    hlast_ref[...] = hcar[...]


def _inproj_lru(x, sc, sh, g, cos, sin, w_in, wc, bc, wbd, ba, bx, lam, batch, seq):
    n = x.shape[0]
    tm = min(ROW_TILE, seq)
    tps = seq // tm
    mod_spec = pl.BlockSpec((None, 1, D_MODEL), lambda i: (i // tps, 0, 0))
    row = lambda w: pl.BlockSpec((tm, w), lambda i: (i, 0))
    full = lambda a: pl.BlockSpec(a.shape, lambda i: (0,) * a.ndim)
    per_seq = lambda r, w: pl.BlockSpec((None, r, w), lambda i: (i // tps, 0, 0))
    outs = [(Q_DIM, BF16), (KV_DIM, F32), (KV_DIM, F32), (D_MODEL, BF16), (D_MODEL, BF16)]
    return pl.pallas_call(
        functools.partial(_inproj_lru_kernel, tiles_per_seq=tps),
        grid=(n // tm,),
        in_specs=[row(D_MODEL), mod_spec, mod_spec, full(g),
                  pl.BlockSpec((tm, LANES), lambda i: (i % tps, 0)),
                  pl.BlockSpec((tm, LANES), lambda i: (i % tps, 0)),
                  full(w_in), full(wc), full(bc), full(wbd), full(ba), full(bx), full(lam)],
        out_specs=[row(w) for w, _ in outs] + [per_seq(WINDOW, KV_DIM), per_seq(WINDOW, KV_DIM)]
        + [row(LRU_WIDTH), per_seq(SUBLANES, LRU_WIDTH), per_seq(1, LRU_WIDTH)],
        out_shape=[jax.ShapeDtypeStruct((n, w), dt) for w, dt in outs]
        + [jax.ShapeDtypeStruct((batch, WINDOW, KV_DIM), F32)] * 2
        + [jax.ShapeDtypeStruct((n, LRU_WIDTH), BF16),
           jax.ShapeDtypeStruct((batch, SUBLANES, LRU_WIDTH), F32),
           jax.ShapeDtypeStruct((batch, 1, LRU_WIDTH), F32)],
        scratch_shapes=[pltpu.VMEM((tm, D_MODEL), BF16),
                        pltpu.VMEM((2 * SUBLANES + tm, LRU_WIDTH), F32), pltpu.VMEM((tm, LRU_WIDTH), F32),
                        pltpu.VMEM((1, LRU_WIDTH), F32)],
        compiler_params=_cparams(1, vmem=VMEM_LIMIT_BIG),
        name="in_proj_lru",
    )(x, sc, sh, g, cos, sin, w_in, wc, bc, wbd, ba, bx, lam)


def _lru_sample_kernel(xl_ref, yl_ref, cs_ref, h0_ref, wc_ref, bc_ref, wbd_ref, ba_ref, bx_ref, lam_ref,
                       o_ref, hlast_ref):
    t_new, nseq, w = xl_ref.shape
    xp = [cs_ref[k] for k in range(CONV_WIDTH - 1)] + [xl_ref[k].astype(F32) for k in range(t_new)]
    xcs = []
    for t in range(t_new):
        acc = bc_ref[...] + xp[t] * wc_ref[0:1, :]
        for k in range(1, CONV_WIDTH):
            acc = acc + xp[t + k] * wc_ref[k:k + 1, :]
        xcs.append(acc)
    xc = jnp.concatenate(xcs, axis=0)
    a, u = _lru_gates(xc, wbd_ref, ba_ref[...], bx_ref[...], lam_ref[...])
    h = h0_ref[...]
    for t in range(t_new):
        h = a[t * nseq:(t + 1) * nseq] * h + u[t * nseq:(t + 1) * nseq]
        o_ref[t] = (h * _gelu_tanh(yl_ref[t].astype(F32))).astype(BF16)
    hlast_ref[...] = h


def _lru_sample(xl, yl, cs, h0, wc, bc, wbd, ba, bx, lam):
    t_new, nseq, w = xl.shape
    full = lambda shp: pl.BlockSpec(shp, lambda i: (0,) * len(shp))
    args = (xl, yl, cs, h0, wc, bc, wbd, ba, bx, lam)
    return pl.pallas_call(
        _lru_sample_kernel,
        grid=(1,),
        in_specs=[full(a.shape) for a in args],
        out_specs=[full((t_new, nseq, w)), full((nseq, w))],
        out_shape=[jax.ShapeDtypeStruct((t_new, nseq, w), BF16), jax.ShapeDtypeStruct((nseq, w), F32)],
        compiler_params=_cparams(1),
        name="lru_sample",
    )(*args)


def _post_kernel(oa_ref, ol_ref, sga_ref, sgr_ref, x_ref, gta_ref, scf_ref, shf_ref,
                 wab_ref, wlb_ref, wout_ref, gpm_ref, gpf_ref, wrp_ref, wrt_ref, br_ref, tri_ref, cin_ref,
                 x1_ref, h2_ref, route_ref, stat_ref, cnt_ref, carry):
    @pl.when(pl.program_id(0) == 0)
    def _():
        carry[...] = cin_ref[...]

    b_attn = _dot(oa_ref[...], wab_ref[...])
    b_lru = _dot(ol_ref[...], wlb_ref[...])
    _post_tail(b_attn, b_lru, sga_ref, sgr_ref, x_ref, gta_ref, scf_ref, shf_ref, wout_ref, gpm_ref, gpf_ref,
               wrp_ref, wrt_ref, br_ref, tri_ref, x1_ref, h2_ref, route_ref, stat_ref, cnt_ref, carry)


def _post_attn_kernel(sink_ref, q_ref, kc_ref, kp_ref, vc_ref, vp_ref,
                      ol_ref, sga_ref, sgr_ref, x_ref, gta_ref, scf_ref, shf_ref,
                      wab_ref, wlb_ref, wout_ref, gpm_ref, gpf_ref, wrp_ref, wrt_ref, br_ref, tri_ref, cin_ref,
                      x1_ref, h2_ref, route_ref, stat_ref, cnt_ref, carry, oabuf, blbuf, *, tiles_per_seq):
    i = pl.program_id(0)

    @pl.when(i == 0)
    def _():
        carry[...] = cin_ref[...]

    cols = D_MODEL // ATTN_Q_BLOCKS

    def lru_piece(c):
        blbuf[:, c * cols:(c + 1) * cols] = _dot(ol_ref[...], wlb_ref[:, c * cols:(c + 1) * cols])

    _attention_tile(sink_ref, q_ref, kc_ref, kp_ref, vc_ref, vp_ref, oabuf, i % tiles_per_seq == 0, lru_piece)
    b_attn = _dot(oabuf[...], wab_ref[...])
    _post_tail(b_attn, blbuf[...], sga_ref, sgr_ref, x_ref, gta_ref, scf_ref, shf_ref, wout_ref, gpm_ref, gpf_ref,
               wrp_ref, wrt_ref, br_ref, tri_ref, x1_ref, h2_ref, route_ref, stat_ref, cnt_ref, carry)


def _post_tail(b_attn, b_lru, sga_ref, sgr_ref, x_ref, gta_ref, scf_ref, shf_ref, wout_ref, gpm_ref, gpf_ref,
               wrp_ref, wrt_ref, br_ref, tri_ref, x1_ref, h2_ref, route_ref, stat_ref, cnt_ref, carry):
    merged = sga_ref[...].astype(F32) * b_attn + sgr_ref[...].astype(F32) * b_lru
    mix = _dot(merged.astype(BF16), wout_ref[...])
    x1 = x_ref[...] + gta_ref[...] * _rms(mix, gpm_ref[...])
    x1_ref[...] = x1
    h2 = _rms(x1, gpf_ref[...]) * (1.0 + scf_ref[...]) + shf_ref[...]
    h2_ref[...] = h2.astype(BF16)

    h_hi = h2.astype(BF16)
    h_lo = (h2 - h_hi.astype(F32)).astype(BF16)
    hi_terms = _dot(h_hi, wrp_ref[...])
    logits = (hi_terms[:, :ROUTE_LANES] + (hi_terms[:, ROUTE_LANES:] + _dot(h_lo, wrt_ref[...]))) + br_ref[...]

    tm = logits.shape[0]
    lane = lax.broadcasted_iota(I32, (tm, ROUTE_LANES), 1)
    big = jnp.int32(ROUTE_LANES)
    is_g = lane < N_GROUPS
    lg = jnp.where(is_g, logits, NEG_INF)
    mg = jnp.max(lg, axis=-1, keepdims=True)
    g_star = jnp.min(jnp.where(lg == mg, lane, big), axis=-1, keepdims=True)
    p_star = 1.0 / jnp.sum(jnp.where(is_g, jnp.exp(lg - mg), 0.0), axis=-1, keepdims=True)
    lo = N_GROUPS + g_star * EXPERTS_PER_GROUP
    in_grp = (lane >= lo) & (lane < lo + EXPERTS_PER_GROUP)
    le = jnp.where(in_grp, logits, NEG_INF)
    m1 = jnp.max(le, axis=-1, keepdims=True)
    i1 = jnp.min(jnp.where(le == m1, lane, big), axis=-1, keepdims=True)
    le2 = jnp.where(lane == i1, NEG_INF, le)
    m2 = jnp.max(le2, axis=-1, keepdims=True)
    i2 = jnp.min(jnp.where(le2 == m2, lane, big), axis=-1, keepdims=True)
    e2x = jnp.exp(m2 - m1)
    wsum = 1.0 + e2x
    w1 = (1.0 / wsum) * p_star
    w2 = (e2x / wsum) * p_star

    oh1 = lane == i1
    oh2 = lane == i2
    cnt = jnp.where(oh1 | oh2, 1.0, 0.0)
    excl = _dot(tri_ref[...], cnt.astype(BF16))
    per_e = jnp.sum(cnt, axis=0, keepdims=True)
    pad8 = jnp.floor((per_e + (GRANULE - 1.0)) * (1.0 / GRANULE)) * GRANULE
    incl = jnp.broadcast_to(pad8, (SUBLANES, ROUTE_LANES))
    lane8 = lax.broadcasted_iota(I32, (SUBLANES, ROUTE_LANES), 1)
    d = 1
    while d < ROUTE_LANES:
        incl = incl + jnp.where(lane8 >= d, pltpu.roll(incl, d, 1), 0.0)
        d *= 2
    seg_start = incl[0:1, :] - pad8
    pos = excl + seg_start
    s1 = jnp.sum(jnp.where(oh1, pos, 0.0), axis=-1, keepdims=True)
    s2 = jnp.sum(jnp.where(oh2, pos, 0.0), axis=-1, keepdims=True)
    rec = jnp.where(lane == 0, s1, 0.0)
    rec = jnp.where(lane == 1, s2, rec)
    rec = jnp.where(lane == 2, w1, rec)
    rec = jnp.where(lane == 3, w2, rec)
    route_ref[...] = rec
    srow = lax.broadcasted_iota(I32, (SUBLANES, ROUTE_LANES), 0)
    stat_ref[...] = jnp.where(srow == 0, pad8, jnp.where(srow == 1, carry[...], 0.0))
    carry[...] = carry[...] + pad8
    cnt_ref[...] = carry[...]


def _post(oa, ol, sga, sgr, x, gta, scf, shf, wab, wlb, wout, gpm, gpf, wrh, wrl, br, tri, cin, rows_per_mod):
    n = x.shape[0]
    tm = min(ROW_TILE, n)
    if rows_per_mod:
        tiles_per_mod = rows_per_mod // tm
        mod_spec = pl.BlockSpec((None, 1, D_MODEL), lambda i: (i // tiles_per_mod, 0, 0))
    else:
        mod_spec = pl.BlockSpec((tm, D_MODEL), lambda i: (i, 0))
    row = lambda w: pl.BlockSpec((tm, w), lambda i: (i, 0))
    full = lambda a: pl.BlockSpec(a.shape, lambda i: (0,) * a.ndim)
    if isinstance(oa, tuple):
        assert tm == ATTN_Q_BLOCKS * WINDOW and rows_per_mod
        sinks, q, k, v = oa
        prev = pl.BlockSpec((WINDOW, KV_DIM), lambda i: (jnp.maximum(i * ATTN_Q_BLOCKS - 1, 0), 0))
        kernel = functools.partial(_post_attn_kernel, tiles_per_seq=rows_per_mod // tm)
        first_specs = [pl.BlockSpec(memory_space=pltpu.SMEM), row(Q_DIM), row(KV_DIM), prev, row(KV_DIM), prev]
        first_args = (sinks, q, k, k, v, v)
        scratch = [pltpu.VMEM((tm, Q_DIM), BF16), pltpu.VMEM((tm, D_MODEL), F32)]
    else:
        kernel, first_specs, first_args, scratch = _post_kernel, [row(Q_DIM)], (oa,), []
    return pl.pallas_call(
        kernel,
        grid=(n // tm,),
        in_specs=first_specs + [row(LRU_WIDTH), row(D_MODEL), row(D_MODEL), row(D_MODEL),
                  mod_spec, mod_spec, mod_spec,
                  full(wab), full(wlb), full(wout), full(gpm), full(gpf), full(wrh), full(wrl), full(br),
                  full(tri), full(cin)],
        out_specs=[row(D_MODEL), row(D_MODEL), row(ROUTE_LANES),
                   pl.BlockSpec((SUBLANES, ROUTE_LANES), lambda i: (i, 0)),
                   pl.BlockSpec((1, ROUTE_LANES), lambda i: (0, 0))],
        out_shape=[jax.ShapeDtypeStruct((n, D_MODEL), F32), jax.ShapeDtypeStruct((n, D_MODEL), BF16),
                   jax.ShapeDtypeStruct((n, ROUTE_LANES), F32),
                   jax.ShapeDtypeStruct((n // tm * SUBLANES, ROUTE_LANES), F32),
                   jax.ShapeDtypeStruct((1, ROUTE_LANES), F32)],
        scratch_shapes=[pltpu.VMEM((1, ROUTE_LANES), F32)] + scratch,
        compiler_params=_cparams(1),
        name="post_mix",
    )(*first_args, ol, sga, sgr, x, gta, scf, shf, wab, wlb, wout, gpm, gpf, wrh, wrl, br, tri, cin)


BLOCK_GRANULES = EXPERT_ROWS // GRANULE
BLOCK_SHIFT = BLOCK_GRANULES.bit_length() - 1
assert BLOCK_GRANULES == 1 << BLOCK_SHIFT


def _padded(c):
    return ((c + (BLOCK_GRANULES - 1)) >> BLOCK_SHIFT) << BLOCK_SHIFT


def _sorted_rows(tm):
    r = 2 * tm + N_EXPERTS * (GRANULE - 1)
    return -(-r // MXU_DIM) * MXU_DIM


def _plan_kernel(tot_ref, be_ref, meta_ref, *, n_blocks):
    def fill(j, _):
        be_ref[j] = N_EXPERTS - 1
        return 0
    lax.fori_loop(0, n_blocks, fill, 0)

    def per_expert(e, nb):
        k = _padded(tot_ref[e]) >> BLOCK_SHIFT

        def put(b, _):
            be_ref[nb + b] = e
            return 0
        lax.fori_loop(0, k, put, 0)
        return nb + k
    n_active = lax.fori_loop(0, N_EXPERTS, per_expert, 0)
    meta_ref[0] = n_active


def _plan(totals, n_blocks):
    return pl.pallas_call(
        functools.partial(_plan_kernel, n_blocks=n_blocks),
        in_specs=[pl.BlockSpec(memory_space=pltpu.SMEM)],
        out_specs=[pl.BlockSpec(memory_space=pltpu.SMEM), pl.BlockSpec(memory_space=pltpu.SMEM)],
        out_shape=[jax.ShapeDtypeStruct((n_blocks,), I32), jax.ShapeDtypeStruct((1,), I32)],
        name="moe_plan",
    )(totals)


def _expert_starts(tot_ref, pstart):
    def body(e, acc):
        pstart[e] = acc
        return acc + _padded(tot_ref[e])
    return lax.fori_loop(0, N_EXPERTS, body, 0)


def _granule(ref, g):
    return ref.at[pl.ds(pl.multiple_of(g * GRANULE, GRANULE), GRANULE)]


def _pack_halves(lo_f32, hi_f32):
    return (pltpu.bitcast(lo_f32, U32) >> 16) | (pltpu.bitcast(hi_f32, U32) & jnp.uint32(0xFFFF0000))


def _unpack_halves(packed):
    lo = pltpu.bitcast(packed << 16, F32).astype(BF16)
    hi = pltpu.bitcast(packed & jnp.uint32(0xFFFF0000), F32).astype(BF16)
    return lo, hi


def _sort_kernel(tot_ref, cnt_ref, goff_ref, rec_p_ref, h2_p_ref, rec_s_ref, h2_s_ref,
                 srt_ref, gsrc_ref, gslot_ref, pstart, *, tiles_p, n_sorted, n_slots):
    i = pl.program_id(0)
    from_sample = i >= tiles_p
    rec = jnp.where(from_sample, rec_s_ref[...], rec_p_ref[...])
    h2 = jnp.where(from_sample, h2_s_ref[...], h2_p_ref[...])
    tm = h2.shape[0]

    rec_t = rec.T
    s1 = rec_t[0:1, :].astype(I32)
    s2 = rec_t[1:2, :].astype(I32)
    rows = lax.broadcasted_iota(I32, (n_sorted, tm), 0)
    sel = jnp.where((rows == s1) | (rows == s2), 1.0, 0.0).astype(BF16)
    srt = _dot(sel, h2)
    srt_ref[...] = _pack_halves(srt[:, :HALF_D], srt[:, HALF_D:])

    zero_granule = n_sorted // GRANULE - 1

    @pl.when(i == 0)
    def _():
        used = _expert_starts(tot_ref, pstart)

        def put_zero(g, _):
            gsrc_ref[g] = zero_granule
            return 0

        def per_expert(e, _):
            total = tot_ref[e]
            lax.fori_loop(pstart[e] + total, pstart[e] + _padded(total), put_zero, 0)
            return 0
        lax.fori_loop(0, N_EXPERTS, per_expert, 0)
        lax.fori_loop(used, n_slots // GRANULE, put_zero, 0)

    tile_granules = n_sorted // GRANULE

    def per_expert(e, seg):
        k = cnt_ref[i * N_EXPERTS + e]
        dst = pstart[e] + goff_ref[i * N_EXPERTS + e]
        src = i * tile_granules + seg

        def put(g, _):
            gsrc_ref[dst + g] = src + g
            gslot_ref[src + g] = dst + g
            return 0
        lax.fori_loop(0, k, put, 0)
        return seg + k
    used_in_tile = lax.fori_loop(0, N_EXPERTS, per_expert, 0, unroll=4)

    def put_any(q, _):
        gslot_ref[i * tile_granules + q] = 0
        return 0
    lax.fori_loop(used_in_tile, tile_granules, put_any, 0)


def _sort(totals, cnt8, goff, rec_p, h2_p, rec_s, h2_s, n_slots):
    tm = min(ROW_TILE, h2_p.shape[0])
    assert h2_s.shape[0] % tm == 0
    tiles_p = h2_p.shape[0] // tm
    tiles_s = h2_s.shape[0] // tm
    n_sorted = _sorted_rows(tm)
    row_p = lambda w: pl.BlockSpec((tm, w), lambda i, *_: (jnp.minimum(i, tiles_p - 1), 0))
    row_s = lambda w: pl.BlockSpec((tm, w), lambda i, *_: (jnp.maximum(i - tiles_p, 0), 0))
    return pl.pallas_call(
        functools.partial(_sort_kernel, tiles_p=tiles_p, n_sorted=n_sorted, n_slots=n_slots),
        grid_spec=pltpu.PrefetchScalarGridSpec(
            num_scalar_prefetch=3,
            grid=(tiles_p + tiles_s,),
            in_specs=[row_p(ROUTE_LANES), row_p(D_MODEL), row_s(ROUTE_LANES), row_s(D_MODEL)],
            out_specs=[pl.BlockSpec((n_sorted, HALF_D), lambda i, *_: (i, 0)),
                       pl.BlockSpec(memory_space=pltpu.SMEM), pl.BlockSpec(memory_space=pltpu.SMEM)],
            scratch_shapes=[pltpu.SMEM((N_EXPERTS,), I32)]),
        out_shape=[jax.ShapeDtypeStruct(((tiles_p + tiles_s) * n_sorted, HALF_D), U32),
                   jax.ShapeDtypeStruct((n_slots // GRANULE,), I32),
                   jax.ShapeDtypeStruct(((tiles_p + tiles_s) * n_sorted // GRANULE,), I32)],
        compiler_params=_cparams(1),
        name="moe_sort",
    )(totals, cnt8, goff, rec_p, h2_p, rec_s, h2_s)


def _expert_kernel(be_ref, meta_ref, gsrc_ref, srt_hbm, wg_hbm, wu_hbm, wd_hbm, ys_ref,
                   xbuf, wgs, wus, wds, wgb, wub, wdb, stage, sems, wsems):
    j = pl.program_id(0)
    n_active = meta_ref[0]
    gran_per_block = BLOCK_GRANULES

    def granule_copy(blk, g, slot):
        return pltpu.make_async_copy(_granule(srt_hbm, gsrc_ref[blk * gran_per_block + g]),
                                     xbuf.at[slot, pl.ds(g * GRANULE, GRANULE)], sems.at[slot])

    def gather(blk, slot):
        for g in range(gran_per_block):
            granule_copy(blk, g, slot).start(priority=g % 2)

    def drain(slot):
        for g in range(gran_per_block):
            granule_copy(0, g, slot).wait()

    def weight_copies(e, p):
        return [pltpu.make_async_copy(wg_hbm.at[e], wgs.at[p], wsems.at[p]),
                pltpu.make_async_copy(wu_hbm.at[e], wus.at[p], wsems.at[p]),
                pltpu.make_async_copy(wd_hbm.at[e], wds.at[p], wsems.at[p])]

    @pl.when(j == 0)
    def _():
        gather(0, 0)
        stage[0] = 0
        for cp in weight_copies(be_ref[0], 0):
            cp.start()

    @pl.when(j < n_active)
    def _():
        slot = j % 2
        e = be_ref[j]

        @pl.when((j == 0) | (e != be_ref[jnp.maximum(j - 1, 0)]))
        def _():
            p = stage[0]
            for cp in weight_copies(e, p):
                cp.wait()
            wgb[...] = wgs[p].astype(BF16)
            wub[...] = wus[p].astype(BF16)
            wdb[...] = wds[p].astype(BF16)
            nxt = lax.while_loop(lambda k: (k < n_active) & (be_ref[jnp.minimum(k, n_active - 1)] == e),
                                 lambda k: k + 1, j + 1)

            @pl.when(nxt < n_active)
            def _():
                for cp in weight_copies(be_ref[nxt], 1 - p):
                    cp.start()
            stage[0] = 1 - p

        gather(jnp.minimum(j + 1, n_active - 1), 1 - slot)
        drain(slot)
        x_lo, x_hi = _unpack_halves(xbuf[slot])
        g = _dot(x_lo, wgb[0:HALF_D, :]) + _dot(x_hi, wgb[HALF_D:D_MODEL, :])
        u = _dot(x_lo, wub[0:HALF_D, :]) + _dot(x_hi, wub[HALF_D:D_MODEL, :])
        hmid = (g * _sigmoid(g) * u).astype(BF16)
        y = _dot(hmid, wdb[...])
        ys_ref[...] = _pack_halves(y[:, :HALF_D].astype(BF16).astype(F32), y[:, HALF_D:].astype(BF16).astype(F32))

        @pl.when(j == n_active - 1)
        def _():
            drain(1 - slot)

    @pl.when(j >= meta_ref[0])
    def _():
        ys_ref[...] = jnp.zeros(ys_ref.shape, U32)


def _experts(block_e, meta, gsrc, srt, wg, wu, wd, n_slots):
    n_blocks = n_slots // EXPERT_ROWS
    anyspec = pl.BlockSpec(memory_space=pl.ANY)
    return pl.pallas_call(
        _expert_kernel,
        grid_spec=pltpu.PrefetchScalarGridSpec(
            num_scalar_prefetch=3,
            grid=(n_blocks,),
            in_specs=[anyspec, anyspec, anyspec, anyspec],
            out_specs=pl.BlockSpec((EXPERT_ROWS, HALF_D), lambda j, be, meta, gs: (j, 0)),
            scratch_shapes=[pltpu.VMEM((2, EXPERT_ROWS, HALF_D), U32),
                            pltpu.VMEM((2, D_MODEL, D_EXPERT), F32), pltpu.VMEM((2, D_MODEL, D_EXPERT), F32),
                            pltpu.VMEM((2, D_EXPERT, D_MODEL), F32),
                            pltpu.VMEM((D_MODEL, D_EXPERT), BF16), pltpu.VMEM((D_MODEL, D_EXPERT), BF16),
                            pltpu.VMEM((D_EXPERT, D_MODEL), BF16),
                            pltpu.SMEM((1,), I32), pltpu.SemaphoreType.DMA((2,)), pltpu.SemaphoreType.DMA((2,))]),
        out_shape=jax.ShapeDtypeStruct((n_slots, HALF_D), U32),
        compiler_params=_cparams(1),
        name="moe_experts",
    )(block_e, meta, gsrc, srt, wg, wu, wd)


def _combine_kernel(gslot_ref, rec_ref, x1_ref, gtf_ref, g_ref, ys_hbm, y_ref, cbuf, sems, *, tile_base, n_tiles):
    i = pl.program_id(0)
    slot = i % 2
    tm = x1_ref.shape[0]
    n_sorted = cbuf.shape[1]
    tile_granules = n_sorted // GRANULE

    def granule_copy(slot_granule, q, slot_):
        return pltpu.make_async_copy(_granule(ys_hbm, slot_granule),
                                     cbuf.at[slot_, pl.ds(q * GRANULE, GRANULE)], sems.at[slot_])

    def gather(tile, slot_):
        for q in range(tile_granules):
            granule_copy(gslot_ref[tile * tile_granules + q], q, slot_).start(priority=q % 2)

    def drain(slot_):
        for q in range(tile_granules):
            granule_copy(0, q, slot_).wait()

    @pl.when(i == 0)
    def _():
        gather(tile_base, 0)

    gather(tile_base + jnp.minimum(i + 1, n_tiles - 1), 1 - slot)
    drain(slot)

    rec = rec_ref[...]
    s1 = rec[:, 0:1].astype(I32)
    s2 = rec[:, 1:2].astype(I32)
    col = lax.broadcasted_iota(I32, (tm, n_sorted), 1)
    wmat = (jnp.where(col == s1, rec[:, 2:3], 0.0) + jnp.where(col == s2, rec[:, 3:4], 0.0)).astype(BF16)
    y_lo, y_hi = _unpack_halves(cbuf[slot])
    f = jnp.concatenate([_dot(wmat, y_lo), _dot(wmat, y_hi)], axis=1)
    y_ref[...] = x1_ref[...] + gtf_ref[...] * _rms(f, g_ref[...])

    @pl.when(i == n_tiles - 1)
    def _():
        drain(1 - slot)


def _combine(gslot, rec, x1, gtf, g, ys, rows_per_mod, tile_base):
    n = x1.shape[0]
    tm = min(ROW_TILE, n)
    if rows_per_mod:
        tiles_per_mod = rows_per_mod // tm
        mod_spec = pl.BlockSpec((None, 1, D_MODEL), lambda i, *_: (i // tiles_per_mod, 0, 0))
    else:
        mod_spec = pl.BlockSpec((tm, D_MODEL), lambda i, *_: (i, 0))
    row = lambda w: pl.BlockSpec((tm, w), lambda i, *_: (i, 0))
    return pl.pallas_call(
        functools.partial(_combine_kernel, tile_base=tile_base, n_tiles=n // tm),
        grid_spec=pltpu.PrefetchScalarGridSpec(
            num_scalar_prefetch=1,
            grid=(n // tm,),
            in_specs=[row(ROUTE_LANES), row(D_MODEL), mod_spec,
                      pl.BlockSpec((1, D_MODEL), lambda i, *_: (0, 0)),
                      pl.BlockSpec(memory_space=pl.ANY)],
            out_specs=row(D_MODEL),
            scratch_shapes=[pltpu.VMEM((2, _sorted_rows(tm), HALF_D), U32), pltpu.SemaphoreType.DMA((2,))]),
        out_shape=jax.ShapeDtypeStruct((n, D_MODEL), F32),
        compiler_params=_cparams(1),
        name="moe_combine",
    )(gslot, rec, x1, gtf, g, ys)


def _rope_tables(pos):
    half = HEAD_DIM // 2
    inv = jnp.power(jnp.float32(ROPE_THETA), -jnp.arange(half, dtype=F32) / half)
    ang = pos.astype(F32)[:, None] * inv[None, :]
    cos = jnp.cos(ang)
    sin = jnp.sin(ang)
    reps = LANES // HEAD_DIM
    cos_t = jnp.tile(jnp.concatenate([cos, cos], axis=-1), (1, reps))
    sin_t = jnp.tile(jnp.concatenate([-sin, sin], axis=-1), (1, reps))
    return cos_t, sin_t


def _block_diag_gates(w_a, w_x):
    per = MXU_DIM // LRU_BLOCK_W
    groups = LRU_BLOCKS // per

    def bd(w):
        w = w.reshape(groups, per, LRU_BLOCK_W, LRU_BLOCK_W)
        eye = jnp.eye(per, dtype=w.dtype)
        full = jnp.einsum('gpij,pq->gpiqj', w, eye)
        return full.reshape(groups, MXU_DIM, MXU_DIM)
    return jnp.concatenate([bd(w_a), bd(w_x)], axis=-1).astype(BF16)


def _layer_forward(xp, xs_tm, ck, cv, cconv, ch, mod_p, mod_s, p):
    batch, seq, _ = xp.shape
    nseq, _, _, _ = ck.shape
    t_new = xs_tm.shape[0] // nseq
    n_p = batch * seq
    n_s = xs_tm.shape[0]

    w_in = p['w_in']
    wq = w_in[:, :Q_DIM].astype(BF16).reshape(D_MODEL, N_KV_HEADS, GQA_GROUP, HEAD_DIM)
    w_in_b = jnp.concatenate([wq.transpose(0, 2, 1, 3).reshape(D_MODEL, Q_DIM), w_in[:, Q_DIM:].astype(BF16)], axis=1)
    wab = (p['w_attn_branch'].astype(BF16).reshape(N_KV_HEADS, GQA_GROUP, HEAD_DIM, D_MODEL)
           .transpose(1, 0, 2, 3).reshape(Q_DIM, D_MODEL))
    wlb = p['w_lru_branch'].astype(BF16)
    wout = p['w_out'].astype(BF16)
    wbd = _block_diag_gates(p['w_lru_a'], p['w_lru_x'])
    row = lambda v: v.reshape(1, -1)
    wr = jnp.concatenate([p['w_router_group'], p['w_router_expert'],
                          jnp.zeros((D_MODEL, ROUTE_LANES - N_GROUPS - N_EXPERTS), F32)], axis=1)
    wr_top = wr.astype(BF16)
    wr_pair = jnp.concatenate([wr_top, (wr - wr_top.astype(F32)).astype(BF16)], axis=1)
    br = jnp.concatenate([p['b_router_group'], p['b_router_expert'],
                          jnp.zeros((ROUTE_LANES - N_GROUPS - N_EXPERTS,), F32)]).reshape(1, -1)
    wg = p['w_exp_gate']
    wu = p['w_exp_up']
    wd = p['w_exp_down']

    def mods(mod):
        return [mod[:, k * D_MODEL:(k + 1) * D_MODEL] for k in range(6)]
    sh_a_p, sc_a_p, gt_a_p, sh_f_p, sc_f_p, gt_f_p = [m.reshape(batch, 1, D_MODEL) for m in mods(mod_p)]
    sh_a_s, sc_a_s, gt_a_s, sh_f_s, sc_f_s, gt_f_s = [jnp.tile(m, (t_new, 1)) for m in mods(mod_s)]

    lru_w = (p['w_conv'], row(p['b_conv']), wbd, row(p['b_lru_a']), row(p['b_lru_x']), row(p['lru_lambda']))
    cos_p, sin_p = _rope_tables(jnp.arange(seq, dtype=I32))
    q_p, k_p, v_p, sga_p, sgr_p, klast_p, vlast_p, ol_p, xtail_p, hlast_p = _inproj_lru(
        xp.reshape(n_p, D_MODEL), sc_a_p, sh_a_p, row(p['g_pre_mix']), cos_p, sin_p, w_in_b, *lru_w, batch, seq)
    pos_s = jnp.repeat(PAST_LEN_ + jnp.arange(t_new, dtype=I32), nseq)
    cos_s, sin_s = _rope_tables(pos_s)
    q_s, k_s, v_s, xl_s, yl_s, sga_s, sgr_s = _inproj(
        xs_tm, sc_a_s, sh_a_s, row(p['g_pre_mix']), cos_s, sin_s, w_in_b, rows_per_mod=0,
        pos_tiles=n_s // min(ROW_TILE, n_s))

    sinks_perm = p['sinks']
    oa_p = (sinks_perm, q_p, k_p, v_p)
    rows = t_new * GQA_GROUP
    q_s3 = q_s.reshape(t_new, nseq, GQA_GROUP, KV_DIM).transpose(1, 0, 2, 3).reshape(nseq, rows, KV_DIM)
    kn = k_s.reshape(t_new, nseq, KV_DIM).transpose(1, 0, 2)
    vn = v_s.reshape(t_new, nseq, KV_DIM).transpose(1, 0, 2)
    kc = ck.transpose(0, 2, 3, 1).reshape(nseq, KV_DIM, WINDOW)
    vc = cv.transpose(0, 2, 3, 1).reshape(nseq, KV_DIM, WINDOW)
    sink_rows = jnp.tile(p['sinks'].reshape(N_KV_HEADS, 1, GQA_GROUP), (1, t_new, 1)).reshape(N_KV_HEADS, rows, 1)
    oa_s3 = _attn_sample(sink_rows, q_s3, kn, vn, kc, vc)
    oa_s = oa_s3.reshape(nseq, t_new, Q_DIM).transpose(1, 0, 2).reshape(n_s, Q_DIM)

    ol_s3, hlast_s = _lru_sample(xl_s.reshape(t_new, nseq, LRU_WIDTH), yl_s.reshape(t_new, nseq, LRU_WIDTH),
                                 cconv.transpose(1, 0, 2), ch, *lru_w)
    ol_s = ol_s3.reshape(n_s, LRU_WIDTH)

    tm_post = min(ROW_TILE, n_p)
    tri = jnp.tril(jnp.ones((tm_post, tm_post), F32), -1).astype(BF16)
    post_w = (wab, wlb, wout, row(p['g_post_mix']), row(p['g_pre_ffn']), wr_pair, wr_top, br)
    zero_cnt = jnp.zeros((1, ROUTE_LANES), F32)
    x1_p, h2_p, route_p, stat_p, cnt_p = _post(oa_p, ol_p, sga_p, sgr_p, xp.reshape(n_p, D_MODEL),
                                               gt_a_p, sc_f_p, sh_f_p, *post_w, tri, zero_cnt, rows_per_mod=seq)
    tm_s = min(ROW_TILE, n_s)
    tri_s = tri if tm_s == tm_post else jnp.tril(jnp.ones((tm_s, tm_s), F32), -1).astype(BF16)
    x1_s, h2_s, route_s, stat_s, cnt_all = _post(oa_s, ol_s, sga_s, sgr_s, xs_tm,
                                                 gt_a_s, sc_f_s, sh_f_s, *post_w, tri_s, cnt_p, rows_per_mod=0)

    e_lanes = slice(N_GROUPS, N_GROUPS + N_EXPERTS)
    to_granules = lambda v: (v.astype(I32) // GRANULE).reshape(-1)
    totals = to_granules(cnt_all[0, e_lanes])
    stats = jnp.concatenate([stat_p, stat_s], axis=0).reshape(-1, SUBLANES, ROUTE_LANES)
    cnt8 = to_granules(stats[:, 0, e_lanes])
    goff = to_granules(stats[:, 1, e_lanes])
    tiles_p = n_p // tm_post
    n_tiles = tiles_p + n_s // tm_s
    max_rows = 2 * (n_p + n_s) + n_tiles * N_EXPERTS * (GRANULE - 1) + N_EXPERTS * (EXPERT_ROWS - GRANULE)
    n_blocks = -(-max_rows // EXPERT_ROWS)
    n_slots = n_blocks * EXPERT_ROWS
    srt, gsrc, gslot = _sort(totals, cnt8, goff, route_p, h2_p, route_s, h2_s, n_slots)
    block_e, meta = _plan(totals, n_blocks)
    ys = _experts(block_e, meta, gsrc, srt, wg, wu, wd, n_slots)
    y_p = _combine(gslot, route_p, x1_p, gt_f_p, row(p['g_post_ffn']), ys, seq, 0)
    y_s = _combine(gslot, route_s, x1_s, gt_f_s, row(p['g_post_ffn']), ys, 0, tiles_p)

    k_new_p = klast_p.reshape(batch, WINDOW, N_KV_HEADS, HEAD_DIM)
    v_new_p = vlast_p.reshape(batch, WINDOW, N_KV_HEADS, HEAD_DIM)
    conv_p = xtail_p[:, -(CONV_WIDTH - 1):]
    h_p = hlast_p.reshape(batch, LRU_WIDTH)
    def next_window(cache_t, new):
        win_t = jnp.concatenate([cache_t[:, :, t_new:], new.transpose(0, 2, 1)], axis=2)
        return win_t.reshape(nseq, N_KV_HEADS, HEAD_DIM, WINDOW).transpose(0, 3, 1, 2)
    k_new_s = next_window(kc, kn)
    v_new_s = next_window(vc, vn)
    xl_s3 = xl_s.reshape(t_new, nseq, LRU_WIDTH).transpose(1, 0, 2).astype(F32)
    conv_s = jnp.concatenate([cconv, xl_s3], axis=1)[:, -(CONV_WIDTH - 1):]
    return (y_p.reshape(batch, seq, D_MODEL), y_s, k_new_p, v_new_p, conv_p, h_p,
            k_new_s, v_new_s, conv_s, hlast_s)


PAST_LEN_ = 16384

PARAM_NAMES = ('w_ada', 'b_ada', 'g_pre_mix', 'g_post_mix', 'g_pre_ffn', 'g_post_ffn', 'w_in', 'sinks',
               'w_conv', 'b_conv', 'w_lru_a', 'b_lru_a', 'w_lru_x', 'b_lru_x', 'lru_lambda',
               'w_attn_branch', 'w_lru_branch', 'w_out', 'w_router_group', 'b_router_group',
               'w_router_expert', 'b_router_expert', 'w_exp_gate', 'w_exp_up', 'w_exp_down')


def kernel(x_prompt, x_sample, cache_k_win, cache_v_win, state_conv, state_h, c_prompt, c_sample, w_ada, b_ada, g_pre_mix, g_post_mix, g_pre_ffn, g_post_ffn, w_in, sinks, w_conv, b_conv, w_lru_a, b_lru_a, w_lru_x, b_lru_x, lru_lambda, w_attn_branch, w_lru_branch, w_out, w_router_group, b_router_group, w_router_expert, b_router_expert, w_exp_gate, w_exp_up, w_exp_down):
    weights = (w_ada, b_ada, g_pre_mix, g_post_mix, g_pre_ffn, g_post_ffn, w_in, sinks,
               w_conv, b_conv, w_lru_a, b_lru_a, w_lru_x, b_lru_x, lru_lambda,
               w_attn_branch, w_lru_branch, w_out, w_router_group, b_router_group,
               w_router_expert, b_router_expert, w_exp_gate, w_exp_up, w_exp_down)
    depth = w_ada.shape[0]
    batch = x_prompt.shape[0]
    nseq, t_new, _ = x_sample.shape
    y_p = x_prompt
    y_s = x_sample.transpose(1, 0, 2).reshape(t_new * nseq, D_MODEL)
    c_all = jnp.concatenate([c_prompt, c_sample], axis=0)
    outs = [[] for _ in range(8)]
    for layer in range(depth):
        p = {name: w[layer] for name, w in zip(PARAM_NAMES, weights)}
        mod = _ada(c_all, p['w_ada'], p['b_ada'].reshape(1, -1))
        res = _layer_forward(y_p, y_s, cache_k_win[layer], cache_v_win[layer], state_conv[layer],
                             state_h[layer], mod[:batch], mod[batch:], p)
        y_p, y_s = res[0], res[1]
        for o, r in zip(outs, res[2:]):
            o.append(r)
    y_sample = y_s.reshape(t_new, nseq, D_MODEL).transpose(1, 0, 2)
    return (y_p, y_sample) + tuple(jnp.stack(o) for o in outs)
```
